```python
import jax, jax.numpy as jnp
from jax import lax
import numpy as np

D_MODEL = 1024
BATCH = 8
SEQ = 4096
DEPTH = 2

GRID_W = 64
CTX_LEN = 256
HEAD_DIM = 64
SWA_HEADS = 6
SWA_KV_HEADS = 2
WINDOW = 128
GLB_HEADS = 6
GLB_KV_HEADS = 2
MLA_HEADS = 4
MLA_NOPE_DIM = 64
MLA_ROPE_DIM = 32
MLA_V_DIM = 64
MLA_Q_RANK = 256
MLA_KV_RANK = 128
MIX_WIDTH = (SWA_HEADS + GLB_HEADS) * HEAD_DIM + MLA_HEADS * MLA_V_DIM
BLOCK = 128
ROPE_THETA = 10000.0
N_EXPERTS = 16
N_GROUPS = 4
EXPERTS_PER_GROUP = N_EXPERTS // N_GROUPS
TOPK_GROUPS = 1
GROUP_SCORE_TOPK = 2
TOP_K = 2
D_EXPERT = 512
D_SHARED = 512
MOE_BLOCK = 128
EPS = 1e-6
NEG_INF = -1e30

IN_SPLIT_SIZES = (SWA_HEADS * HEAD_DIM, SWA_KV_HEADS * HEAD_DIM, SWA_KV_HEADS * HEAD_DIM,
                  GLB_HEADS * HEAD_DIM, GLB_KV_HEADS * HEAD_DIM, GLB_KV_HEADS * HEAD_DIM,
                  MLA_Q_RANK, MLA_KV_RANK, MLA_ROPE_DIM)
IN_WIDTH = sum(IN_SPLIT_SIZES)
IN_SPLIT_POINTS = tuple(int(p) for p in np.cumsum(IN_SPLIT_SIZES)[:-1])

kernel_name = "hymba_hybrid_dit_mla_swa_axial_moe"


def rmsnorm(x, g):
    xf = x.astype(jnp.float32)
    y = xf * lax.rsqrt(jnp.mean(xf * xf, axis=-1, keepdims=True) + EPS)
    return (y * g.astype(jnp.float32)).astype(x.dtype)


def modulate(h, shift, scale):
    return h * (1 + scale) + shift


def swiglu(h, w_gate, w_up, w_down):
    return (jax.nn.silu(h @ w_gate) * (h @ w_up)) @ w_down


def axial_rope(x, rows, cols):
    d = x.shape[-1]
    half = d // 2
    nf = half // 2
    inv = ROPE_THETA ** (-jnp.arange(nf, dtype=jnp.float32) / nf)
    bshape = (1, x.shape[1]) + (1,) * (x.ndim - 3) + (nf,)

    def rotate(xa, pos):
        ang = pos.astype(jnp.float32)[:, None] * inv
        cos = jnp.cos(ang).reshape(bshape)
        sin = jnp.sin(ang).reshape(bshape)
        x1 = xa[..., :nf].astype(jnp.float32)
        x2 = xa[..., nf:].astype(jnp.float32)
        return jnp.concatenate([x1 * cos - x2 * sin, x2 * cos + x1 * sin], axis=-1)

    return jnp.concatenate([rotate(x[..., :half], rows), rotate(x[..., half:], cols)], axis=-1).astype(x.dtype)


def attend(q, k, v, scale, mask=None, sink=None):
    s = jnp.einsum("...qhgd,...khd->...hgqk", q, k).astype(jnp.float32) * scale
    if mask is not None:
        s = jnp.where(mask, s, NEG_INF)
    if sink is not None:
        sink_logit = sink.astype(jnp.float32).reshape(s.shape[-4], s.shape[-3], 1, 1)
        sink_logit = jnp.broadcast_to(sink_logit, s.shape[:-1] + (1,))
        p = jax.nn.softmax(jnp.concatenate([s, sink_logit], axis=-1), axis=-1)[..., :-1]
    else:
        p = jax.nn.softmax(s, axis=-1)
    return jnp.einsum("...hgqk,...khd->...qhgd", p.astype(v.dtype), v)


def window_attention(q, k, v, k_ctx, v_ctx, sink):
    bsz, seq = q.shape[:2]
    nb = seq // BLOCK
    n_ctx = k_ctx.shape[1]
    halo = ((0, 0), (BLOCK, BLOCK), (0, 0), (0, 0))
    k_pad = jnp.pad(k, halo)
    v_pad = jnp.pad(v, halo)
    q_blocks = jnp.moveaxis(q.reshape((bsz, nb, BLOCK) + q.shape[2:]), 1, 0)
    offs_q = jnp.arange(BLOCK)
    offs_k = jnp.arange(3 * BLOCK) - BLOCK
    ctx_ok = jnp.ones((BLOCK, n_ctx), dtype=bool)
    scale = HEAD_DIM ** -0.5

    def one_block(args):
        q_blk, i = args
        start = i * BLOCK
        k_loc = lax.dynamic_slice_in_dim(k_pad, start, 3 * BLOCK, axis=1)
        v_loc = lax.dynamic_slice_in_dim(v_pad, start, 3 * BLOCK, axis=1)
        q_pos = start + offs_q
        k_pos = start + offs_k
        loc_ok = ((jnp.abs(q_pos[:, None] - k_pos[None, :]) <= WINDOW)
                  & (k_pos >= 0)[None, :] & (k_pos < seq)[None, :])
        mask = jnp.concatenate([ctx_ok, loc_ok], axis=1)
        return attend(q_blk, jnp.concatenate([k_ctx, k_loc], axis=1),
                      jnp.concatenate([v_ctx, v_loc], axis=1), scale, mask, sink)

    out = lax.map(one_block, (q_blocks, jnp.arange(nb)))
    return jnp.moveaxis(out, 0, 1).reshape(bsz, seq, -1)


def block_sweep_attention(q, k, v, scale):
    bsz, seq = q.shape[:2]
    nb = seq // BLOCK
    q_blocks = jnp.moveaxis(q.reshape((bsz, nb, BLOCK) + q.shape[2:]), 1, 0)
    out = lax.map(lambda q_blk: attend(q_blk, k, v, scale), q_blocks)
    return jnp.moveaxis(out, 0, 1).reshape(bsz, seq, -1)


def mixer_inputs(h, w_in, glb_q_gain, glb_k_gain, mla_q_gain, mla_w_uq, mla_kv_gain, mla_w_ukv):
    bsz, n = h.shape[:2]
    sq, sk, sv, gq, gk, gv, mq_lat, mkv_lat, mk_rope = jnp.split(h @ w_in, IN_SPLIT_POINTS, axis=-1)
    swa = (sq.reshape(bsz, n, SWA_KV_HEADS, SWA_HEADS // SWA_KV_HEADS, HEAD_DIM),
           sk.reshape(bsz, n, SWA_KV_HEADS, HEAD_DIM),
           sv.reshape(bsz, n, SWA_KV_HEADS, HEAD_DIM))
    glb = (rmsnorm(gq.reshape(bsz, n, GLB_KV_HEADS, GLB_HEADS // GLB_KV_HEADS, HEAD_DIM), glb_q_gain),
           rmsnorm(gk.reshape(bsz, n, GLB_KV_HEADS, HEAD_DIM), glb_k_gain),
           gv.reshape(bsz, n, GLB_KV_HEADS, HEAD_DIM))
    mq = (rmsnorm(mq_lat, mla_q_gain) @ mla_w_uq).reshape(bsz, n, MLA_HEADS, 1, MLA_NOPE_DIM + MLA_ROPE_DIM)
    mkv = (rmsnorm(mkv_lat, mla_kv_gain) @ mla_w_ukv).reshape(bsz, n, MLA_HEADS, MLA_NOPE_DIM + MLA_V_DIM)
    mla = (mq[..., :MLA_NOPE_DIM], mq[..., MLA_NOPE_DIM:],
           mkv[..., :MLA_NOPE_DIM], mk_rope.reshape(bsz, n, 1, MLA_ROPE_DIM), mkv[..., MLA_NOPE_DIM:])
    return swa, glb, mla


def mla_queries(q_nope, q_rope):
    return jnp.concatenate([q_nope, q_rope], axis=-1)


def mla_keys(k_nope, k_rope):
    k_rope = jnp.broadcast_to(k_rope, k_nope.shape[:-1] + (MLA_ROPE_DIM,))
    return jnp.concatenate([k_nope, k_rope], axis=-1)


def routed_experts(h, idx, gate_w, w_gate, w_up, w_down):
    n_tok, d = h.shape
    n_assign = n_tok * TOP_K
    flat_e = idx.reshape(-1)
    order = jnp.argsort(flat_e)
    sorted_e = flat_e[order]
    counts = jnp.bincount(flat_e, length=N_EXPERTS)
    padded = (counts + MOE_BLOCK - 1) // MOE_BLOCK * MOE_BLOCK
    pad_end = jnp.cumsum(padded)
    pad_start = pad_end - padded
    start = jnp.cumsum(counts) - counts
    dest = pad_start[sorted_e] + jnp.arange(n_assign) - start[sorted_e]
    n_blocks = -(-n_assign // MOE_BLOCK) + N_EXPERTS
    src_tok = order // TOP_K
    slot_tok = jnp.full((n_blocks * MOE_BLOCK,), n_tok, jnp.int32).at[dest].set(src_tok)
    h_slots = jnp.concatenate([h, jnp.zeros((1, d), h.dtype)], axis=0)[slot_tok]
    block_e = jnp.minimum(jnp.searchsorted(pad_end, jnp.arange(n_blocks) * MOE_BLOCK, side="right"),
                          N_EXPERTS - 1)

    def expert_block(args):
        hb, e = args
        return swiglu(hb, w_gate[e], w_up[e], w_down[e])

    y = lax.map(expert_block, (h_slots.reshape(n_blocks, MOE_BLOCK, d), block_e)).reshape(-1, d)
    contrib = y[dest] * gate_w.reshape(-1)[order][:, None].astype(y.dtype)
    return jax.ops.segment_sum(contrib, src_tok, num_segments=n_tok)


def moe_ffn(h, router_w, router_bias, w_gate, w_up, w_down, sw_gate, sw_up, sw_down):
    n_tok = h.shape[0]
    scores = jax.nn.sigmoid((h @ router_w).astype(jnp.float32))
    biased = scores + router_bias.astype(jnp.float32)
    grp_score = lax.top_k(biased.reshape(n_tok, N_GROUPS, EXPERTS_PER_GROUP), GROUP_SCORE_TOPK)[0].sum(-1)
    _, grp_idx = lax.top_k(grp_score, TOPK_GROUPS)
    grp_sel = jnp.any(grp_idx[..., None] == jnp.arange(N_GROUPS), axis=-2)
    expert_ok = jnp.repeat(grp_sel, EXPERTS_PER_GROUP, axis=-1)
    _, idx = lax.top_k(jnp.where(expert_ok, biased, NEG_INF), TOP_K)
    w = jnp.take_along_axis(scores, idx, axis=-1)
    w = w / jnp.sum(w, axis=-1, keepdims=True)
    return routed_experts(h, idx, w, w_gate, w_up, w_down) + swiglu(h, sw_gate, sw_up, sw_down)


def setup_inputs(seed: int = 0) -> dict:
    key = jax.random.key(seed)
    ks = jax.random.split(key, 26)
    L, D, E = DEPTH, D_MODEL, N_EXPERTS

    def nrm(k, shape, s):
        return jax.random.normal(k, shape, jnp.float32) * s

    return {
        "x": nrm(ks[0], (BATCH, SEQ, D), 1.0),
        "c": nrm(ks[1], (BATCH, D), 1.0),
        "ctx": nrm(ks[2], (BATCH, CTX_LEN, D), 1.0),
        "c_ctx": nrm(ks[3], (D,), 1.0),
        "w_mod": nrm(ks[4], (L, D, 6 * D), 0.5 * D ** -0.5),
        "b_mod": nrm(ks[5], (L, 6 * D), 0.02),
        "norm_mix_g": 1.0 + nrm(ks[6], (L, D), 0.1),
        "norm_ffn_g": 1.0 + nrm(ks[7], (L, D), 0.1),
        "w_in": nrm(ks[8], (L, D, IN_WIDTH), D ** -0.5),
        "w_out": nrm(ks[9], (L, MIX_WIDTH, D), MIX_WIDTH ** -0.5),
        "swa_sink": nrm(ks[10], (L, SWA_HEADS), 0.5),
        "glb_q_gain": 1.0 + nrm(ks[11], (L, HEAD_DIM), 0.1),
        "glb_k_gain": 1.0 + nrm(ks[12], (L, HEAD_DIM), 0.1),
        "mla_q_gain": 1.0 + nrm(ks[13], (L, MLA_Q_RANK), 0.1),
        "mla_w_uq": nrm(ks[14], (L, MLA_Q_RANK, MLA_HEADS * (MLA_NOPE_DIM + MLA_ROPE_DIM)), MLA_Q_RANK ** -0.5),
        "mla_kv_gain": 1.0 + nrm(ks[15], (L, MLA_KV_RANK), 0.1),
        "mla_w_ukv": nrm(ks[16], (L, MLA_KV_RANK, MLA_HEADS * (MLA_NOPE_DIM + MLA_V_DIM)), MLA_KV_RANK ** -0.5),
        "router_w": nrm(ks[17], (D, E), D ** -0.5),
        "router_bias": nrm(ks[18], (E,), 0.01),
        "exp_w_gate": nrm(ks[19], (L, E, D, D_EXPERT), D ** -0.5),
        "exp_w_up": nrm(ks[20], (L, E, D, D_EXPERT), D ** -0.5),
        "exp_w_down": nrm(ks[21], (L, E, D_EXPERT, D), D_EXPERT ** -0.5),
        "shr_w_gate": nrm(ks[22], (L, D, D_SHARED), D ** -0.5),
        "shr_w_up": nrm(ks[23], (L, D, D_SHARED), D ** -0.5),
        "shr_w_down": nrm(ks[24], (L, D_SHARED, D), D_SHARED ** -0.5),
        "final_norm_g": 1.0 + nrm(ks[25], (D,), 0.1),
    }


def reference(x, c, ctx, c_ctx, w_mod, b_mod, norm_mix_g, norm_ffn_g, w_in, w_out, swa_sink,
              glb_q_gain, glb_k_gain, mla_q_gain, mla_w_uq, mla_kv_gain, mla_w_ukv,
              router_w, router_bias, exp_w_gate, exp_w_up, exp_w_down,
              shr_w_gate, shr_w_up, shr_w_down, final_norm_g):
    bsz, seq, d = x.shape
    n_ctx = ctx.shape[1]
    rows_n = seq // GRID_W
    rows = jnp.repeat(jnp.arange(rows_n), GRID_W)
    cols = jnp.tile(jnp.arange(GRID_W), rows_n)
    hd_scale = HEAD_DIM ** -0.5
    mla_scale = (MLA_NOPE_DIM + MLA_ROPE_DIM) ** -0.5

    for l in range(DEPTH):
        last = l == DEPTH - 1
        mod_x = jnp.split((jax.nn.silu(c) @ w_mod[l] + b_mod[l])[:, None, :], 6, axis=-1)
        mod_c = jnp.split(jax.nn.silu(c_ctx) @ w_mod[l] + b_mod[l], 6, axis=-1)
        mix_params = (w_in[l], glb_q_gain[l], glb_k_gain[l], mla_q_gain[l], mla_w_uq[l],
                      mla_kv_gain[l], mla_w_ukv[l])

        h = modulate(rmsnorm(x, norm_mix_g[l]), mod_x[0], mod_x[1])
        hc = modulate(rmsnorm(ctx, norm_mix_g[l]), mod_c[0], mod_c[1])
        (sq, sk, sv), (gq, gk, gv), (mqn, mqr, mkn, mkr, mv) = mixer_inputs(h, *mix_params)
        (csq, csk, csv), (cgq, cgk, cgv), (cmqn, cmqr, cmkn, cmkr, cmv) = mixer_inputs(hc, *mix_params)
        cmk = mla_keys(cmkn, cmkr)

        y_swa = window_attention(axial_rope(sq, rows, cols), axial_rope(sk, rows, cols), sv,
                                 csk, csv, swa_sink[l])
        y_glb = block_sweep_attention(axial_rope(gq, rows, cols),
                                      jnp.concatenate([cgk, axial_rope(gk, rows, cols)], axis=1),
                                      jnp.concatenate([cgv, gv], axis=1), hd_scale)
        y_mla = block_sweep_attention(mla_queries(mqn, axial_rope(mqr, rows, cols)),
                                      jnp.concatenate([cmk, mla_keys(mkn, axial_rope(mkr, rows, cols))], axis=1),
                                      jnp.concatenate([cmv, mv], axis=1), mla_scale)
        x = x + mod_x[2] * (jnp.concatenate([y_swa, y_glb, y_mla], axis=-1) @ w_out[l])

        h = modulate(rmsnorm(x, norm_ffn_g[l]), mod_x[3], mod_x[4]).reshape(-1, d)
        moe_params = (router_w, router_bias, exp_w_gate[l], exp_w_up[l], exp_w_down[l],
                      shr_w_gate[l], shr_w_up[l], shr_w_down[l])
        if last:
            f = moe_ffn(h, *moe_params)
        else:
            yc = jnp.concatenate([
                attend(csq, csk, csv, hd_scale, sink=swa_sink[l]).reshape(bsz, n_ctx, -1),
                attend(cgq, cgk, cgv, hd_scale).reshape(bsz, n_ctx, -1),
                attend(mla_queries(cmqn, cmqr), cmk, cmv, mla_scale).reshape(bsz, n_ctx, -1)], axis=-1)
            ctx = ctx + mod_c[2] * (yc @ w_out[l])
            hc = modulate(rmsnorm(ctx, norm_ffn_g[l]), mod_c[3], mod_c[4]).reshape(-1, d)
            f_all = moe_ffn(jnp.concatenate([h, hc], axis=0), *moe_params)
            f = f_all[: bsz * seq]
            ctx = ctx + mod_c[5] * f_all[bsz * seq:].reshape(ctx.shape)
        x = x + mod_x[5] * f.reshape(x.shape)

    return rmsnorm(x, final_norm_g)
```

```python
import functools

import jax
import jax.numpy as jnp
import numpy as np
from jax import lax
from jax.experimental import pallas as pl
from jax.experimental.pallas import tpu as pltpu

F32 = jnp.float32
BF16 = jnp.bfloat16

D_MODEL = 1024
GRID_W = 64
HEAD_DIM = 64
GQA_HEADS = 6
GQA_KV = 2
GQA_G = GQA_HEADS // GQA_KV
WINDOW = 128
MLA_HEADS = 4
MLA_NOPE = 64
MLA_ROPE = 32
MLA_V = 64
MLA_Q_RANK = 256
MLA_KV_RANK = 128
ROPE_THETA = 10000.0
N_EXPERTS = 16
N_GROUPS = 4
EXPERTS_PER_GROUP = 4
N_PAIRS = 6
N_BUCKETS = N_GROUPS * N_PAIRS
D_EXPERT = 512
EPS = 1e-6
NEG_INF = -1e30

LANES = 128
TQ = 256
TK = 256
TM = 256
ROW_W = D_MODEL + LANES
IN_W = 1792
VMEM_LIMIT = 56 * 1024 * 1024

_SQ, _SK, _SV, _GQ, _GK, _GV, _MQ, _MKV, _KR = 0, 384, 512, 640, 1024, 1152, 1280, 1536, 1664

_PAIR_LO = (0, 0, 0, 1, 1, 2)
_PAIR_HI = (1, 2, 3, 2, 3, 3)


def _cparams(n_axes):
    return pltpu.CompilerParams(dimension_semantics=("arbitrary",) * n_axes,
                                vmem_limit_bytes=VMEM_LIMIT)


def _mod_kernel(c_ref, w_ref, b_ref, o_ref):
    c = c_ref[...]
    a = (c * jax.nn.sigmoid(c)).astype(BF16)
    o_ref[0] = jnp.dot(a, w_ref[0].astype(BF16), preferred_element_type=F32) + b_ref[0]


def _modulation(c_rows, w_mod, b_mod):
    n_layers, d, width = w_mod.shape
    rows = c_rows.shape[0]
    nb = 1536
    return pl.pallas_call(
        _mod_kernel,
        out_shape=jax.ShapeDtypeStruct((n_layers, rows, width), F32),
        grid=(n_layers, width // nb),
        in_specs=[pl.BlockSpec((rows, d), lambda l, j: (0, 0)),
                  pl.BlockSpec((1, d, nb), lambda l, j: (l, 0, j)),
                  pl.BlockSpec((1, 1, nb), lambda l, j: (l, 0, j))],
        out_specs=pl.BlockSpec((1, rows, nb), lambda l, j: (l, 0, j)),
        compiler_params=_cparams(2),
        name="adaln_mod",
    )(c_rows, w_mod, b_mod.reshape(n_layers, 1, width))


def _rope(x, cos, sin_signed, nf):
    lane = lax.broadcasted_iota(jnp.int32, (1, LANES), 1)
    first = (lane % (2 * nf)) < nf
    tiles = []
    for t in range(x.shape[1] // LANES):
        xt = x[:, t * LANES:(t + 1) * LANES]
        partner = jnp.where(first, pltpu.roll(xt, LANES - nf, 1), pltpu.roll(xt, nf, 1))
        tiles.append(xt * cos + partner * sin_signed)
    return tiles[0] if len(tiles) == 1 else jnp.concatenate(tiles, axis=1)


def _group_padded(q, out_dtype):
    lane = lax.broadcasted_iota(jnp.int32, (1, LANES), 1)
    tiles = []
    for h in range(GQA_HEADS):
        g = h // GQA_G
        tile = q[:, (h // 2) * LANES:(h // 2 + 1) * LANES]
        if (h % 2) != g:
            tile = pltpu.roll(tile, HEAD_DIM, 1)
        tiles.append(jnp.where((lane // HEAD_DIM) == g, tile, 0.0))
    return jnp.concatenate(tiles, axis=1).astype(out_dtype)


def _rms(x, eps=EPS):
    return x * lax.rsqrt(jnp.mean(x * x, axis=-1, keepdims=True) + eps)


def _in_kernel(x_ref, mod_ref, g_ref, win_ref, cos64_ref, sin64_ref, cosm_ref, sinm_ref,
               gain_ref, mqg_ref, mkvg_ref, wuq_ref, wukv_ref, bd_ref,
               qs_ref, ks_ref, vs_ref, qg_ref, kg_ref, vg_ref, qm_ref, km_ref, vm_ref, *, mla_scale):
    x = x_ref[0]
    mod = mod_ref[0, 0]
    h = _rms(x) * g_ref[...]
    h = h * (1.0 + mod[1:2]) + mod[0:1]
    z = jnp.dot(h.astype(BF16), win_ref[...], preferred_element_type=F32)
    cos64, sin64 = cos64_ref[...], sin64_ref[...]
    cosm, sinm = cosm_ref[...], sinm_ref[...]

    sqk = _rope(z[:, _SQ:_SV], cos64, sin64, 16)
    qs_ref[0] = _group_padded(sqk[:, :384] * (HEAD_DIM ** -0.5), BF16)
    ks_ref[0] = sqk[:, 384:].astype(BF16)
    vs_ref[0] = z[:, _SV:_GQ].astype(BF16)

    gqk = z[:, _GQ:_GV]
    sq = gqk * gqk
    hi = sq.astype(BF16)
    lo = (sq - hi.astype(F32)).astype(BF16)
    ssum = (jnp.dot(hi, bd_ref[...], preferred_element_type=F32)
            + jnp.dot(lo, bd_ref[...], preferred_element_type=F32))
    gqk = gqk * lax.rsqrt(ssum * (1.0 / HEAD_DIM) + EPS) * gain_ref[...]
    gqk = _rope(gqk, cos64, sin64, 16)
    qg_ref[0] = _group_padded(gqk[:, :384], BF16)
    kg_ref[0] = gqk[:, 384:].astype(BF16)
    vg_ref[0] = z[:, _GV:_MQ].astype(BF16)

    qn = _rms(z[:, _MQ:_MKV]) * mqg_ref[...]
    mq = jnp.dot(qn.astype(BF16), wuq_ref[...], preferred_element_type=F32)
    qm_ref[0] = (_rope(mq, cosm, sinm, 8) * mla_scale).astype(BF16)
    kvn = _rms(z[:, _MKV:_KR]) * mkvg_ref[...]
    mkv = jnp.dot(kvn.astype(BF16), wukv_ref[...], preferred_element_type=F32)
    kr = _rope(z[:, _KR:IN_W], cosm, sinm, 8)
    km_ref[0] = (mkv[:, :512] + jnp.concatenate([kr] * MLA_HEADS, axis=1)).astype(BF16)
    vm_ref[0] = mkv[:, 512:].astype(BF16)


def _input_projection(xs, modtab, g_mix, lw, tabs, n_ctx_blk):
    bsz, t_all, d = xs.shape
    nblk = t_all // TQ
    tok = lambda w: pl.BlockSpec((1, TQ, w), lambda b, i: (b, i, 0))
    full = lambda a: pl.BlockSpec(a.shape, lambda b, i: (0,) * a.ndim)
    tab = pl.BlockSpec((TQ, LANES), lambda b, i: (i, 0))
    widths = (768, 128, 128, 768, 128, 128, 512, 512, 256)
    consts = (lw["gain512"], lw["mqg"], lw["mkvg"], lw["wuq"], lw["wukv"], lw["bd"])
    return pl.pallas_call(
        functools.partial(_in_kernel, mla_scale=float((MLA_NOPE + MLA_ROPE) ** -0.5)),
        out_shape=tuple(jax.ShapeDtypeStruct((bsz, t_all, w), BF16) for w in widths),
        grid=(bsz, nblk),
        in_specs=[tok(d),
                  pl.BlockSpec((1, 1, 8, d), lambda b, i: (b, jnp.where(i < n_ctx_blk, 1, 0), 0, 0)),
                  full(g_mix), full(lw["win"]), tab, tab, tab, tab] + [full(a) for a in consts],
        out_specs=tuple(tok(w) for w in widths),
        compiler_params=_cparams(2),
        name="in_proj",
    )(xs, modtab, g_mix, lw["win"], *tabs, *consts)


def _softmax_chunk(q, k, v, mask, m_ref, l_ref, acc_ref):
    s = lax.dot_general(q, k, (((1,), (1,)), ((), ())), preferred_element_type=F32)
    if mask is not None:
        s = jnp.where(mask, s, NEG_INF)
    m_prev = m_ref[...]
    m_new = jnp.maximum(m_prev, jnp.max(s, axis=-1, keepdims=True))
    alpha = jnp.exp(m_prev - m_new)
    p = jnp.exp(s - m_new)
    l_ref[...] = alpha * l_ref[...] + jnp.sum(p, axis=-1, keepdims=True)
    acc_ref[...] = alpha * acc_ref[...] + jnp.dot(p.astype(BF16), v, preferred_element_type=F32)
    m_ref[...] = m_new


def _attn_kernel(*refs, kind, n_ctx, n_ctx_blk, q_blk_off, seq):
    if kind == "swa":
        sink_ref, q_ref, k_ref, v_ref, o_ref, qs_ref, m_ref, l_ref, acc_ref = refs
    else:
        q_ref, k_ref, v_ref, o_ref, qs_ref, m_ref, l_ref, acc_ref = refs
    blk = pl.program_id(1) + q_blk_off
    is_lat = blk >= n_ctx_blk
    gqa = kind in ("swa", "glb")
    n_groups = GQA_KV if gqa else MLA_HEADS
    rows = GQA_G * TQ if gqa else TQ
    lane = lax.broadcasted_iota(jnp.int32, (1, LANES), 1)
    outs = []

    for g in range(n_groups):
        if gqa:
            for j in range(GQA_G):
                h = g * GQA_G + j
                qs_ref[j * TQ:(j + 1) * TQ, :] = q_ref[0, :, h * LANES:(h + 1) * LANES]
            kt = vt = 0
        else:
            qs_ref[...] = q_ref[0, :, g * LANES:(g + 1) * LANES]
            kt, vt = g, g // 2
        ksl = slice(kt * LANES, (kt + 1) * LANES)
        vsl = slice(vt * LANES, (vt + 1) * LANES)

        if kind == "swa":
            for j in range(GQA_G):
                m_ref[j * TQ:(j + 1) * TQ, :] = jnp.full((TQ, 1), sink_ref[g * GQA_G + j], F32)
            l_ref[...] = jnp.ones(l_ref.shape, F32)
        else:
            m_ref[...] = jnp.full(m_ref.shape, NEG_INF, F32)
            l_ref[...] = jnp.zeros(l_ref.shape, F32)
        acc_ref[...] = jnp.zeros(acc_ref.shape, F32)

        for c0 in range(0, n_ctx, TK):
            _softmax_chunk(qs_ref[...], k_ref[0, c0:c0 + TK, ksl], v_ref[0, c0:c0 + TK, vsl], None,
                           m_ref, l_ref, acc_ref)

        if kind == "swa":
            @pl.when(is_lat)
            def _():
                q0 = (blk - n_ctx_blk) * TQ
                k0 = pl.multiple_of(jnp.clip(q0 - WINDOW, 0, seq - (TQ + 2 * WINDOW)), WINDOW)
                nk = TQ + 2 * WINDOW
                qpos = q0 + (lax.broadcasted_iota(jnp.int32, (rows, nk), 0) & (TQ - 1))
                kpos = k0 + lax.broadcasted_iota(jnp.int32, (rows, nk), 1)
                mask = jnp.abs(qpos - kpos) <= WINDOW
                r0 = pl.multiple_of(n_ctx + k0, WINDOW)
                _softmax_chunk(qs_ref[...], k_ref[0, pl.ds(r0, nk), ksl], v_ref[0, pl.ds(r0, nk), vsl],
                               mask, m_ref, l_ref, acc_ref)
        else:
            def body(c, carry):
                r0 = pl.multiple_of(n_ctx + c * TK, TK)
                _softmax_chunk(qs_ref[...], k_ref[0, pl.ds(r0, TK), ksl], v_ref[0, pl.ds(r0, TK), vsl],
                               None, m_ref, l_ref, acc_ref)
                return carry
            lax.fori_loop(0, jnp.where(is_lat, seq // TK, 0), body, 0)

        o = acc_ref[...] / l_ref[...]
        if gqa:
            outs.extend((o[j * TQ:(j + 1) * TQ, :], g) for j in range(GQA_G))
        else:
            outs.append((o, g % 2))

    tiles = []
    for t in range(len(outs) // 2):
        (left, lh), (right, rh) = outs[2 * t], outs[2 * t + 1]
        if lh != 0:
            left = pltpu.roll(left, HEAD_DIM, 1)
        if rh != 1:
            right = pltpu.roll(right, HEAD_DIM, 1)
        tiles.append(jnp.where(lane < HEAD_DIM, left, right))
    o_ref[0] = jnp.concatenate(tiles, axis=1).astype(o_ref.dtype)


def _attention(kind, q, k, v, sink, n_ctx, with_ctx_queries):
    bsz, t_all, wq = q.shape
    seq = t_all - n_ctx
    n_ctx_blk = n_ctx // TQ
    q_blk_off = 0 if with_ctx_queries else n_ctx_blk
    nblk = t_all // TQ - q_blk_off
    gqa = kind in ("swa", "glb")
    wo = GQA_HEADS * HEAD_DIM if gqa else MLA_HEADS * MLA_V
    rows = GQA_G * TQ if gqa else TQ
    kern = functools.partial(_attn_kernel, kind=kind, n_ctx=n_ctx, n_ctx_blk=n_ctx_blk,
                             q_blk_off=q_blk_off, seq=seq)
    in_specs = [pl.BlockSpec((1, TQ, wq), lambda b, i: (b, i + q_blk_off, 0)),
                pl.BlockSpec((1, t_all, k.shape[2]), lambda b, i: (b, 0, 0)),
                pl.BlockSpec((1, t_all, v.shape[2]), lambda b, i: (b, 0, 0))]
    args = [q, k, v]
    if kind == "swa":
        in_specs = [pl.BlockSpec(memory_space=pltpu.SMEM)] + in_specs
        args = [sink] + args
    return pl.pallas_call(
        kern,
        out_shape=jax.ShapeDtypeStruct((bsz, nblk * TQ, wo), BF16),
        grid=(bsz, nblk),
        in_specs=in_specs,
        out_specs=pl.BlockSpec((1, TQ, wo), lambda b, i: (b, i, 0)),
        scratch_shapes=[pltpu.VMEM((rows, LANES), BF16), pltpu.VMEM((rows, 1), F32),
                        pltpu.VMEM((rows, 1), F32), pltpu.VMEM((rows, LANES), F32)],
        compiler_params=_cparams(2),
        name="attn_" + kind,
    )(*args)


def _row_select(rows, idx):
    out = rows[0]
    for j in range(1, len(rows)):
        out = jnp.where(idx == j, rows[j], out)
    return out


def _route(scores, biased):
    def top2sum(a, b, c, d):
        hi1, lo1, hi2, lo2 = jnp.maximum(a, b), jnp.minimum(a, b), jnp.maximum(c, d), jnp.minimum(c, d)
        return jnp.maximum(hi1, hi2) + jnp.maximum(jnp.minimum(hi1, hi2), jnp.maximum(lo1, lo2))

    gs = [top2sum(*biased[4 * g:4 * g + 4]) for g in range(N_GROUPS)]
    best, gi = gs[0], jnp.zeros(gs[0].shape, jnp.int32)
    for g in range(1, N_GROUPS):
        better = gs[g] > best
        gi = jnp.where(better, g, gi)
        best = jnp.where(better, gs[g], best)
    a = [_row_select([biased[4 * g + j] for g in range(N_GROUPS)], gi) for j in range(EXPERTS_PER_GROUP)]
    s = [_row_select([scores[4 * g + j] for g in range(N_GROUPS)], gi) for j in range(EXPERTS_PER_GROUP)]
    v1, i1 = a[0], jnp.zeros(gi.shape, jnp.int32)
    for j in range(1, EXPERTS_PER_GROUP):
        better = a[j] > v1
        i1 = jnp.where(better, j, i1)
        v1 = jnp.where(better, a[j], v1)
    v2, i2 = jnp.full(v1.shape, -3.0e38, F32), jnp.zeros(gi.shape, jnp.int32)
    for j in range(EXPERTS_PER_GROUP):
        better = (i1 != j) & (a[j] > v2)
        i2 = jnp.where(better, j, i2)
        v2 = jnp.where(better, a[j], v2)
    lo, hi = jnp.minimum(i1, i2), jnp.maximum(i1, i2)
    pair = jnp.where(lo == 0, hi - 1, jnp.where(lo == 1, hi + 1, 5))
    s_lo, s_hi = _row_select(s, lo), _row_select(s, hi)
    den = s_lo + s_hi
    return gi * N_PAIRS + pair, s_lo / den, s_hi / den


def _out_kernel(os_ref, og_ref, om_ref, x_ref, mod_ref, wout_ref, g_ref, rwh_ref, rwl_ref, rb_ref, tri_ref,
                xo_ref, row_ref, meta_ref, cnt_ref, carry_ref):
    @pl.when((pl.program_id(0) == 0) & (pl.program_id(1) == 0))
    def _():
        carry_ref[...] = jnp.zeros(carry_ref.shape, F32)

    y = jnp.concatenate([os_ref[0], og_ref[0], om_ref[0]], axis=1)
    mod = mod_ref[0, 0]
    x = x_ref[0] + mod[2:3] * jnp.dot(y, wout_ref[...], preferred_element_type=F32)
    xo_ref[0] = x
    h = _rms(x) * g_ref[...]
    h = h * (1.0 + mod[4:5]) + mod[3:4]

    hh = h.astype(BF16)
    hl = (h - hh.astype(F32)).astype(BF16)
    dn = (((1,), (1,)), ((), ()))
    logits = (lax.dot_general(rwh_ref[...], hh, dn, preferred_element_type=F32)
              + lax.dot_general(rwh_ref[...], hl, dn, preferred_element_type=F32)
              + lax.dot_general(rwl_ref[...], hh, dn, preferred_element_type=F32))
    sc = jax.nn.sigmoid(logits)
    bs = sc + rb_ref[...]
    scores = [sc[e:e + 1, :] for e in range(N_EXPERTS)]
    biased = [bs[e:e + 1, :] for e in range(N_EXPERTS)]
    bucket, g_lo, g_hi = _route(scores, biased)

    onehot = jnp.where(lax.broadcasted_iota(jnp.int32, (32, TQ), 0) == bucket, 1.0, 0.0)
    prefix = jnp.dot(onehot.astype(BF16), tri_ref[...], preferred_element_type=F32)
    carry = carry_ref[:, 0:1]
    rank = jnp.sum(onehot * (carry + prefix - 1.0), axis=0, keepdims=True)
    carry_new = jnp.broadcast_to(carry + prefix[:, TQ - 1:TQ], carry_ref.shape)
    carry_ref[...] = carry_new
    cnt_ref[...] = carry_new
    meta_ref[0] = jnp.concatenate([bucket, rank.astype(jnp.int32), jnp.zeros((6, TQ), jnp.int32)], axis=0)

    gates = jnp.concatenate([g_lo, g_hi, jnp.zeros((LANES - 2, TQ), F32)], axis=0)
    row_ref[...] = jnp.concatenate([h, gates.T], axis=1)


def _output_projection(o_swa, o_glb, o_mla, xs, modtab, lw, shared, n_ctx_blk, with_ctx):
    bsz, t_all, d = xs.shape
    off = 0 if with_ctx else n_ctx_blk
    nblk = t_all // TQ - off
    tok = lambda w: pl.BlockSpec((1, TQ, w), lambda b, i: (b, i, 0))
    full = lambda a: pl.BlockSpec(a.shape, lambda b, i: (0,) * a.ndim)
    consts = (lw["wout"], lw["g_ffn"], shared["rw_hi"], shared["rw_lo"], shared["rb"], shared["tri"])
    return pl.pallas_call(
        _out_kernel,
        out_shape=(jax.ShapeDtypeStruct((bsz, nblk * TQ, d), F32),
                   jax.ShapeDtypeStruct((bsz * nblk * TQ, ROW_W), F32),
                   jax.ShapeDtypeStruct((bsz * nblk, 8, TQ), jnp.int32),
                   jax.ShapeDtypeStruct((32, LANES), F32)),
        grid=(bsz, nblk),
        in_specs=[tok(o_swa.shape[2]), tok(o_glb.shape[2]), tok(o_mla.shape[2]),
                  pl.BlockSpec((1, TQ, d), lambda b, i: (b, i + off, 0)),
                  pl.BlockSpec((1, 1, 8, d), lambda b, i: (b, jnp.where(i + off < n_ctx_blk, 1, 0), 0, 0))]
                 + [full(a) for a in consts],
        out_specs=(tok(d),
                   pl.BlockSpec((TQ, ROW_W), lambda b, i: (b * nblk + i, 0)),
                   pl.BlockSpec((1, 8, TQ), lambda b, i: (b * nblk + i, 0, 0)),
                   pl.BlockSpec((32, LANES), lambda b, i: (0, 0))),
        scratch_shapes=[pltpu.VMEM((32, LANES), F32)],
        compiler_params=_cparams(2),
        name="out_proj_router",
    )(o_swa, o_glb, o_mla, xs, modtab, *consts)


def _scatter_kernel(dest_ref, row_ref, init_ref, out_ref, buf, sem, *, n_steps):
    del init_ref
    i = pl.program_id(0)
    slot = i % 2

    def wait_slot(s):
        pltpu.make_async_copy(buf.at[s], out_ref.at[pl.ds(0, TQ)], sem.at[s]).wait()

    @pl.when(i >= 2)
    def _():
        wait_slot(slot)

    buf[slot] = row_ref[...]

    def body(r, carry):
        d = dest_ref[i * TQ + r]
        pltpu.make_async_copy(buf.at[slot, pl.ds(r, 1)], out_ref.at[pl.ds(d, 1)], sem.at[slot]).start()
        return carry
    lax.fori_loop(0, TQ, body, 0)

    @pl.when(i == n_steps - 1)
    def _():
        wait_slot(slot)
        if n_steps >= 2:
            wait_slot(1 - slot)


def _scatter_rows(dest, rows, n_sorted):
    n_tok = rows.shape[0]
    n_steps = n_tok // TQ
    init = jnp.zeros((n_sorted, ROW_W), F32)
    return pl.pallas_call(
        functools.partial(_scatter_kernel, n_steps=n_steps),
        out_shape=jax.ShapeDtypeStruct((n_sorted, ROW_W), F32),
        grid_spec=pltpu.PrefetchScalarGridSpec(
            num_scalar_prefetch=1,
            grid=(n_steps,),
            in_specs=[pl.BlockSpec((TQ, ROW_W), lambda i, d: (i, 0)),
                      pl.BlockSpec(memory_space=pl.ANY)],
            out_specs=pl.BlockSpec(memory_space=pl.ANY),
            scratch_shapes=[pltpu.VMEM((2, TQ, ROW_W), F32), pltpu.SemaphoreType.DMA((2,))]),
        input_output_aliases={2: 0},
        compiler_params=_cparams(1),
        name="moe_scatter",
    )(dest, rows, init)


def _swiglu(h, wgu, wd):
    u = jnp.dot(h, wgu, preferred_element_type=F32)
    a = u[:, :D_EXPERT]
    a = a * jax.nn.sigmoid(a) * u[:, D_EXPERT:]
    return jnp.dot(a.astype(BF16), wd, preferred_element_type=F32)


def _moe_kernel(elo_ref, ehi_ref, nv_ref, row_ref, wgl_ref, wdl_ref, wgh_ref, wdh_ref, swg_ref, swd_ref, y_ref):
    del elo_ref, ehi_ref
    j = pl.program_id(0)

    @pl.when(j < nv_ref[0])
    def _():
        rows = row_ref[...]
        h = rows[:, :D_MODEL].astype(BF16)
        g_lo = rows[:, D_MODEL:D_MODEL + 1]
        g_hi = rows[:, D_MODEL + 1:D_MODEL + 2]
        y_ref[...] = (g_lo * _swiglu(h, wgl_ref[0], wdl_ref[0]) + g_hi * _swiglu(h, wgh_ref[0], wdh_ref[0])
                      + _swiglu(h, swg_ref[...], swd_ref[...]))

    @pl.when(j >= nv_ref[0])
    def _():
        y_ref[...] = jnp.zeros(y_ref.shape, F32)


def _grouped_experts(e_lo, e_hi, n_valid, rows_sorted, lw):
    n_sorted = rows_sorted.shape[0]
    nb = n_sorted // TM
    wgu, wd, swgu, swd = lw["wgu"], lw["wd"], lw["swgu"], lw["swd"]
    return pl.pallas_call(
        _moe_kernel,
        out_shape=jax.ShapeDtypeStruct((n_sorted, D_MODEL), F32),
        grid_spec=pltpu.PrefetchScalarGridSpec(
            num_scalar_prefetch=3,
            grid=(nb,),
            in_specs=[pl.BlockSpec((TM, ROW_W), lambda j, lo, hi, nv: (j, 0)),
                      pl.BlockSpec((1,) + wgu.shape[1:], lambda j, lo, hi, nv: (lo[j], 0, 0)),
                      pl.BlockSpec((1,) + wd.shape[1:], lambda j, lo, hi, nv: (lo[j], 0, 0)),
                      pl.BlockSpec((1,) + wgu.shape[1:], lambda j, lo, hi, nv: (hi[j], 0, 0)),
                      pl.BlockSpec((1,) + wd.shape[1:], lambda j, lo, hi, nv: (hi[j], 0, 0)),
                      pl.BlockSpec(swgu.shape, lambda j, lo, hi, nv: (0, 0)),
                      pl.BlockSpec(swd.shape, lambda j, lo, hi, nv: (0, 0))],
            out_specs=pl.BlockSpec((TM, D_MODEL), lambda j, lo, hi, nv: (j, 0))),
        compiler_params=_cparams(1),
        name="moe_experts",
    )(e_lo, e_hi, n_valid, rows_sorted, wgu, wd, wgu, wd, swgu, swd)


def _gather_kernel(dest_ref, y_ref, x_ref, mod_ref, gf_ref, o_ref, fbuf, sem, *, n_steps, final_norm):
    i = pl.program_id(0)
    slot = i % 2

    def issue(step, s):
        def body(r, carry):
            d = dest_ref[step * TQ + r]
            pltpu.make_async_copy(y_ref.at[pl.ds(d, 1)], fbuf.at[s, pl.ds(r, 1)], sem.at[s]).start()
            return carry
        lax.fori_loop(0, TQ, body, 0)

    @pl.when(i == 0)
    def _():
        issue(0, 0)

    @pl.when(i + 1 < n_steps)
    def _():
        issue(i + 1, 1 - slot)

    pltpu.make_async_copy(y_ref.at[pl.ds(0, TQ)], fbuf.at[slot], sem.at[slot]).wait()
    x = x_ref[0] + mod_ref[0, 0][5:6] * fbuf[slot]
    if final_norm:
        x = _rms(x) * gf_ref[...]
    o_ref[0] = x


def _gather_residual(dest, y_sorted, x_mid, modtab, g_final, n_ctx_blk, with_ctx, final_norm):
    bsz, t_rows, d = x_mid.shape
    nblk = t_rows // TQ
    off = 0 if with_ctx else n_ctx_blk
    n_steps = bsz * nblk
    return pl.pallas_call(
        functools.partial(_gather_kernel, n_steps=n_steps, final_norm=final_norm),
        out_shape=jax.ShapeDtypeStruct((bsz, t_rows, d), F32),
        grid_spec=pltpu.PrefetchScalarGridSpec(
            num_scalar_prefetch=1,
            grid=(n_steps,),
            in_specs=[pl.BlockSpec(memory_space=pl.ANY),
                      pl.BlockSpec((1, TQ, d), lambda i, dst: (i // nblk, i % nblk, 0)),
                      pl.BlockSpec((1, 1, 8, d),
                                   lambda i, dst: (i // nblk, jnp.where(i % nblk + off < n_ctx_blk, 1, 0), 0, 0)),
                      pl.BlockSpec(g_final.shape, lambda i, dst: (0, 0))],
            out_specs=pl.BlockSpec((1, TQ, d), lambda i, dst: (i // nblk, i % nblk, 0)),
            scratch_shapes=[pltpu.VMEM((2, TQ, d), F32), pltpu.SemaphoreType.DMA((2,))]),
        compiler_params=_cparams(1),
        name="moe_gather",
    )(dest, y_sorted, x_mid, modtab, g_final)


def _bucket_layout(meta, counts, n_tok):
    bucket = meta[:, 0, :].reshape(-1)
    rank = meta[:, 1, :].reshape(-1)
    cnt = counts[:N_BUCKETS, 0].astype(jnp.int32)
    padded = (cnt + TM - 1) // TM * TM
    pad_end = jnp.cumsum(padded)
    pad_start = pad_end - padded
    dest = pad_start[bucket] + rank
    nb = n_tok // TM + N_BUCKETS
    n_valid = pad_end[-1] // TM
    blk = jnp.arange(nb, dtype=jnp.int32)
    blk_bucket = jnp.minimum(jnp.searchsorted(pad_end, jnp.minimum(blk, n_valid - 1) * TM, side="right"),
                             N_BUCKETS - 1).astype(jnp.int32)
    grp, pair = blk_bucket // N_PAIRS, blk_bucket % N_PAIRS
    e_lo = grp * EXPERTS_PER_GROUP + jnp.asarray(_PAIR_LO, jnp.int32)[pair]
    e_hi = grp * EXPERTS_PER_GROUP + jnp.asarray(_PAIR_HI, jnp.int32)[pair]
    return dest.astype(jnp.int32), e_lo, e_hi, n_valid.reshape(1).astype(jnp.int32), nb * TM


def _rope_tables(seq, n_ctx):
    t = jnp.arange(seq)
    rows, cols = t // GRID_W, t % GRID_W

    def table(width, nf):
        lane = np.arange(width)
        half, j = lane // (2 * nf), lane % (2 * nf)
        inv = ROPE_THETA ** (-jnp.arange(nf, dtype=F32) / nf)
        pos = jnp.where(jnp.asarray(half == 0)[None, :], rows[:, None], cols[:, None]).astype(F32)
        ang = pos * inv[j % nf][None, :]
        sign = jnp.asarray(np.where(j < nf, -1.0, 1.0), F32)[None, :]
        return jnp.cos(ang), jnp.sin(ang) * sign

    cos64, sin64 = table(HEAD_DIM, 16)
    cos64, sin64 = jnp.tile(cos64, (1, 2)), jnp.tile(sin64, (1, 2))
    cosr, sinr = table(MLA_ROPE, 8)
    cosm = jnp.concatenate([jnp.ones((seq, MLA_NOPE), F32), cosr, jnp.ones((seq, 32), F32)], axis=1)
    sinm = jnp.concatenate([jnp.zeros((seq, MLA_NOPE), F32), sinr, jnp.zeros((seq, 32), F32)], axis=1)
    ctx1, ctx0 = jnp.ones((n_ctx, LANES), F32), jnp.zeros((n_ctx, LANES), F32)
    return tuple(jnp.concatenate([c, a], axis=0) for c, a in ((ctx1, cos64), (ctx0, sin64), (ctx1, cosm), (ctx0, sinm)))


def _layer_weights(l, w_in, w_out, norm_ffn_g, glb_q_gain, glb_k_gain, mla_q_gain, mla_w_uq, mla_kv_gain,
                   mla_w_ukv, exp_w_gate, exp_w_up, exp_w_down, shr_w_gate, shr_w_up, shr_w_down):
    d = w_in.shape[1]
    wi = w_in[l]
    kr = wi[:, 1664:1696]
    win = jnp.concatenate([wi[:, :1664], jnp.zeros((d, MLA_NOPE), F32), kr, jnp.zeros((d, 32), F32)], axis=1)
    uq = mla_w_uq[l].reshape(MLA_Q_RANK, MLA_HEADS, MLA_NOPE + MLA_ROPE)
    wuq = jnp.concatenate([uq, jnp.zeros((MLA_Q_RANK, MLA_HEADS, 32), F32)], axis=2).reshape(MLA_Q_RANK, 512)
    ukv = mla_w_ukv[l].reshape(MLA_KV_RANK, MLA_HEADS, MLA_NOPE + MLA_V)
    wk = jnp.concatenate([ukv[:, :, :MLA_NOPE], jnp.zeros((MLA_KV_RANK, MLA_HEADS, 64), F32)], axis=2)
    wukv = jnp.concatenate([wk.reshape(MLA_KV_RANK, 512), ukv[:, :, MLA_NOPE:].reshape(MLA_KV_RANK, 256)], axis=1)
    gain512 = jnp.concatenate([jnp.tile(glb_q_gain[l] * (HEAD_DIM ** -0.5), GQA_HEADS),
                               jnp.tile(glb_k_gain[l], GQA_KV)]).reshape(1, 512)
    head = np.arange(512) // HEAD_DIM
    return {
        "win": win.astype(BF16),
        "wuq": wuq.astype(BF16),
        "wukv": wukv.astype(BF16),
        "gain512": gain512,
        "mqg": mla_q_gain[l].reshape(1, -1),
        "mkvg": mla_kv_gain[l].reshape(1, -1),
        "bd": jnp.asarray(head[:, None] == head[None, :], BF16),
        "wout": w_out[l].astype(BF16),
        "g_ffn": norm_ffn_g[l].reshape(1, -1),
        "wgu": jnp.concatenate([exp_w_gate[l], exp_w_up[l]], axis=2).astype(BF16),
        "wd": exp_w_down[l].astype(BF16),
        "swgu": jnp.concatenate([shr_w_gate[l], shr_w_up[l]], axis=1).astype(BF16),
        "swd": shr_w_down[l].astype(BF16),
    }


def kernel(x, c, ctx, c_ctx, w_mod, b_mod, norm_mix_g, norm_ffn_g, w_in, w_out, swa_sink, glb_q_gain, glb_k_gain,
           mla_q_gain, mla_w_uq, mla_kv_gain, mla_w_ukv, router_w, router_bias, exp_w_gate, exp_w_up, exp_w_down,
           shr_w_gate, shr_w_up, shr_w_down, final_norm_g):
    bsz, seq, d = x.shape
    n_ctx = ctx.shape[1]
    n_layers = w_mod.shape[0]
    assert d == D_MODEL and n_ctx % TQ == 0 and seq % TQ == 0 and seq >= TQ + 2 * WINDOW and seq % GRID_W == 0
    n_ctx_blk = n_ctx // TQ

    rows = -(-(bsz + 1) // 8) * 8
    c_rows = jnp.concatenate([c, c_ctx[None, :], jnp.zeros((rows - bsz - 1, d), F32)], axis=0)
    mods = _modulation(c_rows, w_mod, b_mod)
    mod_x = mods[:, :bsz].reshape(n_layers, bsz, 6, d)
    mod_c = jnp.broadcast_to(mods[:, bsz].reshape(n_layers, 1, 6, d), (n_layers, bsz, 6, d))
    modtabs = jnp.pad(jnp.stack([mod_x, mod_c], axis=2), ((0, 0), (0, 0), (0, 0), (0, 2), (0, 0)))

    tabs = _rope_tables(seq, n_ctx)
    rw = jnp.pad(router_w.T, ((0, 32 - N_EXPERTS), (0, 0)))
    rw_hi = rw.astype(BF16)
    shared = {
        "rw_hi": rw_hi,
        "rw_lo": (rw - rw_hi.astype(F32)).astype(BF16),
        "rb": jnp.pad(router_bias, (0, 32 - N_EXPERTS)).reshape(32, 1),
        "tri": jnp.asarray(np.arange(TQ)[:, None] <= np.arange(TQ)[None, :], BF16),
    }
    g_final = final_norm_g.reshape(1, d)

    xs = jnp.concatenate([ctx, x], axis=1)
    for l in range(n_layers):
        last = l == n_layers - 1
        with_ctx = not last
        lw = _layer_weights(l, w_in, w_out, norm_ffn_g, glb_q_gain, glb_k_gain, mla_q_gain, mla_w_uq,
                            mla_kv_gain, mla_w_ukv, exp_w_gate, exp_w_up, exp_w_down,
                            shr_w_gate, shr_w_up, shr_w_down)
        modtab = modtabs[l]
        qs, ks, vs, qg, kg, vg, qm, km, vm = _input_projection(
            xs, modtab, norm_mix_g[l].reshape(1, d), lw, tabs, n_ctx_blk)
        sink = jnp.pad(swa_sink[l], (0, 8 - GQA_HEADS))
        o_swa = _attention("swa", qs, ks, vs, sink, n_ctx, with_ctx)
        o_glb = _attention("glb", qg, kg, vg, None, n_ctx, with_ctx)
        o_mla = _attention("mla", qm, km, vm, None, n_ctx, with_ctx)
        x_mid, rows_tok, meta, counts = _output_projection(o_swa, o_glb, o_mla, xs, modtab, lw, shared,
                                                           n_ctx_blk, with_ctx)
        n_tok = rows_tok.shape[0]
        dest, e_lo, e_hi, n_valid, n_sorted = _bucket_layout(meta, counts, n_tok)
        rows_sorted = _scatter_rows(dest, rows_tok, n_sorted)
        y_sorted = _grouped_experts(e_lo, e_hi, n_valid, rows_sorted, lw)
        xs = _gather_residual(dest, y_sorted, x_mid, modtab, g_final, n_ctx_blk, with_ctx, final_norm=last)
    return xs
```

```python
import functools

import jax
import jax.numpy as jnp
import numpy as np
from jax import lax
from jax.experimental import pallas as pl
from jax.experimental.pallas import tpu as pltpu

F32 = jnp.float32
BF16 = jnp.bfloat16

D_MODEL = 1024
GRID_W = 64
HEAD_DIM = 64
GQA_HEADS = 6
GQA_KV = 2
GQA_G = GQA_HEADS // GQA_KV
WINDOW = 128
MLA_HEADS = 4
MLA_NOPE = 64
MLA_ROPE = 32
MLA_V = 64
MLA_Q_RANK = 256
MLA_KV_RANK = 128
ROPE_THETA = 10000.0
N_EXPERTS = 16
N_GROUPS = 4
EXPERTS_PER_GROUP = 4
N_PAIRS = 6
N_BUCKETS = N_GROUPS * N_PAIRS
D_EXPERT = 512
EPS = 1e-6
NEG_INF = -1e30

LANES = 128
TQ = 256
TK = 256
TM = 256
SWA_SLAB = 128
V_ROWS = 80
LOG2E = 1.4426950408889634
ROW_W = D_MODEL + LANES
IN_W = 1792
VMEM_LIMIT = 56 * 1024 * 1024

_SQ, _SK, _SV, _GQ, _GK, _GV, _MQ, _MKV, _KR = 0, 384, 512, 640, 1024, 1152, 1280, 1536, 1664

_PAIR_LO = (0, 0, 0, 1, 1, 2)
_PAIR_HI = (1, 2, 3, 2, 3, 3)


def _cparams(n_axes):
    return pltpu.CompilerParams(dimension_semantics=("arbitrary",) * n_axes,
                                vmem_limit_bytes=VMEM_LIMIT)


def _mod_kernel(c_ref, w_ref, b_ref, o_ref):
    c = c_ref[...]
    a = (c * jax.nn.sigmoid(c)).astype(BF16)
    o_ref[0] = jnp.dot(a, w_ref[0].astype(BF16), preferred_element_type=F32) + b_ref[0]


def _modulation(c_rows, w_mod, b_mod):
    n_layers, d, width = w_mod.shape
    rows = c_rows.shape[0]
    nb = 1536
    return pl.pallas_call(
        _mod_kernel,
        out_shape=jax.ShapeDtypeStruct((n_layers, rows, width), F32),
        grid=(n_layers, width // nb),
        in_specs=[pl.BlockSpec((rows, d), lambda l, j: (0, 0)),
                  pl.BlockSpec((1, d, nb), lambda l, j: (l, 0, j)),
                  pl.BlockSpec((1, 1, nb), lambda l, j: (l, 0, j))],
        out_specs=pl.BlockSpec((1, rows, nb), lambda l, j: (l, 0, j)),
        compiler_params=_cparams(2),
        name="adaln_mod",
    )(c_rows, w_mod, b_mod.reshape(n_layers, 1, width))


def _rope(x, cos, sin_signed, nf):
    lane = lax.broadcasted_iota(jnp.int32, (1, LANES), 1)
    first = (lane % (2 * nf)) < nf
    tiles = []
    for t in range(x.shape[1] // LANES):
        xt = x[:, t * LANES:(t + 1) * LANES]
        partner = jnp.where(first, pltpu.roll(xt, LANES - nf, 1), pltpu.roll(xt, nf, 1))
        tiles.append(xt * cos + partner * sin_signed)
    return tiles[0] if len(tiles) == 1 else jnp.concatenate(tiles, axis=1)


def _group_padded(q):
    lane = lax.broadcasted_iota(jnp.int32, (1, LANES), 1)
    tiles = []
    for h in range(GQA_HEADS):
        g = h // GQA_G
        tile = q[:, (h // 2) * LANES:(h // 2 + 1) * LANES]
        if (h % 2) != g:
            tile = pltpu.roll(tile, HEAD_DIM, 1)
        tiles.append(jnp.where((lane // HEAD_DIM) == g, tile, 0.0))
    return jnp.concatenate(tiles, axis=1)


def _emit_q_t(ref, q):
    for h in range(q.shape[1] // LANES):
        ref[0, h] = q[:, h * LANES:(h + 1) * LANES].T.astype(ref.dtype)


def _emit_v_t(ref, v, slab):
    ones_blk = jnp.where(lax.broadcasted_iota(jnp.int32, (V_ROWS - HEAD_DIM, TQ), 0) == 0, 1.0, 0.0)
    parts = []
    for t in range(v.shape[1] // LANES):
        vt = v[:, t * LANES:(t + 1) * LANES].T
        parts += [vt[:HEAD_DIM], ones_blk, vt[HEAD_DIM:], ones_blk]
    ext = jnp.concatenate(parts, axis=0).astype(ref.dtype)
    for s in range(TQ // slab):
        ref[0, s] = ext[:, s * slab:(s + 1) * slab]


def _rms(x, eps=EPS):
    return x * lax.rsqrt(jnp.mean(x * x, axis=-1, keepdims=True) + eps)


def _in_kernel(x_ref, mod_ref, g_ref, win_ref, cos64_ref, sin64_ref, cosm_ref, sinm_ref,
               gain_ref, mqg_ref, mkvg_ref, wuq_ref, wukv_ref, bd_ref,
               qs_ref, ks_ref, vs_ref, qg_ref, kg_ref, vg_ref, qm_ref, km_ref, vm_ref, *, mla_scale):
    x = x_ref[0]
    mod = mod_ref[0, 0]
    h = _rms(x) * g_ref[...]
    h = h * (1.0 + mod[1:2]) + mod[0:1]
    z = jnp.dot(h.astype(BF16), win_ref[...], preferred_element_type=F32)
    cos64, sin64 = cos64_ref[...], sin64_ref[...]
    cosm, sinm = cosm_ref[...], sinm_ref[...]

    sqk = _rope(z[:, _SQ:_SV], cos64, sin64, 16)
    _emit_q_t(qs_ref, _group_padded(sqk[:, :384] * (HEAD_DIM ** -0.5 * LOG2E)))
    ks_ref[0] = sqk[:, 384:].astype(BF16)
    _emit_v_t(vs_ref, z[:, _SV:_GQ], SWA_SLAB)

    gqk = z[:, _GQ:_GV]
    sq = gqk * gqk
    hi = sq.astype(BF16)
    lo = (sq - hi.astype(F32)).astype(BF16)
    ssum = (jnp.dot(hi, bd_ref[...], preferred_element_type=F32)
            + jnp.dot(lo, bd_ref[...], preferred_element_type=F32))
    gqk = gqk * lax.rsqrt(ssum * (1.0 / HEAD_DIM) + EPS) * gain_ref[...]
    gqk = _rope(gqk, cos64, sin64, 16)
    _emit_q_t(qg_ref, _group_padded(gqk[:, :384]))
    kg_ref[0] = gqk[:, 384:].astype(BF16)
    _emit_v_t(vg_ref, z[:, _GV:_MQ], TK)

    qn = _rms(z[:, _MQ:_MKV]) * mqg_ref[...]
    mq = jnp.dot(qn.astype(BF16), wuq_ref[...], preferred_element_type=F32)
    _emit_q_t(qm_ref, _rope(mq, cosm, sinm, 8) * (mla_scale * LOG2E))
    kvn = _rms(z[:, _MKV:_KR]) * mkvg_ref[...]
    mkv = jnp.dot(kvn.astype(BF16), wukv_ref[...], preferred_element_type=F32)
    kr = _rope(z[:, _KR:IN_W], cosm, sinm, 8)
    km_ref[0] = (mkv[:, :512] + jnp.concatenate([kr] * MLA_HEADS, axis=1)).astype(BF16)
    _emit_v_t(vm_ref, mkv[:, 512:], TK)


def _input_projection(xs, modtab, g_mix, lw, tabs, n_ctx_blk):
    bsz, t_all, d = xs.shape
    nblk = t_all // TQ
    tok = lambda w: pl.BlockSpec((1, TQ, w), lambda b, i: (b, i, 0))
    q_t = lambda nh: pl.BlockSpec((1, nh, LANES, TQ), lambda b, i: (b, 0, 0, i))
    v_t = lambda nh, slab: pl.BlockSpec((1, TQ // slab, nh * V_ROWS, slab), lambda b, i: (b, i, 0, 0))
    full = lambda a: pl.BlockSpec(a.shape, lambda b, i: (0,) * a.ndim)
    tab = pl.BlockSpec((TQ, LANES), lambda b, i: (i, 0))
    q_shape = lambda nh: jax.ShapeDtypeStruct((bsz, nh, LANES, t_all), BF16)
    k_shape = lambda w: jax.ShapeDtypeStruct((bsz, t_all, w), BF16)
    v_shape = lambda nh, slab: jax.ShapeDtypeStruct((bsz, t_all // slab, nh * V_ROWS, slab), BF16)
    consts = (lw["gain512"], lw["mqg"], lw["mkvg"], lw["wuq"], lw["wukv"], lw["bd"])
    return pl.pallas_call(
        functools.partial(_in_kernel, mla_scale=float((MLA_NOPE + MLA_ROPE) ** -0.5)),
        out_shape=(q_shape(GQA_HEADS), k_shape(128), v_shape(GQA_KV, SWA_SLAB),
                   q_shape(GQA_HEADS), k_shape(128), v_shape(GQA_KV, TK),
                   q_shape(MLA_HEADS), k_shape(512), v_shape(MLA_HEADS, TK)),
        grid=(bsz, nblk),
        in_specs=[tok(d),
                  pl.BlockSpec((1, 1, 8, d), lambda b, i: (b, jnp.where(i < n_ctx_blk, 1, 0), 0, 0)),
                  full(g_mix), full(lw["win"]), tab, tab, tab, tab] + [full(a) for a in consts],
        out_specs=(q_t(GQA_HEADS), tok(128), v_t(GQA_KV, SWA_SLAB),
                   q_t(GQA_HEADS), tok(128), v_t(GQA_KV, TK),
                   q_t(MLA_HEADS), tok(512), v_t(MLA_HEADS, TK)),
        compiler_params=_cparams(2),
        name="in_proj",
    )(xs, modtab, g_mix, lw["win"], *tabs, *consts)


def _attn_kernel(*refs, kind, n_ctx, n_ctx_blk, q_blk_off, seq):
    if kind == "swa":
        sink_ref, q_ref, k_ref, v_ref, o_ref, m_ref, acc_ref = refs
    else:
        q_ref, k_ref, v_ref, o_ref, m_ref, acc_ref = refs
    blk = pl.program_id(1) + q_blk_off
    is_lat = blk >= n_ctx_blk
    gqa = kind in ("swa", "glb")
    n_heads = GQA_HEADS if gqa else MLA_HEADS
    n_units = GQA_KV if gqa else MLA_HEADS
    unit_w = n_heads // n_units * TQ

    def update(k_rows, v_t, mask):
        if gqa:
            q_all = jnp.concatenate([q_ref[0, h] for h in range(n_heads)], axis=1)
            s = jnp.dot(k_rows, q_all, preferred_element_type=F32)
        else:
            s = jnp.concatenate([jnp.dot(k_rows[:, h * LANES:(h + 1) * LANES], q_ref[0, h],
                                         preferred_element_type=F32) for h in range(n_heads)], axis=1)
        if mask is not None:
            s = jnp.where(mask, s, NEG_INF)
        m_prev = m_ref[...]
        m_new = jnp.maximum(m_prev, jnp.max(s, axis=0, keepdims=True))
        alpha = jnp.exp2(m_prev - m_new)
        p = jnp.exp2(s - m_new).astype(BF16)
        m_ref[...] = m_new
        for u in range(n_units):
            lanes = slice(u * unit_w, (u + 1) * unit_w)
            acc_ref[u] = (acc_ref[u] * alpha[:, lanes]
                          + jnp.dot(v_t[u * V_ROWS:(u + 1) * V_ROWS], p[:, lanes], preferred_element_type=F32))

    if kind == "swa":
        m_ref[...] = jnp.concatenate([jnp.full((1, TQ), sink_ref[h], F32) for h in range(n_heads)], axis=1)
        l_row = jnp.where(lax.broadcasted_iota(jnp.int32, (V_ROWS, unit_w), 0) == HEAD_DIM, 1.0, 0.0)
        for u in range(n_units):
            acc_ref[u] = l_row
        n_slab = n_ctx // SWA_SLAB
        update(k_ref[0, 0:n_ctx, :], jnp.concatenate([v_ref[0, s] for s in range(n_slab)], axis=1), None)

        @pl.when(is_lat)
        def _():
            nk = TQ + 2 * WINDOW
            q0 = (blk - n_ctx_blk) * TQ
            k0 = pl.multiple_of(jnp.clip(q0 - WINDOW, 0, seq - nk), WINDOW)
            kpos = k0 + lax.broadcasted_iota(jnp.int32, (nk, n_heads * TQ), 0)
            qpos = q0 + (lax.broadcasted_iota(jnp.int32, (nk, n_heads * TQ), 1) & (TQ - 1))
            s0 = (n_ctx + k0) // SWA_SLAB
            v_t = jnp.concatenate([v_ref[0, s0 + s] for s in range(nk // SWA_SLAB)], axis=1)
            update(k_ref[0, pl.ds(pl.multiple_of(n_ctx + k0, WINDOW), nk), :], v_t,
                   jnp.abs(qpos - kpos) <= WINDOW)
    else:
        m_ref[...] = jnp.full(m_ref.shape, NEG_INF, F32)
        acc_ref[...] = jnp.zeros(acc_ref.shape, F32)

        def body(c, carry):
            update(k_ref[0, pl.ds(pl.multiple_of(c * TK, TK), TK), :], v_ref[0, c], None)
            return carry
        lax.fori_loop(0, jnp.where(is_lat, (n_ctx + seq) // TK, n_ctx // TK), body, 0)

    heads = []
    for u in range(n_units):
        a = acc_ref[u]
        o = a[:HEAD_DIM] / a[HEAD_DIM:HEAD_DIM + 1]
        heads += [o[:, j * TQ:(j + 1) * TQ] for j in range(unit_w // TQ)]
    tiles = [jnp.concatenate(heads[2 * t:2 * t + 2], axis=0).T for t in range(n_heads // 2)]
    o_ref[0] = jnp.concatenate(tiles, axis=1).astype(o_ref.dtype)


def _attention(kind, q_t, k, v_t, sink, n_ctx, with_ctx_queries):
    bsz, n_heads, _, t_all = q_t.shape
    seq = t_all - n_ctx
    n_ctx_blk = n_ctx // TQ
    q_blk_off = 0 if with_ctx_queries else n_ctx_blk
    nblk = t_all // TQ - q_blk_off
    n_units = GQA_KV if kind in ("swa", "glb") else MLA_HEADS
    wo = n_heads * HEAD_DIM
    kern = functools.partial(_attn_kernel, kind=kind, n_ctx=n_ctx, n_ctx_blk=n_ctx_blk,
                             q_blk_off=q_blk_off, seq=seq)
    in_specs = [pl.BlockSpec((1, n_heads, LANES, TQ), lambda b, i: (b, 0, 0, i + q_blk_off)),
                pl.BlockSpec((1,) + k.shape[1:], lambda b, i: (b, 0, 0)),
                pl.BlockSpec((1,) + v_t.shape[1:], lambda b, i: (b, 0, 0, 0))]
    args = [q_t, k, v_t]
    if kind == "swa":
        in_specs = [pl.BlockSpec(memory_space=pltpu.SMEM)] + in_specs
        args = [sink] + args
    return pl.pallas_call(
        kern,
        out_shape=jax.ShapeDtypeStruct((bsz, nblk * TQ, wo), BF16),
        grid=(bsz, nblk),
        in_specs=in_specs,
        out_specs=pl.BlockSpec((1, TQ, wo), lambda b, i: (b, i, 0)),
        scratch_shapes=[pltpu.VMEM((1, n_heads * TQ), F32),
                        pltpu.VMEM((n_units, V_ROWS, n_heads // n_units * TQ), F32)],
        compiler_params=_cparams(2),
        name="attn_" + kind,
    )(*args)


def _row_select(rows, idx):
    out = rows[0]
    for j in range(1, len(rows)):
        out = jnp.where(idx == j, rows[j], out)
    return out


def _route(scores, biased):
    def top2sum(a, b, c, d):
        hi1, lo1, hi2, lo2 = jnp.maximum(a, b), jnp.minimum(a, b), jnp.maximum(c, d), jnp.minimum(c, d)
        return jnp.maximum(hi1, hi2) + jnp.maximum(jnp.minimum(hi1, hi2), jnp.maximum(lo1, lo2))

    gs = [top2sum(*biased[4 * g:4 * g + 4]) for g in range(N_GROUPS)]
    best, gi = gs[0], jnp.zeros(gs[0].shape, jnp.int32)
    for g in range(1, N_GROUPS):
        better = gs[g] > best
        gi = jnp.where(better, g, gi)
        best = jnp.where(better, gs[g], best)
    a = [_row_select([biased[4 * g + j] for g in range(N_GROUPS)], gi) for j in range(EXPERTS_PER_GROUP)]
    s = [_row_select([scores[4 * g + j] for g in range(N_GROUPS)], gi) for j in range(EXPERTS_PER_GROUP)]
    v1, i1 = a[0], jnp.zeros(gi.shape, jnp.int32)
    for j in range(1, EXPERTS_PER_GROUP):
        better = a[j] > v1
        i1 = jnp.where(better, j, i1)
        v1 = jnp.where(better, a[j], v1)
    v2, i2 = jnp.full(v1.shape, -3.0e38, F32), jnp.zeros(gi.shape, jnp.int32)
    for j in range(EXPERTS_PER_GROUP):
        better = (i1 != j) & (a[j] > v2)
        i2 = jnp.where(better, j, i2)
        v2 = jnp.where(better, a[j], v2)
    lo, hi = jnp.minimum(i1, i2), jnp.maximum(i1, i2)
    pair = jnp.where(lo == 0, hi - 1, jnp.where(lo == 1, hi + 1, 5))
    s_lo, s_hi = _row_select(s, lo), _row_select(s, hi)
    den = s_lo + s_hi
    return gi * N_PAIRS + pair, s_lo / den, s_hi / den


def _out_kernel(os_ref, og_ref, om_ref, x_ref, mod_ref, wout_ref, g_ref, rwh_ref, rwl_ref, rb_ref, tri_ref,
                xo_ref, row_ref, meta_ref, cnt_ref, carry_ref):
    @pl.when((pl.program_id(0) == 0) & (pl.program_id(1) == 0))
    def _():
        carry_ref[...] = jnp.zeros(carry_ref.shape, F32)

    y = jnp.concatenate([os_ref[0], og_ref[0], om_ref[0]], axis=1)
    mod = mod_ref[0, 0]
    x = x_ref[0] + mod[2:3] * jnp.dot(y, wout_ref[...], preferred_element_type=F32)
    xo_ref[0] = x
    h = _rms(x) * g_ref[...]
    h = h * (1.0 + mod[4:5]) + mod[3:4]

    hh = h.astype(BF16)
    hl = (h - hh.astype(F32)).astype(BF16)
    dn = (((1,), (1,)), ((), ()))
    logits = (lax.dot_general(rwh_ref[...], hh, dn, preferred_element_type=F32)
              + lax.dot_general(rwh_ref[...], hl, dn, preferred_element_type=F32)
              + lax.dot_general(rwl_ref[...], hh, dn, preferred_element_type=F32))
    sc = jax.nn.sigmoid(logits)
    bs = sc + rb_ref[...]
    scores = [sc[e:e + 1, :] for e in range(N_EXPERTS)]
    biased = [bs[e:e + 1, :] for e in range(N_EXPERTS)]
    bucket, g_lo, g_hi = _route(scores, biased)

    onehot = jnp.where(lax.broadcasted_iota(jnp.int32, (32, TQ), 0) == bucket, 1.0, 0.0)
    prefix = jnp.dot(onehot.astype(BF16), tri_ref[...], preferred_element_type=F32)
    carry = carry_ref[:, 0:1]
    rank = jnp.sum(onehot * (carry + prefix - 1.0), axis=0, keepdims=True)
    carry_new = jnp.broadcast_to(carry + prefix[:, TQ - 1:TQ], carry_ref.shape)
    carry_ref[...] = carry_new
    cnt_ref[...] = carry_new
    meta_ref[0] = jnp.concatenate([bucket, rank.astype(jnp.int32), jnp.zeros((6, TQ), jnp.int32)], axis=0)

    gates = jnp.concatenate([g_lo, g_hi, jnp.zeros((LANES - 2, TQ), F32)], axis=0)
    row_ref[...] = jnp.concatenate([h, gates.T], axis=1)


def _output_projection(o_swa, o_glb, o_mla, xs, modtab, lw, shared, n_ctx_blk, with_ctx):
    bsz, t_all, d = xs.shape
    off = 0 if with_ctx else n_ctx_blk
    nblk = t_all // TQ - off
    tok = lambda w: pl.BlockSpec((1, TQ, w), lambda b, i: (b, i, 0))
    full = lambda a: pl.BlockSpec(a.shape, lambda b, i: (0,) * a.ndim)
    consts = (lw["wout"], lw["g_ffn"], shared["rw_hi"], shared["rw_lo"], shared["rb"], shared["tri"])
    return pl.pallas_call(
        _out_kernel,
        out_shape=(jax.ShapeDtypeStruct((bsz, nblk * TQ, d), F32),
                   jax.ShapeDtypeStruct((bsz * nblk * TQ, ROW_W), F32),
                   jax.ShapeDtypeStruct((bsz * nblk, 8, TQ), jnp.int32),
                   jax.ShapeDtypeStruct((32, LANES), F32)),
        grid=(bsz, nblk),
        in_specs=[tok(o_swa.shape[2]), tok(o_glb.shape[2]), tok(o_mla.shape[2]),
                  pl.BlockSpec((1, TQ, d), lambda b, i: (b, i + off, 0)),
                  pl.BlockSpec((1, 1, 8, d), lambda b, i: (b, jnp.where(i + off < n_ctx_blk, 1, 0), 0, 0))]
                 + [full(a) for a in consts],
        out_specs=(tok(d),
                   pl.BlockSpec((TQ, ROW_W), lambda b, i: (b * nblk + i, 0)),
                   pl.BlockSpec((1, 8, TQ), lambda b, i: (b * nblk + i, 0, 0)),
                   pl.BlockSpec((32, LANES), lambda b, i: (0, 0))),
        scratch_shapes=[pltpu.VMEM((32, LANES), F32)],
        compiler_params=_cparams(2),
        name="out_proj_router",
    )(o_swa, o_glb, o_mla, xs, modtab, *consts)


def _scatter_kernel(dest_ref, row_ref, init_ref, out_ref, buf, sem, *, n_steps):
    del init_ref
    i = pl.program_id(0)
    slot = i % 2

    def wait_slot(s):
        pltpu.make_async_copy(buf.at[s], out_ref.at[pl.ds(0, TQ)], sem.at[s]).wait()

    @pl.when(i >= 2)
    def _():
        wait_slot(slot)

    buf[slot] = row_ref[...]

    def body(r, carry):
        d = dest_ref[i * TQ + r]
        pltpu.make_async_copy(buf.at[slot, pl.ds(r, 1)], out_ref.at[pl.ds(d, 1)], sem.at[slot]).start()
        return carry
    lax.fori_loop(0, TQ, body, 0)

    @pl.when(i == n_steps - 1)
    def _():
        wait_slot(slot)
        if n_steps >= 2:
            wait_slot(1 - slot)


def _scatter_rows(dest, rows, n_sorted):
    n_tok = rows.shape[0]
    n_steps = n_tok // TQ
    init = jnp.zeros((n_sorted, ROW_W), F32)
    return pl.pallas_call(
        functools.partial(_scatter_kernel, n_steps=n_steps),
        out_shape=jax.ShapeDtypeStruct((n_sorted, ROW_W), F32),
        grid_spec=pltpu.PrefetchScalarGridSpec(
            num_scalar_prefetch=1,
            grid=(n_steps,),
            in_specs=[pl.BlockSpec((TQ, ROW_W), lambda i, d: (i, 0)),
                      pl.BlockSpec(memory_space=pl.ANY)],
            out_specs=pl.BlockSpec(memory_space=pl.ANY),
            scratch_shapes=[pltpu.VMEM((2, TQ, ROW_W), F32), pltpu.SemaphoreType.DMA((2,))]),
        input_output_aliases={2: 0},
        compiler_params=_cparams(1),
        name="moe_scatter",
    )(dest, rows, init)


def _swiglu(h, wgu, wd):
    u = jnp.dot(h, wgu, preferred_element_type=F32)
    a = u[:, :D_EXPERT]
    a = a * jax.nn.sigmoid(a) * u[:, D_EXPERT:]
    return jnp.dot(a.astype(BF16), wd, preferred_element_type=F32)


def _moe_kernel(elo_ref, ehi_ref, nv_ref, row_ref, wgl_ref, wdl_ref, wgh_ref, wdh_ref, swg_ref, swd_ref, y_ref):
    del elo_ref, ehi_ref
    j = pl.program_id(0)

    @pl.when(j < nv_ref[0])
    def _():
        rows = row_ref[...]
        h = rows[:, :D_MODEL].astype(BF16)
        g_lo = rows[:, D_MODEL:D_MODEL + 1]
        g_hi = rows[:, D_MODEL + 1:D_MODEL + 2]
        y_ref[...] = (g_lo * _swiglu(h, wgl_ref[0], wdl_ref[0]) + g_hi * _swiglu(h, wgh_ref[0], wdh_ref[0])
                      + _swiglu(h, swg_ref[...], swd_ref[...]))

    @pl.when(j >= nv_ref[0])
    def _():
        y_ref[...] = jnp.zeros(y_ref.shape, F32)


def _grouped_experts(e_lo, e_hi, n_valid, rows_sorted, lw):
    n_sorted = rows_sorted.shape[0]
    nb = n_sorted // TM
    wgu, wd, swgu, swd = lw["wgu"], lw["wd"], lw["swgu"], lw["swd"]
    return pl.pallas_call(
        _moe_kernel,
        out_shape=jax.ShapeDtypeStruct((n_sorted, D_MODEL), F32),
        grid_spec=pltpu.PrefetchScalarGridSpec(
            num_scalar_prefetch=3,
            grid=(nb,),
            in_specs=[pl.BlockSpec((TM, ROW_W), lambda j, lo, hi, nv: (j, 0)),
                      pl.BlockSpec((1,) + wgu.shape[1:], lambda j, lo, hi, nv: (lo[j], 0, 0)),
                      pl.BlockSpec((1,) + wd.shape[1:], lambda j, lo, hi, nv: (lo[j], 0, 0)),
                      pl.BlockSpec((1,) + wgu.shape[1:], lambda j, lo, hi, nv: (hi[j], 0, 0)),
                      pl.BlockSpec((1,) + wd.shape[1:], lambda j, lo, hi, nv: (hi[j], 0, 0)),
                      pl.BlockSpec(swgu.shape, lambda j, lo, hi, nv: (0, 0)),
                      pl.BlockSpec(swd.shape, lambda j, lo, hi, nv: (0, 0))],
            out_specs=pl.BlockSpec((TM, D_MODEL), lambda j, lo, hi, nv: (j, 0))),
        compiler_params=_cparams(1),
        name="moe_experts",
    )(e_lo, e_hi, n_valid, rows_sorted, wgu, wd, wgu, wd, swgu, swd)


def _gather_kernel(dest_ref, y_ref, x_ref, mod_ref, gf_ref, o_ref, fbuf, sem, *, n_steps, final_norm):
    i = pl.program_id(0)
    slot = i % 2

    def issue(step, s):
        def body(r, carry):
            d = dest_ref[step * TQ + r]
            pltpu.make_async_copy(y_ref.at[pl.ds(d, 1)], fbuf.at[s, pl.ds(r, 1)], sem.at[s]).start()
            return carry
        lax.fori_loop(0, TQ, body, 0)

    @pl.when(i == 0)
    def _():
        issue(0, 0)

    @pl.when(i + 1 < n_steps)
    def _():
        issue(i + 1, 1 - slot)

    pltpu.make_async_copy(y_ref.at[pl.ds(0, TQ)], fbuf.at[slot], sem.at[slot]).wait()
    x = x_ref[0] + mod_ref[0, 0][5:6] * fbuf[slot]
    if final_norm:
        x = _rms(x) * gf_ref[...]
    o_ref[0] = x


def _gather_residual(dest, y_sorted, x_mid, modtab, g_final, n_ctx_blk, with_ctx, final_norm):
    bsz, t_rows, d = x_mid.shape
    nblk = t_rows // TQ
    off = 0 if with_ctx else n_ctx_blk
    n_steps = bsz * nblk
    return pl.pallas_call(
        functools.partial(_gather_kernel, n_steps=n_steps, final_norm=final_norm),
        out_shape=jax.ShapeDtypeStruct((bsz, t_rows, d), F32),
        grid_spec=pltpu.PrefetchScalarGridSpec(
            num_scalar_prefetch=1,
            grid=(n_steps,),
            in_specs=[pl.BlockSpec(memory_space=pl.ANY),
                      pl.BlockSpec((1, TQ, d), lambda i, dst: (i // nblk, i % nblk, 0)),
                      pl.BlockSpec((1, 1, 8, d),
                                   lambda i, dst: (i // nblk, jnp.where(i % nblk + off < n_ctx_blk, 1, 0), 0, 0)),
                      pl.BlockSpec(g_final.shape, lambda i, dst: (0, 0))],
            out_specs=pl.BlockSpec((1, TQ, d), lambda i, dst: (i // nblk, i % nblk, 0)),
            scratch_shapes=[pltpu.VMEM((2, TQ, d), F32), pltpu.SemaphoreType.DMA((2,))]),
        compiler_params=_cparams(1),
        name="moe_gather",
    )(dest, y_sorted, x_mid, modtab, g_final)


def _bucket_layout(meta, counts, n_tok):
    bucket = meta[:, 0, :].reshape(-1)
    rank = meta[:, 1, :].reshape(-1)
    cnt = counts[:N_BUCKETS, 0].astype(jnp.int32)
    padded = (cnt + TM - 1) // TM * TM
    pad_end = jnp.cumsum(padded)
    pad_start = pad_end - padded
    dest = pad_start[bucket] + rank
    nb = n_tok // TM + N_BUCKETS
    n_valid = pad_end[-1] // TM
    blk = jnp.arange(nb, dtype=jnp.int32)
    blk_bucket = jnp.minimum(jnp.searchsorted(pad_end, jnp.minimum(blk, n_valid - 1) * TM, side="right"),
                             N_BUCKETS - 1).astype(jnp.int32)
    grp, pair = blk_bucket // N_PAIRS, blk_bucket % N_PAIRS
    e_lo = grp * EXPERTS_PER_GROUP + jnp.asarray(_PAIR_LO, jnp.int32)[pair]
    e_hi = grp * EXPERTS_PER_GROUP + jnp.asarray(_PAIR_HI, jnp.int32)[pair]
    return dest.astype(jnp.int32), e_lo, e_hi, n_valid.reshape(1).astype(jnp.int32), nb * TM


def _rope_tables(seq, n_ctx):
    t = jnp.arange(seq)
    rows, cols = t // GRID_W, t % GRID_W

    def table(width, nf):
        lane = np.arange(width)
        half, j = lane // (2 * nf), lane % (2 * nf)
        inv = ROPE_THETA ** (-jnp.arange(nf, dtype=F32) / nf)
        pos = jnp.where(jnp.asarray(half == 0)[None, :], rows[:, None], cols[:, None]).astype(F32)
        ang = pos * inv[j % nf][None, :]
        sign = jnp.asarray(np.where(j < nf, -1.0, 1.0), F32)[None, :]
        return jnp.cos(ang), jnp.sin(ang) * sign

    cos64, sin64 = table(HEAD_DIM, 16)
    cos64, sin64 = jnp.tile(cos64, (1, 2)), jnp.tile(sin64, (1, 2))
    cosr, sinr = table(MLA_ROPE, 8)
    cosm = jnp.concatenate([jnp.ones((seq, MLA_NOPE), F32), cosr, jnp.ones((seq, 32), F32)], axis=1)
    sinm = jnp.concatenate([jnp.zeros((seq, MLA_NOPE), F32), sinr, jnp.zeros((seq, 32), F32)], axis=1)
    ctx1, ctx0 = jnp.ones((n_ctx, LANES), F32), jnp.zeros((n_ctx, LANES), F32)
    return tuple(jnp.concatenate([c, a], axis=0) for c, a in ((ctx1, cos64), (ctx0, sin64), (ctx1, cosm), (ctx0, sinm)))


def _layer_weights(l, w_in, w_out, norm_ffn_g, glb_q_gain, glb_k_gain, mla_q_gain, mla_w_uq, mla_kv_gain,
                   mla_w_ukv, exp_w_gate, exp_w_up, exp_w_down, shr_w_gate, shr_w_up, shr_w_down):
    d = w_in.shape[1]
    wi = w_in[l]
    kr = wi[:, 1664:1696]
    win = jnp.concatenate([wi[:, :1664], jnp.zeros((d, MLA_NOPE), F32), kr, jnp.zeros((d, 32), F32)], axis=1)
    uq = mla_w_uq[l].reshape(MLA_Q_RANK, MLA_HEADS, MLA_NOPE + MLA_ROPE)
    wuq = jnp.concatenate([uq, jnp.zeros((MLA_Q_RANK, MLA_HEADS, 32), F32)], axis=2).reshape(MLA_Q_RANK, 512)
    ukv = mla_w_ukv[l].reshape(MLA_KV_RANK, MLA_HEADS, MLA_NOPE + MLA_V)
    wk = jnp.concatenate([ukv[:, :, :MLA_NOPE], jnp.zeros((MLA_KV_RANK, MLA_HEADS, 64), F32)], axis=2)
    wukv = jnp.concatenate([wk.reshape(MLA_KV_RANK, 512), ukv[:, :, MLA_NOPE:].reshape(MLA_KV_RANK, 256)], axis=1)
    gain512 = jnp.concatenate([jnp.tile(glb_q_gain[l] * (HEAD_DIM ** -0.5 * LOG2E), GQA_HEADS),
                               jnp.tile(glb_k_gain[l], GQA_KV)]).reshape(1, 512)
    head = np.arange(512) // HEAD_DIM
    return {
        "win": win.astype(BF16),
        "wuq": wuq.astype(BF16),
        "wukv": wukv.astype(BF16),
        "gain512": gain512,
        "mqg": mla_q_gain[l].reshape(1, -1),
        "mkvg": mla_kv_gain[l].reshape(1, -1),
        "bd": jnp.asarray(head[:, None] == head[None, :], BF16),
        "wout": w_out[l].astype(BF16),
        "g_ffn": norm_ffn_g[l].reshape(1, -1),
        "wgu": jnp.concatenate([exp_w_gate[l], exp_w_up[l]], axis=2).astype(BF16),
        "wd": exp_w_down[l].astype(BF16),
        "swgu": jnp.concatenate([shr_w_gate[l], shr_w_up[l]], axis=1).astype(BF16),
        "swd": shr_w_down[l].astype(BF16),
    }


def kernel(x, c, ctx, c_ctx, w_mod, b_mod, norm_mix_g, norm_ffn_g, w_in, w_out, swa_sink, glb_q_gain, glb_k_gain,
           mla_q_gain, mla_w_uq, mla_kv_gain, mla_w_ukv, router_w, router_bias, exp_w_gate, exp_w_up, exp_w_down,
           shr_w_gate, shr_w_up, shr_w_down, final_norm_g):
    bsz, seq, d = x.shape
    n_ctx = ctx.shape[1]
    n_layers = w_mod.shape[0]
    assert d == D_MODEL and n_ctx % TQ == 0 and seq % TQ == 0 and seq >= TQ + 2 * WINDOW and seq % GRID_W == 0
    n_ctx_blk = n_ctx // TQ

    rows = -(-(bsz + 1) // 8) * 8
    c_rows = jnp.concatenate([c, c_ctx[None, :], jnp.zeros((rows - bsz - 1, d), F32)], axis=0)
    mods = _modulation(c_rows, w_mod, b_mod)
    mod_x = mods[:, :bsz].reshape(n_layers, bsz, 6, d)
    mod_c = jnp.broadcast_to(mods[:, bsz].reshape(n_layers, 1, 6, d), (n_layers, bsz, 6, d))
    modtabs = jnp.pad(jnp.stack([mod_x, mod_c], axis=2), ((0, 0), (0, 0), (0, 0), (0, 2), (0, 0)))

    tabs = _rope_tables(seq, n_ctx)
    rw = jnp.pad(router_w.T, ((0, 32 - N_EXPERTS), (0, 0)))
    rw_hi = rw.astype(BF16)
    shared = {
        "rw_hi": rw_hi,
        "rw_lo": (rw - rw_hi.astype(F32)).astype(BF16),
        "rb": jnp.pad(router_bias, (0, 32 - N_EXPERTS)).reshape(32, 1),
        "tri": jnp.asarray(np.arange(TQ)[:, None] <= np.arange(TQ)[None, :], BF16),
    }
    g_final = final_norm_g.reshape(1, d)

    xs = jnp.concatenate([ctx, x], axis=1)
    for l in range(n_layers):
        last = l == n_layers - 1
        with_ctx = not last
        lw = _layer_weights(l, w_in, w_out, norm_ffn_g, glb_q_gain, glb_k_gain, mla_q_gain, mla_w_uq,
                            mla_kv_gain, mla_w_ukv, exp_w_gate, exp_w_up, exp_w_down,
                            shr_w_gate, shr_w_up, shr_w_down)
        modtab = modtabs[l]
        qs, ks, vs, qg, kg, vg, qm, km, vm = _input_projection(
            xs, modtab, norm_mix_g[l].reshape(1, d), lw, tabs, n_ctx_blk)
        sink = jnp.pad(swa_sink[l] * LOG2E, (0, 8 - GQA_HEADS))
        o_swa = _attention("swa", qs, ks, vs, sink, n_ctx, with_ctx)
        o_glb = _attention("glb", qg, kg, vg, None, n_ctx, with_ctx)
        o_mla = _attention("mla", qm, km, vm, None, n_ctx, with_ctx)
        x_mid, rows_tok, meta, counts = _output_projection(o_swa, o_glb, o_mla, xs, modtab, lw, shared,
                                                           n_ctx_blk, with_ctx)
        n_tok = rows_tok.shape[0]
        dest, e_lo, e_hi, n_valid, n_sorted = _bucket_layout(meta, counts, n_tok)
        rows_sorted = _scatter_rows(dest, rows_tok, n_sorted)
        y_sorted = _grouped_experts(e_lo, e_hi, n_valid, rows_sorted, lw)
        xs = _gather_residual(dest, y_sorted, x_mid, modtab, g_final, n_ctx_blk, with_ctx, final_norm=last)
    return xs
```

```python
import functools

import jax
import jax.numpy as jnp
import numpy as np
from jax import lax
from jax.experimental import pallas as pl
from jax.experimental.pallas import tpu as pltpu

F32 = jnp.float32
BF16 = jnp.bfloat16

D_MODEL = 1024
GRID_W = 64
HEAD_DIM = 64
GQA_HEADS = 6
GQA_KV = 2
GQA_G = GQA_HEADS // GQA_KV
WINDOW = 128
MLA_HEADS = 4
MLA_NOPE = 64
MLA_ROPE = 32
MLA_V = 64
MLA_Q_RANK = 256
MLA_KV_RANK = 128
ROPE_THETA = 10000.0
N_EXPERTS = 16
N_GROUPS = 4
EXPERTS_PER_GROUP = 4
N_PAIRS = 6
N_BUCKETS = N_GROUPS * N_PAIRS
D_EXPERT = 512
EPS = 1e-6
NEG_INF = -1e30

LANES = 128
TQ = 256
TK = 256
TM = 256
SWA_SLAB = 128
V_ROWS = 80
LOG2E = 1.4426950408889634
ROW_W = D_MODEL + LANES
IN_W = 1792
VMEM_LIMIT = 56 * 1024 * 1024

_SQ, _SK, _SV, _GQ, _GK, _GV, _MQ, _MKV, _KR = 0, 384, 512, 640, 1024, 1152, 1280, 1536, 1664

_PAIR_LO = (0, 0, 0, 1, 1, 2)
_PAIR_HI = (1, 2, 3, 2, 3, 3)


def _cparams(n_axes):
    return pltpu.CompilerParams(dimension_semantics=("arbitrary",) * n_axes,
                                vmem_limit_bytes=VMEM_LIMIT)


def _mod_kernel(c_ref, w_ref, b_ref, o_ref):
    c = c_ref[...]
    a = (c * jax.nn.sigmoid(c)).astype(BF16)
    o_ref[0] = jnp.dot(a, w_ref[0].astype(BF16), preferred_element_type=F32) + b_ref[0]


def _modulation(c_rows, w_mod, b_mod):
    n_layers, d, width = w_mod.shape
    rows = c_rows.shape[0]
    nb = 1536
    return pl.pallas_call(
        _mod_kernel,
        out_shape=jax.ShapeDtypeStruct((n_layers, rows, width), F32),
        grid=(n_layers, width // nb),
        in_specs=[pl.BlockSpec((rows, d), lambda l, j: (0, 0)),
                  pl.BlockSpec((1, d, nb), lambda l, j: (l, 0, j)),
                  pl.BlockSpec((1, 1, nb), lambda l, j: (l, 0, j))],
        out_specs=pl.BlockSpec((1, rows, nb), lambda l, j: (l, 0, j)),
        compiler_params=_cparams(2),
        name="adaln_mod",
    )(c_rows, w_mod, b_mod.reshape(n_layers, 1, width))


def _rope(x, cos, sin_signed, nf):
    lane = lax.broadcasted_iota(jnp.int32, (1, LANES), 1)
    first = (lane % (2 * nf)) < nf
    tiles = []
    for t in range(x.shape[1] // LANES):
        xt = x[:, t * LANES:(t + 1) * LANES]
        partner = jnp.where(first, pltpu.roll(xt, LANES - nf, 1), pltpu.roll(xt, nf, 1))
        tiles.append(xt * cos + partner * sin_signed)
    return tiles[0] if len(tiles) == 1 else jnp.concatenate(tiles, axis=1)


def _group_padded(q):
    lane = lax.broadcasted_iota(jnp.int32, (1, LANES), 1)
    tiles = []
    for h in range(GQA_HEADS):
        g = h // GQA_G
        tile = q[:, (h // 2) * LANES:(h // 2 + 1) * LANES]
        if (h % 2) != g:
            tile = pltpu.roll(tile, HEAD_DIM, 1)
        tiles.append(jnp.where((lane // HEAD_DIM) == g, tile, 0.0))
    return jnp.concatenate(tiles, axis=1)


def _emit_q_t(ref, q):
    for h in range(q.shape[1] // LANES):
        ref[0, h] = q[:, h * LANES:(h + 1) * LANES].T.astype(ref.dtype)


def _emit_v_t(ref, v, slab):
    ones_blk = jnp.where(lax.broadcasted_iota(jnp.int32, (V_ROWS - HEAD_DIM, TQ), 0) == 0, 1.0, 0.0)
    parts = []
    for t in range(v.shape[1] // LANES):
        vt = v[:, t * LANES:(t + 1) * LANES].T
        parts += [vt[:HEAD_DIM], ones_blk, vt[HEAD_DIM:], ones_blk]
    ext = jnp.concatenate(parts, axis=0).astype(ref.dtype)
    for s in range(TQ // slab):
        ref[0, s] = ext[:, s * slab:(s + 1) * slab]


def _rms(x, eps=EPS):
    return x * lax.rsqrt(jnp.mean(x * x, axis=-1, keepdims=True) + eps)


def _in_kernel(x_ref, mod_ref, g_ref, win_ref, cos64_ref, sin64_ref, cosm_ref, sinm_ref,
               gain_ref, mqg_ref, mkvg_ref, wuq_ref, wukv_ref, bd_ref,
               qs_ref, ks_ref, vs_ref, qg_ref, kg_ref, vg_ref, qm_ref, km_ref, vm_ref, *, mla_scale):
    x = x_ref[0]
    mod = mod_ref[0, 0]
    h = _rms(x) * g_ref[...]
    h = h * (1.0 + mod[1:2]) + mod[0:1]
    z = jnp.dot(h.astype(BF16), win_ref[...], preferred_element_type=F32)
    cos64, sin64 = cos64_ref[...], sin64_ref[...]
    cosm, sinm = cosm_ref[...], sinm_ref[...]

    sqk = _rope(z[:, _SQ:_SV], cos64, sin64, 16)
    _emit_q_t(qs_ref, _group_padded(sqk[:, :384] * (HEAD_DIM ** -0.5 * LOG2E)))
    ks_ref[0] = sqk[:, 384:].astype(BF16)
    _emit_v_t(vs_ref, z[:, _SV:_GQ], SWA_SLAB)

    gqk = z[:, _GQ:_GV]
    sq = gqk * gqk
    hi = sq.astype(BF16)
    lo = (sq - hi.astype(F32)).astype(BF16)
    ssum = (jnp.dot(hi, bd_ref[...], preferred_element_type=F32)
            + jnp.dot(lo, bd_ref[...], preferred_element_type=F32))
    gqk = gqk * lax.rsqrt(ssum * (1.0 / HEAD_DIM) + EPS) * gain_ref[...]
    gqk = _rope(gqk, cos64, sin64, 16)
    _emit_q_t(qg_ref, _group_padded(gqk[:, :384]))
    kg_ref[0] = gqk[:, 384:].astype(BF16)
    _emit_v_t(vg_ref, z[:, _GV:_MQ], TK)

    qn = _rms(z[:, _MQ:_MKV]) * mqg_ref[...]
    mq = jnp.dot(qn.astype(BF16), wuq_ref[...], preferred_element_type=F32)
    _emit_q_t(qm_ref, _rope(mq, cosm, sinm, 8) * (mla_scale * LOG2E))
    kvn = _rms(z[:, _MKV:_KR]) * mkvg_ref[...]
    mkv = jnp.dot(kvn.astype(BF16), wukv_ref[...], preferred_element_type=F32)
    kr = _rope(z[:, _KR:IN_W], cosm, sinm, 8)
    km_ref[0] = (mkv[:, :512] + jnp.concatenate([kr] * MLA_HEADS, axis=1)).astype(BF16)
    _emit_v_t(vm_ref, mkv[:, 512:], TK)


def _input_projection(xs, modtab, g_mix, lw, tabs, n_ctx_blk):
    bsz, t_all, d = xs.shape
    nblk = t_all // TQ
    tok = lambda w: pl.BlockSpec((1, TQ, w), lambda b, i: (b, i, 0))
    q_t = lambda nh: pl.BlockSpec((1, nh, LANES, TQ), lambda b, i: (b, 0, 0, i))
    v_t = lambda nh, slab: pl.BlockSpec((1, TQ // slab, nh * V_ROWS, slab), lambda b, i: (b, i, 0, 0))
    full = lambda a: pl.BlockSpec(a.shape, lambda b, i: (0,) * a.ndim)
    tab = pl.BlockSpec((TQ, LANES), lambda b, i: (i, 0))
    q_shape = lambda nh: jax.ShapeDtypeStruct((bsz, nh, LANES, t_all), BF16)
    k_shape = lambda w: jax.ShapeDtypeStruct((bsz, t_all, w), BF16)
    v_shape = lambda nh, slab: jax.ShapeDtypeStruct((bsz, t_all // slab, nh * V_ROWS, slab), BF16)
    consts = (lw["gain512"], lw["mqg"], lw["mkvg"], lw["wuq"], lw["wukv"], lw["bd"])
    return pl.pallas_call(
        functools.partial(_in_kernel, mla_scale=float((MLA_NOPE + MLA_ROPE) ** -0.5)),
        out_shape=(q_shape(GQA_HEADS), k_shape(128), v_shape(GQA_KV, SWA_SLAB),
                   q_shape(GQA_HEADS), k_shape(128), v_shape(GQA_KV, TK),
                   q_shape(MLA_HEADS), k_shape(512), v_shape(MLA_HEADS, TK)),
        grid=(bsz, nblk),
        in_specs=[tok(d),
                  pl.BlockSpec((1, 1, 8, d), lambda b, i: (b, jnp.where(i < n_ctx_blk, 1, 0), 0, 0)),
                  full(g_mix), full(lw["win"]), tab, tab, tab, tab] + [full(a) for a in consts],
        out_specs=(q_t(GQA_HEADS), tok(128), v_t(GQA_KV, SWA_SLAB),
                   q_t(GQA_HEADS), tok(128), v_t(GQA_KV, TK),
                   q_t(MLA_HEADS), tok(512), v_t(MLA_HEADS, TK)),
        compiler_params=_cparams(2),
        name="in_proj",
    )(xs, modtab, g_mix, lw["win"], *tabs, *consts)


def _attn_kernel(*refs, kind, n_ctx, n_ctx_blk, q_blk_off, seq):
    if kind == "swa":
        sink_ref, q_ref, k_ref, v_ref, o_ref, m_ref, acc_ref = refs
    else:
        q_ref, k_ref, v_ref, o_ref, m_ref, acc_ref = refs[:6]
        s_refs, p_refs, al_refs = refs[6:8], refs[8:10], refs[10:12]
    blk = pl.program_id(1) + q_blk_off
    is_lat = blk >= n_ctx_blk
    gqa = kind in ("swa", "glb")
    n_heads = GQA_HEADS if gqa else MLA_HEADS
    n_units = GQA_KV if gqa else MLA_HEADS
    unit_w = n_heads // n_units * TQ

    def update(k_rows, v_t, mask):
        if gqa:
            q_all = jnp.concatenate([q_ref[0, h] for h in range(n_heads)], axis=1)
            s = jnp.dot(k_rows, q_all, preferred_element_type=F32)
        else:
            s = jnp.concatenate([jnp.dot(k_rows[:, h * LANES:(h + 1) * LANES], q_ref[0, h],
                                         preferred_element_type=F32) for h in range(n_heads)], axis=1)
        if mask is not None:
            s = jnp.where(mask, s, NEG_INF)
        m_prev = m_ref[...]
        m_new = jnp.maximum(m_prev, jnp.max(s, axis=0, keepdims=True))
        alpha = jnp.exp2(m_prev - m_new)
        p = jnp.exp2(s - m_new).astype(BF16)
        m_ref[...] = m_new
        for u in range(n_units):
            lanes = slice(u * unit_w, (u + 1) * unit_w)
            acc_ref[u] = (acc_ref[u] * alpha[:, lanes]
                          + jnp.dot(v_t[u * V_ROWS:(u + 1) * V_ROWS], p[:, lanes], preferred_element_type=F32))

    if kind == "swa":
        m_ref[...] = jnp.concatenate([jnp.full((1, TQ), sink_ref[h], F32) for h in range(n_heads)], axis=1)
        l_row = jnp.where(lax.broadcasted_iota(jnp.int32, (V_ROWS, unit_w), 0) == HEAD_DIM, 1.0, 0.0)
        for u in range(n_units):
            acc_ref[u] = l_row
        n_slab = n_ctx // SWA_SLAB
        update(k_ref[0, 0:n_ctx, :], jnp.concatenate([v_ref[0, s] for s in range(n_slab)], axis=1), None)

        @pl.when(is_lat)
        def _():
            nk = TQ + 2 * WINDOW
            q0 = (blk - n_ctx_blk) * TQ
            k0 = pl.multiple_of(jnp.clip(q0 - WINDOW, 0, seq - nk), WINDOW)
            kpos = k0 + lax.broadcasted_iota(jnp.int32, (nk, n_heads * TQ), 0)
            qpos = q0 + (lax.broadcasted_iota(jnp.int32, (nk, n_heads * TQ), 1) & (TQ - 1))
            s0 = (n_ctx + k0) // SWA_SLAB
            v_t = jnp.concatenate([v_ref[0, s0 + s] for s in range(nk // SWA_SLAB)], axis=1)
            update(k_ref[0, pl.ds(pl.multiple_of(n_ctx + k0, WINDOW), nk), :], v_t,
                   jnp.abs(qpos - kpos) <= WINDOW)
    else:
        m_ref[...] = jnp.full(m_ref.shape, NEG_INF, F32)
        acc_ref[...] = jnp.zeros(acc_ref.shape, F32)

        def qk(c, par):
            k_rows = k_ref[0, pl.ds(pl.multiple_of(c * TK, TK), TK), :]
            if gqa:
                q_all = jnp.concatenate([q_ref[0, h] for h in range(n_heads)], axis=1)
                s_refs[par][...] = jnp.dot(k_rows, q_all, preferred_element_type=F32)
            else:
                for h in range(n_heads):
                    s_refs[par][:, h * TQ:(h + 1) * TQ] = jnp.dot(
                        k_rows[:, h * LANES:(h + 1) * LANES], q_ref[0, h], preferred_element_type=F32)

        def softmax(par):
            s = s_refs[par][...]
            m_prev = m_ref[...]
            m_new = jnp.maximum(m_prev, jnp.max(s, axis=0, keepdims=True))
            al_refs[par][...] = jnp.exp2(m_prev - m_new)
            p_refs[par][...] = jnp.exp2(s - m_new).astype(BF16)
            m_ref[...] = m_new

        def pv(c, par):
            v_t = v_ref[0, c]
            alpha = al_refs[par][...]
            for u in range(n_units):
                lanes = slice(u * unit_w, (u + 1) * unit_w)
                acc_ref[u] = (acc_ref[u] * alpha[:, lanes]
                              + jnp.dot(v_t[u * V_ROWS:(u + 1) * V_ROWS], p_refs[par][:, lanes],
                                        preferred_element_type=F32))

        def step(i, par, n):
            if not isinstance(i, int) or i + 1 < n:
                qk(i + 1, 1 - par)
            if not isinstance(i, int) or i < n:
                softmax(par)
            if not isinstance(i, int) or i >= 1:
                pv(i - 1, 1 - par)

        def run(n):
            qk(0, 0)
            step(0, 0, n)
            n_pairs = max(0, (n - 2) // 2)

            def body(j, carry):
                i = 1 + 2 * j
                step(i, 1, n)
                step(i + 1, 0, n)
                return carry
            if n_pairs:
                lax.fori_loop(0, n_pairs, body, 0)
            for i in range(1 + 2 * n_pairs, n + 1):
                step(i, i % 2, n)

        if q_blk_off < n_ctx_blk:
            pl.when(jnp.logical_not(is_lat))(lambda: run(n_ctx // TK))
            pl.when(is_lat)(lambda: run((n_ctx + seq) // TK))
        else:
            run((n_ctx + seq) // TK)

    heads = []
    for u in range(n_units):
        a = acc_ref[u]
        o = a[:HEAD_DIM] / a[HEAD_DIM:HEAD_DIM + 1]
        heads += [o[:, j * TQ:(j + 1) * TQ] for j in range(unit_w // TQ)]
    tiles = [jnp.concatenate(heads[2 * t:2 * t + 2], axis=0).T for t in range(n_heads // 2)]
    o_ref[0] = jnp.concatenate(tiles, axis=1).astype(o_ref.dtype)


def _attention(kind, q_t, k, v_t, sink, n_ctx, with_ctx_queries):
    bsz, n_heads, _, t_all = q_t.shape
    seq = t_all - n_ctx
    n_ctx_blk = n_ctx // TQ
    q_blk_off = 0 if with_ctx_queries else n_ctx_blk
    nblk = t_all // TQ - q_blk_off
    n_units = GQA_KV if kind in ("swa", "glb") else MLA_HEADS
    wo = n_heads * HEAD_DIM
    kern = functools.partial(_attn_kernel, kind=kind, n_ctx=n_ctx, n_ctx_blk=n_ctx_blk,
                             q_blk_off=q_blk_off, seq=seq)
    in_specs = [pl.BlockSpec((1, n_heads, LANES, TQ), lambda b, i: (b, 0, 0, i + q_blk_off)),
                pl.BlockSpec((1,) + k.shape[1:], lambda b, i: (b, 0, 0)),
                pl.BlockSpec((1,) + v_t.shape[1:], lambda b, i: (b, 0, 0, 0))]
    args = [q_t, k, v_t]
    if kind == "swa":
        in_specs = [pl.BlockSpec(memory_space=pltpu.SMEM)] + in_specs
        args = [sink] + args
    nq = n_heads * TQ
    scratch = [pltpu.VMEM((1, nq), F32), pltpu.VMEM((n_units, V_ROWS, nq // n_units), F32)]
    if kind != "swa":
        scratch += [pltpu.VMEM((TK, nq), F32)] * 2 + [pltpu.VMEM((TK, nq), BF16)] * 2 + [pltpu.VMEM((1, nq), F32)] * 2
    return pl.pallas_call(
        kern,
        out_shape=jax.ShapeDtypeStruct((bsz, nblk * TQ, wo), BF16),
        grid=(bsz, nblk),
        in_specs=in_specs,
        out_specs=pl.BlockSpec((1, TQ, wo), lambda b, i: (b, i, 0)),
        scratch_shapes=scratch,
        compiler_params=_cparams(2),
        name="attn_" + kind,
    )(*args)


def _row_select(rows, idx):
    out = rows[0]
    for j in range(1, len(rows)):
        out = jnp.where(idx == j, rows[j], out)
    return out


def _route(scores, biased):
    def top2sum(a, b, c, d):
        hi1, lo1, hi2, lo2 = jnp.maximum(a, b), jnp.minimum(a, b), jnp.maximum(c, d), jnp.minimum(c, d)
        return jnp.maximum(hi1, hi2) + jnp.maximum(jnp.minimum(hi1, hi2), jnp.maximum(lo1, lo2))

    gs = [top2sum(*biased[4 * g:4 * g + 4]) for g in range(N_GROUPS)]
    best, gi = gs[0], jnp.zeros(gs[0].shape, jnp.int32)
    for g in range(1, N_GROUPS):
        better = gs[g] > best
        gi = jnp.where(better, g, gi)
        best = jnp.where(better, gs[g], best)
    a = [_row_select([biased[4 * g + j] for g in range(N_GROUPS)], gi) for j in range(EXPERTS_PER_GROUP)]
    s = [_row_select([scores[4 * g + j] for g in range(N_GROUPS)], gi) for j in range(EXPERTS_PER_GROUP)]
    v1, i1 = a[0], jnp.zeros(gi.shape, jnp.int32)
    for j in range(1, EXPERTS_PER_GROUP):
        better = a[j] > v1
        i1 = jnp.where(better, j, i1)
        v1 = jnp.where(better, a[j], v1)
    v2, i2 = jnp.full(v1.shape, -3.0e38, F32), jnp.zeros(gi.shape, jnp.int32)
    for j in range(EXPERTS_PER_GROUP):
        better = (i1 != j) & (a[j] > v2)
        i2 = jnp.where(better, j, i2)
        v2 = jnp.where(better, a[j], v2)
    lo, hi = jnp.minimum(i1, i2), jnp.maximum(i1, i2)
    pair = jnp.where(lo == 0, hi - 1, jnp.where(lo == 1, hi + 1, 5))
    s_lo, s_hi = _row_select(s, lo), _row_select(s, hi)
    den = s_lo + s_hi
    return gi * N_PAIRS + pair, s_lo / den, s_hi / den


def _out_kernel(os_ref, og_ref, om_ref, x_ref, mod_ref, wout_ref, g_ref, rwh_ref, rwl_ref, rb_ref, tri_ref,
                xo_ref, row_ref, meta_ref, cnt_ref, carry_ref):
    @pl.when((pl.program_id(0) == 0) & (pl.program_id(1) == 0))
    def _():
        carry_ref[...] = jnp.zeros(carry_ref.shape, F32)

    y = jnp.concatenate([os_ref[0], og_ref[0], om_ref[0]], axis=1)
    mod = mod_ref[0, 0]
    x = x_ref[0] + mod[2:3] * jnp.dot(y, wout_ref[...], preferred_element_type=F32)
    xo_ref[0] = x
    h = _rms(x) * g_ref[...]
    h = h * (1.0 + mod[4:5]) + mod[3:4]

    hh = h.astype(BF16)
    hl = (h - hh.astype(F32)).astype(BF16)
    dn = (((1,), (1,)), ((), ()))
    logits = (lax.dot_general(rwh_ref[...], hh, dn, preferred_element_type=F32)
              + lax.dot_general(rwh_ref[...], hl, dn, preferred_element_type=F32)
              + lax.dot_general(rwl_ref[...], hh, dn, preferred_element_type=F32))
    sc = jax.nn.sigmoid(logits)
    bs = sc + rb_ref[...]
    scores = [sc[e:e + 1, :] for e in range(N_EXPERTS)]
    biased = [bs[e:e + 1, :] for e in range(N_EXPERTS)]
    bucket, g_lo, g_hi = _route(scores, biased)

    onehot = jnp.where(lax.broadcasted_iota(jnp.int32, (32, TQ), 0) == bucket, 1.0, 0.0)
    prefix = jnp.dot(onehot.astype(BF16), tri_ref[...], preferred_element_type=F32)
    carry = carry_ref[:, 0:1]
    rank = jnp.sum(onehot * (carry + prefix - 1.0), axis=0, keepdims=True)
    carry_new = jnp.broadcast_to(carry + prefix[:, TQ - 1:TQ], carry_ref.shape)
    carry_ref[...] = carry_new
    cnt_ref[...] = carry_new
    meta_ref[0] = jnp.concatenate([bucket, rank.astype(jnp.int32), jnp.zeros((6, TQ), jnp.int32)], axis=0)

    gates = jnp.concatenate([g_lo, g_hi, jnp.zeros((LANES - 2, TQ), F32)], axis=0)
    row_ref[...] = jnp.concatenate([h, gates.T], axis=1)


def _output_projection(o_swa, o_glb, o_mla, xs, modtab, lw, shared, n_ctx_blk, with_ctx):
    bsz, t_all, d = xs.shape
    off = 0 if with_ctx else n_ctx_blk
    nblk = t_all // TQ - off
    tok = lambda w: pl.BlockSpec((1, TQ, w), lambda b, i: (b, i, 0))
    full = lambda a: pl.BlockSpec(a.shape, lambda b, i: (0,) * a.ndim)
    consts = (lw["wout"], lw["g_ffn"], shared["rw_hi"], shared["rw_lo"], shared["rb"], shared["tri"])
    return pl.pallas_call(
        _out_kernel,
        out_shape=(jax.ShapeDtypeStruct((bsz, nblk * TQ, d), F32),
                   jax.ShapeDtypeStruct((bsz * nblk * TQ, ROW_W), F32),
                   jax.ShapeDtypeStruct((bsz * nblk, 8, TQ), jnp.int32),
                   jax.ShapeDtypeStruct((32, LANES), F32)),
        grid=(bsz, nblk),
        in_specs=[tok(o_swa.shape[2]), tok(o_glb.shape[2]), tok(o_mla.shape[2]),
                  pl.BlockSpec((1, TQ, d), lambda b, i: (b, i + off, 0)),
                  pl.BlockSpec((1, 1, 8, d), lambda b, i: (b, jnp.where(i + off < n_ctx_blk, 1, 0), 0, 0))]
                 + [full(a) for a in consts],
        out_specs=(tok(d),
                   pl.BlockSpec((TQ, ROW_W), lambda b, i: (b * nblk + i, 0)),
                   pl.BlockSpec((1, 8, TQ), lambda b, i: (b * nblk + i, 0, 0)),
                   pl.BlockSpec((32, LANES), lambda b, i: (0, 0))),
        scratch_shapes=[pltpu.VMEM((32, LANES), F32)],
        compiler_params=_cparams(2),
        name="out_proj_router",
    )(o_swa, o_glb, o_mla, xs, modtab, *consts)


def _scatter_kernel(dest_ref, row_ref, init_ref, out_ref, buf, sem, *, n_steps):
    del init_ref
    i = pl.program_id(0)
    slot = i % 2

    def wait_slot(s):
        pltpu.make_async_copy(buf.at[s], out_ref.at[pl.ds(0, TQ)], sem.at[s]).wait()

    @pl.when(i >= 2)
    def _():
        wait_slot(slot)

    buf[slot] = row_ref[...]

    def body(r, carry):
        d = dest_ref[i * TQ + r]
        pltpu.make_async_copy(buf.at[slot, pl.ds(r, 1)], out_ref.at[pl.ds(d, 1)], sem.at[slot]).start()
        return carry
    lax.fori_loop(0, TQ, body, 0, unroll=8)

    @pl.when(i == n_steps - 1)
    def _():
        wait_slot(slot)
        if n_steps >= 2:
            wait_slot(1 - slot)


def _scatter_rows(dest, rows, n_sorted):
    n_tok = rows.shape[0]
    n_steps = n_tok // TQ
    init = jnp.zeros((n_sorted, ROW_W), F32)
    return pl.pallas_call(
        functools.partial(_scatter_kernel, n_steps=n_steps),
        out_shape=jax.ShapeDtypeStruct((n_sorted, ROW_W), F32),
        grid_spec=pltpu.PrefetchScalarGridSpec(
            num_scalar_prefetch=1,
            grid=(n_steps,),
            in_specs=[pl.BlockSpec((TQ, ROW_W), lambda i, d: (i, 0)),
                      pl.BlockSpec(memory_space=pl.ANY)],
            out_specs=pl.BlockSpec(memory_space=pl.ANY),
            scratch_shapes=[pltpu.VMEM((2, TQ, ROW_W), F32), pltpu.SemaphoreType.DMA((2,))]),
        input_output_aliases={2: 0},
        compiler_params=_cparams(1),
        name="moe_scatter",
    )(dest, rows, init)


def _swiglu(h, wgu, wd):
    u = jnp.dot(h, wgu, preferred_element_type=F32)
    a = u[:, :D_EXPERT]
    a = a * jax.nn.sigmoid(a) * u[:, D_EXPERT:]
    return jnp.dot(a.astype(BF16), wd, preferred_element_type=F32)


def _moe_kernel(elo_ref, ehi_ref, nv_ref, row_ref, wgl_ref, wdl_ref, wgh_ref, wdh_ref, swg_ref, swd_ref, y_ref):
    del elo_ref, ehi_ref
    j = pl.program_id(0)

    @pl.when(j < nv_ref[0])
    def _():
        rows = row_ref[...]
        h = rows[:, :D_MODEL].astype(BF16)
        g_lo = rows[:, D_MODEL:D_MODEL + 1]
        g_hi = rows[:, D_MODEL + 1:D_MODEL + 2]
        y_ref[...] = (g_lo * _swiglu(h, wgl_ref[0], wdl_ref[0]) + g_hi * _swiglu(h, wgh_ref[0], wdh_ref[0])
                      + _swiglu(h, swg_ref[...], swd_ref[...]))

    @pl.when(j >= nv_ref[0])
    def _():
        y_ref[...] = jnp.zeros(y_ref.shape, F32)


def _grouped_experts(e_lo, e_hi, n_valid, rows_sorted, lw):
    n_sorted = rows_sorted.shape[0]
    nb = n_sorted // TM
    wgu, wd, swgu, swd = lw["wgu"], lw["wd"], lw["swgu"], lw["swd"]
    return pl.pallas_call(
        _moe_kernel,
        out_shape=jax.ShapeDtypeStruct((n_sorted, D_MODEL), F32),
        grid_spec=pltpu.PrefetchScalarGridSpec(
            num_scalar_prefetch=3,
            grid=(nb,),
            in_specs=[pl.BlockSpec((TM, ROW_W), lambda j, lo, hi, nv: (j, 0)),
                      pl.BlockSpec((1,) + wgu.shape[1:], lambda j, lo, hi, nv: (lo[j], 0, 0)),
                      pl.BlockSpec((1,) + wd.shape[1:], lambda j, lo, hi, nv: (lo[j], 0, 0)),
                      pl.BlockSpec((1,) + wgu.shape[1:], lambda j, lo, hi, nv: (hi[j], 0, 0)),
                      pl.BlockSpec((1,) + wd.shape[1:], lambda j, lo, hi, nv: (hi[j], 0, 0)),
                      pl.BlockSpec(swgu.shape, lambda j, lo, hi, nv: (0, 0)),
                      pl.BlockSpec(swd.shape, lambda j, lo, hi, nv: (0, 0))],
            out_specs=pl.BlockSpec((TM, D_MODEL), lambda j, lo, hi, nv: (j, 0))),
        compiler_params=_cparams(1),
        name="moe_experts",
    )(e_lo, e_hi, n_valid, rows_sorted, wgu, wd, wgu, wd, swgu, swd)


def _gather_kernel(dest_ref, y_ref, x_ref, mod_ref, gf_ref, o_ref, fbuf, sem, *, n_steps, final_norm):
    i = pl.program_id(0)
    slot = i % 2

    def issue(step, s):
        def body(r, carry):
            d = dest_ref[step * TQ + r]
            pltpu.make_async_copy(y_ref.at[pl.ds(d, 1)], fbuf.at[s, pl.ds(r, 1)], sem.at[s]).start()
            return carry
        lax.fori_loop(0, TQ, body, 0, unroll=8)

    @pl.when(i == 0)
    def _():
        issue(0, 0)

    @pl.when(i + 1 < n_steps)
    def _():
        issue(i + 1, 1 - slot)

    pltpu.make_async_copy(y_ref.at[pl.ds(0, TQ)], fbuf.at[slot], sem.at[slot]).wait()
    x = x_ref[0] + mod_ref[0, 0][5:6] * fbuf[slot]
    if final_norm:
        x = _rms(x) * gf_ref[...]
    o_ref[0] = x


def _gather_residual(dest, y_sorted, x_mid, modtab, g_final, n_ctx_blk, with_ctx, final_norm):
    bsz, t_rows, d = x_mid.shape
    nblk = t_rows // TQ
    off = 0 if with_ctx else n_ctx_blk
    n_steps = bsz * nblk
    return pl.pallas_call(
        functools.partial(_gather_kernel, n_steps=n_steps, final_norm=final_norm),
        out_shape=jax.ShapeDtypeStruct((bsz, t_rows, d), F32),
        grid_spec=pltpu.PrefetchScalarGridSpec(
            num_scalar_prefetch=1,
            grid=(n_steps,),
            in_specs=[pl.BlockSpec(memory_space=pl.ANY),
                      pl.BlockSpec((1, TQ, d), lambda i, dst: (i // nblk, i % nblk, 0)),
                      pl.BlockSpec((1, 1, 8, d),
                                   lambda i, dst: (i // nblk, jnp.where(i % nblk + off < n_ctx_blk, 1, 0), 0, 0)),
                      pl.BlockSpec(g_final.shape, lambda i, dst: (0, 0))],
            out_specs=pl.BlockSpec((1, TQ, d), lambda i, dst: (i // nblk, i % nblk, 0)),
            scratch_shapes=[pltpu.VMEM((2, TQ, d), F32), pltpu.SemaphoreType.DMA((2,))]),
        compiler_params=_cparams(1),
        name="moe_gather",
    )(dest, y_sorted, x_mid, modtab, g_final)


def _bucket_layout(meta, counts, n_tok):
    bucket = meta[:, 0, :].reshape(-1)
    rank = meta[:, 1, :].reshape(-1)
    cnt = counts[:N_BUCKETS, 0].astype(jnp.int32)
    padded = (cnt + TM - 1) // TM * TM
    pad_end = jnp.cumsum(padded)
    pad_start = pad_end - padded
    dest = pad_start[bucket] + rank
    nb = n_tok // TM + N_BUCKETS
    n_valid = pad_end[-1] // TM
    blk = jnp.arange(nb, dtype=jnp.int32)
    blk_bucket = jnp.minimum(jnp.searchsorted(pad_end, jnp.minimum(blk, n_valid - 1) * TM, side="right"),
                             N_BUCKETS - 1).astype(jnp.int32)
    grp, pair = blk_bucket // N_PAIRS, blk_bucket % N_PAIRS
    e_lo = grp * EXPERTS_PER_GROUP + jnp.asarray(_PAIR_LO, jnp.int32)[pair]
    e_hi = grp * EXPERTS_PER_GROUP + jnp.asarray(_PAIR_HI, jnp.int32)[pair]
    return dest.astype(jnp.int32), e_lo, e_hi, n_valid.reshape(1).astype(jnp.int32), nb * TM


def _rope_tables(seq, n_ctx):
    t = jnp.arange(seq)
    rows, cols = t // GRID_W, t % GRID_W

    def table(width, nf):
        lane = np.arange(width)
        half, j = lane // (2 * nf), lane % (2 * nf)
        inv = ROPE_THETA ** (-jnp.arange(nf, dtype=F32) / nf)
        pos = jnp.where(jnp.asarray(half == 0)[None, :], rows[:, None], cols[:, None]).astype(F32)
        ang = pos * inv[j % nf][None, :]
        sign = jnp.asarray(np.where(j < nf, -1.0, 1.0), F32)[None, :]
        return jnp.cos(ang), jnp.sin(ang) * sign

    cos64, sin64 = table(HEAD_DIM, 16)
    cos64, sin64 = jnp.tile(cos64, (1, 2)), jnp.tile(sin64, (1, 2))
    cosr, sinr = table(MLA_ROPE, 8)
    cosm = jnp.concatenate([jnp.ones((seq, MLA_NOPE), F32), cosr, jnp.ones((seq, 32), F32)], axis=1)
    sinm = jnp.concatenate([jnp.zeros((seq, MLA_NOPE), F32), sinr, jnp.zeros((seq, 32), F32)], axis=1)
    ctx1, ctx0 = jnp.ones((n_ctx, LANES), F32), jnp.zeros((n_ctx, LANES), F32)
    return tuple(jnp.concatenate([c, a], axis=0) for c, a in ((ctx1, cos64), (ctx0, sin64), (ctx1, cosm), (ctx0, sinm)))


def _layer_weights(l, w_in, w_out, norm_ffn_g, glb_q_gain, glb_k_gain, mla_q_gain, mla_w_uq, mla_kv_gain,
                   mla_w_ukv, exp_w_gate, exp_w_up, exp_w_down, shr_w_gate, shr_w_up, shr_w_down):
    d = w_in.shape[1]
    wi = w_in[l]
    kr = wi[:, 1664:1696]
    win = jnp.concatenate([wi[:, :1664], jnp.zeros((d, MLA_NOPE), F32), kr, jnp.zeros((d, 32), F32)], axis=1)
    uq = mla_w_uq[l].reshape(MLA_Q_RANK, MLA_HEADS, MLA_NOPE + MLA_ROPE)
    wuq = jnp.concatenate([uq, jnp.zeros((MLA_Q_RANK, MLA_HEADS, 32), F32)], axis=2).reshape(MLA_Q_RANK, 512)
    ukv = mla_w_ukv[l].reshape(MLA_KV_RANK, MLA_HEADS, MLA_NOPE + MLA_V)
    wk = jnp.concatenate([ukv[:, :, :MLA_NOPE], jnp.zeros((MLA_KV_RANK, MLA_HEADS, 64), F32)], axis=2)
    wukv = jnp.concatenate([wk.reshape(MLA_KV_RANK, 512), ukv[:, :, MLA_NOPE:].reshape(MLA_KV_RANK, 256)], axis=1)
    gain512 = jnp.concatenate([jnp.tile(glb_q_gain[l] * (HEAD_DIM ** -0.5 * LOG2E), GQA_HEADS),
                               jnp.tile(glb_k_gain[l], GQA_KV)]).reshape(1, 512)
    head = np.arange(512) // HEAD_DIM
    return {
        "win": win.astype(BF16),
        "wuq": wuq.astype(BF16),
        "wukv": wukv.astype(BF16),
        "gain512": gain512,
        "mqg": mla_q_gain[l].reshape(1, -1),
        "mkvg": mla_kv_gain[l].reshape(1, -1),
        "bd": jnp.asarray(head[:, None] == head[None, :], BF16),
        "wout": w_out[l].astype(BF16),
        "g_ffn": norm_ffn_g[l].reshape(1, -1),
        "wgu": jnp.concatenate([exp_w_gate[l], exp_w_up[l]], axis=2).astype(BF16),
        "wd": exp_w_down[l].astype(BF16),
        "swgu": jnp.concatenate([shr_w_gate[l], shr_w_up[l]], axis=1).astype(BF16),
        "swd": shr_w_down[l].astype(BF16),
    }


def kernel(x, c, ctx, c_ctx, w_mod, b_mod, norm_mix_g, norm_ffn_g, w_in, w_out, swa_sink, glb_q_gain, glb_k_gain,
           mla_q_gain, mla_w_uq, mla_kv_gain, mla_w_ukv, router_w, router_bias, exp_w_gate, exp_w_up, exp_w_down,
           shr_w_gate, shr_w_up, shr_w_down, final_norm_g):
    bsz, seq, d = x.shape
    n_ctx = ctx.shape[1]
    n_layers = w_mod.shape[0]
    assert d == D_MODEL and n_ctx % TQ == 0 and seq % TQ == 0 and seq >= TQ + 2 * WINDOW and seq % GRID_W == 0
    n_ctx_blk = n_ctx // TQ

    rows = -(-(bsz + 1) // 8) * 8
    c_rows = jnp.concatenate([c, c_ctx[None, :], jnp.zeros((rows - bsz - 1, d), F32)], axis=0)
    mods = _modulation(c_rows, w_mod, b_mod)
    mod_x = mods[:, :bsz].reshape(n_layers, bsz, 6, d)
    mod_c = jnp.broadcast_to(mods[:, bsz].reshape(n_layers, 1, 6, d), (n_layers, bsz, 6, d))
    modtabs = jnp.pad(jnp.stack([mod_x, mod_c], axis=2), ((0, 0), (0, 0), (0, 0), (0, 2), (0, 0)))

    tabs = _rope_tables(seq, n_ctx)
    rw = jnp.pad(router_w.T, ((0, 32 - N_EXPERTS), (0, 0)))
    rw_hi = rw.astype(BF16)
    shared = {
        "rw_hi": rw_hi,
        "rw_lo": (rw - rw_hi.astype(F32)).astype(BF16),
        "rb": jnp.pad(router_bias, (0, 32 - N_EXPERTS)).reshape(32, 1),
        "tri": jnp.asarray(np.arange(TQ)[:, None] <= np.arange(TQ)[None, :], BF16),
    }
    g_final = final_norm_g.reshape(1, d)

    xs = jnp.concatenate([ctx, x], axis=1)
    for l in range(n_layers):
        last = l == n_layers - 1
        with_ctx = not last
        lw = _layer_weights(l, w_in, w_out, norm_ffn_g, glb_q_gain, glb_k_gain, mla_q_gain, mla_w_uq,
                            mla_kv_gain, mla_w_ukv, exp_w_gate, exp_w_up, exp_w_down,
                            shr_w_gate, shr_w_up, shr_w_down)
        modtab = modtabs[l]
        qs, ks, vs, qg, kg, vg, qm, km, vm = _input_projection(
            xs, modtab, norm_mix_g[l].reshape(1, d), lw, tabs, n_ctx_blk)
        sink = jnp.pad(swa_sink[l] * LOG2E, (0, 8 - GQA_HEADS))
        o_swa = _attention("swa", qs, ks, vs, sink, n_ctx, with_ctx)
        o_glb = _attention("glb", qg, kg, vg, None, n_ctx, with_ctx)
        o_mla = _attention("mla", qm, km, vm, None, n_ctx, with_ctx)
        x_mid, rows_tok, meta, counts = _output_projection(o_swa, o_glb, o_mla, xs, modtab, lw, shared,
                                                           n_ctx_blk, with_ctx)
        n_tok = rows_tok.shape[0]
        dest, e_lo, e_hi, n_valid, n_sorted = _bucket_layout(meta, counts, n_tok)
        rows_sorted = _scatter_rows(dest, rows_tok, n_sorted)
        y_sorted = _grouped_experts(e_lo, e_hi, n_valid, rows_sorted, lw)
        xs = _gather_residual(dest, y_sorted, x_mid, modtab, g_final, n_ctx_blk, with_ctx, final_norm=last)
    return xs
```

```python
import functools

import jax
import jax.numpy as jnp
import numpy as np
from jax import lax
from jax.experimental import pallas as pl
from jax.experimental.pallas import tpu as pltpu

F32 = jnp.float32
BF16 = jnp.bfloat16

D_MODEL = 1024
GRID_W = 64
HEAD_DIM = 64
GQA_HEADS = 6
GQA_KV = 2
GQA_G = GQA_HEADS // GQA_KV
WINDOW = 128
MLA_HEADS = 4
MLA_NOPE = 64
MLA_ROPE = 32
MLA_V = 64
MLA_Q_RANK = 256
MLA_KV_RANK = 128
ROPE_THETA = 10000.0
N_EXPERTS = 16
N_GROUPS = 4
EXPERTS_PER_GROUP = 4
N_PAIRS = 6
N_BUCKETS = N_GROUPS * N_PAIRS
D_EXPERT = 512
EPS = 1e-6
NEG_INF = -1e30

LANES = 128
TQ = 256
TK = 256
TM = 256
SWA_SLAB = 128
V_ROWS = 80
LOG2E = 1.4426950408889634
ROW_W = D_MODEL + LANES
IN_W = 1792
VMEM_LIMIT = 56 * 1024 * 1024

_SQ, _SK, _SV, _GQ, _GK, _GV, _MQ, _MKV, _KR = 0, 384, 512, 640, 1024, 1152, 1280, 1536, 1664

_PAIR_LO = (0, 0, 0, 1, 1, 2)
_PAIR_HI = (1, 2, 3, 2, 3, 3)


def _cparams(n_axes):
    return pltpu.CompilerParams(dimension_semantics=("arbitrary",) * n_axes,
                                vmem_limit_bytes=VMEM_LIMIT)


def _mod_kernel(c_ref, w_ref, b_ref, o_ref):
    c = c_ref[...]
    a = (c * jax.nn.sigmoid(c)).astype(BF16)
    o_ref[0] = jnp.dot(a, w_ref[0].astype(BF16), preferred_element_type=F32) + b_ref[0]


def _modulation(c_rows, w_mod, b_mod):
    n_layers, d, width = w_mod.shape
    rows = c_rows.shape[0]
    nb = 1536
    return pl.pallas_call(
        _mod_kernel,
        out_shape=jax.ShapeDtypeStruct((n_layers, rows, width), F32),
        grid=(n_layers, width // nb),
        in_specs=[pl.BlockSpec((rows, d), lambda l, j: (0, 0)),
                  pl.BlockSpec((1, d, nb), lambda l, j: (l, 0, j)),
                  pl.BlockSpec((1, 1, nb), lambda l, j: (l, 0, j))],
        out_specs=pl.BlockSpec((1, rows, nb), lambda l, j: (l, 0, j)),
        compiler_params=_cparams(2),
        name="adaln_mod",
    )(c_rows, w_mod, b_mod.reshape(n_layers, 1, width))


def _rope(x, cos, sin_signed, nf):
    lane = lax.broadcasted_iota(jnp.int32, (1, LANES), 1)
    first = (lane % (2 * nf)) < nf
    tiles = []
    for t in range(x.shape[1] // LANES):
        xt = x[:, t * LANES:(t + 1) * LANES]
        partner = jnp.where(first, pltpu.roll(xt, LANES - nf, 1), pltpu.roll(xt, nf, 1))
        tiles.append(xt * cos + partner * sin_signed)
    return tiles[0] if len(tiles) == 1 else jnp.concatenate(tiles, axis=1)


def _group_padded(q):
    lane = lax.broadcasted_iota(jnp.int32, (1, LANES), 1)
    tiles = []
    for h in range(GQA_HEADS):
        g = h // GQA_G
        tile = q[:, (h // 2) * LANES:(h // 2 + 1) * LANES]
        if (h % 2) != g:
            tile = pltpu.roll(tile, HEAD_DIM, 1)
        tiles.append(jnp.where((lane // HEAD_DIM) == g, tile, 0.0))
    return jnp.concatenate(tiles, axis=1)


def _emit_q_t(ref, q):
    for h in range(q.shape[1] // LANES):
        ref[0, h] = q[:, h * LANES:(h + 1) * LANES].T.astype(ref.dtype)


def _emit_v_t(ref, v, slab):
    ones_blk = jnp.where(lax.broadcasted_iota(jnp.int32, (V_ROWS - HEAD_DIM, TQ), 0) == 0, 1.0, 0.0)
    parts = []
    for t in range(v.shape[1] // LANES):
        vt = v[:, t * LANES:(t + 1) * LANES].T
        parts += [vt[:HEAD_DIM], ones_blk, vt[HEAD_DIM:], ones_blk]
    ext = jnp.concatenate(parts, axis=0).astype(ref.dtype)
    for s in range(TQ // slab):
        ref[0, s] = ext[:, s * slab:(s + 1) * slab]


def _rms(x, eps=EPS):
    return x * lax.rsqrt(jnp.mean(x * x, axis=-1, keepdims=True) + eps)


def _in_kernel(x_ref, mod_ref, g_ref, win_ref, cos64_ref, sin64_ref, cosm_ref, sinm_ref,
               gain_ref, mqg_ref, mkvg_ref, wuq_ref, wukv_ref, bd_ref,
               qs_ref, ks_ref, vs_ref, qg_ref, kg_ref, vg_ref, qm_ref, km_ref, vm_ref, *, mla_scale):
    x = x_ref[0]
    mod = mod_ref[0, 0]
    h = _rms(x) * g_ref[...]
    h = h * (1.0 + mod[1:2]) + mod[0:1]
    z = jnp.dot(h.astype(BF16), win_ref[...], preferred_element_type=F32)
    cos64, sin64 = cos64_ref[...], sin64_ref[...]
    cosm, sinm = cosm_ref[...], sinm_ref[...]

    sqk = _rope(z[:, _SQ:_SV], cos64, sin64, 16)
    _emit_q_t(qs_ref, _group_padded(sqk[:, :384] * (HEAD_DIM ** -0.5 * LOG2E)))
    ks_ref[0] = sqk[:, 384:].astype(BF16)
    _emit_v_t(vs_ref, z[:, _SV:_GQ], SWA_SLAB)

    gqk = z[:, _GQ:_GV]
    sq = gqk * gqk
    hi = sq.astype(BF16)
    lo = (sq - hi.astype(F32)).astype(BF16)
    ssum = (jnp.dot(hi, bd_ref[...], preferred_element_type=F32)
            + jnp.dot(lo, bd_ref[...], preferred_element_type=F32))
    gqk = gqk * lax.rsqrt(ssum * (1.0 / HEAD_DIM) + EPS) * gain_ref[...]
    gqk = _rope(gqk, cos64, sin64, 16)
    _emit_q_t(qg_ref, _group_padded(gqk[:, :384]))
    kg_ref[0] = gqk[:, 384:].astype(BF16)
    _emit_v_t(vg_ref, z[:, _GV:_MQ], TK)

    qn = _rms(z[:, _MQ:_MKV]) * mqg_ref[...]
    mq = jnp.dot(qn.astype(BF16), wuq_ref[...], preferred_element_type=F32)
    _emit_q_t(qm_ref, _rope(mq, cosm, sinm, 8) * (mla_scale * LOG2E))
    kvn = _rms(z[:, _MKV:_KR]) * mkvg_ref[...]
    mkv = jnp.dot(kvn.astype(BF16), wukv_ref[...], preferred_element_type=F32)
    kr = _rope(z[:, _KR:IN_W], cosm, sinm, 8)
    km_ref[0] = (mkv[:, :512] + jnp.concatenate([kr] * MLA_HEADS, axis=1)).astype(BF16)
    _emit_v_t(vm_ref, mkv[:, 512:], TK)


def _input_projection(xs, modtab, g_mix, lw, tabs, n_ctx_blk):
    bsz, t_all, d = xs.shape
    nblk = t_all // TQ
    tok = lambda w: pl.BlockSpec((1, TQ, w), lambda b, i: (b, i, 0))
    q_t = lambda nh: pl.BlockSpec((1, nh, LANES, TQ), lambda b, i: (b, 0, 0, i))
    v_t = lambda nh, slab: pl.BlockSpec((1, TQ // slab, nh * V_ROWS, slab), lambda b, i: (b, i, 0, 0))
    full = lambda a: pl.BlockSpec(a.shape, lambda b, i: (0,) * a.ndim)
    tab = pl.BlockSpec((TQ, LANES), lambda b, i: (i, 0))
    q_shape = lambda nh: jax.ShapeDtypeStruct((bsz, nh, LANES, t_all), BF16)
    k_shape = lambda w: jax.ShapeDtypeStruct((bsz, t_all, w), BF16)
    v_shape = lambda nh, slab: jax.ShapeDtypeStruct((bsz, t_all // slab, nh * V_ROWS, slab), BF16)
    consts = (lw["gain512"], lw["mqg"], lw["mkvg"], lw["wuq"], lw["wukv"], lw["bd"])
    return pl.pallas_call(
        functools.partial(_in_kernel, mla_scale=float((MLA_NOPE + MLA_ROPE) ** -0.5)),
        out_shape=(q_shape(GQA_HEADS), k_shape(128), v_shape(GQA_KV, SWA_SLAB),
                   q_shape(GQA_HEADS), k_shape(128), v_shape(GQA_KV, TK),
                   q_shape(MLA_HEADS), k_shape(512), v_shape(MLA_HEADS, TK)),
        grid=(bsz, nblk),
        in_specs=[tok(d),
                  pl.BlockSpec((1, 1, 8, d), lambda b, i: (b, jnp.where(i < n_ctx_blk, 1, 0), 0, 0)),
                  full(g_mix), full(lw["win"]), tab, tab, tab, tab] + [full(a) for a in consts],
        out_specs=(q_t(GQA_HEADS), tok(128), v_t(GQA_KV, SWA_SLAB),
                   q_t(GQA_HEADS), tok(128), v_t(GQA_KV, TK),
                   q_t(MLA_HEADS), tok(512), v_t(MLA_HEADS, TK)),
        compiler_params=_cparams(2),
        name="in_proj",
    )(xs, modtab, g_mix, lw["win"], *tabs, *consts)


def _attn_kernel(*refs, kind, n_ctx, n_ctx_blk, q_blk_off, seq):
    if kind == "swa":
        sink_ref, q_ref, k_ref, v_ref, o_ref, m_ref, acc_ref = refs
    else:
        q_ref, k_ref, v_ref, o_ref, m_ref, acc_ref = refs[:6]
        s_refs, cm_refs, p_refs, al_refs = refs[6:9], refs[9:12], refs[12:14], refs[14:16]
    blk = pl.program_id(1) + q_blk_off
    is_lat = blk >= n_ctx_blk
    gqa = kind in ("swa", "glb")
    n_heads = GQA_HEADS if gqa else MLA_HEADS
    n_units = GQA_KV if gqa else MLA_HEADS
    unit_w = n_heads // n_units * TQ

    def update(k_rows, v_t, mask):
        if gqa:
            q_all = jnp.concatenate([q_ref[0, h] for h in range(n_heads)], axis=1)
            s = jnp.dot(k_rows, q_all, preferred_element_type=F32)
        else:
            s = jnp.concatenate([jnp.dot(k_rows[:, h * LANES:(h + 1) * LANES], q_ref[0, h],
                                         preferred_element_type=F32) for h in range(n_heads)], axis=1)
        if mask is not None:
            s = jnp.where(mask, s, NEG_INF)
        m_prev = m_ref[...]
        m_new = jnp.maximum(m_prev, jnp.max(s, axis=0, keepdims=True))
        alpha = jnp.exp2(m_prev - m_new)
        p = jnp.exp2((s - m_new).astype(BF16))
        m_ref[...] = m_new
        for u in range(n_units):
            lanes = slice(u * unit_w, (u + 1) * unit_w)
            acc_ref[u] = (acc_ref[u] * alpha[:, lanes]
                          + jnp.dot(v_t[u * V_ROWS:(u + 1) * V_ROWS], p[:, lanes], preferred_element_type=F32))

    if kind == "swa":
        m_ref[...] = jnp.concatenate([jnp.full((1, TQ), sink_ref[h], F32) for h in range(n_heads)], axis=1)
        l_row = jnp.where(lax.broadcasted_iota(jnp.int32, (V_ROWS, unit_w), 0) == HEAD_DIM, 1.0, 0.0)
        for u in range(n_units):
            acc_ref[u] = l_row
        n_slab = n_ctx // SWA_SLAB
        update(k_ref[0, 0:n_ctx, :], jnp.concatenate([v_ref[0, s] for s in range(n_slab)], axis=1), None)

        @pl.when(is_lat)
        def _():
            nk = TQ + 2 * WINDOW
            q0 = (blk - n_ctx_blk) * TQ
            k0 = pl.multiple_of(jnp.clip(q0 - WINDOW, 0, seq - nk), WINDOW)
            kpos = k0 + lax.broadcasted_iota(jnp.int32, (nk, n_heads * TQ), 0)
            qpos = q0 + (lax.broadcasted_iota(jnp.int32, (nk, n_heads * TQ), 1) & (TQ - 1))
            s0 = (n_ctx + k0) // SWA_SLAB
            v_t = jnp.concatenate([v_ref[0, s0 + s] for s in range(nk // SWA_SLAB)], axis=1)
            update(k_ref[0, pl.ds(pl.multiple_of(n_ctx + k0, WINDOW), nk), :], v_t,
                   jnp.abs(qpos - kpos) <= WINDOW)
    else:
        m_ref[...] = jnp.full(m_ref.shape, NEG_INF, F32)
        acc_ref[...] = jnp.zeros(acc_ref.shape, F32)

        def qk(c, par, h):
            lanes = slice(h * TQ, (h + 1) * TQ)
            k_cols = slice(0, LANES) if gqa else slice(h * LANES, (h + 1) * LANES)
            k_rows = k_ref[0, pl.ds(pl.multiple_of(c * TK, TK), TK), k_cols]
            s = jnp.dot(k_rows, q_ref[0, h], preferred_element_type=F32)
            s_refs[par][:, lanes] = s
            cm_refs[par][:, lanes] = jnp.max(s, axis=0, keepdims=True)

        def softmax(sb, par, h):
            lanes = slice(h * TQ, (h + 1) * TQ)
            m_prev = m_ref[:, lanes]
            m_new = jnp.maximum(m_prev, cm_refs[sb][:, lanes])
            m_ref[:, lanes] = m_new
            al_refs[par][:, lanes] = jnp.exp2(m_prev - m_new)
            p_refs[par][:, lanes] = jnp.exp2((s_refs[sb][:, lanes] - m_new).astype(BF16))

        def pv(c, par, h):
            lanes = slice(h * TQ, (h + 1) * TQ)
            u, j = divmod(h, n_heads // n_units)
            ul = slice(j * TQ, (j + 1) * TQ)
            acc_ref[u, :, ul] = (acc_ref[u, :, ul] * al_refs[par][:, lanes]
                                 + jnp.dot(v_ref[0, c, u * V_ROWS:(u + 1) * V_ROWS, :], p_refs[par][:, lanes],
                                           preferred_element_type=F32))

        def step(i, r, n):
            for h in range(n_heads):
                if not isinstance(i, int) or i + 2 < n:
                    qk(i + 2, (r + 2) % 3, h)
                if not isinstance(i, int) or i < n:
                    softmax(r % 3, r % 2, h)
                if not isinstance(i, int) or i >= 1:
                    pv(i - 1, (r + 1) % 2, h)

        def run(n):
            for c in range(min(2, n)):
                for h in range(n_heads):
                    qk(c, c, h)
            step(0, 0, n)
            n_groups = max(0, (n - 3) // 6)

            def body(j, carry):
                for r in range(6):
                    step(1 + 6 * j + r, (1 + r) % 6, n)
                return carry
            if n_groups:
                lax.fori_loop(0, n_groups, body, 0)
            for i in range(1 + 6 * n_groups, n + 1):
                step(i, i % 6, n)

        if q_blk_off < n_ctx_blk:
            pl.when(jnp.logical_not(is_lat))(lambda: run(n_ctx // TK))
            pl.when(is_lat)(lambda: run((n_ctx + seq) // TK))
        else:
            run((n_ctx + seq) // TK)

    heads = []
    for u in range(n_units):
        a = acc_ref[u]
        o = a[:HEAD_DIM] / a[HEAD_DIM:HEAD_DIM + 1]
        heads += [o[:, j * TQ:(j + 1) * TQ] for j in range(unit_w // TQ)]
    tiles = [jnp.concatenate(heads[2 * t:2 * t + 2], axis=0).T for t in range(n_heads // 2)]
    o_ref[0] = jnp.concatenate(tiles, axis=1).astype(o_ref.dtype)


def _attention(kind, q_t, k, v_t, sink, n_ctx, with_ctx_queries):
    bsz, n_heads, _, t_all = q_t.shape
    seq = t_all - n_ctx
    n_ctx_blk = n_ctx // TQ
    q_blk_off = 0 if with_ctx_queries else n_ctx_blk
    nblk = t_all // TQ - q_blk_off
    n_units = GQA_KV if kind in ("swa", "glb") else MLA_HEADS
    wo = n_heads * HEAD_DIM
    kern = functools.partial(_attn_kernel, kind=kind, n_ctx=n_ctx, n_ctx_blk=n_ctx_blk,
                             q_blk_off=q_blk_off, seq=seq)
    in_specs = [pl.BlockSpec((1, n_heads, LANES, TQ), lambda b, i: (b, 0, 0, i + q_blk_off)),
                pl.BlockSpec((1,) + k.shape[1:], lambda b, i: (b, 0, 0)),
                pl.BlockSpec((1,) + v_t.shape[1:], lambda b, i: (b, 0, 0, 0))]
    args = [q_t, k, v_t]
    if kind == "swa":
        in_specs = [pl.BlockSpec(memory_space=pltpu.SMEM)] + in_specs
        args = [sink] + args
    nq = n_heads * TQ
    scratch = [pltpu.VMEM((1, nq), F32), pltpu.VMEM((n_units, V_ROWS, nq // n_units), F32)]
    if kind != "swa":
        scratch += ([pltpu.VMEM((TK, nq), F32)] * 3 + [pltpu.VMEM((1, nq), F32)] * 3
                    + [pltpu.VMEM((TK, nq), BF16)] * 2 + [pltpu.VMEM((1, nq), F32)] * 2)
    return pl.pallas_call(
        kern,
        out_shape=jax.ShapeDtypeStruct((bsz, nblk * TQ, wo), BF16),
        grid=(bsz, nblk),
        in_specs=in_specs,
        out_specs=pl.BlockSpec((1, TQ, wo), lambda b, i: (b, i, 0)),
        scratch_shapes=scratch,
        compiler_params=_cparams(2),
        name="attn_" + kind,
    )(*args)


def _row_select(rows, idx):
    out = rows[0]
    for j in range(1, len(rows)):
        out = jnp.where(idx == j, rows[j], out)
    return out


def _route(scores, biased):
    def top2sum(a, b, c, d):
        hi1, lo1, hi2, lo2 = jnp.maximum(a, b), jnp.minimum(a, b), jnp.maximum(c, d), jnp.minimum(c, d)
        return jnp.maximum(hi1, hi2) + jnp.maximum(jnp.minimum(hi1, hi2), jnp.maximum(lo1, lo2))

    gs = [top2sum(*biased[4 * g:4 * g + 4]) for g in range(N_GROUPS)]
    best, gi = gs[0], jnp.zeros(gs[0].shape, jnp.int32)
    for g in range(1, N_GROUPS):
        better = gs[g] > best
        gi = jnp.where(better, g, gi)
        best = jnp.where(better, gs[g], best)
    a = [_row_select([biased[4 * g + j] for g in range(N_GROUPS)], gi) for j in range(EXPERTS_PER_GROUP)]
    s = [_row_select([scores[4 * g + j] for g in range(N_GROUPS)], gi) for j in range(EXPERTS_PER_GROUP)]
    v1, i1 = a[0], jnp.zeros(gi.shape, jnp.int32)
    for j in range(1, EXPERTS_PER_GROUP):
        better = a[j] > v1
        i1 = jnp.where(better, j, i1)
        v1 = jnp.where(better, a[j], v1)
    v2, i2 = jnp.full(v1.shape, -3.0e38, F32), jnp.zeros(gi.shape, jnp.int32)
    for j in range(EXPERTS_PER_GROUP):
        better = (i1 != j) & (a[j] > v2)
        i2 = jnp.where(better, j, i2)
        v2 = jnp.where(better, a[j], v2)
    lo, hi = jnp.minimum(i1, i2), jnp.maximum(i1, i2)
    pair = jnp.where(lo == 0, hi - 1, jnp.where(lo == 1, hi + 1, 5))
    s_lo, s_hi = _row_select(s, lo), _row_select(s, hi)
    den = s_lo + s_hi
    return gi * N_PAIRS + pair, s_lo / den, s_hi / den


def _out_kernel(os_ref, og_ref, om_ref, x_ref, mod_ref, wout_ref, g_ref, rwh_ref, rwl_ref, rb_ref, tri_ref,
                xo_ref, row_ref, meta_ref, cnt_ref, carry_ref):
    @pl.when((pl.program_id(0) == 0) & (pl.program_id(1) == 0))
    def _():
        carry_ref[...] = jnp.zeros(carry_ref.shape, F32)

    y = jnp.concatenate([os_ref[0], og_ref[0], om_ref[0]], axis=1)
    mod = mod_ref[0, 0]
    x = x_ref[0] + mod[2:3] * jnp.dot(y, wout_ref[...], preferred_element_type=F32)
    xo_ref[0] = x
    h = _rms(x) * g_ref[...]
    h = h * (1.0 + mod[4:5]) + mod[3:4]

    hh = h.astype(BF16)
    hl = (h - hh.astype(F32)).astype(BF16)
    dn = (((1,), (1,)), ((), ()))
    logits = (lax.dot_general(rwh_ref[...], hh, dn, preferred_element_type=F32)
              + lax.dot_general(rwh_ref[...], hl, dn, preferred_element_type=F32)
              + lax.dot_general(rwl_ref[...], hh, dn, preferred_element_type=F32))
    sc = jax.nn.sigmoid(logits)
    bs = sc + rb_ref[...]
    scores = [sc[e:e + 1, :] for e in range(N_EXPERTS)]
    biased = [bs[e:e + 1, :] for e in range(N_EXPERTS)]
    bucket, g_lo, g_hi = _route(scores, biased)

    onehot = jnp.where(lax.broadcasted_iota(jnp.int32, (32, TQ), 0) == bucket, 1.0, 0.0)
    prefix = jnp.dot(onehot.astype(BF16), tri_ref[...], preferred_element_type=F32)
    carry = carry_ref[:, 0:1]
    rank = jnp.sum(onehot * (carry + prefix - 1.0), axis=0, keepdims=True)
    carry_new = jnp.broadcast_to(carry + prefix[:, TQ - 1:TQ], carry_ref.shape)
    carry_ref[...] = carry_new
    cnt_ref[...] = carry_new
    meta_ref[0] = jnp.concatenate([bucket, rank.astype(jnp.int32), jnp.zeros((6, TQ), jnp.int32)], axis=0)

    gates = jnp.concatenate([g_lo, g_hi, jnp.zeros((LANES - 2, TQ), F32)], axis=0)
    row_ref[...] = jnp.concatenate([h, gates.T], axis=1)


def _output_projection(o_swa, o_glb, o_mla, xs, modtab, lw, shared, n_ctx_blk, with_ctx):
    bsz, t_all, d = xs.shape
    off = 0 if with_ctx else n_ctx_blk
    nblk = t_all // TQ - off
    tok = lambda w: pl.BlockSpec((1, TQ, w), lambda b, i: (b, i, 0))
    full = lambda a: pl.BlockSpec(a.shape, lambda b, i: (0,) * a.ndim)
    consts = (lw["wout"], lw["g_ffn"], shared["rw_hi"], shared["rw_lo"], shared["rb"], shared["tri"])
    return pl.pallas_call(
        _out_kernel,
        out_shape=(jax.ShapeDtypeStruct((bsz, nblk * TQ, d), F32),
                   jax.ShapeDtypeStruct((bsz * nblk * TQ, ROW_W), F32),
                   jax.ShapeDtypeStruct((bsz * nblk, 8, TQ), jnp.int32),
                   jax.ShapeDtypeStruct((32, LANES), F32)),
        grid=(bsz, nblk),
        in_specs=[tok(o_swa.shape[2]), tok(o_glb.shape[2]), tok(o_mla.shape[2]),
                  pl.BlockSpec((1, TQ, d), lambda b, i: (b, i + off, 0)),
                  pl.BlockSpec((1, 1, 8, d), lambda b, i: (b, jnp.where(i + off < n_ctx_blk, 1, 0), 0, 0))]
                 + [full(a) for a in consts],
        out_specs=(tok(d),
                   pl.BlockSpec((TQ, ROW_W), lambda b, i: (b * nblk + i, 0)),
                   pl.BlockSpec((1, 8, TQ), lambda b, i: (b * nblk + i, 0, 0)),
                   pl.BlockSpec((32, LANES), lambda b, i: (0, 0))),
        scratch_shapes=[pltpu.VMEM((32, LANES), F32)],
        compiler_params=_cparams(2),
        name="out_proj_router",
    )(o_swa, o_glb, o_mla, xs, modtab, *consts)


def _scatter_kernel(dest_ref, row_ref, init_ref, out_ref, buf, sem, *, n_steps):
    del init_ref
    i = pl.program_id(0)
    slot = i % 2

    def wait_slot(s):
        pltpu.make_async_copy(buf.at[s], out_ref.at[pl.ds(0, TQ)], sem.at[s]).wait()

    @pl.when(i >= 2)
    def _():
        wait_slot(slot)

    buf[slot] = row_ref[...]

    def body(r, carry):
        d = dest_ref[i * TQ + r]
        pltpu.make_async_copy(buf.at[slot, pl.ds(r, 1)], out_ref.at[pl.ds(d, 1)], sem.at[slot]).start()
        return carry
    lax.fori_loop(0, TQ, body, 0, unroll=32)

    @pl.when(i == n_steps - 1)
    def _():
        wait_slot(slot)
        if n_steps >= 2:
            wait_slot(1 - slot)


def _scatter_rows(dest, rows, n_sorted):
    n_tok = rows.shape[0]
    n_steps = n_tok // TQ
    init = jnp.zeros((n_sorted, ROW_W), F32)
    return pl.pallas_call(
        functools.partial(_scatter_kernel, n_steps=n_steps),
        out_shape=jax.ShapeDtypeStruct((n_sorted, ROW_W), F32),
        grid_spec=pltpu.PrefetchScalarGridSpec(
            num_scalar_prefetch=1,
            grid=(n_steps,),
            in_specs=[pl.BlockSpec((TQ, ROW_W), lambda i, d: (i, 0)),
                      pl.BlockSpec(memory_space=pl.ANY)],
            out_specs=pl.BlockSpec(memory_space=pl.ANY),
            scratch_shapes=[pltpu.VMEM((2, TQ, ROW_W), F32), pltpu.SemaphoreType.DMA((2,))]),
        input_output_aliases={2: 0},
        compiler_params=_cparams(1),
        name="moe_scatter",
    )(dest, rows, init)


def _swiglu(h, wgu, wd):
    u = jnp.dot(h, wgu, preferred_element_type=F32)
    a = u[:, :D_EXPERT]
    a = a * jax.nn.sigmoid(a) * u[:, D_EXPERT:]
    return jnp.dot(a.astype(BF16), wd, preferred_element_type=F32)


def _moe_kernel(elo_ref, ehi_ref, nv_ref, row_ref, wgl_ref, wdl_ref, wgh_ref, wdh_ref, swg_ref, swd_ref, y_ref):
    del elo_ref, ehi_ref
    j = pl.program_id(0)

    @pl.when(j < nv_ref[0])
    def _():
        rows = row_ref[...]
        h = rows[:, :D_MODEL].astype(BF16)
        g_lo = rows[:, D_MODEL:D_MODEL + 1]
        g_hi = rows[:, D_MODEL + 1:D_MODEL + 2]
        y_ref[...] = (g_lo * _swiglu(h, wgl_ref[0], wdl_ref[0]) + g_hi * _swiglu(h, wgh_ref[0], wdh_ref[0])
                      + _swiglu(h, swg_ref[...], swd_ref[...]))

    @pl.when(j >= nv_ref[0])
    def _():
        y_ref[...] = jnp.zeros(y_ref.shape, F32)


def _grouped_experts(e_lo, e_hi, n_valid, rows_sorted, lw):
    n_sorted = rows_sorted.shape[0]
    nb = n_sorted // TM
    wgu, wd, swgu, swd = lw["wgu"], lw["wd"], lw["swgu"], lw["swd"]
    return pl.pallas_call(
        _moe_kernel,
        out_shape=jax.ShapeDtypeStruct((n_sorted, D_MODEL), F32),
        grid_spec=pltpu.PrefetchScalarGridSpec(
            num_scalar_prefetch=3,
            grid=(nb,),
            in_specs=[pl.BlockSpec((TM, ROW_W), lambda j, lo, hi, nv: (j, 0)),
                      pl.BlockSpec((1,) + wgu.shape[1:], lambda j, lo, hi, nv: (lo[j], 0, 0)),
                      pl.BlockSpec((1,) + wd.shape[1:], lambda j, lo, hi, nv: (lo[j], 0, 0)),
                      pl.BlockSpec((1,) + wgu.shape[1:], lambda j, lo, hi, nv: (hi[j], 0, 0)),
                      pl.BlockSpec((1,) + wd.shape[1:], lambda j, lo, hi, nv: (hi[j], 0, 0)),
                      pl.BlockSpec(swgu.shape, lambda j, lo, hi, nv: (0, 0)),
                      pl.BlockSpec(swd.shape, lambda j, lo, hi, nv: (0, 0))],
            out_specs=pl.BlockSpec((TM, D_MODEL), lambda j, lo, hi, nv: (j, 0))),
        compiler_params=_cparams(1),
        name="moe_experts",
    )(e_lo, e_hi, n_valid, rows_sorted, wgu, wd, wgu, wd, swgu, swd)


def _gather_kernel(dest_ref, y_ref, x_ref, mod_ref, gf_ref, o_ref, fbuf, sem, *, n_steps, final_norm):
    i = pl.program_id(0)
    slot = i % 2

    def issue(step, s):
        def body(r, carry):
            d = dest_ref[step * TQ + r]
            pltpu.make_async_copy(y_ref.at[pl.ds(d, 1)], fbuf.at[s, pl.ds(r, 1)], sem.at[s]).start()
            return carry
        lax.fori_loop(0, TQ, body, 0, unroll=32)

    @pl.when(i == 0)
    def _():
        issue(0, 0)

    @pl.when(i + 1 < n_steps)
    def _():
        issue(i + 1, 1 - slot)

    pltpu.make_async_copy(y_ref.at[pl.ds(0, TQ)], fbuf.at[slot], sem.at[slot]).wait()
    x = x_ref[0] + mod_ref[0, 0][5:6] * fbuf[slot]
    if final_norm:
        x = _rms(x) * gf_ref[...]
    o_ref[0] = x


def _gather_residual(dest, y_sorted, x_mid, modtab, g_final, n_ctx_blk, with_ctx, final_norm):
    bsz, t_rows, d = x_mid.shape
    nblk = t_rows // TQ
    off = 0 if with_ctx else n_ctx_blk
    n_steps = bsz * nblk
    return pl.pallas_call(
        functools.partial(_gather_kernel, n_steps=n_steps, final_norm=final_norm),
        out_shape=jax.ShapeDtypeStruct((bsz, t_rows, d), F32),
        grid_spec=pltpu.PrefetchScalarGridSpec(
            num_scalar_prefetch=1,
            grid=(n_steps,),
            in_specs=[pl.BlockSpec(memory_space=pl.ANY),
                      pl.BlockSpec((1, TQ, d), lambda i, dst: (i // nblk, i % nblk, 0)),
                      pl.BlockSpec((1, 1, 8, d),
                                   lambda i, dst: (i // nblk, jnp.where(i % nblk + off < n_ctx_blk, 1, 0), 0, 0)),
                      pl.BlockSpec(g_final.shape, lambda i, dst: (0, 0))],
            out_specs=pl.BlockSpec((1, TQ, d), lambda i, dst: (i // nblk, i % nblk, 0)),
            scratch_shapes=[pltpu.VMEM((2, TQ, d), F32), pltpu.SemaphoreType.DMA((2,))]),
        compiler_params=_cparams(1),
        name="moe_gather",
    )(dest, y_sorted, x_mid, modtab, g_final)


def _bucket_layout(meta, counts, n_tok):
    bucket = meta[:, 0, :].reshape(-1)
    rank = meta[:, 1, :].reshape(-1)
    cnt = counts[:N_BUCKETS, 0].astype(jnp.int32)
    padded = (cnt + TM - 1) // TM * TM
    pad_end = jnp.cumsum(padded)
    pad_start = pad_end - padded
    dest = pad_start[bucket] + rank
    nb = n_tok // TM + N_BUCKETS
    n_valid = pad_end[-1] // TM
    blk = jnp.arange(nb, dtype=jnp.int32)
    blk_row = jnp.minimum(blk, n_valid - 1) * TM
    blk_bucket = jnp.minimum(jnp.sum(pad_end[None, :] <= blk_row[:, None], axis=1), N_BUCKETS - 1).astype(jnp.int32)
    grp, pair = blk_bucket // N_PAIRS, blk_bucket % N_PAIRS
    e_lo = grp * EXPERTS_PER_GROUP + jnp.asarray(_PAIR_LO, jnp.int32)[pair]
    e_hi = grp * EXPERTS_PER_GROUP + jnp.asarray(_PAIR_HI, jnp.int32)[pair]
    return dest.astype(jnp.int32), e_lo, e_hi, n_valid.reshape(1).astype(jnp.int32), nb * TM


def _rope_tables(seq, n_ctx):
    t = jnp.arange(seq)
    rows, cols = t // GRID_W, t % GRID_W

    def table(width, nf):
        lane = np.arange(width)
        half, j = lane // (2 * nf), lane % (2 * nf)
        inv = ROPE_THETA ** (-jnp.arange(nf, dtype=F32) / nf)
        pos = jnp.where(jnp.asarray(half == 0)[None, :], rows[:, None], cols[:, None]).astype(F32)
        ang = pos * inv[j % nf][None, :]
        sign = jnp.asarray(np.where(j < nf, -1.0, 1.0), F32)[None, :]
        return jnp.cos(ang), jnp.sin(ang) * sign

    cos64, sin64 = table(HEAD_DIM, 16)
    cos64, sin64 = jnp.tile(cos64, (1, 2)), jnp.tile(sin64, (1, 2))
    cosr, sinr = table(MLA_ROPE, 8)
    cosm = jnp.concatenate([jnp.ones((seq, MLA_NOPE), F32), cosr, jnp.ones((seq, 32), F32)], axis=1)
    sinm = jnp.concatenate([jnp.zeros((seq, MLA_NOPE), F32), sinr, jnp.zeros((seq, 32), F32)], axis=1)
    ctx1, ctx0 = jnp.ones((n_ctx, LANES), F32), jnp.zeros((n_ctx, LANES), F32)
    return tuple(jnp.concatenate([c, a], axis=0) for c, a in ((ctx1, cos64), (ctx0, sin64), (ctx1, cosm), (ctx0, sinm)))


def _layer_weights(l, w_in, w_out, norm_ffn_g, glb_q_gain, glb_k_gain, mla_q_gain, mla_w_uq, mla_kv_gain,
                   mla_w_ukv, exp_w_gate, exp_w_up, exp_w_down, shr_w_gate, shr_w_up, shr_w_down):
    d = w_in.shape[1]
    wi = w_in[l]
    kr = wi[:, 1664:1696]
    win = jnp.concatenate([wi[:, :1664], jnp.zeros((d, MLA_NOPE), F32), kr, jnp.zeros((d, 32), F32)], axis=1)
    uq = mla_w_uq[l].reshape(MLA_Q_RANK, MLA_HEADS, MLA_NOPE + MLA_ROPE)
    wuq = jnp.concatenate([uq, jnp.zeros((MLA_Q_RANK, MLA_HEADS, 32), F32)], axis=2).reshape(MLA_Q_RANK, 512)
    ukv = mla_w_ukv[l].reshape(MLA_KV_RANK, MLA_HEADS, MLA_NOPE + MLA_V)
    wk = jnp.concatenate([ukv[:, :, :MLA_NOPE], jnp.zeros((MLA_KV_RANK, MLA_HEADS, 64), F32)], axis=2)
    wukv = jnp.concatenate([wk.reshape(MLA_KV_RANK, 512), ukv[:, :, MLA_NOPE:].reshape(MLA_KV_RANK, 256)], axis=1)
    gain512 = jnp.concatenate([jnp.tile(glb_q_gain[l] * (HEAD_DIM ** -0.5 * LOG2E), GQA_HEADS),
                               jnp.tile(glb_k_gain[l], GQA_KV)]).reshape(1, 512)
    head = np.arange(512) // HEAD_DIM
    return {
        "win": win.astype(BF16),
        "wuq": wuq.astype(BF16),
        "wukv": wukv.astype(BF16),
        "gain512": gain512,
        "mqg": mla_q_gain[l].reshape(1, -1),
        "mkvg": mla_kv_gain[l].reshape(1, -1),
        "bd": jnp.asarray(head[:, None] == head[None, :], BF16),
        "wout": w_out[l].astype(BF16),
        "g_ffn": norm_ffn_g[l].reshape(1, -1),
        "wgu": jnp.concatenate([exp_w_gate[l], exp_w_up[l]], axis=2).astype(BF16),
        "wd": exp_w_down[l].astype(BF16),
        "swgu": jnp.concatenate([shr_w_gate[l], shr_w_up[l]], axis=1).astype(BF16),
        "swd": shr_w_down[l].astype(BF16),
    }


def kernel(x, c, ctx, c_ctx, w_mod, b_mod, norm_mix_g, norm_ffn_g, w_in, w_out, swa_sink, glb_q_gain, glb_k_gain,
           mla_q_gain, mla_w_uq, mla_kv_gain, mla_w_ukv, router_w, router_bias, exp_w_gate, exp_w_up, exp_w_down,
           shr_w_gate, shr_w_up, shr_w_down, final_norm_g):
    bsz, seq, d = x.shape
    n_ctx = ctx.shape[1]
    n_layers = w_mod.shape[0]
    assert d == D_MODEL and n_ctx % TQ == 0 and seq % TQ == 0 and seq >= TQ + 2 * WINDOW and seq % GRID_W == 0
    n_ctx_blk = n_ctx // TQ

    rows = -(-(bsz + 1) // 8) * 8
    c_rows = jnp.concatenate([c, c_ctx[None, :], jnp.zeros((rows - bsz - 1, d), F32)], axis=0)
    mods = _modulation(c_rows, w_mod, b_mod)
    mod_x = mods[:, :bsz].reshape(n_layers, bsz, 6, d)
    mod_c = jnp.broadcast_to(mods[:, bsz].reshape(n_layers, 1, 6, d), (n_layers, bsz, 6, d))
    modtabs = jnp.pad(jnp.stack([mod_x, mod_c], axis=2), ((0, 0), (0, 0), (0, 0), (0, 2), (0, 0)))

    tabs = _rope_tables(seq, n_ctx)
    rw = jnp.pad(router_w.T, ((0, 32 - N_EXPERTS), (0, 0)))
    rw_hi = rw.astype(BF16)
    shared = {
        "rw_hi": rw_hi,
        "rw_lo": (rw - rw_hi.astype(F32)).astype(BF16),
        "rb": jnp.pad(router_bias, (0, 32 - N_EXPERTS)).reshape(32, 1),
        "tri": jnp.asarray(np.arange(TQ)[:, None] <= np.arange(TQ)[None, :], BF16),
    }
    g_final = final_norm_g.reshape(1, d)

    xs = jnp.concatenate([ctx, x], axis=1)
    for l in range(n_layers):
        last = l == n_layers - 1
        with_ctx = not last
        lw = _layer_weights(l, w_in, w_out, norm_ffn_g, glb_q_gain, glb_k_gain, mla_q_gain, mla_w_uq,
                            mla_kv_gain, mla_w_ukv, exp_w_gate, exp_w_up, exp_w_down,
                            shr_w_gate, shr_w_up, shr_w_down)
        modtab = modtabs[l]
        qs, ks, vs, qg, kg, vg, qm, km, vm = _input_projection(
            xs, modtab, norm_mix_g[l].reshape(1, d), lw, tabs, n_ctx_blk)
        sink = jnp.pad(swa_sink[l] * LOG2E, (0, 8 - GQA_HEADS))
        o_swa = _attention("swa", qs, ks, vs, sink, n_ctx, with_ctx)
        o_glb = _attention("glb", qg, kg, vg, None, n_ctx, with_ctx)
        o_mla = _attention("mla", qm, km, vm, None, n_ctx, with_ctx)
        x_mid, rows_tok, meta, counts = _output_projection(o_swa, o_glb, o_mla, xs, modtab, lw, shared,
                                                           n_ctx_blk, with_ctx)
        n_tok = rows_tok.shape[0]
        dest, e_lo, e_hi, n_valid, n_sorted = _bucket_layout(meta, counts, n_tok)
        rows_sorted = _scatter_rows(dest, rows_tok, n_sorted)
        y_sorted = _grouped_experts(e_lo, e_hi, n_valid, rows_sorted, lw)
        xs = _gather_residual(dest, y_sorted, x_mid, modtab, g_final, n_ctx_blk, with_ctx, final_norm=last)
    return xs
```

```python
import functools

import jax
import jax.numpy as jnp
import numpy as np
from jax import lax
from jax.experimental import pallas as pl
from jax.experimental.pallas import tpu as pltpu

F32 = jnp.float32
BF16 = jnp.bfloat16

D_MODEL = 1024
GRID_W = 64
HEAD_DIM = 64
GQA_HEADS = 6
GQA_KV = 2
GQA_G = GQA_HEADS // GQA_KV
WINDOW = 128
MLA_HEADS = 4
MLA_NOPE = 64
MLA_ROPE = 32
MLA_V = 64
MLA_Q_RANK = 256
MLA_KV_RANK = 128
ROPE_THETA = 10000.0
N_EXPERTS = 16
N_GROUPS = 4
EXPERTS_PER_GROUP = 4
N_PAIRS = 6
N_BUCKETS = N_GROUPS * N_PAIRS
D_EXPERT = 512
EPS = 1e-6
NEG_INF = -1e30

LANES = 128
TQ = 256
TK = 256
TM = 256
SWA_SLAB = 128
V_ROWS = 80
LOG2E = 1.4426950408889634
ROW_W = D_MODEL + LANES
IN_W = 1792
VMEM_LIMIT = 56 * 1024 * 1024

_SQ, _SK, _SV, _GQ, _GK, _GV, _MQ, _MKV, _KR = 0, 384, 512, 640, 1024, 1152, 1280, 1536, 1664

_PAIR_LO = (0, 0, 0, 1, 1, 2)
_PAIR_HI = (1, 2, 3, 2, 3, 3)


def _cparams(n_axes):
    return pltpu.CompilerParams(dimension_semantics=("arbitrary",) * n_axes,
                                vmem_limit_bytes=VMEM_LIMIT)


def _mod_kernel(c_ref, w_ref, b_ref, o_ref):
    c = c_ref[...]
    a = (c * jax.nn.sigmoid(c)).astype(BF16)
    o_ref[0] = jnp.dot(a, w_ref[0].astype(BF16), preferred_element_type=F32) + b_ref[0]


def _modulation(c_rows, w_mod, b_mod):
    n_layers, d, width = w_mod.shape
    rows = c_rows.shape[0]
    nb = 1536
    return pl.pallas_call(
        _mod_kernel,
        out_shape=jax.ShapeDtypeStruct((n_layers, rows, width), F32),
        grid=(n_layers, width // nb),
        in_specs=[pl.BlockSpec((rows, d), lambda l, j: (0, 0)),
                  pl.BlockSpec((1, d, nb), lambda l, j: (l, 0, j)),
                  pl.BlockSpec((1, 1, nb), lambda l, j: (l, 0, j))],
        out_specs=pl.BlockSpec((1, rows, nb), lambda l, j: (l, 0, j)),
        compiler_params=_cparams(2),
        name="adaln_mod",
    )(c_rows, w_mod, b_mod.reshape(n_layers, 1, width))


def _rope(x, cos, sin_signed, nf):
    lane = lax.broadcasted_iota(jnp.int32, (1, LANES), 1)
    first = (lane % (2 * nf)) < nf
    tiles = []
    for t in range(x.shape[1] // LANES):
        xt = x[:, t * LANES:(t + 1) * LANES]
        partner = jnp.where(first, pltpu.roll(xt, LANES - nf, 1), pltpu.roll(xt, nf, 1))
        tiles.append(xt * cos + partner * sin_signed)
    return tiles[0] if len(tiles) == 1 else jnp.concatenate(tiles, axis=1)


def _group_padded(q):
    lane = lax.broadcasted_iota(jnp.int32, (1, LANES), 1)
    tiles = []
    for h in range(GQA_HEADS):
        g = h // GQA_G
        tile = q[:, (h // 2) * LANES:(h // 2 + 1) * LANES]
        if (h % 2) != g:
            tile = pltpu.roll(tile, HEAD_DIM, 1)
        tiles.append(jnp.where((lane // HEAD_DIM) == g, tile, 0.0))
    return jnp.concatenate(tiles, axis=1)


def _emit_q_t(ref, q):
    for h in range(q.shape[1] // LANES):
        ref[0, h] = q[:, h * LANES:(h + 1) * LANES].T.astype(ref.dtype)


def _emit_v_t(ref, v, slab):
    ones_blk = jnp.where(lax.broadcasted_iota(jnp.int32, (V_ROWS - HEAD_DIM, TQ), 0) == 0, 1.0, 0.0)
    parts = []
    for t in range(v.shape[1] // LANES):
        vt = v[:, t * LANES:(t + 1) * LANES].T
        parts += [vt[:HEAD_DIM], ones_blk, vt[HEAD_DIM:], ones_blk]
    ext = jnp.concatenate(parts, axis=0).astype(ref.dtype)
    for s in range(TQ // slab):
        ref[0, s] = ext[:, s * slab:(s + 1) * slab]


def _rms(x, eps=EPS):
    return x * lax.rsqrt(jnp.mean(x * x, axis=-1, keepdims=True) + eps)


def _in_kernel(x_ref, mod_ref, g_ref, win_ref, cos64_ref, sin64_ref, cosm_ref, sinm_ref,
               gain_ref, mqg_ref, mkvg_ref, wuq_ref, wukv_ref, bd_ref,
               qs_ref, ks_ref, vs_ref, qg_ref, kg_ref, vg_ref, qm_ref, km_ref, vm_ref, *, mla_scale):
    x = x_ref[0]
    mod = mod_ref[0, 0]
    h = _rms(x) * g_ref[...]
    h = h * (1.0 + mod[1:2]) + mod[0:1]
    z = jnp.dot(h.astype(BF16), win_ref[...], preferred_element_type=F32)
    cos64, sin64 = cos64_ref[...], sin64_ref[...]
    cosm, sinm = cosm_ref[...], sinm_ref[...]

    sqk = _rope(z[:, _SQ:_SV], cos64, sin64, 16)
    _emit_q_t(qs_ref, _group_padded(sqk[:, :384] * (HEAD_DIM ** -0.5 * LOG2E)))
    ks_ref[0] = sqk[:, 384:].astype(BF16)
    _emit_v_t(vs_ref, z[:, _SV:_GQ], SWA_SLAB)

    gqk = z[:, _GQ:_GV]
    sq = gqk * gqk
    hi = sq.astype(BF16)
    lo = (sq - hi.astype(F32)).astype(BF16)
    ssum = (jnp.dot(hi, bd_ref[...], preferred_element_type=F32)
            + jnp.dot(lo, bd_ref[...], preferred_element_type=F32))
    gqk = gqk * lax.rsqrt(ssum * (1.0 / HEAD_DIM) + EPS) * gain_ref[...]
    gqk = _rope(gqk, cos64, sin64, 16)
    _emit_q_t(qg_ref, _group_padded(gqk[:, :384]))
    kg_ref[0] = gqk[:, 384:].astype(BF16)
    _emit_v_t(vg_ref, z[:, _GV:_MQ], TK)

    qn = _rms(z[:, _MQ:_MKV]) * mqg_ref[...]
    mq = jnp.dot(qn.astype(BF16), wuq_ref[...], preferred_element_type=F32)
    _emit_q_t(qm_ref, _rope(mq, cosm, sinm, 8) * (mla_scale * LOG2E))
    kvn = _rms(z[:, _MKV:_KR]) * mkvg_ref[...]
    mkv = jnp.dot(kvn.astype(BF16), wukv_ref[...], preferred_element_type=F32)
    kr = _rope(z[:, _KR:IN_W], cosm, sinm, 8)
    km_ref[0] = (mkv[:, :512] + jnp.concatenate([kr] * MLA_HEADS, axis=1)).astype(BF16)
    _emit_v_t(vm_ref, mkv[:, 512:], TK)


def _input_projection(xs, modtab, g_mix, lw, tabs, n_ctx_blk):
    bsz, t_all, d = xs.shape
    nblk = t_all // TQ
    tok = lambda w: pl.BlockSpec((1, TQ, w), lambda b, i: (b, i, 0))
    q_t = lambda nh: pl.BlockSpec((1, nh, LANES, TQ), lambda b, i: (b, 0, 0, i))
    v_t = lambda nh, slab: pl.BlockSpec((1, TQ // slab, nh * V_ROWS, slab), lambda b, i: (b, i, 0, 0))
    full = lambda a: pl.BlockSpec(a.shape, lambda b, i: (0,) * a.ndim)
    tab = pl.BlockSpec((TQ, LANES), lambda b, i: (i, 0))
    q_shape = lambda nh: jax.ShapeDtypeStruct((bsz, nh, LANES, t_all), BF16)
    k_shape = lambda w: jax.ShapeDtypeStruct((bsz, t_all, w), BF16)
    v_shape = lambda nh, slab: jax.ShapeDtypeStruct((bsz, t_all // slab, nh * V_ROWS, slab), BF16)
    consts = (lw["gain512"], lw["mqg"], lw["mkvg"], lw["wuq"], lw["wukv"], lw["bd"])
    return pl.pallas_call(
        functools.partial(_in_kernel, mla_scale=float((MLA_NOPE + MLA_ROPE) ** -0.5)),
        out_shape=(q_shape(GQA_HEADS), k_shape(128), v_shape(GQA_KV, SWA_SLAB),
                   q_shape(GQA_HEADS), k_shape(128), v_shape(GQA_KV, TK),
                   q_shape(MLA_HEADS), k_shape(512), v_shape(MLA_HEADS, TK)),
        grid=(bsz, nblk),
        in_specs=[tok(d),
                  pl.BlockSpec((1, 1, 8, d), lambda b, i: (b, jnp.where(i < n_ctx_blk, 1, 0), 0, 0)),
                  full(g_mix), full(lw["win"]), tab, tab, tab, tab] + [full(a) for a in consts],
        out_specs=(q_t(GQA_HEADS), tok(128), v_t(GQA_KV, SWA_SLAB),
                   q_t(GQA_HEADS), tok(128), v_t(GQA_KV, TK),
                   q_t(MLA_HEADS), tok(512), v_t(MLA_HEADS, TK)),
        compiler_params=_cparams(2),
        name="in_proj",
    )(xs, modtab, g_mix, lw["win"], *tabs, *consts)


def _attn_kernel(*refs, kind, n_ctx, n_ctx_blk, q_blk_off, seq):
    if kind == "swa":
        sink_ref, bias_ref, q_ref, k_ref, v_ref, o_ref, acc_ref = refs[:7]
        s_refs, p_refs = refs[7:10], refs[10:12]
    else:
        q_ref, k_ref, v_ref, o_ref, acc_ref, m_ref = refs[:6]
        s_refs, cm_refs, p_refs, al_refs = refs[6:9], refs[9:12], refs[12:14], refs[14:16]
    blk = pl.program_id(1) + q_blk_off
    is_lat = blk >= n_ctx_blk
    gqa = kind in ("swa", "glb")
    n_heads = GQA_HEADS if gqa else MLA_HEADS
    n_units = GQA_KV if gqa else MLA_HEADS
    unit_w = n_heads // n_units * TQ

    if kind == "swa":
        nk = TQ + 2 * WINDOW
        n_slab = n_ctx // SWA_SLAB
        row_l = lax.broadcasted_iota(jnp.int32, (V_ROWS, TQ), 0) == HEAD_DIM

        def scores(h, parts):
            cm = None
            for k_rows, _, bias, r0, nr in parts:
                s = jnp.dot(k_rows(), q_ref[0, h], preferred_element_type=F32)
                if bias is not None:
                    s = s + bias()
                s_refs[h % 3][r0:r0 + nr, :] = s
                c = jnp.max(s, axis=0, keepdims=True)
                cm = c if cm is None else jnp.maximum(cm, c)
            return cm

        def softmax(h, cm, nrows):
            m = jnp.maximum(cm, sink_ref[h])
            p_refs[h % 2][0:nrows, :] = jnp.exp2((s_refs[h % 3][0:nrows, :] - m).astype(BF16))
            return jnp.exp2(sink_ref[h] - m)

        def values(h, e_sink, parts):
            u, j = divmod(h, GQA_G)
            acc = None
            for _, v_t, _, r0, nr in parts:
                a = jnp.dot(v_t(u), p_refs[h % 2][r0:r0 + nr, :], preferred_element_type=F32)
                acc = a if acc is None else acc + a
            acc_ref[u, :, j * TQ:(j + 1) * TQ] = acc + jnp.where(row_l, e_sink, 0.0)

        def run(parts):
            nrows = sum(p[4] for p in parts)
            cms = {h: scores(h, parts) for h in range(min(2, n_heads))}
            e_sinks = {}
            for h in range(n_heads + 1):
                if h + 2 < n_heads:
                    cms[h + 2] = scores(h + 2, parts)
                if h < n_heads:
                    e_sinks[h] = softmax(h, cms[h], nrows)
                if h >= 1:
                    values(h - 1, e_sinks[h - 1], parts)

        ctx_part = (lambda: k_ref[0, 0:n_ctx, :],
                    lambda u: jnp.concatenate([v_ref[0, s, u * V_ROWS:(u + 1) * V_ROWS, :] for s in range(n_slab)],
                                              axis=1),
                    None, 0, n_ctx)

        def latent():
            q0 = (blk - n_ctx_blk) * TQ
            k0 = pl.multiple_of(jnp.clip(q0 - WINDOW, 0, seq - nk), WINDOW)
            geom = (q0 - k0) // WINDOW
            s0 = (n_ctx + k0) // SWA_SLAB
            win_part = (lambda: k_ref[0, pl.ds(pl.multiple_of(n_ctx + k0, WINDOW), nk), :],
                        lambda u: jnp.concatenate([v_ref[0, s0 + s, u * V_ROWS:(u + 1) * V_ROWS, :]
                                                   for s in range(nk // SWA_SLAB)], axis=1),
                        lambda: bias_ref[geom], n_ctx, nk)
            run([ctx_part, win_part])

        if q_blk_off < n_ctx_blk:
            pl.when(jnp.logical_not(is_lat))(lambda: run([ctx_part]))
            pl.when(is_lat)(latent)
        else:
            latent()
    else:
        m_ref[...] = jnp.full(m_ref.shape, NEG_INF, F32)
        acc_ref[...] = jnp.zeros(acc_ref.shape, F32)

        def qk(c, par, h):
            lanes = slice(h * TQ, (h + 1) * TQ)
            k_cols = slice(0, LANES) if gqa else slice(h * LANES, (h + 1) * LANES)
            k_rows = k_ref[0, pl.ds(pl.multiple_of(c * TK, TK), TK), k_cols]
            s = jnp.dot(k_rows, q_ref[0, h], preferred_element_type=F32)
            s_refs[par][:, lanes] = s
            cm_refs[par][:, lanes] = jnp.max(s, axis=0, keepdims=True)

        def softmax(sb, par, h):
            lanes = slice(h * TQ, (h + 1) * TQ)
            m_prev = m_ref[:, lanes]
            m_new = jnp.maximum(m_prev, cm_refs[sb][:, lanes])
            m_ref[:, lanes] = m_new
            al_refs[par][:, lanes] = jnp.exp2(m_prev - m_new)
            p_refs[par][:, lanes] = jnp.exp2((s_refs[sb][:, lanes] - m_new).astype(BF16))

        def pv(c, par, h):
            lanes = slice(h * TQ, (h + 1) * TQ)
            u, j = divmod(h, n_heads // n_units)
            ul = slice(j * TQ, (j + 1) * TQ)
            acc_ref[u, :, ul] = (acc_ref[u, :, ul] * al_refs[par][:, lanes]
                                 + jnp.dot(v_ref[0, c, u * V_ROWS:(u + 1) * V_ROWS, :], p_refs[par][:, lanes],
                                           preferred_element_type=F32))

        def step(i, r, n):
            for h in range(n_heads):
                if not isinstance(i, int) or i + 2 < n:
                    qk(i + 2, (r + 2) % 3, h)
                if not isinstance(i, int) or i < n:
                    softmax(r % 3, r % 2, h)
                if not isinstance(i, int) or i >= 1:
                    pv(i - 1, (r + 1) % 2, h)

        def run(n):
            for c in range(min(2, n)):
                for h in range(n_heads):
                    qk(c, c, h)
            step(0, 0, n)
            n_groups = max(0, (n - 3) // 6)

            def body(j, carry):
                for r in range(6):
                    step(1 + 6 * j + r, (1 + r) % 6, n)
                return carry
            if n_groups:
                lax.fori_loop(0, n_groups, body, 0)
            for i in range(1 + 6 * n_groups, n + 1):
                step(i, i % 6, n)

        if q_blk_off < n_ctx_blk:
            pl.when(jnp.logical_not(is_lat))(lambda: run(n_ctx // TK))
            pl.when(is_lat)(lambda: run((n_ctx + seq) // TK))
        else:
            run((n_ctx + seq) // TK)

    heads = []
    for u in range(n_units):
        a = acc_ref[u]
        o = a[:HEAD_DIM] / a[HEAD_DIM:HEAD_DIM + 1]
        heads += [o[:, j * TQ:(j + 1) * TQ] for j in range(unit_w // TQ)]
    tiles = [jnp.concatenate(heads[2 * t:2 * t + 2], axis=0).T for t in range(n_heads // 2)]
    o_ref[0] = jnp.concatenate(tiles, axis=1).astype(o_ref.dtype)


def _attention(kind, q_t, k, v_t, sink, n_ctx, with_ctx_queries):
    bsz, n_heads, _, t_all = q_t.shape
    seq = t_all - n_ctx
    n_ctx_blk = n_ctx // TQ
    q_blk_off = 0 if with_ctx_queries else n_ctx_blk
    nblk = t_all // TQ - q_blk_off
    n_units = GQA_KV if kind in ("swa", "glb") else MLA_HEADS
    wo = n_heads * HEAD_DIM
    kern = functools.partial(_attn_kernel, kind=kind, n_ctx=n_ctx, n_ctx_blk=n_ctx_blk,
                             q_blk_off=q_blk_off, seq=seq)
    in_specs = [pl.BlockSpec((1, n_heads, LANES, TQ), lambda b, i: (b, 0, 0, i + q_blk_off)),
                pl.BlockSpec((1,) + k.shape[1:], lambda b, i: (b, 0, 0)),
                pl.BlockSpec((1,) + v_t.shape[1:], lambda b, i: (b, 0, 0, 0))]
    args = [q_t, k, v_t]
    nq = n_heads * TQ
    scratch = [pltpu.VMEM((n_units, V_ROWS, nq // n_units), F32)]
    if kind == "swa":
        nk = TQ + 2 * WINDOW
        off = np.arange(TQ)[None, None, :] + WINDOW * np.arange(3)[:, None, None] - np.arange(nk)[None, :, None]
        bias = jnp.asarray(np.where(np.abs(off) <= WINDOW, 0.0, NEG_INF), F32)
        in_specs = [pl.BlockSpec(memory_space=pltpu.SMEM),
                    pl.BlockSpec(bias.shape, lambda b, i: (0, 0, 0))] + in_specs
        args = [sink, bias] + args
        rows = n_ctx + nk
        scratch += [pltpu.VMEM((rows, TQ), F32)] * 3 + [pltpu.VMEM((rows, TQ), BF16)] * 2
    else:
        scratch += ([pltpu.VMEM((1, nq), F32)] + [pltpu.VMEM((TK, nq), F32)] * 3 + [pltpu.VMEM((1, nq), F32)] * 3
                    + [pltpu.VMEM((TK, nq), BF16)] * 2 + [pltpu.VMEM((1, nq), F32)] * 2)
    return pl.pallas_call(
        kern,
        out_shape=jax.ShapeDtypeStruct((bsz, nblk * TQ, wo), BF16),
        grid=(bsz, nblk),
        in_specs=in_specs,
        out_specs=pl.BlockSpec((1, TQ, wo), lambda b, i: (b, i, 0)),
        scratch_shapes=scratch,
        compiler_params=_cparams(2),
        name="attn_" + kind,
    )(*args)


def _row_select(rows, idx):
    out = rows[0]
    for j in range(1, len(rows)):
        out = jnp.where(idx == j, rows[j], out)
    return out


def _route(scores, biased):
    def top2sum(a, b, c, d):
        hi1, lo1, hi2, lo2 = jnp.maximum(a, b), jnp.minimum(a, b), jnp.maximum(c, d), jnp.minimum(c, d)
        return jnp.maximum(hi1, hi2) + jnp.maximum(jnp.minimum(hi1, hi2), jnp.maximum(lo1, lo2))

    gs = [top2sum(*biased[4 * g:4 * g + 4]) for g in range(N_GROUPS)]
    best, gi = gs[0], jnp.zeros(gs[0].shape, jnp.int32)
    for g in range(1, N_GROUPS):
        better = gs[g] > best
        gi = jnp.where(better, g, gi)
        best = jnp.where(better, gs[g], best)
    a = [_row_select([biased[4 * g + j] for g in range(N_GROUPS)], gi) for j in range(EXPERTS_PER_GROUP)]
    s = [_row_select([scores[4 * g + j] for g in range(N_GROUPS)], gi) for j in range(EXPERTS_PER_GROUP)]
    v1, i1 = a[0], jnp.zeros(gi.shape, jnp.int32)
    for j in range(1, EXPERTS_PER_GROUP):
        better = a[j] > v1
        i1 = jnp.where(better, j, i1)
        v1 = jnp.where(better, a[j], v1)
    v2, i2 = jnp.full(v1.shape, -3.0e38, F32), jnp.zeros(gi.shape, jnp.int32)
    for j in range(EXPERTS_PER_GROUP):
        better = (i1 != j) & (a[j] > v2)
        i2 = jnp.where(better, j, i2)
        v2 = jnp.where(better, a[j], v2)
    lo, hi = jnp.minimum(i1, i2), jnp.maximum(i1, i2)
    pair = jnp.where(lo == 0, hi - 1, jnp.where(lo == 1, hi + 1, 5))
    s_lo, s_hi = _row_select(s, lo), _row_select(s, hi)
    den = s_lo + s_hi
    return gi * N_PAIRS + pair, s_lo / den, s_hi / den


def _out_kernel(os_ref, og_ref, om_ref, x_ref, mod_ref, wout_ref, g_ref, rwh_ref, rwl_ref, rb_ref, tri_ref,
                xo_ref, row_ref, meta_ref, cnt_ref, carry_ref):
    @pl.when((pl.program_id(0) == 0) & (pl.program_id(1) == 0))
    def _():
        carry_ref[...] = jnp.zeros(carry_ref.shape, F32)

    y = jnp.concatenate([os_ref[0], og_ref[0], om_ref[0]], axis=1)
    mod = mod_ref[0, 0]
    x = x_ref[0] + mod[2:3] * jnp.dot(y, wout_ref[...], preferred_element_type=F32)
    xo_ref[0] = x
    h = _rms(x) * g_ref[...]
    h = h * (1.0 + mod[4:5]) + mod[3:4]

    hh = h.astype(BF16)
    hl = (h - hh.astype(F32)).astype(BF16)
    dn = (((1,), (1,)), ((), ()))
    logits = (lax.dot_general(rwh_ref[...], hh, dn, preferred_element_type=F32)
              + lax.dot_general(rwh_ref[...], hl, dn, preferred_element_type=F32)
              + lax.dot_general(rwl_ref[...], hh, dn, preferred_element_type=F32))
    sc = jax.nn.sigmoid(logits)
    bs = sc + rb_ref[...]
    scores = [sc[e:e + 1, :] for e in range(N_EXPERTS)]
    biased = [bs[e:e + 1, :] for e in range(N_EXPERTS)]
    bucket, g_lo, g_hi = _route(scores, biased)

    onehot = jnp.where(lax.broadcasted_iota(jnp.int32, (32, TQ), 0) == bucket, 1.0, 0.0)
    prefix = jnp.dot(onehot.astype(BF16), tri_ref[...], preferred_element_type=F32)
    carry = carry_ref[:, 0:1]
    rank = jnp.sum(onehot * (carry + prefix - 1.0), axis=0, keepdims=True)
    carry_new = jnp.broadcast_to(carry + prefix[:, TQ - 1:TQ], carry_ref.shape)
    carry_ref[...] = carry_new
    cnt_ref[...] = carry_new
    meta_ref[0] = jnp.concatenate([bucket, rank.astype(jnp.int32), jnp.zeros((6, TQ), jnp.int32)], axis=0)

    gates = jnp.concatenate([g_lo, g_hi, jnp.zeros((LANES - 2, TQ), F32)], axis=0)
    row_ref[...] = jnp.concatenate([h, gates.T], axis=1)


def _output_projection(o_swa, o_glb, o_mla, xs, modtab, lw, shared, n_ctx_blk, with_ctx):
    bsz, t_all, d = xs.shape
    off = 0 if with_ctx else n_ctx_blk
    nblk = t_all // TQ - off
    tok = lambda w: pl.BlockSpec((1, TQ, w), lambda b, i: (b, i, 0))
    full = lambda a: pl.BlockSpec(a.shape, lambda b, i: (0,) * a.ndim)
    consts = (lw["wout"], lw["g_ffn"], shared["rw_hi"], shared["rw_lo"], shared["rb"], shared["tri"])
    return pl.pallas_call(
        _out_kernel,
        out_shape=(jax.ShapeDtypeStruct((bsz, nblk * TQ, d), F32),
                   jax.ShapeDtypeStruct((bsz * nblk * TQ, ROW_W), F32),
                   jax.ShapeDtypeStruct((bsz * nblk, 8, TQ), jnp.int32),
                   jax.ShapeDtypeStruct((32, LANES), F32)),
        grid=(bsz, nblk),
        in_specs=[tok(o_swa.shape[2]), tok(o_glb.shape[2]), tok(o_mla.shape[2]),
                  pl.BlockSpec((1, TQ, d), lambda b, i: (b, i + off, 0)),
                  pl.BlockSpec((1, 1, 8, d), lambda b, i: (b, jnp.where(i + off < n_ctx_blk, 1, 0), 0, 0))]
                 + [full(a) for a in consts],
        out_specs=(tok(d),
                   pl.BlockSpec((TQ, ROW_W), lambda b, i: (b * nblk + i, 0)),
                   pl.BlockSpec((1, 8, TQ), lambda b, i: (b * nblk + i, 0, 0)),
                   pl.BlockSpec((32, LANES), lambda b, i: (0, 0))),
        scratch_shapes=[pltpu.VMEM((32, LANES), F32)],
        compiler_params=_cparams(2),
        name="out_proj_router",
    )(o_swa, o_glb, o_mla, xs, modtab, *consts)


def _scatter_kernel(dest_ref, row_ref, init_ref, out_ref, buf, sem, *, n_steps):
    del init_ref
    i = pl.program_id(0)
    slot = i % 2

    def wait_slot(s):
        pltpu.make_async_copy(buf.at[s], out_ref.at[pl.ds(0, TQ)], sem.at[s]).wait()

    for s in range(2):
        @pl.when(slot == s)
        def _(s=s):
            @pl.when(i >= 2)
            def _():
                wait_slot(s)

            buf[s] = row_ref[...]
            for r in range(TQ):
                d = dest_ref[i * TQ + r]
                pltpu.make_async_copy(buf.at[s, pl.ds(r, 1)], out_ref.at[pl.ds(d, 1)], sem.at[s]).start()

    @pl.when(i == n_steps - 1)
    def _():
        wait_slot(slot)
        if n_steps >= 2:
            wait_slot(1 - slot)


def _scatter_rows(dest, rows, n_sorted):
    n_tok = rows.shape[0]
    n_steps = n_tok // TQ
    init = jnp.zeros((n_sorted, ROW_W), F32)
    return pl.pallas_call(
        functools.partial(_scatter_kernel, n_steps=n_steps),
        out_shape=jax.ShapeDtypeStruct((n_sorted, ROW_W), F32),
        grid_spec=pltpu.PrefetchScalarGridSpec(
            num_scalar_prefetch=1,
            grid=(n_steps,),
            in_specs=[pl.BlockSpec((TQ, ROW_W), lambda i, d: (i, 0)),
                      pl.BlockSpec(memory_space=pl.ANY)],
            out_specs=pl.BlockSpec(memory_space=pl.ANY),
            scratch_shapes=[pltpu.VMEM((2, TQ, ROW_W), F32), pltpu.SemaphoreType.DMA((2,))]),
        input_output_aliases={2: 0},
        compiler_params=_cparams(1),
        name="moe_scatter",
    )(dest, rows, init)


def _swiglu(h, wgu, wd):
    u = jnp.dot(h, wgu, preferred_element_type=F32)
    a = u[:, :D_EXPERT]
    a = a * jax.nn.sigmoid(a) * u[:, D_EXPERT:]
    return jnp.dot(a.astype(BF16), wd, preferred_element_type=F32)


def _moe_kernel(elo_ref, ehi_ref, nv_ref, row_ref, wgl_ref, wdl_ref, wgh_ref, wdh_ref, swg_ref, swd_ref, y_ref):
    del elo_ref, ehi_ref
    j = pl.program_id(0)

    @pl.when(j < nv_ref[0])
    def _():
        rows = row_ref[...]
        h = rows[:, :D_MODEL].astype(BF16)
        g_lo = rows[:, D_MODEL:D_MODEL + 1]
        g_hi = rows[:, D_MODEL + 1:D_MODEL + 2]
        y_ref[...] = (g_lo * _swiglu(h, wgl_ref[0], wdl_ref[0]) + g_hi * _swiglu(h, wgh_ref[0], wdh_ref[0])
                      + _swiglu(h, swg_ref[...], swd_ref[...]))

    @pl.when(j >= nv_ref[0])
    def _():
        y_ref[...] = jnp.zeros(y_ref.shape, F32)


def _grouped_experts(e_lo, e_hi, n_valid, rows_sorted, lw):
    n_sorted = rows_sorted.shape[0]
    nb = n_sorted // TM
    wgu, wd, swgu, swd = lw["wgu"], lw["wd"], lw["swgu"], lw["swd"]
    return pl.pallas_call(
        _moe_kernel,
        out_shape=jax.ShapeDtypeStruct((n_sorted, D_MODEL), F32),
        grid_spec=pltpu.PrefetchScalarGridSpec(
            num_scalar_prefetch=3,
            grid=(nb,),
            in_specs=[pl.BlockSpec((TM, ROW_W), lambda j, lo, hi, nv: (j, 0)),
                      pl.BlockSpec((1,) + wgu.shape[1:], lambda j, lo, hi, nv: (lo[j], 0, 0)),
                      pl.BlockSpec((1,) + wd.shape[1:], lambda j, lo, hi, nv: (lo[j], 0, 0)),
                      pl.BlockSpec((1,) + wgu.shape[1:], lambda j, lo, hi, nv: (hi[j], 0, 0)),
                      pl.BlockSpec((1,) + wd.shape[1:], lambda j, lo, hi, nv: (hi[j], 0, 0)),
                      pl.BlockSpec(swgu.shape, lambda j, lo, hi, nv: (0, 0)),
                      pl.BlockSpec(swd.shape, lambda j, lo, hi, nv: (0, 0))],
            out_specs=pl.BlockSpec((TM, D_MODEL), lambda j, lo, hi, nv: (j, 0))),
        compiler_params=_cparams(1),
        name="moe_experts",
    )(e_lo, e_hi, n_valid, rows_sorted, wgu, wd, wgu, wd, swgu, swd)


def _gather_kernel(dest_ref, y_ref, x_ref, mod_ref, gf_ref, o_ref, fbuf, sem, *, n_steps, final_norm):
    i = pl.program_id(0)
    slot = i % 2

    def issue(step, s):
        for r in range(TQ):
            d = dest_ref[step * TQ + r]
            pltpu.make_async_copy(y_ref.at[pl.ds(d, 1)], fbuf.at[s, pl.ds(r, 1)], sem.at[s]).start()

    @pl.when(i == 0)
    def _():
        issue(0, 0)

    for s in range(2):
        @pl.when((i + 1 < n_steps) & (slot == s))
        def _(s=s):
            issue(i + 1, 1 - s)

    pltpu.make_async_copy(y_ref.at[pl.ds(0, TQ)], fbuf.at[slot], sem.at[slot]).wait()
    x = x_ref[0] + mod_ref[0, 0][5:6] * fbuf[slot]
    if final_norm:
        x = _rms(x) * gf_ref[...]
    o_ref[0] = x


def _gather_residual(dest, y_sorted, x_mid, modtab, g_final, n_ctx_blk, with_ctx, final_norm):
    bsz, t_rows, d = x_mid.shape
    nblk = t_rows // TQ
    off = 0 if with_ctx else n_ctx_blk
    n_steps = bsz * nblk
    return pl.pallas_call(
        functools.partial(_gather_kernel, n_steps=n_steps, final_norm=final_norm),
        out_shape=jax.ShapeDtypeStruct((bsz, t_rows, d), F32),
        grid_spec=pltpu.PrefetchScalarGridSpec(
            num_scalar_prefetch=1,
            grid=(n_steps,),
            in_specs=[pl.BlockSpec(memory_space=pl.ANY),
                      pl.BlockSpec((1, TQ, d), lambda i, dst: (i // nblk, i % nblk, 0)),
                      pl.BlockSpec((1, 1, 8, d),
                                   lambda i, dst: (i // nblk, jnp.where(i % nblk + off < n_ctx_blk, 1, 0), 0, 0)),
                      pl.BlockSpec(g_final.shape, lambda i, dst: (0, 0))],
            out_specs=pl.BlockSpec((1, TQ, d), lambda i, dst: (i // nblk, i % nblk, 0)),
            scratch_shapes=[pltpu.VMEM((2, TQ, d), F32), pltpu.SemaphoreType.DMA((2,))]),
        compiler_params=_cparams(1),
        name="moe_gather",
    )(dest, y_sorted, x_mid, modtab, g_final)


def _bucket_layout(meta, counts, n_tok):
    bucket = meta[:, 0, :].reshape(-1)
    rank = meta[:, 1, :].reshape(-1)
    cnt = counts[:N_BUCKETS, 0].astype(jnp.int32)
    padded = (cnt + TM - 1) // TM * TM
    pad_end = jnp.cumsum(padded)
    pad_start = pad_end - padded
    dest = pad_start[bucket] + rank
    nb = n_tok // TM + N_BUCKETS
    n_valid = pad_end[-1] // TM
    blk = jnp.arange(nb, dtype=jnp.int32)
    blk_row = jnp.minimum(blk, n_valid - 1) * TM
    blk_bucket = jnp.minimum(jnp.sum(pad_end[None, :] <= blk_row[:, None], axis=1), N_BUCKETS - 1).astype(jnp.int32)
    grp, pair = blk_bucket // N_PAIRS, blk_bucket % N_PAIRS
    e_lo = grp * EXPERTS_PER_GROUP + jnp.asarray(_PAIR_LO, jnp.int32)[pair]
    e_hi = grp * EXPERTS_PER_GROUP + jnp.asarray(_PAIR_HI, jnp.int32)[pair]
    return dest.astype(jnp.int32), e_lo, e_hi, n_valid.reshape(1).astype(jnp.int32), nb * TM


def _rope_tables(seq, n_ctx):
    t = jnp.arange(seq)
    rows, cols = t // GRID_W, t % GRID_W

    def table(width, nf):
        lane = np.arange(width)
        half, j = lane // (2 * nf), lane % (2 * nf)
        inv = ROPE_THETA ** (-jnp.arange(nf, dtype=F32) / nf)
        pos = jnp.where(jnp.asarray(half == 0)[None, :], rows[:, None], cols[:, None]).astype(F32)
        ang = pos * inv[j % nf][None, :]
        sign = jnp.asarray(np.where(j < nf, -1.0, 1.0), F32)[None, :]
        return jnp.cos(ang), jnp.sin(ang) * sign

    cos64, sin64 = table(HEAD_DIM, 16)
    cos64, sin64 = jnp.tile(cos64, (1, 2)), jnp.tile(sin64, (1, 2))
    cosr, sinr = table(MLA_ROPE, 8)
    cosm = jnp.concatenate([jnp.ones((seq, MLA_NOPE), F32), cosr, jnp.ones((seq, 32), F32)], axis=1)
    sinm = jnp.concatenate([jnp.zeros((seq, MLA_NOPE), F32), sinr, jnp.zeros((seq, 32), F32)], axis=1)
    ctx1, ctx0 = jnp.ones((n_ctx, LANES), F32), jnp.zeros((n_ctx, LANES), F32)
    return tuple(jnp.concatenate([c, a], axis=0) for c, a in ((ctx1, cos64), (ctx0, sin64), (ctx1, cosm), (ctx0, sinm)))


def _layer_weights(l, w_in, w_out, norm_ffn_g, glb_q_gain, glb_k_gain, mla_q_gain, mla_w_uq, mla_kv_gain,
                   mla_w_ukv, exp_w_gate, exp_w_up, exp_w_down, shr_w_gate, shr_w_up, shr_w_down):
    d = w_in.shape[1]
    wi = w_in[l]
    kr = wi[:, 1664:1696]
    win = jnp.concatenate([wi[:, :1664], jnp.zeros((d, MLA_NOPE), F32), kr, jnp.zeros((d, 32), F32)], axis=1)
    uq = mla_w_uq[l].reshape(MLA_Q_RANK, MLA_HEADS, MLA_NOPE + MLA_ROPE)
    wuq = jnp.concatenate([uq, jnp.zeros((MLA_Q_RANK, MLA_HEADS, 32), F32)], axis=2).reshape(MLA_Q_RANK, 512)
    ukv = mla_w_ukv[l].reshape(MLA_KV_RANK, MLA_HEADS, MLA_NOPE + MLA_V)
    wk = jnp.concatenate([ukv[:, :, :MLA_NOPE], jnp.zeros((MLA_KV_RANK, MLA_HEADS, 64), F32)], axis=2)
    wukv = jnp.concatenate([wk.reshape(MLA_KV_RANK, 512), ukv[:, :, MLA_NOPE:].reshape(MLA_KV_RANK, 256)], axis=1)
    gain512 = jnp.concatenate([jnp.tile(glb_q_gain[l] * (HEAD_DIM ** -0.5 * LOG2E), GQA_HEADS),
                               jnp.tile(glb_k_gain[l], GQA_KV)]).reshape(1, 512)
    head = np.arange(512) // HEAD_DIM
    return {
        "win": win.astype(BF16),
        "wuq": wuq.astype(BF16),
        "wukv": wukv.astype(BF16),
        "gain512": gain512,
        "mqg": mla_q_gain[l].reshape(1, -1),
        "mkvg": mla_kv_gain[l].reshape(1, -1),
        "bd": jnp.asarray(head[:, None] == head[None, :], BF16),
        "wout": w_out[l].astype(BF16),
        "g_ffn": norm_ffn_g[l].reshape(1, -1),
        "wgu": jnp.concatenate([exp_w_gate[l], exp_w_up[l]], axis=2).astype(BF16),
        "wd": exp_w_down[l].astype(BF16),
        "swgu": jnp.concatenate([shr_w_gate[l], shr_w_up[l]], axis=1).astype(BF16),
        "swd": shr_w_down[l].astype(BF16),
    }


def kernel(x, c, ctx, c_ctx, w_mod, b_mod, norm_mix_g, norm_ffn_g, w_in, w_out, swa_sink, glb_q_gain, glb_k_gain,
           mla_q_gain, mla_w_uq, mla_kv_gain, mla_w_ukv, router_w, router_bias, exp_w_gate, exp_w_up, exp_w_down,
           shr_w_gate, shr_w_up, shr_w_down, final_norm_g):
    bsz, seq, d = x.shape
    n_ctx = ctx.shape[1]
    n_layers = w_mod.shape[0]
    assert d == D_MODEL and n_ctx % TQ == 0 and seq % TQ == 0 and seq >= TQ + 2 * WINDOW and seq % GRID_W == 0
    n_ctx_blk = n_ctx // TQ

    rows = -(-(bsz + 1) // 8) * 8
    c_rows = jnp.concatenate([c, c_ctx[None, :], jnp.zeros((rows - bsz - 1, d), F32)], axis=0)
    mods = _modulation(c_rows, w_mod, b_mod)
    mod_x = mods[:, :bsz].reshape(n_layers, bsz, 6, d)
    mod_c = jnp.broadcast_to(mods[:, bsz].reshape(n_layers, 1, 6, d), (n_layers, bsz, 6, d))
    modtabs = jnp.pad(jnp.stack([mod_x, mod_c], axis=2), ((0, 0), (0, 0), (0, 0), (0, 2), (0, 0)))

    tabs = _rope_tables(seq, n_ctx)
    rw = jnp.pad(router_w.T, ((0, 32 - N_EXPERTS), (0, 0)))
    rw_hi = rw.astype(BF16)
    shared = {
        "rw_hi": rw_hi,
        "rw_lo": (rw - rw_hi.astype(F32)).astype(BF16),
        "rb": jnp.pad(router_bias, (0, 32 - N_EXPERTS)).reshape(32, 1),
        "tri": jnp.asarray(np.arange(TQ)[:, None] <= np.arange(TQ)[None, :], BF16),
    }
    g_final = final_norm_g.reshape(1, d)

    xs = jnp.concatenate([ctx, x], axis=1)
    for l in range(n_layers):
        last = l == n_layers - 1
        with_ctx = not last
        lw = _layer_weights(l, w_in, w_out, norm_ffn_g, glb_q_gain, glb_k_gain, mla_q_gain, mla_w_uq,
                            mla_kv_gain, mla_w_ukv, exp_w_gate, exp_w_up, exp_w_down,
                            shr_w_gate, shr_w_up, shr_w_down)
        modtab = modtabs[l]
        qs, ks, vs, qg, kg, vg, qm, km, vm = _input_projection(
            xs, modtab, norm_mix_g[l].reshape(1, d), lw, tabs, n_ctx_blk)
        sink = jnp.pad(swa_sink[l] * LOG2E, (0, 8 - GQA_HEADS))
        o_swa = _attention("swa", qs, ks, vs, sink, n_ctx, with_ctx)
        o_glb = _attention("glb", qg, kg, vg, None, n_ctx, with_ctx)
        o_mla = _attention("mla", qm, km, vm, None, n_ctx, with_ctx)
        x_mid, rows_tok, meta, counts = _output_projection(o_swa, o_glb, o_mla, xs, modtab, lw, shared,
                                                           n_ctx_blk, with_ctx)
        n_tok = rows_tok.shape[0]
        dest, e_lo, e_hi, n_valid, n_sorted = _bucket_layout(meta, counts, n_tok)
        rows_sorted = _scatter_rows(dest, rows_tok, n_sorted)
        y_sorted = _grouped_experts(e_lo, e_hi, n_valid, rows_sorted, lw)
        xs = _gather_residual(dest, y_sorted, x_mid, modtab, g_final, n_ctx_blk, with_ctx, final_norm=last)
    return xs
```

```python
import functools

import jax
import jax.numpy as jnp
import numpy as np
from jax import lax
from jax.experimental import pallas as pl
from jax.experimental.pallas import tpu as pltpu

F32 = jnp.float32
BF16 = jnp.bfloat16

D_MODEL = 1024
GRID_W = 64
HEAD_DIM = 64
GQA_HEADS = 6
GQA_KV = 2
GQA_G = GQA_HEADS // GQA_KV
WINDOW = 128
MLA_HEADS = 4
MLA_NOPE = 64
MLA_ROPE = 32
MLA_V = 64
MLA_Q_RANK = 256
MLA_KV_RANK = 128
ROPE_THETA = 10000.0
N_EXPERTS = 16
N_GROUPS = 4
EXPERTS_PER_GROUP = 4
N_PAIRS = 6
N_BUCKETS = N_GROUPS * N_PAIRS
D_EXPERT = 512
EPS = 1e-6
NEG_INF = -1e30

LANES = 128
TQ = 256
TK = 256
TM = 256
SWA_SLAB = 128
V_ROWS = 80
LOG2E = 1.4426950408889634
ROW_W = D_MODEL + LANES
IN_W = 1792
VMEM_LIMIT = 56 * 1024 * 1024

_SQ, _SK, _SV, _GQ, _GK, _GV, _MQ, _MKV, _KR = 0, 384, 512, 640, 1024, 1152, 1280, 1536, 1664

_PAIR_LO = (0, 0, 0, 1, 1, 2)
_PAIR_HI = (1, 2, 3, 2, 3, 3)


def _cparams(n_axes):
    return pltpu.CompilerParams(dimension_semantics=("arbitrary",) * n_axes,
                                vmem_limit_bytes=VMEM_LIMIT)


def _mod_kernel(c_ref, w_ref, b_ref, o_ref):
    c = c_ref[...]
    a = (c * jax.nn.sigmoid(c)).astype(BF16)
    o_ref[0] = jnp.dot(a, w_ref[0].astype(BF16), preferred_element_type=F32) + b_ref[0]


def _modulation(c_rows, w_mod, b_mod):
    n_layers, d, width = w_mod.shape
    rows = c_rows.shape[0]
    nb = 1536
    return pl.pallas_call(
        _mod_kernel,
        out_shape=jax.ShapeDtypeStruct((n_layers, rows, width), F32),
        grid=(n_layers, width // nb),
        in_specs=[pl.BlockSpec((rows, d), lambda l, j: (0, 0)),
                  pl.BlockSpec((1, d, nb), lambda l, j: (l, 0, j)),
                  pl.BlockSpec((1, 1, nb), lambda l, j: (l, 0, j))],
        out_specs=pl.BlockSpec((1, rows, nb), lambda l, j: (l, 0, j)),
        compiler_params=_cparams(2),
        name="adaln_mod",
    )(c_rows, w_mod, b_mod.reshape(n_layers, 1, width))


def _rope(x, cos, sin_signed, nf):
    lane = lax.broadcasted_iota(jnp.int32, (1, LANES), 1)
    first = (lane % (2 * nf)) < nf
    tiles = []
    for t in range(x.shape[1] // LANES):
        xt = x[:, t * LANES:(t + 1) * LANES]
        partner = jnp.where(first, pltpu.roll(xt, LANES - nf, 1), pltpu.roll(xt, nf, 1))
        tiles.append(xt * cos + partner * sin_signed)
    return tiles[0] if len(tiles) == 1 else jnp.concatenate(tiles, axis=1)


def _group_padded(q):
    lane = lax.broadcasted_iota(jnp.int32, (1, LANES), 1)
    tiles = []
    for h in range(GQA_HEADS):
        g = h // GQA_G
        tile = q[:, (h // 2) * LANES:(h // 2 + 1) * LANES]
        if (h % 2) != g:
            tile = pltpu.roll(tile, HEAD_DIM, 1)
        tiles.append(jnp.where((lane // HEAD_DIM) == g, tile, 0.0))
    return jnp.concatenate(tiles, axis=1)


def _emit_q_t(ref, q):
    for h in range(q.shape[1] // LANES):
        ref[0, h] = q[:, h * LANES:(h + 1) * LANES].T.astype(ref.dtype)


def _emit_v_t(ref, v, slab):
    ones_blk = jnp.where(lax.broadcasted_iota(jnp.int32, (V_ROWS - HEAD_DIM, TQ), 0) == 0, 1.0, 0.0)
    parts = []
    for t in range(v.shape[1] // LANES):
        vt = v[:, t * LANES:(t + 1) * LANES].T
        parts += [vt[:HEAD_DIM], ones_blk, vt[HEAD_DIM:], ones_blk]
    ext = jnp.concatenate(parts, axis=0).astype(ref.dtype)
    for s in range(TQ // slab):
        ref[0, s] = ext[:, s * slab:(s + 1) * slab]


def _rms(x, eps=EPS):
    return x * lax.rsqrt(jnp.mean(x * x, axis=-1, keepdims=True) + eps)


def _in_kernel(x_ref, mod_ref, g_ref, win_ref, cos64_ref, sin64_ref, cosm_ref, sinm_ref,
               gain_ref, mqg_ref, mkvg_ref, wuq_ref, wukv_ref, bd_ref,
               qs_ref, ks_ref, vs_ref, qg_ref, kg_ref, vg_ref, qm_ref, km_ref, vm_ref, *, mla_scale):
    x = x_ref[0]
    mod = mod_ref[0, 0]
    h = _rms(x) * g_ref[...]
    h = h * (1.0 + mod[1:2]) + mod[0:1]
    hb = h.astype(BF16)
    z = jnp.concatenate([jnp.dot(hb, win_ref[:, c0:c1], preferred_element_type=F32)
                         for c0, c1 in ((0, _GK), (_GK, IN_W))], axis=1)
    cos64, sin64 = cos64_ref[...], sin64_ref[...]
    cosm, sinm = cosm_ref[...], sinm_ref[...]

    sqk = _rope(z[:, _SQ:_SV], cos64, sin64, 16)
    _emit_q_t(qs_ref, _group_padded(sqk[:, :384] * (HEAD_DIM ** -0.5 * LOG2E)))
    ks_ref[0] = sqk[:, 384:].astype(BF16)
    _emit_v_t(vs_ref, z[:, _SV:_GQ], SWA_SLAB)

    gqk = z[:, _GQ:_GV]
    sq = gqk * gqk
    hi = sq.astype(BF16)
    lo = (sq - hi.astype(F32)).astype(BF16)
    ssum = (jnp.dot(hi, bd_ref[...], preferred_element_type=F32)
            + jnp.dot(lo, bd_ref[...], preferred_element_type=F32))
    gqk = gqk * lax.rsqrt(ssum * (1.0 / HEAD_DIM) + EPS) * gain_ref[...]
    gqk = _rope(gqk, cos64, sin64, 16)
    _emit_q_t(qg_ref, _group_padded(gqk[:, :384]))
    kg_ref[0] = gqk[:, 384:].astype(BF16)
    _emit_v_t(vg_ref, z[:, _GV:_MQ], TK)

    qn = _rms(z[:, _MQ:_MKV]) * mqg_ref[...]
    mq = jnp.dot(qn.astype(BF16), wuq_ref[...], preferred_element_type=F32)
    _emit_q_t(qm_ref, _rope(mq, cosm, sinm, 8) * (mla_scale * LOG2E))
    kvn = _rms(z[:, _MKV:_KR]) * mkvg_ref[...]
    mkv = jnp.dot(kvn.astype(BF16), wukv_ref[...], preferred_element_type=F32)
    kr = _rope(z[:, _KR:IN_W], cosm, sinm, 8)
    km_ref[0] = (mkv[:, :512] + jnp.concatenate([kr] * MLA_HEADS, axis=1)).astype(BF16)
    _emit_v_t(vm_ref, mkv[:, 512:], TK)


def _input_projection(xs, modtab, g_mix, lw, tabs, n_ctx_blk):
    bsz, t_all, d = xs.shape
    nblk = t_all // TQ
    tok = lambda w: pl.BlockSpec((1, TQ, w), lambda b, i: (b, i, 0))
    q_t = lambda nh: pl.BlockSpec((1, nh, LANES, TQ), lambda b, i: (b, 0, 0, i))
    v_t = lambda nh, slab: pl.BlockSpec((1, TQ // slab, nh * V_ROWS, slab), lambda b, i: (b, i, 0, 0))
    full = lambda a: pl.BlockSpec(a.shape, lambda b, i: (0,) * a.ndim)
    tab = pl.BlockSpec((TQ, LANES), lambda b, i: (i, 0))
    q_shape = lambda nh: jax.ShapeDtypeStruct((bsz, nh, LANES, t_all), BF16)
    k_shape = lambda w: jax.ShapeDtypeStruct((bsz, t_all, w), BF16)
    v_shape = lambda nh, slab: jax.ShapeDtypeStruct((bsz, t_all // slab, nh * V_ROWS, slab), BF16)
    consts = (lw["gain512"], lw["mqg"], lw["mkvg"], lw["wuq"], lw["wukv"], lw["bd"])
    return pl.pallas_call(
        functools.partial(_in_kernel, mla_scale=float((MLA_NOPE + MLA_ROPE) ** -0.5)),
        out_shape=(q_shape(GQA_HEADS), k_shape(128), v_shape(GQA_KV, SWA_SLAB),
                   q_shape(GQA_HEADS), k_shape(128), v_shape(GQA_KV, TK),
                   q_shape(MLA_HEADS), k_shape(512), v_shape(MLA_HEADS, TK)),
        grid=(bsz, nblk),
        in_specs=[tok(d),
                  pl.BlockSpec((1, 1, 8, d), lambda b, i: (b, jnp.where(i < n_ctx_blk, 1, 0), 0, 0)),
                  full(g_mix), full(lw["win"]), tab, tab, tab, tab] + [full(a) for a in consts],
        out_specs=(q_t(GQA_HEADS), tok(128), v_t(GQA_KV, SWA_SLAB),
                   q_t(GQA_HEADS), tok(128), v_t(GQA_KV, TK),
                   q_t(MLA_HEADS), tok(512), v_t(MLA_HEADS, TK)),
        compiler_params=_cparams(2),
        name="in_proj",
    )(xs, modtab, g_mix, lw["win"], *tabs, *consts)


def _attn_kernel(*refs, kind, n_ctx, n_ctx_blk, q_blk_off, seq):
    if kind == "swa":
        sink_ref, bias_ref, q_ref, k_ref, v_ref, o_ref, acc_ref = refs[:7]
        s_refs, p_refs = refs[7:10], refs[10:12]
    else:
        q_ref, k_ref, v_ref, o_ref, acc_ref, m_ref = refs[:6]
        s_refs, cm_refs, p_refs, al_refs = refs[6:9], refs[9:12], refs[12:14], refs[14:16]
    blk = pl.program_id(1) + q_blk_off
    is_lat = blk >= n_ctx_blk
    gqa = kind in ("swa", "glb")
    n_heads = GQA_HEADS if gqa else MLA_HEADS
    n_units = GQA_KV if gqa else MLA_HEADS
    unit_w = n_heads // n_units * TQ

    if kind == "swa":
        nk = TQ + 2 * WINDOW
        n_slab = n_ctx // SWA_SLAB
        row_l = lax.broadcasted_iota(jnp.int32, (V_ROWS, TQ), 0) == HEAD_DIM

        def scores(h, parts):
            cm = None
            for k_rows, _, bias, r0, nr in parts:
                s = jnp.dot(k_rows(), q_ref[0, h], preferred_element_type=F32)
                if bias is not None:
                    s = s + bias()
                s_refs[h % 3][r0:r0 + nr, :] = s
                c = jnp.max(s, axis=0, keepdims=True)
                cm = c if cm is None else jnp.maximum(cm, c)
            return cm

        def softmax(h, cm, nrows):
            m = jnp.maximum(cm, sink_ref[h])
            p_refs[h % 2][0:nrows, :] = jnp.exp2((s_refs[h % 3][0:nrows, :] - m).astype(BF16))
            return jnp.exp2(sink_ref[h] - m)

        def values(h, e_sink, parts):
            u, j = divmod(h, GQA_G)
            acc = None
            for _, v_t, _, r0, nr in parts:
                a = jnp.dot(v_t(u), p_refs[h % 2][r0:r0 + nr, :], preferred_element_type=F32)
                acc = a if acc is None else acc + a
            acc_ref[u, :, j * TQ:(j + 1) * TQ] = acc + jnp.where(row_l, e_sink, 0.0)

        def run(parts):
            nrows = sum(p[4] for p in parts)
            cms = {h: scores(h, parts) for h in range(min(2, n_heads))}
            e_sinks = {}
            for h in range(n_heads + 1):
                if h + 2 < n_heads:
                    cms[h + 2] = scores(h + 2, parts)
                if h < n_heads:
                    e_sinks[h] = softmax(h, cms[h], nrows)
                if h >= 1:
                    values(h - 1, e_sinks[h - 1], parts)

        ctx_part = (lambda: k_ref[0, 0:n_ctx, :],
                    lambda u: jnp.concatenate([v_ref[0, s, u * V_ROWS:(u + 1) * V_ROWS, :] for s in range(n_slab)],
                                              axis=1),
                    None, 0, n_ctx)

        def latent():
            q0 = (blk - n_ctx_blk) * TQ
            k0 = pl.multiple_of(jnp.clip(q0 - WINDOW, 0, seq - nk), WINDOW)
            geom = (q0 - k0) // WINDOW
            s0 = (n_ctx + k0) // SWA_SLAB
            win_part = (lambda: k_ref[0, pl.ds(pl.multiple_of(n_ctx + k0, WINDOW), nk), :],
                        lambda u: jnp.concatenate([v_ref[0, s0 + s, u * V_ROWS:(u + 1) * V_ROWS, :]
                                                   for s in range(nk // SWA_SLAB)], axis=1),
                        lambda: bias_ref[geom], n_ctx, nk)
            run([ctx_part, win_part])

        if q_blk_off < n_ctx_blk:
            pl.when(jnp.logical_not(is_lat))(lambda: run([ctx_part]))
            pl.when(is_lat)(latent)
        else:
            latent()
    else:
        m_ref[...] = jnp.full(m_ref.shape, NEG_INF, F32)
        acc_ref[...] = jnp.zeros(acc_ref.shape, F32)

        def qk(c, par, h):
            lanes = slice(h * TQ, (h + 1) * TQ)
            k_cols = slice(0, LANES) if gqa else slice(h * LANES, (h + 1) * LANES)
            k_rows = k_ref[0, pl.ds(pl.multiple_of(c * TK, TK), TK), k_cols]
            s = jnp.dot(k_rows, q_ref[0, h], preferred_element_type=F32)
            s_refs[par][:, lanes] = s
            cm_refs[par][:, lanes] = jnp.max(s, axis=0, keepdims=True)

        def softmax(sb, par, h):
            lanes = slice(h * TQ, (h + 1) * TQ)
            m_prev = m_ref[:, lanes]
            m_new = jnp.maximum(m_prev, cm_refs[sb][:, lanes])
            m_ref[:, lanes] = m_new
            al_refs[par][:, lanes] = jnp.exp2(m_prev - m_new)
            p_refs[par][:, lanes] = jnp.exp2((s_refs[sb][:, lanes] - m_new).astype(BF16))

        def pv(c, par, h):
            lanes = slice(h * TQ, (h + 1) * TQ)
            u, j = divmod(h, n_heads // n_units)
            ul = slice(j * TQ, (j + 1) * TQ)
            acc_ref[u, :, ul] = (acc_ref[u, :, ul] * al_refs[par][:, lanes]
                                 + jnp.dot(v_ref[0, c, u * V_ROWS:(u + 1) * V_ROWS, :], p_refs[par][:, lanes],
                                           preferred_element_type=F32))

        def step(i, r, n):
            for h in range(n_heads):
                if not isinstance(i, int) or i + 2 < n:
                    qk(i + 2, (r + 2) % 3, h)
                if not isinstance(i, int) or i < n:
                    softmax(r % 3, r % 2, h)
                if not isinstance(i, int) or i >= 1:
                    pv(i - 1, (r + 1) % 2, h)

        def run(n):
            for c in range(min(2, n)):
                for h in range(n_heads):
                    qk(c, c, h)
            step(0, 0, n)
            n_groups = max(0, (n - 3) // 6)

            def body(j, carry):
                for r in range(6):
                    step(1 + 6 * j + r, (1 + r) % 6, n)
                return carry
            if n_groups:
                lax.fori_loop(0, n_groups, body, 0)
            for i in range(1 + 6 * n_groups, n + 1):
                step(i, i % 6, n)

        if q_blk_off < n_ctx_blk:
            pl.when(jnp.logical_not(is_lat))(lambda: run(n_ctx // TK))
            pl.when(is_lat)(lambda: run((n_ctx + seq) // TK))
        else:
            run((n_ctx + seq) // TK)

    heads = []
    for u in range(n_units):
        a = acc_ref[u]
        o = a[:HEAD_DIM] / a[HEAD_DIM:HEAD_DIM + 1]
        heads += [o[:, j * TQ:(j + 1) * TQ] for j in range(unit_w // TQ)]
    tiles = [jnp.concatenate(heads[2 * t:2 * t + 2], axis=0).T for t in range(n_heads // 2)]
    o_ref[0] = jnp.concatenate(tiles, axis=1).astype(o_ref.dtype)


def _attention(kind, q_t, k, v_t, sink, n_ctx, with_ctx_queries):
    bsz, n_heads, _, t_all = q_t.shape
    seq = t_all - n_ctx
    n_ctx_blk = n_ctx // TQ
    q_blk_off = 0 if with_ctx_queries else n_ctx_blk
    nblk = t_all // TQ - q_blk_off
    n_units = GQA_KV if kind in ("swa", "glb") else MLA_HEADS
    wo = n_heads * HEAD_DIM
    kern = functools.partial(_attn_kernel, kind=kind, n_ctx=n_ctx, n_ctx_blk=n_ctx_blk,
                             q_blk_off=q_blk_off, seq=seq)
    in_specs = [pl.BlockSpec((1, n_heads, LANES, TQ), lambda b, i: (b, 0, 0, i + q_blk_off)),
                pl.BlockSpec((1,) + k.shape[1:], lambda b, i: (b, 0, 0)),
                pl.BlockSpec((1,) + v_t.shape[1:], lambda b, i: (b, 0, 0, 0))]
    args = [q_t, k, v_t]
    nq = n_heads * TQ
    scratch = [pltpu.VMEM((n_units, V_ROWS, nq // n_units), F32)]
    if kind == "swa":
        nk = TQ + 2 * WINDOW
        off = np.arange(TQ)[None, None, :] + WINDOW * np.arange(3)[:, None, None] - np.arange(nk)[None, :, None]
        bias = jnp.asarray(np.where(np.abs(off) <= WINDOW, 0.0, NEG_INF), F32)
        in_specs = [pl.BlockSpec(memory_space=pltpu.SMEM),
                    pl.BlockSpec(bias.shape, lambda b, i: (0, 0, 0))] + in_specs
        args = [sink, bias] + args
        rows = n_ctx + nk
        scratch += [pltpu.VMEM((rows, TQ), F32)] * 3 + [pltpu.VMEM((rows, TQ), BF16)] * 2
    else:
        scratch += ([pltpu.VMEM((1, nq), F32)] + [pltpu.VMEM((TK, nq), F32)] * 3 + [pltpu.VMEM((1, nq), F32)] * 3
                    + [pltpu.VMEM((TK, nq), BF16)] * 2 + [pltpu.VMEM((1, nq), F32)] * 2)
    return pl.pallas_call(
        kern,
        out_shape=jax.ShapeDtypeStruct((bsz, nblk * TQ, wo), BF16),
        grid=(bsz, nblk),
        in_specs=in_specs,
        out_specs=pl.BlockSpec((1, TQ, wo), lambda b, i: (b, i, 0)),
        scratch_shapes=scratch,
        compiler_params=_cparams(2),
        name="attn_" + kind,
    )(*args)


def _row_select(rows, idx):
    out = rows[0]
    for j in range(1, len(rows)):
        out = jnp.where(idx == j, rows[j], out)
    return out


def _route(scores, biased):
    def top2sum(a, b, c, d):
        hi1, lo1, hi2, lo2 = jnp.maximum(a, b), jnp.minimum(a, b), jnp.maximum(c, d), jnp.minimum(c, d)
        return jnp.maximum(hi1, hi2) + jnp.maximum(jnp.minimum(hi1, hi2), jnp.maximum(lo1, lo2))

    gs = [top2sum(*biased[4 * g:4 * g + 4]) for g in range(N_GROUPS)]
    best, gi = gs[0], jnp.zeros(gs[0].shape, jnp.int32)
    for g in range(1, N_GROUPS):
        better = gs[g] > best
        gi = jnp.where(better, g, gi)
        best = jnp.where(better, gs[g], best)
    a = [_row_select([biased[4 * g + j] for g in range(N_GROUPS)], gi) for j in range(EXPERTS_PER_GROUP)]
    s = [_row_select([scores[4 * g + j] for g in range(N_GROUPS)], gi) for j in range(EXPERTS_PER_GROUP)]
    v1, i1 = a[0], jnp.zeros(gi.shape, jnp.int32)
    for j in range(1, EXPERTS_PER_GROUP):
        better = a[j] > v1
        i1 = jnp.where(better, j, i1)
        v1 = jnp.where(better, a[j], v1)
    v2, i2 = jnp.full(v1.shape, -3.0e38, F32), jnp.zeros(gi.shape, jnp.int32)
    for j in range(EXPERTS_PER_GROUP):
        better = (i1 != j) & (a[j] > v2)
        i2 = jnp.where(better, j, i2)
        v2 = jnp.where(better, a[j], v2)
    lo, hi = jnp.minimum(i1, i2), jnp.maximum(i1, i2)
    pair = jnp.where(lo == 0, hi - 1, jnp.where(lo == 1, hi + 1, 5))
    s_lo, s_hi = _row_select(s, lo), _row_select(s, hi)
    den = s_lo + s_hi
    return gi * N_PAIRS + pair, s_lo / den, s_hi / den


def _out_kernel(os_ref, og_ref, om_ref, x_ref, mod_ref, wout_ref, g_ref, rwh_ref, rwl_ref, rb_ref, tri_ref,
                xo_ref, row_ref, meta_ref, cnt_ref, carry_ref):
    @pl.when((pl.program_id(0) == 0) & (pl.program_id(1) == 0))
    def _():
        carry_ref[...] = jnp.zeros(carry_ref.shape, F32)

    y = jnp.concatenate([os_ref[0], og_ref[0], om_ref[0]], axis=1)
    mod = mod_ref[0, 0]
    half = D_MODEL // 2
    a = jnp.concatenate([jnp.dot(y, wout_ref[:, :half], preferred_element_type=F32),
                         jnp.dot(y, wout_ref[:, half:], preferred_element_type=F32)], axis=1)
    x = x_ref[0] + mod[2:3] * a
    xo_ref[0] = x
    h = _rms(x) * g_ref[...]
    h = h * (1.0 + mod[4:5]) + mod[3:4]

    hh = h.astype(BF16)
    hl = (h - hh.astype(F32)).astype(BF16)
    logits = (jnp.dot(hh, rwh_ref[...], preferred_element_type=F32)
              + jnp.dot(hl, rwh_ref[...], preferred_element_type=F32)
              + jnp.dot(hh, rwl_ref[...], preferred_element_type=F32)).T[:32]
    sc = jax.nn.sigmoid(logits)
    bs = sc + rb_ref[...]
    scores = [sc[e:e + 1, :] for e in range(N_EXPERTS)]
    biased = [bs[e:e + 1, :] for e in range(N_EXPERTS)]
    bucket, g_lo, g_hi = _route(scores, biased)

    onehot = jnp.where(lax.broadcasted_iota(jnp.int32, (32, TQ), 0) == bucket, 1.0, 0.0)
    prefix = jnp.dot(onehot.astype(BF16), tri_ref[...], preferred_element_type=F32)
    carry = carry_ref[:, 0:1]
    rank = jnp.sum(onehot * (carry + prefix - 1.0), axis=0, keepdims=True)
    carry_new = jnp.broadcast_to(carry + prefix[:, TQ - 1:TQ], carry_ref.shape)
    carry_ref[...] = carry_new
    cnt_ref[...] = carry_new
    meta_ref[0] = jnp.concatenate([bucket, rank.astype(jnp.int32), jnp.zeros((6, TQ), jnp.int32)], axis=0)

    gates = jnp.concatenate([g_lo, g_hi, jnp.zeros((LANES - 2, TQ), F32)], axis=0)
    row_ref[...] = jnp.concatenate([h, gates.T], axis=1)


def _output_projection(o_swa, o_glb, o_mla, xs, modtab, lw, shared, n_ctx_blk, with_ctx):
    bsz, t_all, d = xs.shape
    off = 0 if with_ctx else n_ctx_blk
    nblk = t_all // TQ - off
    tok = lambda w: pl.BlockSpec((1, TQ, w), lambda b, i: (b, i, 0))
    full = lambda a: pl.BlockSpec(a.shape, lambda b, i: (0,) * a.ndim)
    consts = (lw["wout"], lw["g_ffn"], shared["rw_hi"], shared["rw_lo"], shared["rb"], shared["tri"])
    return pl.pallas_call(
        _out_kernel,
        out_shape=(jax.ShapeDtypeStruct((bsz, nblk * TQ, d), F32),
                   jax.ShapeDtypeStruct((bsz * nblk * TQ, ROW_W), F32),
                   jax.ShapeDtypeStruct((bsz * nblk, 8, TQ), jnp.int32),
                   jax.ShapeDtypeStruct((32, LANES), F32)),
        grid=(bsz, nblk),
        in_specs=[tok(o_swa.shape[2]), tok(o_glb.shape[2]), tok(o_mla.shape[2]),
                  pl.BlockSpec((1, TQ, d), lambda b, i: (b, i + off, 0)),
                  pl.BlockSpec((1, 1, 8, d), lambda b, i: (b, jnp.where(i + off < n_ctx_blk, 1, 0), 0, 0))]
                 + [full(a) for a in consts],
        out_specs=(tok(d),
                   pl.BlockSpec((TQ, ROW_W), lambda b, i: (b * nblk + i, 0)),
                   pl.BlockSpec((1, 8, TQ), lambda b, i: (b * nblk + i, 0, 0)),
                   pl.BlockSpec((32, LANES), lambda b, i: (0, 0))),
        scratch_shapes=[pltpu.VMEM((32, LANES), F32)],
        compiler_params=_cparams(2),
        name="out_proj_router",
    )(o_swa, o_glb, o_mla, xs, modtab, *consts)


def _swiglu(h, wgu, wd):
    u = jnp.dot(h, wgu, preferred_element_type=F32)
    a = u[:, :D_EXPERT]
    a = a * jax.nn.sigmoid(a) * u[:, D_EXPERT:]
    return jnp.dot(a.astype(BF16), wd, preferred_element_type=F32)


def _moe_kernel(elo_ref, ehi_ref, nv_ref, src_ref, dst_ref, rows_ref, wgl_ref, wdl_ref, wgh_ref, wdh_ref, swg_ref,
                swd_ref, y_ref, gbuf0, gbuf1, ybuf0, ybuf1, gsem, ssem, *, n_sorted):
    del elo_ref, ehi_ref
    j = pl.program_id(0)
    nv = nv_ref[0]
    slot = j % 2
    gbuf, ybuf = (gbuf0, gbuf1), (ybuf0, ybuf1)

    def fetch(row0, s):
        for r in range(TM):
            pltpu.make_async_copy(rows_ref.at[pl.ds(src_ref[row0 + r], 1)], gbuf[s].at[pl.ds(r, 1)],
                                  gsem.at[s]).start()

    def send(dst_of_row, s):
        for r in range(TM):
            pltpu.make_async_copy(ybuf[s].at[pl.ds(r, 1)], y_ref.at[pl.ds(dst_of_row(r), 1)], ssem.at[s]).start()

    def wait_fetch(s):
        pltpu.make_async_copy(rows_ref.at[pl.ds(0, TM)], gbuf[s], gsem.at[s]).wait()

    def wait_send(s):
        pltpu.make_async_copy(ybuf[s], y_ref.at[pl.ds(0, TM)], ssem.at[s]).wait()

    @pl.when(j == 0)
    def _():
        ybuf0[...] = jnp.zeros(ybuf0.shape, F32)
        ybuf1[...] = jnp.zeros(ybuf1.shape, F32)
        fetch(0, 0)
        send(lambda r: n_sorted + r, 0)

    for s in range(2):
        @pl.when((j < nv) & (slot == s))
        def _(s=s):
            wait_fetch(s)
            wait_send(s)
            fetch(jnp.minimum(j + 1, nv - 1) * TM, 1 - s)
            send(lambda r: dst_ref[j * TM + r], 1 - s)
            rows = gbuf[s][...]
            h = rows[:, :D_MODEL].astype(BF16)
            g_lo = rows[:, D_MODEL:D_MODEL + 1]
            g_hi = rows[:, D_MODEL + 1:D_MODEL + 2]
            ybuf[s][...] = (g_lo * _swiglu(h, wgl_ref[0], wdl_ref[0]) + g_hi * _swiglu(h, wgh_ref[0], wdh_ref[0])
                            + _swiglu(h, swg_ref[...], swd_ref[...]))

        @pl.when((j == nv) & (slot == s))
        def _(s=s):
            wait_fetch(s)
            wait_send(s)
            send(lambda r: dst_ref[j * TM + r], 1 - s)
            wait_send(1 - s)
            ybuf[s][...] = jnp.zeros(ybuf[s].shape, F32)

            def fill(b, carry):
                cp = pltpu.make_async_copy(ybuf[s], y_ref.at[pl.ds(pl.multiple_of(b * TM, TM), TM)], ssem.at[s])
                cp.start()
                cp.wait()
                return carry
            lax.fori_loop(nv, n_sorted // TM, fill, 0)


def _grouped_experts(e_lo, e_hi, n_valid, src_tok, dst_row, rows_tok, lw):
    n_tok = rows_tok.shape[0]
    n_sorted = src_tok.shape[0]
    nb = n_sorted // TM
    wgu, wd, swgu, swd = lw["wgu"], lw["wd"], lw["swgu"], lw["swd"]
    return pl.pallas_call(
        functools.partial(_moe_kernel, n_sorted=n_sorted),
        out_shape=jax.ShapeDtypeStruct((n_sorted + 2 * TM, D_MODEL), F32),
        grid_spec=pltpu.PrefetchScalarGridSpec(
            num_scalar_prefetch=5,
            grid=(nb + 1,),
            in_specs=[pl.BlockSpec(memory_space=pl.ANY),
                      pl.BlockSpec((1,) + wgu.shape[1:], lambda j, lo, hi, nv, src, dst: (lo[j], 0, 0)),
                      pl.BlockSpec((1,) + wd.shape[1:], lambda j, lo, hi, nv, src, dst: (lo[j], 0, 0)),
                      pl.BlockSpec((1,) + wgu.shape[1:], lambda j, lo, hi, nv, src, dst: (hi[j], 0, 0)),
                      pl.BlockSpec((1,) + wd.shape[1:], lambda j, lo, hi, nv, src, dst: (hi[j], 0, 0)),
                      pl.BlockSpec(swgu.shape, lambda j, lo, hi, nv, src, dst: (0, 0)),
                      pl.BlockSpec(swd.shape, lambda j, lo, hi, nv, src, dst: (0, 0))],
            out_specs=pl.BlockSpec(memory_space=pl.ANY),
            scratch_shapes=[pltpu.VMEM((TM, ROW_W), F32)] * 2 + [pltpu.VMEM((TM, D_MODEL), F32)] * 2 + [
                            pltpu.SemaphoreType.DMA((2,)), pltpu.SemaphoreType.DMA((2,))]),
        compiler_params=_cparams(1),
        name="moe_experts",
    )(e_lo, e_hi, n_valid, src_tok, dst_row, rows_tok, wgu, wd, wgu, wd, swgu, swd)


def _residual_kernel(y_ref, x_ref, mod_ref, gf_ref, o_ref, *, final_norm):
    x = x_ref[0] + mod_ref[0, 0][5:6] * y_ref[...]
    if final_norm:
        x = _rms(x) * gf_ref[...]
    o_ref[0] = x


def _ffn_residual(y_tok, x_mid, modtab, g_final, n_ctx_blk, with_ctx, final_norm):
    bsz, t_rows, d = x_mid.shape
    nblk = t_rows // TQ
    off = 0 if with_ctx else n_ctx_blk
    return pl.pallas_call(
        functools.partial(_residual_kernel, final_norm=final_norm),
        out_shape=jax.ShapeDtypeStruct((bsz, t_rows, d), F32),
        grid=(bsz, nblk),
        in_specs=[pl.BlockSpec((TQ, d), lambda b, i: (b * nblk + i, 0)),
                  pl.BlockSpec((1, TQ, d), lambda b, i: (b, i, 0)),
                  pl.BlockSpec((1, 1, 8, d), lambda b, i: (b, jnp.where(i + off < n_ctx_blk, 1, 0), 0, 0)),
                  pl.BlockSpec(g_final.shape, lambda b, i: (0, 0))],
        out_specs=pl.BlockSpec((1, TQ, d), lambda b, i: (b, i, 0)),
        compiler_params=_cparams(2),
        name="ffn_residual",
    )(y_tok, x_mid, modtab, g_final)


def _bucket_layout(meta, counts, n_tok):
    bucket = meta[:, 0, :].reshape(-1)
    rank = meta[:, 1, :].reshape(-1)
    cnt = counts[:N_BUCKETS, 0].astype(jnp.int32)
    padded = (cnt + TM - 1) // TM * TM
    pad_end = jnp.cumsum(padded)
    pad_start = pad_end - padded
    dest = pad_start[bucket] + rank
    nb = n_tok // TM + N_BUCKETS
    n_sorted = nb * TM
    src_tok = jnp.full((n_sorted,), -1, jnp.int32).at[dest].set(jnp.arange(n_tok, dtype=jnp.int32))
    is_pad = src_tok < 0
    dump = n_tok + jnp.cumsum(is_pad.astype(jnp.int32)) - 1
    dst_row = jnp.concatenate([n_sorted + TM + jnp.arange(TM, dtype=jnp.int32), jnp.where(is_pad, dump, src_tok)])
    src_tok = jnp.maximum(src_tok, 0)
    n_valid = pad_end[-1] // TM
    blk = jnp.arange(nb + 1, dtype=jnp.int32)
    blk_row = jnp.minimum(blk, n_valid - 1) * TM
    blk_bucket = jnp.minimum(jnp.sum(pad_end[None, :] <= blk_row[:, None], axis=1), N_BUCKETS - 1).astype(jnp.int32)
    grp, pair = blk_bucket // N_PAIRS, blk_bucket % N_PAIRS
    e_lo = grp * EXPERTS_PER_GROUP + jnp.asarray(_PAIR_LO, jnp.int32)[pair]
    e_hi = grp * EXPERTS_PER_GROUP + jnp.asarray(_PAIR_HI, jnp.int32)[pair]
    return src_tok, dst_row, e_lo, e_hi, n_valid.reshape(1).astype(jnp.int32)


def _rope_tables(seq, n_ctx):
    t = jnp.arange(seq)
    rows, cols = t // GRID_W, t % GRID_W

    def table(width, nf):
        lane = np.arange(width)
        half, j = lane // (2 * nf), lane % (2 * nf)
        inv = ROPE_THETA ** (-jnp.arange(nf, dtype=F32) / nf)
        pos = jnp.where(jnp.asarray(half == 0)[None, :], rows[:, None], cols[:, None]).astype(F32)
        ang = pos * inv[j % nf][None, :]
        sign = jnp.asarray(np.where(j < nf, -1.0, 1.0), F32)[None, :]
        return jnp.cos(ang), jnp.sin(ang) * sign

    cos64, sin64 = table(HEAD_DIM, 16)
    cos64, sin64 = jnp.tile(cos64, (1, 2)), jnp.tile(sin64, (1, 2))
    cosr, sinr = table(MLA_ROPE, 8)
    cosm = jnp.concatenate([jnp.ones((seq, MLA_NOPE), F32), cosr, jnp.ones((seq, 32), F32)], axis=1)
    sinm = jnp.concatenate([jnp.zeros((seq, MLA_NOPE), F32), sinr, jnp.zeros((seq, 32), F32)], axis=1)
    ctx1, ctx0 = jnp.ones((n_ctx, LANES), F32), jnp.zeros((n_ctx, LANES), F32)
    return tuple(jnp.concatenate([c, a], axis=0) for c, a in ((ctx1, cos64), (ctx0, sin64), (ctx1, cosm), (ctx0, sinm)))


def _layer_weights(l, w_in, w_out, norm_ffn_g, glb_q_gain, glb_k_gain, mla_q_gain, mla_w_uq, mla_kv_gain,
                   mla_w_ukv, exp_w_gate, exp_w_up, exp_w_down, shr_w_gate, shr_w_up, shr_w_down):
    d = w_in.shape[1]
    wi = w_in[l]
    kr = wi[:, 1664:1696]
    win = jnp.concatenate([wi[:, :1664], jnp.zeros((d, MLA_NOPE), F32), kr, jnp.zeros((d, 32), F32)], axis=1)
    uq = mla_w_uq[l].reshape(MLA_Q_RANK, MLA_HEADS, MLA_NOPE + MLA_ROPE)
    wuq = jnp.concatenate([uq, jnp.zeros((MLA_Q_RANK, MLA_HEADS, 32), F32)], axis=2).reshape(MLA_Q_RANK, 512)
    ukv = mla_w_ukv[l].reshape(MLA_KV_RANK, MLA_HEADS, MLA_NOPE + MLA_V)
    wk = jnp.concatenate([ukv[:, :, :MLA_NOPE], jnp.zeros((MLA_KV_RANK, MLA_HEADS, 64), F32)], axis=2)
    wukv = jnp.concatenate([wk.reshape(MLA_KV_RANK, 512), ukv[:, :, MLA_NOPE:].reshape(MLA_KV_RANK, 256)], axis=1)
    gain512 = jnp.concatenate([jnp.tile(glb_q_gain[l] * (HEAD_DIM ** -0.5 * LOG2E), GQA_HEADS),
                               jnp.tile(glb_k_gain[l], GQA_KV)]).reshape(1, 512)
    head = np.arange(512) // HEAD_DIM
    return {
        "win": win.astype(BF16),
        "wuq": wuq.astype(BF16),
        "wukv": wukv.astype(BF16),
        "gain512": gain512,
        "mqg": mla_q_gain[l].reshape(1, -1),
        "mkvg": mla_kv_gain[l].reshape(1, -1),
        "bd": jnp.asarray(head[:, None] == head[None, :], BF16),
        "wout": w_out[l].astype(BF16),
        "g_ffn": norm_ffn_g[l].reshape(1, -1),
        "wgu": jnp.concatenate([exp_w_gate[l], exp_w_up[l]], axis=2).astype(BF16),
        "wd": exp_w_down[l].astype(BF16),
        "swgu": jnp.concatenate([shr_w_gate[l], shr_w_up[l]], axis=1).astype(BF16),
        "swd": shr_w_down[l].astype(BF16),
    }


def kernel(x, c, ctx, c_ctx, w_mod, b_mod, norm_mix_g, norm_ffn_g, w_in, w_out, swa_sink, glb_q_gain, glb_k_gain,
           mla_q_gain, mla_w_uq, mla_kv_gain, mla_w_ukv, router_w, router_bias, exp_w_gate, exp_w_up, exp_w_down,
           shr_w_gate, shr_w_up, shr_w_down, final_norm_g):
    bsz, seq, d = x.shape
    n_ctx = ctx.shape[1]
    n_layers = w_mod.shape[0]
    assert d == D_MODEL and n_ctx % TQ == 0 and seq % TQ == 0 and seq >= TQ + 2 * WINDOW and seq % GRID_W == 0
    n_ctx_blk = n_ctx // TQ

    rows = -(-(bsz + 1) // 8) * 8
    c_rows = jnp.concatenate([c, c_ctx[None, :], jnp.zeros((rows - bsz - 1, d), F32)], axis=0)
    mods = _modulation(c_rows, w_mod, b_mod)
    mod_x = mods[:, :bsz].reshape(n_layers, bsz, 6, d)
    mod_c = jnp.broadcast_to(mods[:, bsz].reshape(n_layers, 1, 6, d), (n_layers, bsz, 6, d))
    modtabs = jnp.pad(jnp.stack([mod_x, mod_c], axis=2), ((0, 0), (0, 0), (0, 0), (0, 2), (0, 0)))

    tabs = _rope_tables(seq, n_ctx)
    rw = jnp.pad(router_w, ((0, 0), (0, LANES - N_EXPERTS)))
    rw_hi = rw.astype(BF16)
    shared = {
        "rw_hi": rw_hi,
        "rw_lo": (rw - rw_hi.astype(F32)).astype(BF16),
        "rb": jnp.pad(router_bias, (0, 32 - N_EXPERTS)).reshape(32, 1),
        "tri": jnp.asarray(np.arange(TQ)[:, None] <= np.arange(TQ)[None, :], BF16),
    }
    g_final = final_norm_g.reshape(1, d)

    xs = jnp.concatenate([ctx, x], axis=1)
    for l in range(n_layers):
        last = l == n_layers - 1
        with_ctx = not last
        lw = _layer_weights(l, w_in, w_out, norm_ffn_g, glb_q_gain, glb_k_gain, mla_q_gain, mla_w_uq,
                            mla_kv_gain, mla_w_ukv, exp_w_gate, exp_w_up, exp_w_down,
                            shr_w_gate, shr_w_up, shr_w_down)
        modtab = modtabs[l]
        qs, ks, vs, qg, kg, vg, qm, km, vm = _input_projection(
            xs, modtab, norm_mix_g[l].reshape(1, d), lw, tabs, n_ctx_blk)
        sink = jnp.pad(swa_sink[l] * LOG2E, (0, 8 - GQA_HEADS))
        o_swa = _attention("swa", qs, ks, vs, sink, n_ctx, with_ctx)
        o_glb = _attention("glb", qg, kg, vg, None, n_ctx, with_ctx)
        o_mla = _attention("mla", qm, km, vm, None, n_ctx, with_ctx)
        x_mid, rows_tok, meta, counts = _output_projection(o_swa, o_glb, o_mla, xs, modtab, lw, shared,
                                                           n_ctx_blk, with_ctx)
        n_tok = rows_tok.shape[0]
        src_tok, dst_row, e_lo, e_hi, n_valid = _bucket_layout(meta, counts, n_tok)
        y_tok = _grouped_experts(e_lo, e_hi, n_valid, src_tok, dst_row, rows_tok, lw)
        xs = _ffn_residual(y_tok, x_mid, modtab, g_final, n_ctx_blk, with_ctx, final_norm=last)
    return xs
```

```python
import functools

import jax
import jax.numpy as jnp
import numpy as np
from jax import lax
from jax.experimental import pallas as pl
from jax.experimental.pallas import tpu as pltpu

F32 = jnp.float32
BF16 = jnp.bfloat16

D_MODEL = 1024
GRID_W = 64
HEAD_DIM = 64
GQA_HEADS = 6
GQA_KV = 2
GQA_G = GQA_HEADS // GQA_KV
WINDOW = 128
MLA_HEADS = 4
MLA_NOPE = 64
MLA_ROPE = 32
MLA_V = 64
MLA_Q_RANK = 256
MLA_KV_RANK = 128
ROPE_THETA = 10000.0
N_EXPERTS = 16
N_GROUPS = 4
EXPERTS_PER_GROUP = 4
N_PAIRS = 6
N_BUCKETS = N_GROUPS * N_PAIRS
D_EXPERT = 512
EPS = 1e-6
NEG_INF = -1e30

LANES = 128
TQ = 256
TK = 256
TM = 256
SWA_SLAB = 128
V_ROWS = 80
LOG2E = 1.4426950408889634
ROW_W = D_MODEL + LANES
IN_W = 1792
VMEM_LIMIT = 56 * 1024 * 1024

_SQ, _SK, _SV, _GQ, _GK, _GV, _MQ, _MKV, _KR = 0, 384, 512, 640, 1024, 1152, 1280, 1536, 1664

_PAIR_LO = (0, 0, 0, 1, 1, 2)
_PAIR_HI = (1, 2, 3, 2, 3, 3)


def _cparams(n_axes):
    return pltpu.CompilerParams(dimension_semantics=("arbitrary",) * n_axes,
                                vmem_limit_bytes=VMEM_LIMIT)


def _mod_kernel(c_ref, w_ref, b_ref, o_ref):
    c = c_ref[...]
    a = (c * jax.nn.sigmoid(c)).astype(BF16)
    o_ref[0] = jnp.dot(a, w_ref[0].astype(BF16), preferred_element_type=F32) + b_ref[0]


def _modulation(c_rows, w_mod, b_mod):
    n_layers, d, width = w_mod.shape
    rows = c_rows.shape[0]
    nb = 1536
    return pl.pallas_call(
        _mod_kernel,
        out_shape=jax.ShapeDtypeStruct((n_layers, rows, width), F32),
        grid=(n_layers, width // nb),
        in_specs=[pl.BlockSpec((rows, d), lambda l, j: (0, 0)),
                  pl.BlockSpec((1, d, nb), lambda l, j: (l, 0, j)),
                  pl.BlockSpec((1, 1, nb), lambda l, j: (l, 0, j))],
        out_specs=pl.BlockSpec((1, rows, nb), lambda l, j: (l, 0, j)),
        compiler_params=_cparams(2),
        name="adaln_mod",
    )(c_rows, w_mod, b_mod.reshape(n_layers, 1, width))


def _rope(x, cos, sin_signed, nf):
    lane = lax.broadcasted_iota(jnp.int32, (1, LANES), 1)
    first = (lane % (2 * nf)) < nf
    tiles = []
    for t in range(x.shape[1] // LANES):
        xt = x[:, t * LANES:(t + 1) * LANES]
        partner = jnp.where(first, pltpu.roll(xt, LANES - nf, 1), pltpu.roll(xt, nf, 1))
        tiles.append(xt * cos + partner * sin_signed)
    return tiles[0] if len(tiles) == 1 else jnp.concatenate(tiles, axis=1)


def _group_padded(q):
    lane = lax.broadcasted_iota(jnp.int32, (1, LANES), 1)
    tiles = []
    for h in range(GQA_HEADS):
        g = h // GQA_G
        tile = q[:, (h // 2) * LANES:(h // 2 + 1) * LANES]
        if (h % 2) != g:
            tile = pltpu.roll(tile, HEAD_DIM, 1)
        tiles.append(jnp.where((lane // HEAD_DIM) == g, tile, 0.0))
    return jnp.concatenate(tiles, axis=1)


def _emit_q_t(ref, q):
    for h in range(q.shape[1] // LANES):
        ref[0, h] = q[:, h * LANES:(h + 1) * LANES].T.astype(ref.dtype)


def _emit_v_t(ref, v, slab):
    ones_blk = jnp.where(lax.broadcasted_iota(jnp.int32, (V_ROWS - HEAD_DIM, TQ), 0) == 0, 1.0, 0.0)
    parts = []
    for t in range(v.shape[1] // LANES):
        vt = v[:, t * LANES:(t + 1) * LANES].T
        parts += [vt[:HEAD_DIM], ones_blk, vt[HEAD_DIM:], ones_blk]
    ext = jnp.concatenate(parts, axis=0).astype(ref.dtype)
    for s in range(TQ // slab):
        ref[0, s] = ext[:, s * slab:(s + 1) * slab]


def _rms(x, eps=EPS):
    return x * lax.rsqrt(jnp.mean(x * x, axis=-1, keepdims=True) + eps)


def _in_kernel(x_ref, mod_ref, g_ref, win_ref, cos64_ref, sin64_ref, cosm_ref, sinm_ref,
               gain_ref, mqg_ref, mkvg_ref, wuq_ref, wukv_ref, bd_ref,
               qs_ref, ks_ref, vs_ref, qg_ref, kg_ref, vg_ref, qm_ref, km_ref, vm_ref, z0_ref, z1_ref, *, mla_scale):
    i = pl.program_id(1)

    @pl.when((pl.program_id(0) == 0) & (i == 0))
    def _():
        z1_ref[...] = jnp.zeros(z1_ref.shape, F32)

    def project(z_ref):
        mod = mod_ref[0, 0]
        h = _rms(x_ref[0]) * g_ref[...]
        h = h * (1.0 + mod[1:2]) + mod[0:1]
        z_ref[...] = jnp.dot(h.astype(BF16), win_ref[...], preferred_element_type=F32)

    def emit(z_ref):
        cos64, sin64 = cos64_ref[...], sin64_ref[...]
        cosm, sinm = cosm_ref[...], sinm_ref[...]

        sqk = _rope(z_ref[:, _SQ:_SV], cos64, sin64, 16)
        _emit_q_t(qs_ref, _group_padded(sqk[:, :384] * (HEAD_DIM ** -0.5 * LOG2E)))
        ks_ref[0] = sqk[:, 384:].astype(BF16)
        _emit_v_t(vs_ref, z_ref[:, _SV:_GQ], SWA_SLAB)

        gqk = z_ref[:, _GQ:_GV]
        sq = gqk * gqk
        hi = sq.astype(BF16)
        lo = (sq - hi.astype(F32)).astype(BF16)
        ssum = (jnp.dot(hi, bd_ref[...], preferred_element_type=F32)
                + jnp.dot(lo, bd_ref[...], preferred_element_type=F32))
        gqk = gqk * lax.rsqrt(ssum * (1.0 / HEAD_DIM) + EPS) * gain_ref[...]
        gqk = _rope(gqk, cos64, sin64, 16)
        _emit_q_t(qg_ref, _group_padded(gqk[:, :384]))
        kg_ref[0] = gqk[:, 384:].astype(BF16)
        _emit_v_t(vg_ref, z_ref[:, _GV:_MQ], TK)

        qn = _rms(z_ref[:, _MQ:_MKV]) * mqg_ref[...]
        mq = jnp.dot(qn.astype(BF16), wuq_ref[...], preferred_element_type=F32)
        _emit_q_t(qm_ref, _rope(mq, cosm, sinm, 8) * (mla_scale * LOG2E))
        kvn = _rms(z_ref[:, _MKV:_KR]) * mkvg_ref[...]
        mkv = jnp.dot(kvn.astype(BF16), wukv_ref[...], preferred_element_type=F32)
        kr = _rope(z_ref[:, _KR:IN_W], cosm, sinm, 8)
        km_ref[0] = (mkv[:, :512] + jnp.concatenate([kr] * MLA_HEADS, axis=1)).astype(BF16)
        _emit_v_t(vm_ref, mkv[:, 512:], TK)

    for par, (z_new, z_old) in enumerate(((z0_ref, z1_ref), (z1_ref, z0_ref))):
        @pl.when(i % 2 == par)
        def _(z_new=z_new, z_old=z_old):
            project(z_new)
            emit(z_old)


def _input_projection(xs, modtab, g_mix, lw, tabs, n_ctx_blk):
    bsz, t_all, d = xs.shape
    nblk = t_all // TQ
    cur = lambda i: jnp.minimum(i, nblk - 1)
    prev = lambda i: jnp.maximum(i - 1, 0)
    tok = lambda w: pl.BlockSpec((1, TQ, w), lambda b, i: (b, prev(i), 0))
    q_t = lambda nh: pl.BlockSpec((1, nh, LANES, TQ), lambda b, i: (b, 0, 0, prev(i)))
    v_t = lambda nh, slab: pl.BlockSpec((1, TQ // slab, nh * V_ROWS, slab), lambda b, i: (b, prev(i), 0, 0))
    full = lambda a: pl.BlockSpec(a.shape, lambda b, i: (0,) * a.ndim)
    tab = pl.BlockSpec((TQ, LANES), lambda b, i: (prev(i), 0))
    q_shape = lambda nh: jax.ShapeDtypeStruct((bsz, nh, LANES, t_all), BF16)
    k_shape = lambda w: jax.ShapeDtypeStruct((bsz, t_all, w), BF16)
    v_shape = lambda nh, slab: jax.ShapeDtypeStruct((bsz, t_all // slab, nh * V_ROWS, slab), BF16)
    consts = (lw["gain512"], lw["mqg"], lw["mkvg"], lw["wuq"], lw["wukv"], lw["bd"])
    return pl.pallas_call(
        functools.partial(_in_kernel, mla_scale=float((MLA_NOPE + MLA_ROPE) ** -0.5)),
        out_shape=(q_shape(GQA_HEADS), k_shape(128), v_shape(GQA_KV, SWA_SLAB),
                   q_shape(GQA_HEADS), k_shape(128), v_shape(GQA_KV, TK),
                   q_shape(MLA_HEADS), k_shape(512), v_shape(MLA_HEADS, TK)),
        grid=(bsz, nblk + 1),
        in_specs=[pl.BlockSpec((1, TQ, d), lambda b, i: (b, cur(i), 0)),
                  pl.BlockSpec((1, 1, 8, d), lambda b, i: (b, jnp.where(cur(i) < n_ctx_blk, 1, 0), 0, 0)),
                  full(g_mix), full(lw["win"]), tab, tab, tab, tab] + [full(a) for a in consts],
        out_specs=(q_t(GQA_HEADS), tok(128), v_t(GQA_KV, SWA_SLAB),
                   q_t(GQA_HEADS), tok(128), v_t(GQA_KV, TK),
                   q_t(MLA_HEADS), tok(512), v_t(MLA_HEADS, TK)),
        scratch_shapes=[pltpu.VMEM((TQ, IN_W), F32)] * 2,
        compiler_params=_cparams(2),
        name="in_proj",
    )(xs, modtab, g_mix, lw["win"], *tabs, *consts)


_N_IN = {"swa": 5, "glb": 3, "mla": 3}
_N_SCRATCH = {"swa": 6, "glb": 12, "mla": 12}
_OUT_COL = {"swa": 0, "glb": GQA_HEADS * HEAD_DIM, "mla": 2 * GQA_HEADS * HEAD_DIM}


def _attn_kernel(*refs, n_ctx, n_ctx_blk, q_blk_off, seq):
    kinds = ("swa", "glb", "mla")
    n_in = sum(_N_IN[k] for k in kinds)
    o_ref = refs[n_in]
    i0, s0 = 0, n_in + 1
    for kind in kinds:
        _attn_group(refs[i0:i0 + _N_IN[kind]], o_ref, refs[s0:s0 + _N_SCRATCH[kind]], kind=kind, n_ctx=n_ctx,
                    n_ctx_blk=n_ctx_blk, q_blk_off=q_blk_off, seq=seq)
        i0 += _N_IN[kind]
        s0 += _N_SCRATCH[kind]


def _attn_group(ins, o_ref, scratch, *, kind, n_ctx, n_ctx_blk, q_blk_off, seq):
    if kind == "swa":
        sink_ref, bias_ref, q_ref, k_ref, v_ref = ins
        acc_ref, s_refs, p_refs = scratch[0], scratch[1:4], scratch[4:6]
    else:
        q_ref, k_ref, v_ref = ins
        acc_ref, m_ref = scratch[:2]
        s_refs, cm_refs, p_refs, al_refs = scratch[2:5], scratch[5:8], scratch[8:10], scratch[10:12]
    blk = pl.program_id(1) + q_blk_off
    is_lat = blk >= n_ctx_blk
    gqa = kind in ("swa", "glb")
    n_heads = GQA_HEADS if gqa else MLA_HEADS
    n_units = GQA_KV if gqa else MLA_HEADS
    unit_w = n_heads // n_units * TQ

    if kind == "swa":
        nk = TQ + 2 * WINDOW
        n_slab = n_ctx // SWA_SLAB
        row_l = lax.broadcasted_iota(jnp.int32, (V_ROWS, TQ), 0) == HEAD_DIM

        def scores(h, parts):
            cm = None
            for k_rows, _, bias, r0, nr in parts:
                s = jnp.dot(k_rows(), q_ref[0, h], preferred_element_type=F32)
                if bias is not None:
                    s = s + bias()
                s_refs[h % 3][r0:r0 + nr, :] = s
                c = jnp.max(s, axis=0, keepdims=True)
                cm = c if cm is None else jnp.maximum(cm, c)
            return cm

        def softmax(h, cm, nrows):
            m = jnp.maximum(cm, sink_ref[h])
            p_refs[h % 2][0:nrows, :] = jnp.exp2((s_refs[h % 3][0:nrows, :] - m).astype(BF16))
            return jnp.exp2(sink_ref[h] - m)

        def values(h, e_sink, parts):
            u, j = divmod(h, GQA_G)
            acc = None
            for _, v_t, _, r0, nr in parts:
                a = jnp.dot(v_t(u), p_refs[h % 2][r0:r0 + nr, :], preferred_element_type=F32)
                acc = a if acc is None else acc + a
            acc_ref[u, :, j * TQ:(j + 1) * TQ] = acc + jnp.where(row_l, e_sink, 0.0)

        def run(parts):
            nrows = sum(p[4] for p in parts)
            cms = {h: scores(h, parts) for h in range(min(2, n_heads))}
            e_sinks = {}
            for h in range(n_heads + 1):
                if h + 2 < n_heads:
                    cms[h + 2] = scores(h + 2, parts)
                if h < n_heads:
                    e_sinks[h] = softmax(h, cms[h], nrows)
                if h >= 1:
                    values(h - 1, e_sinks[h - 1], parts)

        ctx_part = (lambda: k_ref[0, 0:n_ctx, :],
                    lambda u: jnp.concatenate([v_ref[0, s, u * V_ROWS:(u + 1) * V_ROWS, :] for s in range(n_slab)],
                                              axis=1),
                    None, 0, n_ctx)

        def latent():
            q0 = (blk - n_ctx_blk) * TQ
            k0 = pl.multiple_of(jnp.clip(q0 - WINDOW, 0, seq - nk), WINDOW)
            geom = (q0 - k0) // WINDOW
            s0 = (n_ctx + k0) // SWA_SLAB
            win_part = (lambda: k_ref[0, pl.ds(pl.multiple_of(n_ctx + k0, WINDOW), nk), :],
                        lambda u: jnp.concatenate([v_ref[0, s0 + s, u * V_ROWS:(u + 1) * V_ROWS, :]
                                                   for s in range(nk // SWA_SLAB)], axis=1),
                        lambda: bias_ref[geom], n_ctx, nk)
            run([ctx_part, win_part])

        if q_blk_off < n_ctx_blk:
            pl.when(jnp.logical_not(is_lat))(lambda: run([ctx_part]))
            pl.when(is_lat)(latent)
        else:
            latent()
    else:
        m_ref[...] = jnp.full(m_ref.shape, NEG_INF, F32)
        acc_ref[...] = jnp.zeros(acc_ref.shape, F32)

        def qk(c, par, h):
            lanes = slice(h * TQ, (h + 1) * TQ)
            k_cols = slice(0, LANES) if gqa else slice(h * LANES, (h + 1) * LANES)
            k_rows = k_ref[0, pl.ds(pl.multiple_of(c * TK, TK), TK), k_cols]
            s = jnp.dot(k_rows, q_ref[0, h], preferred_element_type=F32)
            s_refs[par][:, lanes] = s
            cm_refs[par][:, lanes] = jnp.max(s, axis=0, keepdims=True)

        def softmax(sb, par, h):
            lanes = slice(h * TQ, (h + 1) * TQ)
            m_prev = m_ref[:, lanes]
            m_new = jnp.maximum(m_prev, cm_refs[sb][:, lanes])
            m_ref[:, lanes] = m_new
            al_refs[par][:, lanes] = jnp.exp2(m_prev - m_new)
            p_refs[par][:, lanes] = jnp.exp2((s_refs[sb][:, lanes] - m_new).astype(BF16))

        def pv(c, par, h):
            lanes = slice(h * TQ, (h + 1) * TQ)
            u, j = divmod(h, n_heads // n_units)
            ul = slice(j * TQ, (j + 1) * TQ)
            acc_ref[u, :, ul] = (acc_ref[u, :, ul] * al_refs[par][:, lanes]
                                 + jnp.dot(v_ref[0, c, u * V_ROWS:(u + 1) * V_ROWS, :], p_refs[par][:, lanes],
                                           preferred_element_type=F32))

        def step(i, r, n):
            for h in range(n_heads):
                if not isinstance(i, int) or i + 2 < n:
                    qk(i + 2, (r + 2) % 3, h)
                if not isinstance(i, int) or i < n:
                    softmax(r % 3, r % 2, h)
                if not isinstance(i, int) or i >= 1:
                    pv(i - 1, (r + 1) % 2, h)

        def run(n):
            for c in range(min(2, n)):
                for h in range(n_heads):
                    qk(c, c, h)
            step(0, 0, n)
            n_groups = max(0, (n - 3) // 6)

            def body(j, carry):
                for r in range(6):
                    step(1 + 6 * j + r, (1 + r) % 6, n)
                return carry
            if n_groups:
                lax.fori_loop(0, n_groups, body, 0)
            for i in range(1 + 6 * n_groups, n + 1):
                step(i, i % 6, n)

        if q_blk_off < n_ctx_blk:
            pl.when(jnp.logical_not(is_lat))(lambda: run(n_ctx // TK))
            pl.when(is_lat)(lambda: run((n_ctx + seq) // TK))
        else:
            run((n_ctx + seq) // TK)

    heads = []
    for u in range(n_units):
        a = acc_ref[u]
        o = a[:HEAD_DIM] / a[HEAD_DIM:HEAD_DIM + 1]
        heads += [o[:, j * TQ:(j + 1) * TQ] for j in range(unit_w // TQ)]
    tiles = [jnp.concatenate(heads[2 * t:2 * t + 2], axis=0).T for t in range(n_heads // 2)]
    c0 = _OUT_COL[kind]
    o_ref[0, :, c0:c0 + n_heads * HEAD_DIM] = jnp.concatenate(tiles, axis=1).astype(o_ref.dtype)


def _attention(groups, sink, n_ctx, with_ctx_queries):
    bsz, _, _, t_all = groups["swa"][0].shape
    seq = t_all - n_ctx
    n_ctx_blk = n_ctx // TQ
    q_blk_off = 0 if with_ctx_queries else n_ctx_blk
    nblk = t_all // TQ - q_blk_off
    in_specs, args, scratch = [], [], []
    for kind in ("swa", "glb", "mla"):
        q_t, k, v_t = groups[kind]
        n_heads = q_t.shape[1]
        n_units = GQA_KV if kind in ("swa", "glb") else MLA_HEADS
        nq = n_heads * TQ
        if kind == "swa":
            nk = TQ + 2 * WINDOW
            off = np.arange(TQ)[None, None, :] + WINDOW * np.arange(3)[:, None, None] - np.arange(nk)[None, :, None]
            bias = jnp.asarray(np.where(np.abs(off) <= WINDOW, 0.0, NEG_INF), F32)
            in_specs += [pl.BlockSpec(memory_space=pltpu.SMEM), pl.BlockSpec(bias.shape, lambda b, i: (0, 0, 0))]
            args += [sink, bias]
        in_specs += [pl.BlockSpec((1, n_heads, LANES, TQ), lambda b, i: (b, 0, 0, i + q_blk_off)),
                     pl.BlockSpec((1,) + k.shape[1:], lambda b, i: (b, 0, 0)),
                     pl.BlockSpec((1,) + v_t.shape[1:], lambda b, i: (b, 0, 0, 0))]
        args += [q_t, k, v_t]
        scratch += [pltpu.VMEM((n_units, V_ROWS, nq // n_units), F32)]
        if kind == "swa":
            rows = n_ctx + nk
            scratch += [pltpu.VMEM((rows, TQ), F32)] * 3 + [pltpu.VMEM((rows, TQ), BF16)] * 2
        else:
            scratch += ([pltpu.VMEM((1, nq), F32)] + [pltpu.VMEM((TK, nq), F32)] * 3 + [pltpu.VMEM((1, nq), F32)] * 3
                        + [pltpu.VMEM((TK, nq), BF16)] * 2 + [pltpu.VMEM((1, nq), F32)] * 2)
    return pl.pallas_call(
        functools.partial(_attn_kernel, n_ctx=n_ctx, n_ctx_blk=n_ctx_blk, q_blk_off=q_blk_off, seq=seq),
        out_shape=jax.ShapeDtypeStruct((bsz, nblk * TQ, D_MODEL), BF16),
        grid=(bsz, nblk),
        in_specs=in_specs,
        out_specs=pl.BlockSpec((1, TQ, D_MODEL), lambda b, i: (b, i, 0)),
        scratch_shapes=scratch,
        compiler_params=_cparams(2),
        name="attention",
    )(*args)


def _row_select(rows, idx):
    out = rows[0]
    for j in range(1, len(rows)):
        out = jnp.where(idx == j, rows[j], out)
    return out


def _route(scores, biased):
    def top2sum(a, b, c, d):
        hi1, lo1, hi2, lo2 = jnp.maximum(a, b), jnp.minimum(a, b), jnp.maximum(c, d), jnp.minimum(c, d)
        return jnp.maximum(hi1, hi2) + jnp.maximum(jnp.minimum(hi1, hi2), jnp.maximum(lo1, lo2))

    gs = [top2sum(*biased[4 * g:4 * g + 4]) for g in range(N_GROUPS)]
    best, gi = gs[0], jnp.zeros(gs[0].shape, jnp.int32)
    for g in range(1, N_GROUPS):
        better = gs[g] > best
        gi = jnp.where(better, g, gi)
        best = jnp.where(better, gs[g], best)
    a = [_row_select([biased[4 * g + j] for g in range(N_GROUPS)], gi) for j in range(EXPERTS_PER_GROUP)]
    s = [_row_select([scores[4 * g + j] for g in range(N_GROUPS)], gi) for j in range(EXPERTS_PER_GROUP)]
    v1, i1 = a[0], jnp.zeros(gi.shape, jnp.int32)
    for j in range(1, EXPERTS_PER_GROUP):
        better = a[j] > v1
        i1 = jnp.where(better, j, i1)
        v1 = jnp.where(better, a[j], v1)
    v2, i2 = jnp.full(v1.shape, -3.0e38, F32), jnp.zeros(gi.shape, jnp.int32)
    for j in range(EXPERTS_PER_GROUP):
        better = (i1 != j) & (a[j] > v2)
        i2 = jnp.where(better, j, i2)
        v2 = jnp.where(better, a[j], v2)
    lo, hi = jnp.minimum(i1, i2), jnp.maximum(i1, i2)
    pair = jnp.where(lo == 0, hi - 1, jnp.where(lo == 1, hi + 1, 5))
    s_lo, s_hi = _row_select(s, lo), _row_select(s, hi)
    den = s_lo + s_hi
    return gi * N_PAIRS + pair, s_lo / den, s_hi / den


def _out_kernel(y_ref, x_ref, mod_ref, wout_ref, g_ref, rwh_ref, rwl_ref, rb_ref, tri_ref,
                xo_ref, row_ref, meta_ref, cnt_ref, carry_ref):
    @pl.when((pl.program_id(0) == 0) & (pl.program_id(1) == 0))
    def _():
        carry_ref[...] = jnp.zeros(carry_ref.shape, F32)

    y = y_ref[0]
    mod = mod_ref[0, 0]
    half = D_MODEL // 2
    a = jnp.concatenate([jnp.dot(y, wout_ref[:, :half], preferred_element_type=F32),
                         jnp.dot(y, wout_ref[:, half:], preferred_element_type=F32)], axis=1)
    x = x_ref[0] + mod[2:3] * a
    xo_ref[0] = x
    h = _rms(x) * g_ref[...]
    h = h * (1.0 + mod[4:5]) + mod[3:4]

    hh = h.astype(BF16)
    hl = (h - hh.astype(F32)).astype(BF16)
    logits = (jnp.dot(hh, rwh_ref[...], preferred_element_type=F32)
              + jnp.dot(hl, rwh_ref[...], preferred_element_type=F32)
              + jnp.dot(hh, rwl_ref[...], preferred_element_type=F32)).T[:32]
    sc = jax.nn.sigmoid(logits)
    bs = sc + rb_ref[...]
    scores = [sc[e:e + 1, :] for e in range(N_EXPERTS)]
    biased = [bs[e:e + 1, :] for e in range(N_EXPERTS)]
    bucket, g_lo, g_hi = _route(scores, biased)

    onehot = jnp.where(lax.broadcasted_iota(jnp.int32, (32, TQ), 0) == bucket, 1.0, 0.0)
    prefix = jnp.dot(onehot.astype(BF16), tri_ref[...], preferred_element_type=F32)
    carry = carry_ref[:, 0:1]
    rank = jnp.sum(onehot * (carry + prefix - 1.0), axis=0, keepdims=True)
    carry_new = jnp.broadcast_to(carry + prefix[:, TQ - 1:TQ], carry_ref.shape)
    carry_ref[...] = carry_new
    cnt_ref[...] = carry_new
    meta_ref[0] = jnp.concatenate([bucket, rank.astype(jnp.int32), jnp.zeros((6, TQ), jnp.int32)], axis=0)

    gates = jnp.concatenate([g_lo, g_hi, jnp.zeros((LANES - 2, TQ), F32)], axis=0)
    row_ref[...] = jnp.concatenate([h, gates.T], axis=1)


def _output_projection(y_mix, xs, modtab, lw, shared, n_ctx_blk, with_ctx):
    bsz, t_all, d = xs.shape
    off = 0 if with_ctx else n_ctx_blk
    nblk = t_all // TQ - off
    tok = lambda w: pl.BlockSpec((1, TQ, w), lambda b, i: (b, i, 0))
    full = lambda a: pl.BlockSpec(a.shape, lambda b, i: (0,) * a.ndim)
    consts = (lw["wout"], lw["g_ffn"], shared["rw_hi"], shared["rw_lo"], shared["rb"], shared["tri"])
    return pl.pallas_call(
        _out_kernel,
        out_shape=(jax.ShapeDtypeStruct((bsz, nblk * TQ, d), F32),
                   jax.ShapeDtypeStruct((bsz * nblk * TQ, ROW_W), F32),
                   jax.ShapeDtypeStruct((bsz * nblk, 8, TQ), jnp.int32),
                   jax.ShapeDtypeStruct((32, LANES), F32)),
        grid=(bsz, nblk),
        in_specs=[tok(d),
                  pl.BlockSpec((1, TQ, d), lambda b, i: (b, i + off, 0)),
                  pl.BlockSpec((1, 1, 8, d), lambda b, i: (b, jnp.where(i + off < n_ctx_blk, 1, 0), 0, 0))]
                 + [full(a) for a in consts],
        out_specs=(tok(d),
                   pl.BlockSpec((TQ, ROW_W), lambda b, i: (b * nblk + i, 0)),
                   pl.BlockSpec((1, 8, TQ), lambda b, i: (b * nblk + i, 0, 0)),
                   pl.BlockSpec((32, LANES), lambda b, i: (0, 0))),
        scratch_shapes=[pltpu.VMEM((32, LANES), F32)],
        compiler_params=_cparams(2),
        name="out_proj_router",
    )(y_mix, xs, modtab, *consts)


def _scatter_kernel(dest_ref, row_ref, init_ref, out_ref, buf, sem, *, n_steps):
    del init_ref
    i = pl.program_id(0)
    slot = i % 2

    def wait_slot(s):
        pltpu.make_async_copy(buf.at[s], out_ref.at[pl.ds(0, TQ)], sem.at[s]).wait()

    for s in range(2):
        @pl.when(slot == s)
        def _(s=s):
            @pl.when(i >= 2)
            def _():
                wait_slot(s)

            buf[s] = row_ref[...]
            for r in range(TQ):
                d = dest_ref[i * TQ + r]
                pltpu.make_async_copy(buf.at[s, pl.ds(r, 1)], out_ref.at[pl.ds(d, 1)], sem.at[s]).start()

    @pl.when(i == n_steps - 1)
    def _():
        wait_slot(slot)
        if n_steps >= 2:
            wait_slot(1 - slot)


def _scatter_rows(dest, rows, n_sorted):
    n_tok = rows.shape[0]
    n_steps = n_tok // TQ
    init = jnp.zeros((n_sorted, ROW_W), F32)
    return pl.pallas_call(
        functools.partial(_scatter_kernel, n_steps=n_steps),
        out_shape=jax.ShapeDtypeStruct((n_sorted, ROW_W), F32),
        grid_spec=pltpu.PrefetchScalarGridSpec(
            num_scalar_prefetch=1,
            grid=(n_steps,),
            in_specs=[pl.BlockSpec((TQ, ROW_W), lambda i, d: (i, 0)),
                      pl.BlockSpec(memory_space=pl.ANY)],
            out_specs=pl.BlockSpec(memory_space=pl.ANY),
            scratch_shapes=[pltpu.VMEM((2, TQ, ROW_W), F32), pltpu.SemaphoreType.DMA((2,))]),
        input_output_aliases={2: 0},
        compiler_params=_cparams(1),
        name="moe_scatter",
    )(dest, rows, init)


def _swiglu(h, wgu, wd):
    u = jnp.dot(h, wgu, preferred_element_type=F32)
    a = u[:, :D_EXPERT]
    a = a * jax.nn.sigmoid(a) * u[:, D_EXPERT:]
    return jnp.dot(a.astype(BF16), wd, preferred_element_type=F32)


def _moe_kernel(elo_ref, ehi_ref, nv_ref, row_ref, wgl_ref, wdl_ref, wgh_ref, wdh_ref, swg_ref, swd_ref, y_ref):
    del elo_ref, ehi_ref
    j = pl.program_id(0)

    @pl.when(j < nv_ref[0])
    def _():
        rows = row_ref[...]
        h = rows[:, :D_MODEL].astype(BF16)
        g_lo = rows[:, D_MODEL:D_MODEL + 1]
        g_hi = rows[:, D_MODEL + 1:D_MODEL + 2]
        y_ref[...] = (g_lo * _swiglu(h, wgl_ref[0], wdl_ref[0]) + g_hi * _swiglu(h, wgh_ref[0], wdh_ref[0])
                      + _swiglu(h, swg_ref[...], swd_ref[...]))

    @pl.when(j >= nv_ref[0])
    def _():
        y_ref[...] = jnp.zeros(y_ref.shape, F32)


def _grouped_experts(e_lo, e_hi, n_valid, rows_sorted, lw):
    n_sorted = rows_sorted.shape[0]
    nb = n_sorted // TM
    wgu, wd, swgu, swd = lw["wgu"], lw["wd"], lw["swgu"], lw["swd"]
    return pl.pallas_call(
        _moe_kernel,
        out_shape=jax.ShapeDtypeStruct((n_sorted, D_MODEL), F32),
        grid_spec=pltpu.PrefetchScalarGridSpec(
            num_scalar_prefetch=3,
            grid=(nb,),
            in_specs=[pl.BlockSpec((TM, ROW_W), lambda j, lo, hi, nv: (j, 0)),
                      pl.BlockSpec((1,) + wgu.shape[1:], lambda j, lo, hi, nv: (lo[j], 0, 0)),
                      pl.BlockSpec((1,) + wd.shape[1:], lambda j, lo, hi, nv: (lo[j], 0, 0)),
                      pl.BlockSpec((1,) + wgu.shape[1:], lambda j, lo, hi, nv: (hi[j], 0, 0)),
                      pl.BlockSpec((1,) + wd.shape[1:], lambda j, lo, hi, nv: (hi[j], 0, 0)),
                      pl.BlockSpec(swgu.shape, lambda j, lo, hi, nv: (0, 0)),
                      pl.BlockSpec(swd.shape, lambda j, lo, hi, nv: (0, 0))],
            out_specs=pl.BlockSpec((TM, D_MODEL), lambda j, lo, hi, nv: (j, 0))),
        compiler_params=_cparams(1),
        name="moe_experts",
    )(e_lo, e_hi, n_valid, rows_sorted, wgu, wd, wgu, wd, swgu, swd)


def _gather_kernel(dest_ref, y_ref, x_ref, mod_ref, gf_ref, o_ref, fbuf, sem, *, n_steps, final_norm):
    i = pl.program_id(0)
    slot = i % 2

    def issue(step, s):
        for r in range(TQ):
            d = dest_ref[step * TQ + r]
            pltpu.make_async_copy(y_ref.at[pl.ds(d, 1)], fbuf.at[s, pl.ds(r, 1)], sem.at[s]).start()

    @pl.when(i == 0)
    def _():
        issue(0, 0)

    for s in range(2):
        @pl.when((i + 1 < n_steps) & (slot == s))
        def _(s=s):
            issue(i + 1, 1 - s)

    pltpu.make_async_copy(y_ref.at[pl.ds(0, TQ)], fbuf.at[slot], sem.at[slot]).wait()
    x = x_ref[0] + mod_ref[0, 0][5:6] * fbuf[slot]
    if final_norm:
        x = _rms(x) * gf_ref[...]
    o_ref[0] = x


def _gather_residual(dest, y_sorted, x_mid, modtab, g_final, n_ctx_blk, with_ctx, final_norm):
    bsz, t_rows, d = x_mid.shape
    nblk = t_rows // TQ
    off = 0 if with_ctx else n_ctx_blk
    n_steps = bsz * nblk
    return pl.pallas_call(
        functools.partial(_gather_kernel, n_steps=n_steps, final_norm=final_norm),
        out_shape=jax.ShapeDtypeStruct((bsz, t_rows, d), F32),
        grid_spec=pltpu.PrefetchScalarGridSpec(
            num_scalar_prefetch=1,
            grid=(n_steps,),
            in_specs=[pl.BlockSpec(memory_space=pl.ANY),
                      pl.BlockSpec((1, TQ, d), lambda i, dst: (i // nblk, i % nblk, 0)),
                      pl.BlockSpec((1, 1, 8, d),
                                   lambda i, dst: (i // nblk, jnp.where(i % nblk + off < n_ctx_blk, 1, 0), 0, 0)),
                      pl.BlockSpec(g_final.shape, lambda i, dst: (0, 0))],
            out_specs=pl.BlockSpec((1, TQ, d), lambda i, dst: (i // nblk, i % nblk, 0)),
            scratch_shapes=[pltpu.VMEM((2, TQ, d), F32), pltpu.SemaphoreType.DMA((2,))]),
        compiler_params=_cparams(1),
        name="moe_gather",
    )(dest, y_sorted, x_mid, modtab, g_final)


def _bucket_layout(meta, counts, n_tok):
    bucket = meta[:, 0, :].reshape(-1)
    rank = meta[:, 1, :].reshape(-1)
    cnt = counts[:N_BUCKETS, 0].astype(jnp.int32)
    padded = (cnt + TM - 1) // TM * TM
    pad_end = jnp.cumsum(padded)
    pad_start = pad_end - padded
    dest = pad_start[bucket] + rank
    nb = n_tok // TM + N_BUCKETS
    n_valid = pad_end[-1] // TM
    blk = jnp.arange(nb, dtype=jnp.int32)
    blk_row = jnp.minimum(blk, n_valid - 1) * TM
    blk_bucket = jnp.minimum(jnp.sum(pad_end[None, :] <= blk_row[:, None], axis=1), N_BUCKETS - 1).astype(jnp.int32)
    grp, pair = blk_bucket // N_PAIRS, blk_bucket % N_PAIRS
    e_lo = grp * EXPERTS_PER_GROUP + jnp.asarray(_PAIR_LO, jnp.int32)[pair]
    e_hi = grp * EXPERTS_PER_GROUP + jnp.asarray(_PAIR_HI, jnp.int32)[pair]
    return dest.astype(jnp.int32), e_lo, e_hi, n_valid.reshape(1).astype(jnp.int32), nb * TM


def _rope_tables(seq, n_ctx):
    t = jnp.arange(seq)
    rows, cols = t // GRID_W, t % GRID_W

    def table(width, nf):
        lane = np.arange(width)
        half, j = lane // (2 * nf), lane % (2 * nf)
        inv = ROPE_THETA ** (-jnp.arange(nf, dtype=F32) / nf)
        pos = jnp.where(jnp.asarray(half == 0)[None, :], rows[:, None], cols[:, None]).astype(F32)
        ang = pos * inv[j % nf][None, :]
        sign = jnp.asarray(np.where(j < nf, -1.0, 1.0), F32)[None, :]
        return jnp.cos(ang), jnp.sin(ang) * sign

    cos64, sin64 = table(HEAD_DIM, 16)
    cos64, sin64 = jnp.tile(cos64, (1, 2)), jnp.tile(sin64, (1, 2))
    cosr, sinr = table(MLA_ROPE, 8)
    cosm = jnp.concatenate([jnp.ones((seq, MLA_NOPE), F32), cosr, jnp.ones((seq, 32), F32)], axis=1)
    sinm = jnp.concatenate([jnp.zeros((seq, MLA_NOPE), F32), sinr, jnp.zeros((seq, 32), F32)], axis=1)
    ctx1, ctx0 = jnp.ones((n_ctx, LANES), F32), jnp.zeros((n_ctx, LANES), F32)
    return tuple(jnp.concatenate([c, a], axis=0) for c, a in ((ctx1, cos64), (ctx0, sin64), (ctx1, cosm), (ctx0, sinm)))


def _layer_weights(l, w_in, w_out, norm_ffn_g, glb_q_gain, glb_k_gain, mla_q_gain, mla_w_uq, mla_kv_gain,
                   mla_w_ukv, exp_w_gate, exp_w_up, exp_w_down, shr_w_gate, shr_w_up, shr_w_down):
    d = w_in.shape[1]
    wi = w_in[l]
    kr = wi[:, 1664:1696]
    win = jnp.concatenate([wi[:, :1664], jnp.zeros((d, MLA_NOPE), F32), kr, jnp.zeros((d, 32), F32)], axis=1)
    uq = mla_w_uq[l].reshape(MLA_Q_RANK, MLA_HEADS, MLA_NOPE + MLA_ROPE)
    wuq = jnp.concatenate([uq, jnp.zeros((MLA_Q_RANK, MLA_HEADS, 32), F32)], axis=2).reshape(MLA_Q_RANK, 512)
    ukv = mla_w_ukv[l].reshape(MLA_KV_RANK, MLA_HEADS, MLA_NOPE + MLA_V)
    wk = jnp.concatenate([ukv[:, :, :MLA_NOPE], jnp.zeros((MLA_KV_RANK, MLA_HEADS, 64), F32)], axis=2)
    wukv = jnp.concatenate([wk.reshape(MLA_KV_RANK, 512), ukv[:, :, MLA_NOPE:].reshape(MLA_KV_RANK, 256)], axis=1)
    gain512 = jnp.concatenate([jnp.tile(glb_q_gain[l] * (HEAD_DIM ** -0.5 * LOG2E), GQA_HEADS),
                               jnp.tile(glb_k_gain[l], GQA_KV)]).reshape(1, 512)
    head = np.arange(512) // HEAD_DIM
    return {
        "win": win.astype(BF16),
        "wuq": wuq.astype(BF16),
        "wukv": wukv.astype(BF16),
        "gain512": gain512,
        "mqg": mla_q_gain[l].reshape(1, -1),
        "mkvg": mla_kv_gain[l].reshape(1, -1),
        "bd": jnp.asarray(head[:, None] == head[None, :], BF16),
        "wout": w_out[l].astype(BF16),
        "g_ffn": norm_ffn_g[l].reshape(1, -1),
        "wgu": jnp.concatenate([exp_w_gate[l], exp_w_up[l]], axis=2).astype(BF16),
        "wd": exp_w_down[l].astype(BF16),
        "swgu": jnp.concatenate([shr_w_gate[l], shr_w_up[l]], axis=1).astype(BF16),
        "swd": shr_w_down[l].astype(BF16),
    }


def kernel(x, c, ctx, c_ctx, w_mod, b_mod, norm_mix_g, norm_ffn_g, w_in, w_out, swa_sink, glb_q_gain, glb_k_gain,
           mla_q_gain, mla_w_uq, mla_kv_gain, mla_w_ukv, router_w, router_bias, exp_w_gate, exp_w_up, exp_w_down,
           shr_w_gate, shr_w_up, shr_w_down, final_norm_g):
    bsz, seq, d = x.shape
    n_ctx = ctx.shape[1]
    n_layers = w_mod.shape[0]
    assert d == D_MODEL and n_ctx % TQ == 0 and seq % TQ == 0 and seq >= TQ + 2 * WINDOW and seq % GRID_W == 0
    n_ctx_blk = n_ctx // TQ

    rows = -(-(bsz + 1) // 8) * 8
    c_rows = jnp.concatenate([c, c_ctx[None, :], jnp.zeros((rows - bsz - 1, d), F32)], axis=0)
    mods = _modulation(c_rows, w_mod, b_mod)
    mod_x = mods[:, :bsz].reshape(n_layers, bsz, 6, d)
    mod_c = jnp.broadcast_to(mods[:, bsz].reshape(n_layers, 1, 6, d), (n_layers, bsz, 6, d))
    modtabs = jnp.pad(jnp.stack([mod_x, mod_c], axis=2), ((0, 0), (0, 0), (0, 0), (0, 2), (0, 0)))

    tabs = _rope_tables(seq, n_ctx)
    rw = jnp.pad(router_w, ((0, 0), (0, LANES - N_EXPERTS)))
    rw_hi = rw.astype(BF16)
    shared = {
        "rw_hi": rw_hi,
        "rw_lo": (rw - rw_hi.astype(F32)).astype(BF16),
        "rb": jnp.pad(router_bias, (0, 32 - N_EXPERTS)).reshape(32, 1),
        "tri": jnp.asarray(np.arange(TQ)[:, None] <= np.arange(TQ)[None, :], BF16),
    }
    g_final = final_norm_g.reshape(1, d)

    xs = jnp.concatenate([ctx, x], axis=1)
    for l in range(n_layers):
        last = l == n_layers - 1
        with_ctx = not last
        lw = _layer_weights(l, w_in, w_out, norm_ffn_g, glb_q_gain, glb_k_gain, mla_q_gain, mla_w_uq,
                            mla_kv_gain, mla_w_ukv, exp_w_gate, exp_w_up, exp_w_down,
                            shr_w_gate, shr_w_up, shr_w_down)
        modtab = modtabs[l]
        qs, ks, vs, qg, kg, vg, qm, km, vm = _input_projection(
            xs, modtab, norm_mix_g[l].reshape(1, d), lw, tabs, n_ctx_blk)
        sink = jnp.pad(swa_sink[l] * LOG2E, (0, 8 - GQA_HEADS))
        y_mix = _attention({"swa": (qs, ks, vs), "glb": (qg, kg, vg), "mla": (qm, km, vm)}, sink, n_ctx, with_ctx)
        x_mid, rows_tok, meta, counts = _output_projection(y_mix, xs, modtab, lw, shared, n_ctx_blk, with_ctx)
        n_tok = rows_tok.shape[0]
        dest, e_lo, e_hi, n_valid, n_sorted = _bucket_layout(meta, counts, n_tok)
        rows_sorted = _scatter_rows(dest, rows_tok, n_sorted)
        y_sorted = _grouped_experts(e_lo, e_hi, n_valid, rows_sorted, lw)
        xs = _gather_residual(dest, y_sorted, x_mid, modtab, g_final, n_ctx_blk, with_ctx, final_norm=last)
    return xs
```

```python
import functools

import jax
import jax.numpy as jnp
import numpy as np
from jax import lax
from jax.experimental import pallas as pl
from jax.experimental.pallas import tpu as pltpu

F32 = jnp.float32
BF16 = jnp.bfloat16

D_MODEL = 1024
GRID_W = 64
HEAD_DIM = 64
GQA_HEADS = 6
GQA_KV = 2
GQA_G = GQA_HEADS // GQA_KV
WINDOW = 128
MLA_HEADS = 4
MLA_NOPE = 64
MLA_ROPE = 32
MLA_V = 64
MLA_Q_RANK = 256
MLA_KV_RANK = 128
ROPE_THETA = 10000.0
N_EXPERTS = 16
N_GROUPS = 4
EXPERTS_PER_GROUP = 4
N_PAIRS = 6
N_BUCKETS = N_GROUPS * N_PAIRS
D_EXPERT = 512
EPS = 1e-6
NEG_INF = -1e30

LANES = 128
TQ = 256
TK = 256
TM = 256
SWA_SLAB = 128
V_ROWS = 80
LOG2E = 1.4426950408889634
ROW_W = D_MODEL + LANES
IN_W = 1792
VMEM_LIMIT = 56 * 1024 * 1024

_SQ, _SK, _SV, _GQ, _GK, _GV, _MQ, _MKV, _KR = 0, 384, 512, 640, 1024, 1152, 1280, 1536, 1664

_PAIR_LO = (0, 0, 0, 1, 1, 2)
_PAIR_HI = (1, 2, 3, 2, 3, 3)


def _cparams(n_axes):
    return pltpu.CompilerParams(dimension_semantics=("arbitrary",) * n_axes,
                                vmem_limit_bytes=VMEM_LIMIT)


def _mod_kernel(c_ref, w_ref, b_ref, o_ref):
    c = c_ref[...]
    a = (c * jax.nn.sigmoid(c)).astype(BF16)
    o_ref[0] = jnp.dot(a, w_ref[0].astype(BF16), preferred_element_type=F32) + b_ref[0]


def _modulation(c_rows, w_mod, b_mod):
    n_layers, d, width = w_mod.shape
    rows = c_rows.shape[0]
    nb = 1536
    return pl.pallas_call(
        _mod_kernel,
        out_shape=jax.ShapeDtypeStruct((n_layers, rows, width), F32),
        grid=(n_layers, width // nb),
        in_specs=[pl.BlockSpec((rows, d), lambda l, j: (0, 0)),
                  pl.BlockSpec((1, d, nb), lambda l, j: (l, 0, j)),
                  pl.BlockSpec((1, 1, nb), lambda l, j: (l, 0, j))],
        out_specs=pl.BlockSpec((1, rows, nb), lambda l, j: (l, 0, j)),
        compiler_params=_cparams(2),
        name="adaln_mod",
    )(c_rows, w_mod, b_mod.reshape(n_layers, 1, width))


def _rope(x, cos, sin_signed, nf):
    lane = lax.broadcasted_iota(jnp.int32, (1, LANES), 1)
    first = (lane % (2 * nf)) < nf
    tiles = []
    for t in range(x.shape[1] // LANES):
        xt = x[:, t * LANES:(t + 1) * LANES]
        partner = jnp.where(first, pltpu.roll(xt, LANES - nf, 1), pltpu.roll(xt, nf, 1))
        tiles.append(xt * cos + partner * sin_signed)
    return tiles[0] if len(tiles) == 1 else jnp.concatenate(tiles, axis=1)


def _group_padded(q):
    lane = lax.broadcasted_iota(jnp.int32, (1, LANES), 1)
    tiles = []
    for h in range(GQA_HEADS):
        g = h // GQA_G
        tile = q[:, (h // 2) * LANES:(h // 2 + 1) * LANES]
        if (h % 2) != g:
            tile = pltpu.roll(tile, HEAD_DIM, 1)
        tiles.append(jnp.where((lane // HEAD_DIM) == g, tile, 0.0))
    return jnp.concatenate(tiles, axis=1)


def _emit_q_t(ref, q):
    for h in range(q.shape[1] // LANES):
        ref[0, h] = q[:, h * LANES:(h + 1) * LANES].T.astype(ref.dtype)


def _emit_v_t(ref, v, slab):
    ones_blk = jnp.where(lax.broadcasted_iota(jnp.int32, (V_ROWS - HEAD_DIM, TQ), 0) == 0, 1.0, 0.0)
    parts = []
    for t in range(v.shape[1] // LANES):
        vt = v[:, t * LANES:(t + 1) * LANES].T
        parts += [vt[:HEAD_DIM], ones_blk, vt[HEAD_DIM:], ones_blk]
    ext = jnp.concatenate(parts, axis=0).astype(ref.dtype)
    for s in range(TQ // slab):
        ref[0, s] = ext[:, s * slab:(s + 1) * slab]


def _rms(x, eps=EPS):
    return x * lax.rsqrt(jnp.mean(x * x, axis=-1, keepdims=True) + eps)


def _stream_specs(stream, n_ctx_blk, blk_off):
    lat, lat_off, ctx = stream
    d = lat.shape[2]
    return [pl.BlockSpec((1, TQ, d), lambda b, i: (b, jnp.maximum(i + blk_off - n_ctx_blk, 0) + lat_off, 0)),
            pl.BlockSpec((1, TQ, d), lambda b, i: (b, jnp.minimum(i + blk_off, n_ctx_blk - 1), 0))]


def _in_kernel(x_ref, c_ref, mod_ref, g_ref, win_ref, cos64_ref, sin64_ref, cosm_ref, sinm_ref,
               gain_ref, mqg_ref, mkvg_ref, wuq_ref, wukv_ref, bd_ref,
               qs_ref, ks_ref, vs_ref, qg_ref, kg_ref, vg_ref, qm_ref, km_ref, vm_ref, *, mla_scale, n_ctx_blk):
    x = jnp.where(pl.program_id(1) < n_ctx_blk, c_ref[0], x_ref[0])
    mod = mod_ref[0, 0]
    h = _rms(x) * g_ref[...]
    h = h * (1.0 + mod[1:2]) + mod[0:1]
    z = jnp.dot(h.astype(BF16), win_ref[...], preferred_element_type=F32)
    cos64, sin64 = cos64_ref[...], sin64_ref[...]
    cosm, sinm = cosm_ref[...], sinm_ref[...]

    sqk = _rope(z[:, _SQ:_SV], cos64, sin64, 16)
    _emit_q_t(qs_ref, _group_padded(sqk[:, :384] * (HEAD_DIM ** -0.5 * LOG2E)))
    ks_ref[0] = sqk[:, 384:].astype(BF16)
    _emit_v_t(vs_ref, z[:, _SV:_GQ], SWA_SLAB)

    gqk = z[:, _GQ:_GV]
    sq = gqk * gqk
    hi = sq.astype(BF16)
    lo = (sq - hi.astype(F32)).astype(BF16)
    ssum = (jnp.dot(hi, bd_ref[...], preferred_element_type=F32)
            + jnp.dot(lo, bd_ref[...], preferred_element_type=F32))
    gqk = gqk * lax.rsqrt(ssum * (1.0 / HEAD_DIM) + EPS) * gain_ref[...]
    gqk = _rope(gqk, cos64, sin64, 16)
    _emit_q_t(qg_ref, _group_padded(gqk[:, :384]))
    kg_ref[0] = gqk[:, 384:].astype(BF16)
    _emit_v_t(vg_ref, z[:, _GV:_MQ], TK)

    qn = _rms(z[:, _MQ:_MKV]) * mqg_ref[...]
    mq = jnp.dot(qn.astype(BF16), wuq_ref[...], preferred_element_type=F32)
    _emit_q_t(qm_ref, _rope(mq, cosm, sinm, 8) * (mla_scale * LOG2E))
    kvn = _rms(z[:, _MKV:_KR]) * mkvg_ref[...]
    mkv = jnp.dot(kvn.astype(BF16), wukv_ref[...], preferred_element_type=F32)
    kr = _rope(z[:, _KR:IN_W], cosm, sinm, 8)
    km_ref[0] = (mkv[:, :512] + jnp.concatenate([kr] * MLA_HEADS, axis=1)).astype(BF16)
    _emit_v_t(vm_ref, mkv[:, 512:], TK)


def _input_projection(stream, t_all, modtab, g_mix, lw, tabs, n_ctx_blk):
    bsz, _, d = stream[0].shape
    nblk = t_all // TQ
    tok = lambda w: pl.BlockSpec((1, TQ, w), lambda b, i: (b, i, 0))
    q_t = lambda nh: pl.BlockSpec((1, nh, LANES, TQ), lambda b, i: (b, 0, 0, i))
    v_t = lambda nh, slab: pl.BlockSpec((1, TQ // slab, nh * V_ROWS, slab), lambda b, i: (b, i, 0, 0))
    full = lambda a: pl.BlockSpec(a.shape, lambda b, i: (0,) * a.ndim)
    tab = pl.BlockSpec((TQ, LANES), lambda b, i: (i, 0))
    q_shape = lambda nh: jax.ShapeDtypeStruct((bsz, nh, LANES, t_all), BF16)
    k_shape = lambda w: jax.ShapeDtypeStruct((bsz, t_all, w), BF16)
    v_shape = lambda nh, slab: jax.ShapeDtypeStruct((bsz, t_all // slab, nh * V_ROWS, slab), BF16)
    consts = (lw["gain512"], lw["mqg"], lw["mkvg"], lw["wuq"], lw["wukv"], lw["bd"])
    return pl.pallas_call(
        functools.partial(_in_kernel, mla_scale=float((MLA_NOPE + MLA_ROPE) ** -0.5), n_ctx_blk=n_ctx_blk),
        out_shape=(q_shape(GQA_HEADS), k_shape(128), v_shape(GQA_KV, SWA_SLAB),
                   q_shape(GQA_HEADS), k_shape(128), v_shape(GQA_KV, TK),
                   q_shape(MLA_HEADS), k_shape(512), v_shape(MLA_HEADS, TK)),
        grid=(bsz, nblk),
        in_specs=_stream_specs(stream, n_ctx_blk, 0)
                 + [pl.BlockSpec((1, 1, 8, d), lambda b, i: (b, jnp.where(i < n_ctx_blk, 1, 0), 0, 0)),
                    full(g_mix), full(lw["win"]), tab, tab, tab, tab] + [full(a) for a in consts],
        out_specs=(q_t(GQA_HEADS), tok(128), v_t(GQA_KV, SWA_SLAB),
                   q_t(GQA_HEADS), tok(128), v_t(GQA_KV, TK),
                   q_t(MLA_HEADS), tok(512), v_t(MLA_HEADS, TK)),
        compiler_params=_cparams(2),
        name="in_proj",
    )(stream[0], stream[2], modtab, g_mix, lw["win"], *tabs, *consts)


_N_IN = {"swa": 5, "glb": 3, "mla": 3}
_N_SCRATCH = {"swa": 6, "glb": 12, "mla": 12}
_OUT_COL = {"swa": 0, "glb": GQA_HEADS * HEAD_DIM, "mla": 2 * GQA_HEADS * HEAD_DIM}


def _attn_kernel(*refs, n_ctx, n_ctx_blk, q_blk_off, seq):
    kinds = ("swa", "glb", "mla")
    n_in = sum(_N_IN[k] for k in kinds)
    o_ref = refs[n_in]
    i0, s0 = 0, n_in + 1
    for kind in kinds:
        _attn_group(refs[i0:i0 + _N_IN[kind]], o_ref, refs[s0:s0 + _N_SCRATCH[kind]], kind=kind, n_ctx=n_ctx,
                    n_ctx_blk=n_ctx_blk, q_blk_off=q_blk_off, seq=seq)
        i0 += _N_IN[kind]
        s0 += _N_SCRATCH[kind]


def _attn_group(ins, o_ref, scratch, *, kind, n_ctx, n_ctx_blk, q_blk_off, seq):
    if kind == "swa":
        sink_ref, bias_ref, q_ref, k_ref, v_ref = ins
        acc_ref, s_refs, p_refs = scratch[0], scratch[1:4], scratch[4:6]
    else:
        q_ref, k_ref, v_ref = ins
        acc_ref, m_ref = scratch[:2]
        s_refs, cm_refs, p_refs, al_refs = scratch[2:5], scratch[5:8], scratch[8:10], scratch[10:12]
    blk = pl.program_id(1) + q_blk_off
    is_lat = blk >= n_ctx_blk
    gqa = kind in ("swa", "glb")
    n_heads = GQA_HEADS if gqa else MLA_HEADS
    n_units = GQA_KV if gqa else MLA_HEADS
    unit_w = n_heads // n_units * TQ

    if kind == "swa":
        nk = TQ + 2 * WINDOW
        n_slab = n_ctx // SWA_SLAB
        row_l = lax.broadcasted_iota(jnp.int32, (V_ROWS, TQ), 0) == HEAD_DIM

        def scores(h, parts):
            cm = None
            for k_rows, _, bias, r0, nr in parts:
                s = jnp.dot(k_rows(), q_ref[0, h], preferred_element_type=F32)
                if bias is not None:
                    s = s + bias()
                s_refs[h % 3][r0:r0 + nr, :] = s
                c = jnp.max(s, axis=0, keepdims=True)
                cm = c if cm is None else jnp.maximum(cm, c)
            return cm

        def softmax(h, cm, nrows):
            m = jnp.maximum(cm, sink_ref[h])
            p_refs[h % 2][0:nrows, :] = jnp.exp2((s_refs[h % 3][0:nrows, :] - m).astype(BF16))
            return jnp.exp2(sink_ref[h] - m)

        def values(h, e_sink, parts):
            u, j = divmod(h, GQA_G)
            acc = None
            for _, v_t, _, r0, nr in parts:
                a = jnp.dot(v_t(u), p_refs[h % 2][r0:r0 + nr, :], preferred_element_type=F32)
                acc = a if acc is None else acc + a
            acc_ref[u, :, j * TQ:(j + 1) * TQ] = acc + jnp.where(row_l, e_sink, 0.0)

        def run(parts):
            nrows = sum(p[4] for p in parts)
            cms = {h: scores(h, parts) for h in range(min(2, n_heads))}
            e_sinks = {}
            for h in range(n_heads + 1):
                if h + 2 < n_heads:
                    cms[h + 2] = scores(h + 2, parts)
                if h < n_heads:
                    e_sinks[h] = softmax(h, cms[h], nrows)
                if h >= 1:
                    values(h - 1, e_sinks[h - 1], parts)

        ctx_part = (lambda: k_ref[0, 0:n_ctx, :],
                    lambda u: jnp.concatenate([v_ref[0, s, u * V_ROWS:(u + 1) * V_ROWS, :] for s in range(n_slab)],
                                              axis=1),
                    None, 0, n_ctx)

        def latent():
            q0 = (blk - n_ctx_blk) * TQ
            k0 = pl.multiple_of(jnp.clip(q0 - WINDOW, 0, seq - nk), WINDOW)
            geom = (q0 - k0) // WINDOW
            s0 = (n_ctx + k0) // SWA_SLAB
            win_part = (lambda: k_ref[0, pl.ds(pl.multiple_of(n_ctx + k0, WINDOW), nk), :],
                        lambda u: jnp.concatenate([v_ref[0, s0 + s, u * V_ROWS:(u + 1) * V_ROWS, :]
                                                   for s in range(nk // SWA_SLAB)], axis=1),
                        lambda: bias_ref[geom], n_ctx, nk)
            run([ctx_part, win_part])

        if q_blk_off < n_ctx_blk:
            pl.when(jnp.logical_not(is_lat))(lambda: run([ctx_part]))
            pl.when(is_lat)(latent)
        else:
            latent()
    else:
        m_ref[...] = jnp.full(m_ref.shape, NEG_INF, F32)
        acc_ref[...] = jnp.zeros(acc_ref.shape, F32)

        def qk(c, par, h):
            lanes = slice(h * TQ, (h + 1) * TQ)
            k_cols = slice(0, LANES) if gqa else slice(h * LANES, (h + 1) * LANES)
            k_rows = k_ref[0, pl.ds(pl.multiple_of(c * TK, TK), TK), k_cols]
            s = jnp.dot(k_rows, q_ref[0, h], preferred_element_type=F32)
            s_refs[par][:, lanes] = s
            cm_refs[par][:, lanes] = jnp.max(s, axis=0, keepdims=True)

        def softmax(sb, par, h):
            lanes = slice(h * TQ, (h + 1) * TQ)
            m_prev = m_ref[:, lanes]
            m_new = jnp.maximum(m_prev, cm_refs[sb][:, lanes])
            m_ref[:, lanes] = m_new
            al_refs[par][:, lanes] = jnp.exp2(m_prev - m_new)
            p_refs[par][:, lanes] = jnp.exp2((s_refs[sb][:, lanes] - m_new).astype(BF16))

        def pv(c, par, h):
            lanes = slice(h * TQ, (h + 1) * TQ)
            u, j = divmod(h, n_heads // n_units)
            ul = slice(j * TQ, (j + 1) * TQ)
            acc_ref[u, :, ul] = (acc_ref[u, :, ul] * al_refs[par][:, lanes]
                                 + jnp.dot(v_ref[0, c, u * V_ROWS:(u + 1) * V_ROWS, :], p_refs[par][:, lanes],
                                           preferred_element_type=F32))

        def step(i, r, n):
            for h in range(n_heads):
                if not isinstance(i, int) or i + 2 < n:
                    qk(i + 2, (r + 2) % 3, h)
                if not isinstance(i, int) or i < n:
                    softmax(r % 3, r % 2, h)
                if not isinstance(i, int) or i >= 1:
                    pv(i - 1, (r + 1) % 2, h)

        def run(n):
            for c in range(min(2, n)):
                for h in range(n_heads):
                    qk(c, c, h)
            step(0, 0, n)
            n_groups = max(0, (n - 3) // 6)

            def body(j, carry):
                for r in range(6):
                    step(1 + 6 * j + r, (1 + r) % 6, n)
                return carry
            if n_groups:
                lax.fori_loop(0, n_groups, body, 0)
            for i in range(1 + 6 * n_groups, n + 1):
                step(i, i % 6, n)

        if q_blk_off < n_ctx_blk:
            pl.when(jnp.logical_not(is_lat))(lambda: run(n_ctx // TK))
            pl.when(is_lat)(lambda: run((n_ctx + seq) // TK))
        else:
            run((n_ctx + seq) // TK)

    heads = []
    for u in range(n_units):
        a = acc_ref[u]
        o = a[:HEAD_DIM] / a[HEAD_DIM:HEAD_DIM + 1]
        heads += [o[:, j * TQ:(j + 1) * TQ] for j in range(unit_w // TQ)]
    tiles = [jnp.concatenate(heads[2 * t:2 * t + 2], axis=0).T for t in range(n_heads // 2)]
    c0 = _OUT_COL[kind]
    o_ref[0, :, c0:c0 + n_heads * HEAD_DIM] = jnp.concatenate(tiles, axis=1).astype(o_ref.dtype)


def _attention(groups, sink, n_ctx, with_ctx_queries):
    bsz, _, _, t_all = groups["swa"][0].shape
    seq = t_all - n_ctx
    n_ctx_blk = n_ctx // TQ
    q_blk_off = 0 if with_ctx_queries else n_ctx_blk
    nblk = t_all // TQ - q_blk_off
    in_specs, args, scratch = [], [], []
    for kind in ("swa", "glb", "mla"):
        q_t, k, v_t = groups[kind]
        n_heads = q_t.shape[1]
        n_units = GQA_KV if kind in ("swa", "glb") else MLA_HEADS
        nq = n_heads * TQ
        if kind == "swa":
            nk = TQ + 2 * WINDOW
            off = np.arange(TQ)[None, None, :] + WINDOW * np.arange(3)[:, None, None] - np.arange(nk)[None, :, None]
            bias = jnp.asarray(np.where(np.abs(off) <= WINDOW, 0.0, NEG_INF), F32)
            in_specs += [pl.BlockSpec(memory_space=pltpu.SMEM), pl.BlockSpec(bias.shape, lambda b, i: (0, 0, 0))]
            args += [sink, bias]
        in_specs += [pl.BlockSpec((1, n_heads, LANES, TQ), lambda b, i: (b, 0, 0, i + q_blk_off)),
                     pl.BlockSpec((1,) + k.shape[1:], lambda b, i: (b, 0, 0)),
                     pl.BlockSpec((1,) + v_t.shape[1:], lambda b, i: (b, 0, 0, 0))]
        args += [q_t, k, v_t]
        scratch += [pltpu.VMEM((n_units, V_ROWS, nq // n_units), F32)]
        if kind == "swa":
            rows = n_ctx + nk
            scratch += [pltpu.VMEM((rows, TQ), F32)] * 3 + [pltpu.VMEM((rows, TQ), BF16)] * 2
        else:
            scratch += ([pltpu.VMEM((1, nq), F32)] + [pltpu.VMEM((TK, nq), F32)] * 3 + [pltpu.VMEM((1, nq), F32)] * 3
                        + [pltpu.VMEM((TK, nq), BF16)] * 2 + [pltpu.VMEM((1, nq), F32)] * 2)
    return pl.pallas_call(
        functools.partial(_attn_kernel, n_ctx=n_ctx, n_ctx_blk=n_ctx_blk, q_blk_off=q_blk_off, seq=seq),
        out_shape=jax.ShapeDtypeStruct((bsz, nblk * TQ, D_MODEL), BF16),
        grid=(bsz, nblk),
        in_specs=in_specs,
        out_specs=pl.BlockSpec((1, TQ, D_MODEL), lambda b, i: (b, i, 0)),
        scratch_shapes=scratch,
        compiler_params=_cparams(2),
        name="attention",
    )(*args)


def _row_select(rows, idx):
    out = rows[0]
    for j in range(1, len(rows)):
        out = jnp.where(idx == j, rows[j], out)
    return out


def _route(scores, biased):
    def top2sum(a, b, c, d):
        hi1, lo1, hi2, lo2 = jnp.maximum(a, b), jnp.minimum(a, b), jnp.maximum(c, d), jnp.minimum(c, d)
        return jnp.maximum(hi1, hi2) + jnp.maximum(jnp.minimum(hi1, hi2), jnp.maximum(lo1, lo2))

    gs = [top2sum(*biased[4 * g:4 * g + 4]) for g in range(N_GROUPS)]
    best, gi = gs[0], jnp.zeros(gs[0].shape, jnp.int32)
    for g in range(1, N_GROUPS):
        better = gs[g] > best
        gi = jnp.where(better, g, gi)
        best = jnp.where(better, gs[g], best)
    a = [_row_select([biased[4 * g + j] for g in range(N_GROUPS)], gi) for j in range(EXPERTS_PER_GROUP)]
    s = [_row_select([scores[4 * g + j] for g in range(N_GROUPS)], gi) for j in range(EXPERTS_PER_GROUP)]
    v1, i1 = a[0], jnp.zeros(gi.shape, jnp.int32)
    for j in range(1, EXPERTS_PER_GROUP):
        better = a[j] > v1
        i1 = jnp.where(better, j, i1)
        v1 = jnp.where(better, a[j], v1)
    v2, i2 = jnp.full(v1.shape, -3.0e38, F32), jnp.zeros(gi.shape, jnp.int32)
    for j in range(EXPERTS_PER_GROUP):
        better = (i1 != j) & (a[j] > v2)
        i2 = jnp.where(better, j, i2)
        v2 = jnp.where(better, a[j], v2)
    lo, hi = jnp.minimum(i1, i2), jnp.maximum(i1, i2)
    pair = jnp.where(lo == 0, hi - 1, jnp.where(lo == 1, hi + 1, 5))
    s_lo, s_hi = _row_select(s, lo), _row_select(s, hi)
    den = s_lo + s_hi
    return gi * N_PAIRS + pair, s_lo / den, s_hi / den


def _out_kernel(y_ref, x_ref, c_ref, mod_ref, wout_ref, g_ref, rwh_ref, rwl_ref, rb_ref, tri_ref,
                xo_ref, row_ref, meta_ref, cnt_ref, carry_ref, *, n_ctx_blk, blk_off):
    @pl.when((pl.program_id(0) == 0) & (pl.program_id(1) == 0))
    def _():
        carry_ref[...] = jnp.zeros(carry_ref.shape, F32)

    y = y_ref[0]
    mod = mod_ref[0, 0]
    half = D_MODEL // 2
    a = jnp.concatenate([jnp.dot(y, wout_ref[:, :half], preferred_element_type=F32),
                         jnp.dot(y, wout_ref[:, half:], preferred_element_type=F32)], axis=1)
    x = jnp.where(pl.program_id(1) + blk_off < n_ctx_blk, c_ref[0], x_ref[0]) + mod[2:3] * a
    xo_ref[0] = x
    h = _rms(x) * g_ref[...]
    h = h * (1.0 + mod[4:5]) + mod[3:4]

    hh = h.astype(BF16)
    hl = (h - hh.astype(F32)).astype(BF16)
    logits = (jnp.dot(hh, rwh_ref[...], preferred_element_type=F32)
              + jnp.dot(hl, rwh_ref[...], preferred_element_type=F32)
              + jnp.dot(hh, rwl_ref[...], preferred_element_type=F32)).T[:32]
    sc = jax.nn.sigmoid(logits)
    bs = sc + rb_ref[...]
    scores = [sc[e:e + 1, :] for e in range(N_EXPERTS)]
    biased = [bs[e:e + 1, :] for e in range(N_EXPERTS)]
    bucket, g_lo, g_hi = _route(scores, biased)

    onehot = jnp.where(lax.broadcasted_iota(jnp.int32, (32, TQ), 0) == bucket, 1.0, 0.0)
    prefix = jnp.dot(onehot.astype(BF16), tri_ref[...], preferred_element_type=F32)
    carry = carry_ref[:, 0:1]
    rank = jnp.sum(onehot * (carry + prefix - 1.0), axis=0, keepdims=True)
    carry_new = jnp.broadcast_to(carry + prefix[:, TQ - 1:TQ], carry_ref.shape)
    carry_ref[...] = carry_new
    cnt_ref[...] = carry_new
    meta_ref[0] = jnp.concatenate([bucket, rank.astype(jnp.int32), jnp.zeros((6, TQ), jnp.int32)], axis=0)

    gates = jnp.concatenate([g_lo, g_hi, jnp.zeros((LANES - 2, TQ), F32)], axis=0)
    row_ref[...] = jnp.concatenate([h, gates.T], axis=1)


def _output_projection(y_mix, stream, t_all, modtab, lw, shared, n_ctx_blk, with_ctx):
    bsz, _, d = stream[0].shape
    off = 0 if with_ctx else n_ctx_blk
    nblk = t_all // TQ - off
    tok = lambda w: pl.BlockSpec((1, TQ, w), lambda b, i: (b, i, 0))
    full = lambda a: pl.BlockSpec(a.shape, lambda b, i: (0,) * a.ndim)
    consts = (lw["wout"], lw["g_ffn"], shared["rw_hi"], shared["rw_lo"], shared["rb"], shared["tri"])
    return pl.pallas_call(
        functools.partial(_out_kernel, n_ctx_blk=n_ctx_blk, blk_off=off),
        out_shape=(jax.ShapeDtypeStruct((bsz, nblk * TQ, d), F32),
                   jax.ShapeDtypeStruct((bsz * nblk * TQ, ROW_W), F32),
                   jax.ShapeDtypeStruct((bsz * nblk, 8, TQ), jnp.int32),
                   jax.ShapeDtypeStruct((32, LANES), F32)),
        grid=(bsz, nblk),
        in_specs=[tok(d)] + _stream_specs(stream, n_ctx_blk, off)
                 + [pl.BlockSpec((1, 1, 8, d), lambda b, i: (b, jnp.where(i + off < n_ctx_blk, 1, 0), 0, 0))]
                 + [full(a) for a in consts],
        out_specs=(tok(d),
                   pl.BlockSpec((TQ, ROW_W), lambda b, i: (b * nblk + i, 0)),
                   pl.BlockSpec((1, 8, TQ), lambda b, i: (b * nblk + i, 0, 0)),
                   pl.BlockSpec((32, LANES), lambda b, i: (0, 0))),
        scratch_shapes=[pltpu.VMEM((32, LANES), F32)],
        compiler_params=_cparams(2),
        name="out_proj_router",
    )(y_mix, stream[0], stream[2], modtab, *consts)


def _scatter_kernel(dest_ref, fill_ref, nv_ref, row_ref, out_ref, buf, sem, *, n_steps, n_blocks):
    i = pl.program_id(0)
    slot = i % 2

    def wait_slot(s):
        pltpu.make_async_copy(buf.at[s], out_ref.at[pl.ds(0, TQ)], sem.at[s]).wait()

    @pl.when(i == 0)
    def _():
        buf[1] = jnp.zeros((TQ, ROW_W), F32)

        def zero_block(blk):
            cp = pltpu.make_async_copy(buf.at[1], out_ref.at[pl.ds(pl.multiple_of(blk * TM, TM), TM)], sem.at[1])
            cp.start()
            cp.wait()

        for b in range(N_BUCKETS):
            @pl.when(fill_ref[b] >= 0)
            def _(b=b):
                zero_block(fill_ref[b])

        def tail(blk, carry):
            zero_block(blk)
            return carry
        lax.fori_loop(nv_ref[0], n_blocks, tail, 0)

    for s in range(2):
        @pl.when(slot == s)
        def _(s=s):
            @pl.when(i >= 2)
            def _():
                wait_slot(s)

            buf[s] = row_ref[...]
            for r in range(TQ):
                d = dest_ref[i * TQ + r]
                pltpu.make_async_copy(buf.at[s, pl.ds(r, 1)], out_ref.at[pl.ds(d, 1)], sem.at[s]).start()

    @pl.when(i == n_steps - 1)
    def _():
        wait_slot(slot)
        if n_steps >= 2:
            wait_slot(1 - slot)


def _scatter_rows(dest, fill_blk, n_valid, rows, n_sorted):
    n_tok = rows.shape[0]
    n_steps = n_tok // TQ
    assert TQ == TM
    return pl.pallas_call(
        functools.partial(_scatter_kernel, n_steps=n_steps, n_blocks=n_sorted // TM),
        out_shape=jax.ShapeDtypeStruct((n_sorted, ROW_W), F32),
        grid_spec=pltpu.PrefetchScalarGridSpec(
            num_scalar_prefetch=3,
            grid=(n_steps,),
            in_specs=[pl.BlockSpec((TQ, ROW_W), lambda i, d, f, nv: (i, 0))],
            out_specs=pl.BlockSpec(memory_space=pl.ANY),
            scratch_shapes=[pltpu.VMEM((2, TQ, ROW_W), F32), pltpu.SemaphoreType.DMA((2,))]),
        compiler_params=_cparams(1),
        name="moe_scatter",
    )(dest, fill_blk, n_valid, rows)


def _swiglu(h, wgu, wd):
    u = jnp.dot(h, wgu, preferred_element_type=F32)
    a = u[:, :D_EXPERT]
    a = a * jax.nn.sigmoid(a) * u[:, D_EXPERT:]
    return jnp.dot(a.astype(BF16), wd, preferred_element_type=F32)


def _moe_kernel(elo_ref, ehi_ref, nv_ref, row_ref, wgl_ref, wdl_ref, wgh_ref, wdh_ref, swg_ref, swd_ref, y_ref):
    del elo_ref, ehi_ref
    j = pl.program_id(0)

    @pl.when(j < nv_ref[0])
    def _():
        rows = row_ref[...]
        h = rows[:, :D_MODEL].astype(BF16)
        g_lo = rows[:, D_MODEL:D_MODEL + 1]
        g_hi = rows[:, D_MODEL + 1:D_MODEL + 2]
        y_ref[...] = (g_lo * _swiglu(h, wgl_ref[0], wdl_ref[0]) + g_hi * _swiglu(h, wgh_ref[0], wdh_ref[0])
                      + _swiglu(h, swg_ref[...], swd_ref[...]))

    @pl.when(j >= nv_ref[0])
    def _():
        y_ref[...] = jnp.zeros(y_ref.shape, F32)


def _grouped_experts(e_lo, e_hi, n_valid, rows_sorted, lw):
    n_sorted = rows_sorted.shape[0]
    nb = n_sorted // TM
    wgu, wd, swgu, swd = lw["wgu"], lw["wd"], lw["swgu"], lw["swd"]
    return pl.pallas_call(
        _moe_kernel,
        out_shape=jax.ShapeDtypeStruct((n_sorted, D_MODEL), F32),
        grid_spec=pltpu.PrefetchScalarGridSpec(
            num_scalar_prefetch=3,
            grid=(nb,),
            in_specs=[pl.BlockSpec((TM, ROW_W), lambda j, lo, hi, nv: (j, 0)),
                      pl.BlockSpec((1,) + wgu.shape[1:], lambda j, lo, hi, nv: (lo[j], 0, 0)),
                      pl.BlockSpec((1,) + wd.shape[1:], lambda j, lo, hi, nv: (lo[j], 0, 0)),
                      pl.BlockSpec((1,) + wgu.shape[1:], lambda j, lo, hi, nv: (hi[j], 0, 0)),
                      pl.BlockSpec((1,) + wd.shape[1:], lambda j, lo, hi, nv: (hi[j], 0, 0)),
                      pl.BlockSpec(swgu.shape, lambda j, lo, hi, nv: (0, 0)),
                      pl.BlockSpec(swd.shape, lambda j, lo, hi, nv: (0, 0))],
            out_specs=pl.BlockSpec((TM, D_MODEL), lambda j, lo, hi, nv: (j, 0))),
        compiler_params=_cparams(1),
        name="moe_experts",
    )(e_lo, e_hi, n_valid, rows_sorted, wgu, wd, wgu, wd, swgu, swd)


def _gather_kernel(dest_ref, y_ref, x_ref, mod_ref, gf_ref, o_ref, fbuf, sem, *, n_steps, final_norm):
    i = pl.program_id(0)
    slot = i % 2

    def issue(step, s):
        for r in range(TQ):
            d = dest_ref[step * TQ + r]
            pltpu.make_async_copy(y_ref.at[pl.ds(d, 1)], fbuf.at[s, pl.ds(r, 1)], sem.at[s]).start()

    @pl.when(i == 0)
    def _():
        issue(0, 0)

    for s in range(2):
        @pl.when((i + 1 < n_steps) & (slot == s))
        def _(s=s):
            issue(i + 1, 1 - s)

    pltpu.make_async_copy(y_ref.at[pl.ds(0, TQ)], fbuf.at[slot], sem.at[slot]).wait()
    x = x_ref[0] + mod_ref[0, 0][5:6] * fbuf[slot]
    if final_norm:
        x = _rms(x) * gf_ref[...]
    o_ref[0] = x


def _gather_residual(dest, y_sorted, x_mid, modtab, g_final, n_ctx_blk, with_ctx, final_norm):
    bsz, t_rows, d = x_mid.shape
    nblk = t_rows // TQ
    off = 0 if with_ctx else n_ctx_blk
    n_steps = bsz * nblk
    return pl.pallas_call(
        functools.partial(_gather_kernel, n_steps=n_steps, final_norm=final_norm),
        out_shape=jax.ShapeDtypeStruct((bsz, t_rows, d), F32),
        grid_spec=pltpu.PrefetchScalarGridSpec(
            num_scalar_prefetch=1,
            grid=(n_steps,),
            in_specs=[pl.BlockSpec(memory_space=pl.ANY),
                      pl.BlockSpec((1, TQ, d), lambda i, dst: (i // nblk, i % nblk, 0)),
                      pl.BlockSpec((1, 1, 8, d),
                                   lambda i, dst: (i // nblk, jnp.where(i % nblk + off < n_ctx_blk, 1, 0), 0, 0)),
                      pl.BlockSpec(g_final.shape, lambda i, dst: (0, 0))],
            out_specs=pl.BlockSpec((1, TQ, d), lambda i, dst: (i // nblk, i % nblk, 0)),
            scratch_shapes=[pltpu.VMEM((2, TQ, d), F32), pltpu.SemaphoreType.DMA((2,))]),
        compiler_params=_cparams(1),
        name="moe_gather",
    )(dest, y_sorted, x_mid, modtab, g_final)


def _bucket_layout(meta, counts, n_tok):
    bucket = meta[:, 0, :].reshape(-1)
    rank = meta[:, 1, :].reshape(-1)
    cnt = counts[:N_BUCKETS, 0].astype(jnp.int32)
    padded = (cnt + TM - 1) // TM * TM
    pad_end = jnp.cumsum(padded)
    pad_start = pad_end - padded
    dest = pad_start[bucket] + rank
    nb = n_tok // TM + N_BUCKETS
    n_valid = pad_end[-1] // TM
    blk = jnp.arange(nb, dtype=jnp.int32)
    blk_row = jnp.minimum(blk, n_valid - 1) * TM
    blk_bucket = jnp.minimum(jnp.sum(pad_end[None, :] <= blk_row[:, None], axis=1), N_BUCKETS - 1).astype(jnp.int32)
    grp, pair = blk_bucket // N_PAIRS, blk_bucket % N_PAIRS
    e_lo = grp * EXPERTS_PER_GROUP + jnp.asarray(_PAIR_LO, jnp.int32)[pair]
    e_hi = grp * EXPERTS_PER_GROUP + jnp.asarray(_PAIR_HI, jnp.int32)[pair]
    fill_blk = jnp.where(padded > 0, pad_end // TM - 1, -1).astype(jnp.int32)
    return dest.astype(jnp.int32), fill_blk, e_lo, e_hi, n_valid.reshape(1).astype(jnp.int32), nb * TM


def _rope_tables(seq, n_ctx):
    t = jnp.arange(seq)
    rows, cols = t // GRID_W, t % GRID_W

    def table(width, nf):
        lane = np.arange(width)
        half, j = lane // (2 * nf), lane % (2 * nf)
        inv = ROPE_THETA ** (-jnp.arange(nf, dtype=F32) / nf)
        pos = jnp.where(jnp.asarray(half == 0)[None, :], rows[:, None], cols[:, None]).astype(F32)
        ang = pos * inv[j % nf][None, :]
        sign = jnp.asarray(np.where(j < nf, -1.0, 1.0), F32)[None, :]
        return jnp.cos(ang), jnp.sin(ang) * sign

    cos64, sin64 = table(HEAD_DIM, 16)
    cos64, sin64 = jnp.tile(cos64, (1, 2)), jnp.tile(sin64, (1, 2))
    cosr, sinr = table(MLA_ROPE, 8)
    cosm = jnp.concatenate([jnp.ones((seq, MLA_NOPE), F32), cosr, jnp.ones((seq, 32), F32)], axis=1)
    sinm = jnp.concatenate([jnp.zeros((seq, MLA_NOPE), F32), sinr, jnp.zeros((seq, 32), F32)], axis=1)
    ctx1, ctx0 = jnp.ones((n_ctx, LANES), F32), jnp.zeros((n_ctx, LANES), F32)
    return tuple(jnp.concatenate([c, a], axis=0) for c, a in ((ctx1, cos64), (ctx0, sin64), (ctx1, cosm), (ctx0, sinm)))


def _layer_weights(l, w_in, w_out, norm_ffn_g, glb_q_gain, glb_k_gain, mla_q_gain, mla_w_uq, mla_kv_gain,
                   mla_w_ukv, exp_w_gate, exp_w_up, exp_w_down, shr_w_gate, shr_w_up, shr_w_down):
    d = w_in.shape[1]
    wi = w_in[l]
    kr = wi[:, 1664:1696]
    win = jnp.concatenate([wi[:, :1664], jnp.zeros((d, MLA_NOPE), F32), kr, jnp.zeros((d, 32), F32)], axis=1)
    uq = mla_w_uq[l].reshape(MLA_Q_RANK, MLA_HEADS, MLA_NOPE + MLA_ROPE)
    wuq = jnp.concatenate([uq, jnp.zeros((MLA_Q_RANK, MLA_HEADS, 32), F32)], axis=2).reshape(MLA_Q_RANK, 512)
    ukv = mla_w_ukv[l].reshape(MLA_KV_RANK, MLA_HEADS, MLA_NOPE + MLA_V)
    wk = jnp.concatenate([ukv[:, :, :MLA_NOPE], jnp.zeros((MLA_KV_RANK, MLA_HEADS, 64), F32)], axis=2)
    wukv = jnp.concatenate([wk.reshape(MLA_KV_RANK, 512), ukv[:, :, MLA_NOPE:].reshape(MLA_KV_RANK, 256)], axis=1)
    gain512 = jnp.concatenate([jnp.tile(glb_q_gain[l] * (HEAD_DIM ** -0.5 * LOG2E), GQA_HEADS),
                               jnp.tile(glb_k_gain[l], GQA_KV)]).reshape(1, 512)
    head = np.arange(512) // HEAD_DIM
    return {
        "win": win.astype(BF16),
        "wuq": wuq.astype(BF16),
        "wukv": wukv.astype(BF16),
        "gain512": gain512,
        "mqg": mla_q_gain[l].reshape(1, -1),
        "mkvg": mla_kv_gain[l].reshape(1, -1),
        "bd": jnp.asarray(head[:, None] == head[None, :], BF16),
        "wout": w_out[l].astype(BF16),
        "g_ffn": norm_ffn_g[l].reshape(1, -1),
        "wgu": jnp.concatenate([exp_w_gate[l], exp_w_up[l]], axis=2).astype(BF16),
        "wd": exp_w_down[l].astype(BF16),
        "swgu": jnp.concatenate([shr_w_gate[l], shr_w_up[l]], axis=1).astype(BF16),
        "swd": shr_w_down[l].astype(BF16),
    }


def kernel(x, c, ctx, c_ctx, w_mod, b_mod, norm_mix_g, norm_ffn_g, w_in, w_out, swa_sink, glb_q_gain, glb_k_gain,
           mla_q_gain, mla_w_uq, mla_kv_gain, mla_w_ukv, router_w, router_bias, exp_w_gate, exp_w_up, exp_w_down,
           shr_w_gate, shr_w_up, shr_w_down, final_norm_g):
    bsz, seq, d = x.shape
    n_ctx = ctx.shape[1]
    n_layers = w_mod.shape[0]
    assert d == D_MODEL and n_ctx % TQ == 0 and seq % TQ == 0 and seq >= TQ + 2 * WINDOW and seq % GRID_W == 0
    n_ctx_blk = n_ctx // TQ

    rows = -(-(bsz + 1) // 8) * 8
    c_rows = jnp.concatenate([c, c_ctx[None, :], jnp.zeros((rows - bsz - 1, d), F32)], axis=0)
    mods = _modulation(c_rows, w_mod, b_mod)
    mod_x = mods[:, :bsz].reshape(n_layers, bsz, 6, d)
    mod_c = jnp.broadcast_to(mods[:, bsz].reshape(n_layers, 1, 6, d), (n_layers, bsz, 6, d))
    modtabs = jnp.pad(jnp.stack([mod_x, mod_c], axis=2), ((0, 0), (0, 0), (0, 0), (0, 2), (0, 0)))

    tabs = _rope_tables(seq, n_ctx)
    rw = jnp.pad(router_w, ((0, 0), (0, LANES - N_EXPERTS)))
    rw_hi = rw.astype(BF16)
    shared = {
        "rw_hi": rw_hi,
        "rw_lo": (rw - rw_hi.astype(F32)).astype(BF16),
        "rb": jnp.pad(router_bias, (0, 32 - N_EXPERTS)).reshape(32, 1),
        "tri": jnp.asarray(np.arange(TQ)[:, None] <= np.arange(TQ)[None, :], BF16),
    }
    g_final = final_norm_g.reshape(1, d)

    t_all = n_ctx + seq
    stream = (x, 0, ctx)
    for l in range(n_layers):
        last = l == n_layers - 1
        with_ctx = not last
        lw = _layer_weights(l, w_in, w_out, norm_ffn_g, glb_q_gain, glb_k_gain, mla_q_gain, mla_w_uq,
                            mla_kv_gain, mla_w_ukv, exp_w_gate, exp_w_up, exp_w_down,
                            shr_w_gate, shr_w_up, shr_w_down)
        modtab = modtabs[l]
        qs, ks, vs, qg, kg, vg, qm, km, vm = _input_projection(
            stream, t_all, modtab, norm_mix_g[l].reshape(1, d), lw, tabs, n_ctx_blk)
        sink = jnp.pad(swa_sink[l] * LOG2E, (0, 8 - GQA_HEADS))
        y_mix = _attention({"swa": (qs, ks, vs), "glb": (qg, kg, vg), "mla": (qm, km, vm)}, sink, n_ctx, with_ctx)
        x_mid, rows_tok, meta, counts = _output_projection(y_mix, stream, t_all, modtab, lw, shared, n_ctx_blk,
                                                           with_ctx)
        n_tok = rows_tok.shape[0]
        dest, fill_blk, e_lo, e_hi, n_valid, n_sorted = _bucket_layout(meta, counts, n_tok)
        rows_sorted = _scatter_rows(dest, fill_blk, n_valid, rows_tok, n_sorted)
        y_sorted = _grouped_experts(e_lo, e_hi, n_valid, rows_sorted, lw)
        xs = _gather_residual(dest, y_sorted, x_mid, modtab, g_final, n_ctx_blk, with_ctx, final_norm=last)
        stream = (xs, n_ctx_blk, xs)
    return xs
```

```python
import functools

import jax
import jax.numpy as jnp
import numpy as np
from jax import lax
from jax.experimental import pallas as pl
from jax.experimental.pallas import tpu as pltpu

F32 = jnp.float32
BF16 = jnp.bfloat16

D_MODEL = 1024
GRID_W = 64
HEAD_DIM = 64
GQA_HEADS = 6
GQA_KV = 2
GQA_G = GQA_HEADS // GQA_KV
WINDOW = 128
MLA_HEADS = 4
MLA_NOPE = 64
MLA_ROPE = 32
MLA_V = 64
MLA_Q_RANK = 256
MLA_KV_RANK = 128
ROPE_THETA = 10000.0
N_EXPERTS = 16
N_GROUPS = 4
EXPERTS_PER_GROUP = 4
N_PAIRS = 6
N_BUCKETS = N_GROUPS * N_PAIRS
D_EXPERT = 512
EPS = 1e-6
NEG_INF = -1e30

LANES = 128
TQ = 256
TK = 256
TM = 256
SWA_SLAB = 128
V_ROWS = 80
LOG2E = 1.4426950408889634
ROW_W = D_MODEL + LANES
IN_W = 1792
VMEM_LIMIT = 56 * 1024 * 1024

_SQ, _SK, _SV, _GQ, _GK, _GV, _MQ, _MKV, _KR = 0, 384, 512, 640, 1024, 1152, 1280, 1536, 1664

_PAIR_LO = (0, 0, 0, 1, 1, 2)
_PAIR_HI = (1, 2, 3, 2, 3, 3)


def _cparams(n_axes):
    return pltpu.CompilerParams(dimension_semantics=("arbitrary",) * n_axes,
                                vmem_limit_bytes=VMEM_LIMIT)


def _mod_kernel(c_ref, w_ref, b_ref, o_ref):
    c = c_ref[...]
    a = (c * jax.nn.sigmoid(c)).astype(BF16)
    o_ref[0] = jnp.dot(a, w_ref[0].astype(BF16), preferred_element_type=F32) + b_ref[0]


def _modulation(c_rows, w_mod, b_mod):
    n_layers, d, width = w_mod.shape
    rows = c_rows.shape[0]
    nb = 1536
    return pl.pallas_call(
        _mod_kernel,
        out_shape=jax.ShapeDtypeStruct((n_layers, rows, width), F32),
        grid=(n_layers, width // nb),
        in_specs=[pl.BlockSpec((rows, d), lambda l, j: (0, 0)),
                  pl.BlockSpec((1, d, nb), lambda l, j: (l, 0, j)),
                  pl.BlockSpec((1, 1, nb), lambda l, j: (l, 0, j))],
        out_specs=pl.BlockSpec((1, rows, nb), lambda l, j: (l, 0, j)),
        compiler_params=_cparams(2),
        name="adaln_mod",
    )(c_rows, w_mod, b_mod.reshape(n_layers, 1, width))


def _rope(x, cos, sin_signed, nf):
    lane = lax.broadcasted_iota(jnp.int32, (1, LANES), 1)
    first = (lane % (2 * nf)) < nf
    tiles = []
    for t in range(x.shape[1] // LANES):
        xt = x[:, t * LANES:(t + 1) * LANES]
        partner = jnp.where(first, pltpu.roll(xt, LANES - nf, 1), pltpu.roll(xt, nf, 1))
        tiles.append(xt * cos + partner * sin_signed)
    return tiles[0] if len(tiles) == 1 else jnp.concatenate(tiles, axis=1)


def _group_padded(q):
    lane = lax.broadcasted_iota(jnp.int32, (1, LANES), 1)
    tiles = []
    for h in range(GQA_HEADS):
        g = h // GQA_G
        tile = q[:, (h // 2) * LANES:(h // 2 + 1) * LANES]
        if (h % 2) != g:
            tile = pltpu.roll(tile, HEAD_DIM, 1)
        tiles.append(jnp.where((lane // HEAD_DIM) == g, tile, 0.0))
    return jnp.concatenate(tiles, axis=1)


def _emit_q_t(ref, q):
    for h in range(q.shape[1] // LANES):
        ref[0, h] = q[:, h * LANES:(h + 1) * LANES].T.astype(ref.dtype)


def _emit_v_t(ref, v, slab):
    ones_blk = jnp.where(lax.broadcasted_iota(jnp.int32, (V_ROWS - HEAD_DIM, TQ), 0) == 0, 1.0, 0.0)
    parts = []
    for t in range(v.shape[1] // LANES):
        vt = v[:, t * LANES:(t + 1) * LANES].T
        parts += [vt[:HEAD_DIM], ones_blk, vt[HEAD_DIM:], ones_blk]
    ext = jnp.concatenate(parts, axis=0).astype(ref.dtype)
    for s in range(TQ // slab):
        ref[0, s] = ext[:, s * slab:(s + 1) * slab]


def _rms(x, eps=EPS):
    return x * lax.rsqrt(jnp.mean(x * x, axis=-1, keepdims=True) + eps)


def _stream_specs(stream, n_ctx_blk, blk_off):
    lat, lat_off, ctx = stream
    d = lat.shape[2]
    return [pl.BlockSpec((1, TQ, d), lambda b, i: (b, jnp.maximum(i + blk_off - n_ctx_blk, 0) + lat_off, 0)),
            pl.BlockSpec((1, TQ, d), lambda b, i: (b, jnp.minimum(i + blk_off, n_ctx_blk - 1), 0))]


def _in_kernel(x_ref, c_ref, mod_ref, g_ref, win_ref, cos64_ref, sin64_ref, cosm_ref, sinm_ref,
               gain_ref, mqg_ref, mkvg_ref, wuq_ref, wukv_ref, bd_ref,
               qs_ref, ks_ref, vs_ref, qg_ref, kg_ref, vg_ref, qm_ref, km_ref, vm_ref, *, mla_scale, n_ctx_blk):
    x = jnp.where(pl.program_id(1) < n_ctx_blk, c_ref[0], x_ref[0])
    mod = mod_ref[0, 0]
    h = _rms(x) * g_ref[...]
    h = h * (1.0 + mod[1:2]) + mod[0:1]
    z = jnp.dot(h.astype(BF16), win_ref[...], preferred_element_type=F32)
    cos64, sin64 = cos64_ref[...], sin64_ref[...]
    cosm, sinm = cosm_ref[...], sinm_ref[...]

    sqk = _rope(z[:, _SQ:_SV], cos64, sin64, 16)
    _emit_q_t(qs_ref, _group_padded(sqk[:, :384] * (HEAD_DIM ** -0.5 * LOG2E)))
    ks_ref[0] = sqk[:, 384:].astype(BF16)
    _emit_v_t(vs_ref, z[:, _SV:_GQ], SWA_SLAB)

    gqk = z[:, _GQ:_GV]
    sq = gqk * gqk
    hi = sq.astype(BF16)
    lo = (sq - hi.astype(F32)).astype(BF16)
    ssum = (jnp.dot(hi, bd_ref[...], preferred_element_type=F32)
            + jnp.dot(lo, bd_ref[...], preferred_element_type=F32))
    gqk = gqk * lax.rsqrt(ssum * (1.0 / HEAD_DIM) + EPS) * gain_ref[...]
    gqk = _rope(gqk, cos64, sin64, 16)
    _emit_q_t(qg_ref, _group_padded(gqk[:, :384]))
    kg_ref[0] = gqk[:, 384:].astype(BF16)
    _emit_v_t(vg_ref, z[:, _GV:_MQ], TK)

    qn = _rms(z[:, _MQ:_MKV]) * mqg_ref[...]
    mq = jnp.dot(qn.astype(BF16), wuq_ref[...], preferred_element_type=F32)
    _emit_q_t(qm_ref, _rope(mq, cosm, sinm, 8) * (mla_scale * LOG2E))
    kvn = _rms(z[:, _MKV:_KR]) * mkvg_ref[...]
    mkv = jnp.dot(kvn.astype(BF16), wukv_ref[...], preferred_element_type=F32)
    kr = _rope(z[:, _KR:IN_W], cosm, sinm, 8)
    km_ref[0] = (mkv[:, :512] + jnp.concatenate([kr] * MLA_HEADS, axis=1)).astype(BF16)
    _emit_v_t(vm_ref, mkv[:, 512:], TK)


def _input_projection(stream, t_all, modtab, g_mix, lw, tabs, n_ctx_blk):
    bsz, _, d = stream[0].shape
    nblk = t_all // TQ
    tok = lambda w: pl.BlockSpec((1, TQ, w), lambda b, i: (b, i, 0))
    q_t = lambda nh: pl.BlockSpec((1, nh, LANES, TQ), lambda b, i: (b, 0, 0, i))
    v_t = lambda nh, slab: pl.BlockSpec((1, TQ // slab, nh * V_ROWS, slab), lambda b, i: (b, i, 0, 0))
    full = lambda a: pl.BlockSpec(a.shape, lambda b, i: (0,) * a.ndim)
    tab = pl.BlockSpec((TQ, LANES), lambda b, i: (i, 0))
    q_shape = lambda nh: jax.ShapeDtypeStruct((bsz, nh, LANES, t_all), BF16)
    k_shape = lambda w: jax.ShapeDtypeStruct((bsz, t_all, w), BF16)
    v_shape = lambda nh, slab: jax.ShapeDtypeStruct((bsz, t_all // slab, nh * V_ROWS, slab), BF16)
    consts = (lw["gain512"], lw["mqg"], lw["mkvg"], lw["wuq"], lw["wukv"], lw["bd"])
    return pl.pallas_call(
        functools.partial(_in_kernel, mla_scale=float((MLA_NOPE + MLA_ROPE) ** -0.5), n_ctx_blk=n_ctx_blk),
        out_shape=(q_shape(GQA_HEADS), k_shape(128), v_shape(GQA_KV, SWA_SLAB),
                   q_shape(GQA_HEADS), k_shape(128), v_shape(GQA_KV, TK),
                   q_shape(MLA_HEADS), k_shape(512), v_shape(MLA_HEADS, TK)),
        grid=(bsz, nblk),
        in_specs=_stream_specs(stream, n_ctx_blk, 0)
                 + [pl.BlockSpec((1, 1, 8, d), lambda b, i: (b, jnp.where(i < n_ctx_blk, 1, 0), 0, 0)),
                    full(g_mix), full(lw["win"]), tab, tab, tab, tab] + [full(a) for a in consts],
        out_specs=(q_t(GQA_HEADS), tok(128), v_t(GQA_KV, SWA_SLAB),
                   q_t(GQA_HEADS), tok(128), v_t(GQA_KV, TK),
                   q_t(MLA_HEADS), tok(512), v_t(MLA_HEADS, TK)),
        compiler_params=_cparams(2),
        name="in_proj",
    )(stream[0], stream[2], modtab, g_mix, lw["win"], *tabs, *consts)


_N_IN = {"swa": 5, "glb": 3, "mla": 3}
_N_SCRATCH = {"swa": 6, "glb": 12, "mla": 12}
_OUT_COL = {"swa": 0, "glb": GQA_HEADS * HEAD_DIM, "mla": 2 * GQA_HEADS * HEAD_DIM}


def _attn_kernel(*refs, n_ctx, n_ctx_blk, q_blk_off, seq):
    kinds = ("swa", "glb", "mla")
    n_in = sum(_N_IN[k] for k in kinds)
    o_ref = refs[n_in]
    i0, s0 = 0, n_in + 1
    for kind in kinds:
        _attn_group(refs[i0:i0 + _N_IN[kind]], o_ref, refs[s0:s0 + _N_SCRATCH[kind]], kind=kind, n_ctx=n_ctx,
                    n_ctx_blk=n_ctx_blk, q_blk_off=q_blk_off, seq=seq)
        i0 += _N_IN[kind]
        s0 += _N_SCRATCH[kind]


def _attn_group(ins, o_ref, scratch, *, kind, n_ctx, n_ctx_blk, q_blk_off, seq):
    if kind == "swa":
        sink_ref, bias_ref, q_ref, k_ref, v_ref = ins
        acc_ref, s_refs, p_refs = scratch[0], scratch[1:4], scratch[4:6]
    else:
        q_ref, k_ref, v_ref = ins
        acc_ref, m_ref = scratch[:2]
        s_refs, cm_refs, p_refs, al_refs = scratch[2:5], scratch[5:8], scratch[8:10], scratch[10:12]
    blk = pl.program_id(1) + q_blk_off
    is_lat = blk >= n_ctx_blk
    gqa = kind in ("swa", "glb")
    n_heads = GQA_HEADS if gqa else MLA_HEADS
    n_units = GQA_KV if gqa else MLA_HEADS
    unit_w = n_heads // n_units * TQ

    if kind == "swa":
        nk = TQ + 2 * WINDOW
        n_slab = n_ctx // SWA_SLAB
        row_l = lax.broadcasted_iota(jnp.int32, (V_ROWS, TQ), 0) == HEAD_DIM

        def scores(h, parts):
            cm = None
            for k_rows, _, bias, r0, nr in parts:
                s = jnp.dot(k_rows(), q_ref[0, h], preferred_element_type=F32)
                if bias is not None:
                    s = s + bias()
                s_refs[h % 3][r0:r0 + nr, :] = s
                c = jnp.max(s, axis=0, keepdims=True)
                cm = c if cm is None else jnp.maximum(cm, c)
            return cm

        def softmax(h, cm, nrows):
            m = jnp.maximum(cm, sink_ref[h])
            p_refs[h % 2][0:nrows, :] = jnp.exp2((s_refs[h % 3][0:nrows, :] - m).astype(BF16))
            return jnp.exp2(sink_ref[h] - m)

        def values(h, e_sink, parts):
            u, j = divmod(h, GQA_G)
            acc = None
            for _, v_t, _, r0, nr in parts:
                a = jnp.dot(v_t(u), p_refs[h % 2][r0:r0 + nr, :], preferred_element_type=F32)
                acc = a if acc is None else acc + a
            acc_ref[u, :, j * TQ:(j + 1) * TQ] = acc + jnp.where(row_l, e_sink, 0.0)

        def run(parts):
            nrows = sum(p[4] for p in parts)
            cms = {h: scores(h, parts) for h in range(min(2, n_heads))}
            e_sinks = {}
            for h in range(n_heads + 1):
                if h + 2 < n_heads:
                    cms[h + 2] = scores(h + 2, parts)
                if h < n_heads:
                    e_sinks[h] = softmax(h, cms[h], nrows)
                if h >= 1:
                    values(h - 1, e_sinks[h - 1], parts)

        ctx_part = (lambda: k_ref[0, 0:n_ctx, :],
                    lambda u: jnp.concatenate([v_ref[0, s, u * V_ROWS:(u + 1) * V_ROWS, :] for s in range(n_slab)],
                                              axis=1),
                    None, 0, n_ctx)

        def latent():
            q0 = (blk - n_ctx_blk) * TQ
            k0 = pl.multiple_of(jnp.clip(q0 - WINDOW, 0, seq - nk), WINDOW)
            geom = (q0 - k0) // WINDOW
            s0 = (n_ctx + k0) // SWA_SLAB
            win_part = (lambda: k_ref[0, pl.ds(pl.multiple_of(n_ctx + k0, WINDOW), nk), :],
                        lambda u: jnp.concatenate([v_ref[0, s0 + s, u * V_ROWS:(u + 1) * V_ROWS, :]
                                                   for s in range(nk // SWA_SLAB)], axis=1),
                        lambda: bias_ref[geom], n_ctx, nk)
            run([ctx_part, win_part])

        if q_blk_off < n_ctx_blk:
            pl.when(jnp.logical_not(is_lat))(lambda: run([ctx_part]))
            pl.when(is_lat)(latent)
        else:
            latent()
    else:
        m_ref[...] = jnp.full(m_ref.shape, NEG_INF, F32)
        acc_ref[...] = jnp.zeros(acc_ref.shape, F32)

        def qk(c, par, h):
            lanes = slice(h * TQ, (h + 1) * TQ)
            k_cols = slice(0, LANES) if gqa else slice(h * LANES, (h + 1) * LANES)
            k_rows = k_ref[0, pl.ds(pl.multiple_of(c * TK, TK), TK), k_cols]
            s = jnp.dot(k_rows, q_ref[0, h], preferred_element_type=F32)
            s_refs[par][:, lanes] = s
            cm_refs[par][:, lanes] = jnp.max(s, axis=0, keepdims=True)

        def softmax(sb, par, h):
            lanes = slice(h * TQ, (h + 1) * TQ)
            m_prev = m_ref[:, lanes]
            m_new = jnp.maximum(m_prev, cm_refs[sb][:, lanes])
            m_ref[:, lanes] = m_new
            al_refs[par][:, lanes] = jnp.exp2(m_prev - m_new)
            p_refs[par][:, lanes] = jnp.exp2((s_refs[sb][:, lanes] - m_new).astype(BF16))

        def pv(c, par, h):
            lanes = slice(h * TQ, (h + 1) * TQ)
            u, j = divmod(h, n_heads // n_units)
            ul = slice(j * TQ, (j + 1) * TQ)
            acc_ref[u, :, ul] = (acc_ref[u, :, ul] * al_refs[par][:, lanes]
                                 + jnp.dot(v_ref[0, c, u * V_ROWS:(u + 1) * V_ROWS, :], p_refs[par][:, lanes],
                                           preferred_element_type=F32))

        def step(i, r, n):
            for h in range(n_heads):
                if not isinstance(i, int) or i + 2 < n:
                    qk(i + 2, (r + 2) % 3, h)
                if not isinstance(i, int) or i < n:
                    softmax(r % 3, r % 2, h)
                if not isinstance(i, int) or i >= 1:
                    pv(i - 1, (r + 1) % 2, h)

        def run(n):
            for c in range(min(2, n)):
                for h in range(n_heads):
                    qk(c, c, h)
            step(0, 0, n)
            n_groups = max(0, (n - 3) // 6)

            def body(j, carry):
                for r in range(6):
                    step(1 + 6 * j + r, (1 + r) % 6, n)
                return carry
            if n_groups:
                lax.fori_loop(0, n_groups, body, 0)
            for i in range(1 + 6 * n_groups, n + 1):
                step(i, i % 6, n)

        if q_blk_off < n_ctx_blk:
            pl.when(jnp.logical_not(is_lat))(lambda: run(n_ctx // TK))
            pl.when(is_lat)(lambda: run((n_ctx + seq) // TK))
        else:
            run((n_ctx + seq) // TK)

    heads = []
    for u in range(n_units):
        a = acc_ref[u]
        o = a[:HEAD_DIM] / a[HEAD_DIM:HEAD_DIM + 1]
        heads += [o[:, j * TQ:(j + 1) * TQ] for j in range(unit_w // TQ)]
    tiles = [jnp.concatenate(heads[2 * t:2 * t + 2], axis=0).T for t in range(n_heads // 2)]
    c0 = _OUT_COL[kind]
    o_ref[0, :, c0:c0 + n_heads * HEAD_DIM] = jnp.concatenate(tiles, axis=1).astype(o_ref.dtype)


def _attention(groups, sink, n_ctx, with_ctx_queries):
    bsz, _, _, t_all = groups["swa"][0].shape
    seq = t_all - n_ctx
    n_ctx_blk = n_ctx // TQ
    q_blk_off = 0 if with_ctx_queries else n_ctx_blk
    nblk = t_all // TQ - q_blk_off
    in_specs, args, scratch = [], [], []
    for kind in ("swa", "glb", "mla"):
        q_t, k, v_t = groups[kind]
        n_heads = q_t.shape[1]
        n_units = GQA_KV if kind in ("swa", "glb") else MLA_HEADS
        nq = n_heads * TQ
        if kind == "swa":
            nk = TQ + 2 * WINDOW
            off = np.arange(TQ)[None, None, :] + WINDOW * np.arange(3)[:, None, None] - np.arange(nk)[None, :, None]
            bias = jnp.asarray(np.where(np.abs(off) <= WINDOW, 0.0, NEG_INF), F32)
            in_specs += [pl.BlockSpec(memory_space=pltpu.SMEM), pl.BlockSpec(bias.shape, lambda b, i: (0, 0, 0))]
            args += [sink, bias]
        in_specs += [pl.BlockSpec((1, n_heads, LANES, TQ), lambda b, i: (b, 0, 0, i + q_blk_off)),
                     pl.BlockSpec((1,) + k.shape[1:], lambda b, i: (b, 0, 0)),
                     pl.BlockSpec((1,) + v_t.shape[1:], lambda b, i: (b, 0, 0, 0))]
        args += [q_t, k, v_t]
        scratch += [pltpu.VMEM((n_units, V_ROWS, nq // n_units), F32)]
        if kind == "swa":
            rows = n_ctx + nk
            scratch += [pltpu.VMEM((rows, TQ), F32)] * 3 + [pltpu.VMEM((rows, TQ), BF16)] * 2
        else:
            scratch += ([pltpu.VMEM((1, nq), F32)] + [pltpu.VMEM((TK, nq), F32)] * 3 + [pltpu.VMEM((1, nq), F32)] * 3
                        + [pltpu.VMEM((TK, nq), BF16)] * 2 + [pltpu.VMEM((1, nq), F32)] * 2)
    return pl.pallas_call(
        functools.partial(_attn_kernel, n_ctx=n_ctx, n_ctx_blk=n_ctx_blk, q_blk_off=q_blk_off, seq=seq),
        out_shape=jax.ShapeDtypeStruct((bsz, nblk * TQ, D_MODEL), BF16),
        grid=(bsz, nblk),
        in_specs=in_specs,
        out_specs=pl.BlockSpec((1, TQ, D_MODEL), lambda b, i: (b, i, 0)),
        scratch_shapes=scratch,
        compiler_params=_cparams(2),
        name="attention",
    )(*args)


def _row_select(rows, idx):
    out = rows[0]
    for j in range(1, len(rows)):
        out = jnp.where(idx == j, rows[j], out)
    return out


def _route(scores, biased):
    def top2sum(a, b, c, d):
        hi1, lo1, hi2, lo2 = jnp.maximum(a, b), jnp.minimum(a, b), jnp.maximum(c, d), jnp.minimum(c, d)
        return jnp.maximum(hi1, hi2) + jnp.maximum(jnp.minimum(hi1, hi2), jnp.maximum(lo1, lo2))

    gs = [top2sum(*biased[4 * g:4 * g + 4]) for g in range(N_GROUPS)]
    best, gi = gs[0], jnp.zeros(gs[0].shape, jnp.int32)
    for g in range(1, N_GROUPS):
        better = gs[g] > best
        gi = jnp.where(better, g, gi)
        best = jnp.where(better, gs[g], best)
    a = [_row_select([biased[4 * g + j] for g in range(N_GROUPS)], gi) for j in range(EXPERTS_PER_GROUP)]
    s = [_row_select([scores[4 * g + j] for g in range(N_GROUPS)], gi) for j in range(EXPERTS_PER_GROUP)]
    v1, i1 = a[0], jnp.zeros(gi.shape, jnp.int32)
    for j in range(1, EXPERTS_PER_GROUP):
        better = a[j] > v1
        i1 = jnp.where(better, j, i1)
        v1 = jnp.where(better, a[j], v1)
    v2, i2 = jnp.full(v1.shape, -3.0e38, F32), jnp.zeros(gi.shape, jnp.int32)
    for j in range(EXPERTS_PER_GROUP):
        better = (i1 != j) & (a[j] > v2)
        i2 = jnp.where(better, j, i2)
        v2 = jnp.where(better, a[j], v2)
    lo, hi = jnp.minimum(i1, i2), jnp.maximum(i1, i2)
    pair = jnp.where(lo == 0, hi - 1, jnp.where(lo == 1, hi + 1, 5))
    s_lo, s_hi = _row_select(s, lo), _row_select(s, hi)
    den = s_lo + s_hi
    return gi * N_PAIRS + pair, s_lo / den, s_hi / den


def _out_kernel(y_ref, x_ref, c_ref, mod_ref, wout_ref, g_ref, rwh_ref, rwl_ref, rb_ref, tri_ref,
                xo_ref, row_ref, meta_ref, cnt_ref, carry_ref, *, n_ctx_blk, blk_off):
    @pl.when((pl.program_id(0) == 0) & (pl.program_id(1) == 0))
    def _():
        carry_ref[...] = jnp.zeros(carry_ref.shape, F32)

    y = y_ref[0]
    mod = mod_ref[0, 0]
    half = D_MODEL // 2
    a = jnp.concatenate([jnp.dot(y, wout_ref[:, :half], preferred_element_type=F32),
                         jnp.dot(y, wout_ref[:, half:], preferred_element_type=F32)], axis=1)
    x = jnp.where(pl.program_id(1) + blk_off < n_ctx_blk, c_ref[0], x_ref[0]) + mod[2:3] * a
    xo_ref[0] = x
    h = _rms(x) * g_ref[...]
    h = h * (1.0 + mod[4:5]) + mod[3:4]

    hh = h.astype(BF16)
    hl = (h - hh.astype(F32)).astype(BF16)
    logits = (jnp.dot(hh, rwh_ref[...], preferred_element_type=F32)
              + jnp.dot(hl, rwh_ref[...], preferred_element_type=F32)
              + jnp.dot(hh, rwl_ref[...], preferred_element_type=F32)).T[:32]
    sc = jax.nn.sigmoid(logits)
    bs = sc + rb_ref[...]
    scores = [sc[e:e + 1, :] for e in range(N_EXPERTS)]
    biased = [bs[e:e + 1, :] for e in range(N_EXPERTS)]
    bucket, g_lo, g_hi = _route(scores, biased)

    onehot = jnp.where(lax.broadcasted_iota(jnp.int32, (32, TQ), 0) == bucket, 1.0, 0.0)
    prefix = jnp.dot(onehot.astype(BF16), tri_ref[...], preferred_element_type=F32)
    carry = carry_ref[:, 0:1]
    rank = jnp.sum(onehot * (carry + prefix - 1.0), axis=0, keepdims=True)
    carry_new = jnp.broadcast_to(carry + prefix[:, TQ - 1:TQ], carry_ref.shape)
    carry_ref[...] = carry_new
    cnt_ref[...] = carry_new
    meta_ref[0] = jnp.concatenate([bucket, rank.astype(jnp.int32), jnp.zeros((6, TQ), jnp.int32)], axis=0)

    gates = jnp.concatenate([g_lo, g_hi, jnp.zeros((LANES - 2, TQ), F32)], axis=0)
    row_ref[...] = jnp.concatenate([h, gates.T], axis=1)


def _output_projection(y_mix, stream, t_all, modtab, lw, shared, n_ctx_blk, with_ctx):
    bsz, _, d = stream[0].shape
    off = 0 if with_ctx else n_ctx_blk
    nblk = t_all // TQ - off
    tok = lambda w: pl.BlockSpec((1, TQ, w), lambda b, i: (b, i, 0))
    full = lambda a: pl.BlockSpec(a.shape, lambda b, i: (0,) * a.ndim)
    consts = (lw["wout"], lw["g_ffn"], shared["rw_hi"], shared["rw_lo"], shared["rb"], shared["tri"])
    return pl.pallas_call(
        functools.partial(_out_kernel, n_ctx_blk=n_ctx_blk, blk_off=off),
        out_shape=(jax.ShapeDtypeStruct((bsz, nblk * TQ, d), F32),
                   jax.ShapeDtypeStruct((bsz * nblk * TQ, ROW_W), F32),
                   jax.ShapeDtypeStruct((bsz * nblk, 8, TQ), jnp.int32),
                   jax.ShapeDtypeStruct((32, LANES), F32)),
        grid=(bsz, nblk),
        in_specs=[tok(d)] + _stream_specs(stream, n_ctx_blk, off)
                 + [pl.BlockSpec((1, 1, 8, d), lambda b, i: (b, jnp.where(i + off < n_ctx_blk, 1, 0), 0, 0))]
                 + [full(a) for a in consts],
        out_specs=(tok(d),
                   pl.BlockSpec((TQ, ROW_W), lambda b, i: (b * nblk + i, 0)),
                   pl.BlockSpec((1, 8, TQ), lambda b, i: (b * nblk + i, 0, 0)),
                   pl.BlockSpec((32, LANES), lambda b, i: (0, 0))),
        scratch_shapes=[pltpu.VMEM((32, LANES), F32)],
        compiler_params=_cparams(2),
        name="out_proj_router",
    )(y_mix, stream[0], stream[2], modtab, *consts)


def _scatter_kernel(dest_ref, fill_ref, nv_ref, row_ref, out_ref, buf, sem, *, n_steps, n_blocks):
    i = pl.program_id(0)
    slot = i % 2

    def wait_slot(s):
        pltpu.make_async_copy(buf.at[s], out_ref.at[pl.ds(0, TQ)], sem.at[s]).wait()

    @pl.when(i == 0)
    def _():
        buf[1] = jnp.zeros((TQ, ROW_W), F32)

        def zero_block(blk):
            return pltpu.make_async_copy(buf.at[1], out_ref.at[pl.ds(pl.multiple_of(blk * TM, TM), TM)], sem.at[1])

        for wait in (False, True):
            for b in range(N_BUCKETS):
                @pl.when(fill_ref[b] >= 0)
                def _(b=b, wait=wait):
                    zero_block(fill_ref[b]).wait() if wait else zero_block(fill_ref[b]).start()

            def tail(blk, carry, wait=wait):
                zero_block(blk).wait() if wait else zero_block(blk).start()
                return carry
            lax.fori_loop(nv_ref[0], n_blocks, tail, 0)

    for s in range(2):
        @pl.when(slot == s)
        def _(s=s):
            @pl.when(i >= 2)
            def _():
                wait_slot(s)

            buf[s] = row_ref[...]
            for r in range(TQ):
                d = dest_ref[i * TQ + r]
                pltpu.make_async_copy(buf.at[s, pl.ds(r, 1)], out_ref.at[pl.ds(d, 1)], sem.at[s]).start()

    @pl.when(i == n_steps - 1)
    def _():
        wait_slot(slot)
        if n_steps >= 2:
            wait_slot(1 - slot)


def _scatter_rows(dest, fill_blk, n_valid, rows, n_sorted):
    n_tok = rows.shape[0]
    n_steps = n_tok // TQ
    assert TQ == TM
    return pl.pallas_call(
        functools.partial(_scatter_kernel, n_steps=n_steps, n_blocks=n_sorted // TM),
        out_shape=jax.ShapeDtypeStruct((n_sorted, ROW_W), F32),
        grid_spec=pltpu.PrefetchScalarGridSpec(
            num_scalar_prefetch=3,
            grid=(n_steps,),
            in_specs=[pl.BlockSpec((TQ, ROW_W), lambda i, d, f, nv: (i, 0))],
            out_specs=pl.BlockSpec(memory_space=pl.ANY),
            scratch_shapes=[pltpu.VMEM((2, TQ, ROW_W), F32), pltpu.SemaphoreType.DMA((2,))]),
        compiler_params=_cparams(1),
        name="moe_scatter",
    )(dest, fill_blk, n_valid, rows)


def _swiglu(h, w_gate, w_up, w_down):
    a = jnp.dot(h, w_gate, preferred_element_type=F32)
    a = a * jax.nn.sigmoid(a) * jnp.dot(h, w_up, preferred_element_type=F32)
    return jnp.dot(a.astype(BF16), w_down, preferred_element_type=F32)


def _moe_kernel(elo_ref, ehi_ref, nv_ref, row_ref, gl_ref, ul_ref, dl_ref, gh_ref, uh_ref, dh_ref,
                sg_ref, su_ref, sd_ref, y_ref):
    del elo_ref, ehi_ref
    j = pl.program_id(0)

    @pl.when(j < nv_ref[0])
    def _():
        rows = row_ref[...]
        h = rows[:, :D_MODEL].astype(BF16)
        g_lo = rows[:, D_MODEL:D_MODEL + 1]
        g_hi = rows[:, D_MODEL + 1:D_MODEL + 2]
        y_ref[...] = (g_lo * _swiglu(h, gl_ref[0, 0], ul_ref[0, 0], dl_ref[0, 0])
                      + g_hi * _swiglu(h, gh_ref[0, 0], uh_ref[0, 0], dh_ref[0, 0])
                      + _swiglu(h, sg_ref[0], su_ref[0], sd_ref[0]))

    @pl.when(j >= nv_ref[0])
    def _():
        y_ref[...] = jnp.zeros(y_ref.shape, F32)


def _grouped_experts(e_lo, e_hi, n_valid, rows_sorted, ew, layer):
    n_sorted = rows_sorted.shape[0]
    nb = n_sorted // TM
    routed = lambda w, tab: pl.BlockSpec((1, 1) + w.shape[2:], lambda j, lo, hi, nv: (layer, (lo, hi)[tab][j], 0, 0))
    shared = lambda w: pl.BlockSpec((1,) + w.shape[1:], lambda j, lo, hi, nv: (layer, 0, 0))
    return pl.pallas_call(
        _moe_kernel,
        out_shape=jax.ShapeDtypeStruct((n_sorted, D_MODEL), F32),
        grid_spec=pltpu.PrefetchScalarGridSpec(
            num_scalar_prefetch=3,
            grid=(nb,),
            in_specs=[pl.BlockSpec((TM, ROW_W), lambda j, lo, hi, nv: (j, 0))]
                     + [routed(ew[k], tab) for tab in (0, 1) for k in ("gate", "up", "down")]
                     + [shared(ew[k]) for k in ("s_gate", "s_up", "s_down")],
            out_specs=pl.BlockSpec((TM, D_MODEL), lambda j, lo, hi, nv: (j, 0))),
        compiler_params=_cparams(1),
        name="moe_experts",
    )(e_lo, e_hi, n_valid, rows_sorted, *(ew[k] for k in ("gate", "up", "down")) , *(ew[k] for k in ("gate", "up", "down")),
      *(ew[k] for k in ("s_gate", "s_up", "s_down")))


def _gather_kernel(dest_ref, y_ref, x_ref, mod_ref, gf_ref, o_ref, fbuf, sem, *, n_steps, final_norm):
    i = pl.program_id(0)
    slot = i % 2

    def issue(step, s):
        for r in range(TQ):
            d = dest_ref[step * TQ + r]
            pltpu.make_async_copy(y_ref.at[pl.ds(d, 1)], fbuf.at[s, pl.ds(r, 1)], sem.at[s]).start()

    @pl.when(i == 0)
    def _():
        issue(0, 0)

    for s in range(2):
        @pl.when((i + 1 < n_steps) & (slot == s))
        def _(s=s):
            issue(i + 1, 1 - s)

    pltpu.make_async_copy(y_ref.at[pl.ds(0, TQ)], fbuf.at[slot], sem.at[slot]).wait()
    x = x_ref[0] + mod_ref[0, 0][5:6] * fbuf[slot]
    if final_norm:
        x = _rms(x) * gf_ref[...]
    o_ref[0] = x


def _gather_residual(dest, y_sorted, x_mid, modtab, g_final, n_ctx_blk, with_ctx, final_norm):
    bsz, t_rows, d = x_mid.shape
    nblk = t_rows // TQ
    off = 0 if with_ctx else n_ctx_blk
    n_steps = bsz * nblk
    return pl.pallas_call(
        functools.partial(_gather_kernel, n_steps=n_steps, final_norm=final_norm),
        out_shape=jax.ShapeDtypeStruct((bsz, t_rows, d), F32),
        grid_spec=pltpu.PrefetchScalarGridSpec(
            num_scalar_prefetch=1,
            grid=(n_steps,),
            in_specs=[pl.BlockSpec(memory_space=pl.ANY),
                      pl.BlockSpec((1, TQ, d), lambda i, dst: (i // nblk, i % nblk, 0)),
                      pl.BlockSpec((1, 1, 8, d),
                                   lambda i, dst: (i // nblk, jnp.where(i % nblk + off < n_ctx_blk, 1, 0), 0, 0)),
                      pl.BlockSpec(g_final.shape, lambda i, dst: (0, 0))],
            out_specs=pl.BlockSpec((1, TQ, d), lambda i, dst: (i // nblk, i % nblk, 0)),
            scratch_shapes=[pltpu.VMEM((2, TQ, d), F32), pltpu.SemaphoreType.DMA((2,))]),
        compiler_params=_cparams(1),
        name="moe_gather",
    )(dest, y_sorted, x_mid, modtab, g_final)


def _bucket_layout(meta, counts, n_tok):
    bucket = meta[:, 0, :].reshape(-1)
    rank = meta[:, 1, :].reshape(-1)
    cnt = counts[:N_BUCKETS, 0].astype(jnp.int32)
    padded = (cnt + TM - 1) // TM * TM
    pad_end = jnp.cumsum(padded)
    pad_start = pad_end - padded
    dest = pad_start[bucket] + rank
    nb = n_tok // TM + N_BUCKETS
    n_valid = pad_end[-1] // TM
    blk = jnp.arange(nb, dtype=jnp.int32)
    blk_row = jnp.minimum(blk, n_valid - 1) * TM
    blk_bucket = jnp.minimum(jnp.sum(pad_end[None, :] <= blk_row[:, None], axis=1), N_BUCKETS - 1).astype(jnp.int32)
    grp, pair = blk_bucket // N_PAIRS, blk_bucket % N_PAIRS
    e_lo = grp * EXPERTS_PER_GROUP + jnp.asarray(_PAIR_LO, jnp.int32)[pair]
    e_hi = grp * EXPERTS_PER_GROUP + jnp.asarray(_PAIR_HI, jnp.int32)[pair]
    fill_blk = jnp.where(padded > 0, pad_end // TM - 1, -1).astype(jnp.int32)
    return dest.astype(jnp.int32), fill_blk, e_lo, e_hi, n_valid.reshape(1).astype(jnp.int32), nb * TM


def _rope_tables(seq, n_ctx):
    t = jnp.arange(seq)
    rows, cols = t // GRID_W, t % GRID_W

    def table(width, nf):
        lane = np.arange(width)
        half, j = lane // (2 * nf), lane % (2 * nf)
        inv = ROPE_THETA ** (-jnp.arange(nf, dtype=F32) / nf)
        pos = jnp.where(jnp.asarray(half == 0)[None, :], rows[:, None], cols[:, None]).astype(F32)
        ang = pos * inv[j % nf][None, :]
        sign = jnp.asarray(np.where(j < nf, -1.0, 1.0), F32)[None, :]
        return jnp.cos(ang), jnp.sin(ang) * sign

    cos64, sin64 = table(HEAD_DIM, 16)
    cos64, sin64 = jnp.tile(cos64, (1, 2)), jnp.tile(sin64, (1, 2))
    cosr, sinr = table(MLA_ROPE, 8)
    cosm = jnp.concatenate([jnp.ones((seq, MLA_NOPE), F32), cosr, jnp.ones((seq, 32), F32)], axis=1)
    sinm = jnp.concatenate([jnp.zeros((seq, MLA_NOPE), F32), sinr, jnp.zeros((seq, 32), F32)], axis=1)
    ctx1, ctx0 = jnp.ones((n_ctx, LANES), F32), jnp.zeros((n_ctx, LANES), F32)
    return tuple(jnp.concatenate([c, a], axis=0) for c, a in ((ctx1, cos64), (ctx0, sin64), (ctx1, cosm), (ctx0, sinm)))


def _layer_weights(l, w_in, w_out, norm_ffn_g, glb_q_gain, glb_k_gain, mla_q_gain, mla_w_uq, mla_kv_gain,
                   mla_w_ukv):
    d = w_in.shape[1]
    wi = w_in[l]
    kr = wi[:, 1664:1696]
    win = jnp.concatenate([wi[:, :1664], jnp.zeros((d, MLA_NOPE), F32), kr, jnp.zeros((d, 32), F32)], axis=1)
    uq = mla_w_uq[l].reshape(MLA_Q_RANK, MLA_HEADS, MLA_NOPE + MLA_ROPE)
    wuq = jnp.concatenate([uq, jnp.zeros((MLA_Q_RANK, MLA_HEADS, 32), F32)], axis=2).reshape(MLA_Q_RANK, 512)
    ukv = mla_w_ukv[l].reshape(MLA_KV_RANK, MLA_HEADS, MLA_NOPE + MLA_V)
    wk = jnp.concatenate([ukv[:, :, :MLA_NOPE], jnp.zeros((MLA_KV_RANK, MLA_HEADS, 64), F32)], axis=2)
    wukv = jnp.concatenate([wk.reshape(MLA_KV_RANK, 512), ukv[:, :, MLA_NOPE:].reshape(MLA_KV_RANK, 256)], axis=1)
    gain512 = jnp.concatenate([jnp.tile(glb_q_gain[l] * (HEAD_DIM ** -0.5 * LOG2E), GQA_HEADS),
                               jnp.tile(glb_k_gain[l], GQA_KV)]).reshape(1, 512)
    head = np.arange(512) // HEAD_DIM
    return {
        "win": win.astype(BF16),
        "wuq": wuq.astype(BF16),
        "wukv": wukv.astype(BF16),
        "gain512": gain512,
        "mqg": mla_q_gain[l].reshape(1, -1),
        "mkvg": mla_kv_gain[l].reshape(1, -1),
        "bd": jnp.asarray(head[:, None] == head[None, :], BF16),
        "wout": w_out[l].astype(BF16),
        "g_ffn": norm_ffn_g[l].reshape(1, -1),
    }


def kernel(x, c, ctx, c_ctx, w_mod, b_mod, norm_mix_g, norm_ffn_g, w_in, w_out, swa_sink, glb_q_gain, glb_k_gain,
           mla_q_gain, mla_w_uq, mla_kv_gain, mla_w_ukv, router_w, router_bias, exp_w_gate, exp_w_up, exp_w_down,
           shr_w_gate, shr_w_up, shr_w_down, final_norm_g):
    bsz, seq, d = x.shape
    n_ctx = ctx.shape[1]
    n_layers = w_mod.shape[0]
    assert d == D_MODEL and n_ctx % TQ == 0 and seq % TQ == 0 and seq >= TQ + 2 * WINDOW and seq % GRID_W == 0
    n_ctx_blk = n_ctx // TQ

    rows = -(-(bsz + 1) // 8) * 8
    c_rows = jnp.concatenate([c, c_ctx[None, :], jnp.zeros((rows - bsz - 1, d), F32)], axis=0)
    mods = _modulation(c_rows, w_mod, b_mod)
    mod_x = mods[:, :bsz].reshape(n_layers, bsz, 6, d)
    mod_c = jnp.broadcast_to(mods[:, bsz].reshape(n_layers, 1, 6, d), (n_layers, bsz, 6, d))
    modtabs = jnp.pad(jnp.stack([mod_x, mod_c], axis=2), ((0, 0), (0, 0), (0, 0), (0, 2), (0, 0)))

    tabs = _rope_tables(seq, n_ctx)
    rw = jnp.pad(router_w, ((0, 0), (0, LANES - N_EXPERTS)))
    rw_hi = rw.astype(BF16)
    shared = {
        "rw_hi": rw_hi,
        "rw_lo": (rw - rw_hi.astype(F32)).astype(BF16),
        "rb": jnp.pad(router_bias, (0, 32 - N_EXPERTS)).reshape(32, 1),
        "tri": jnp.asarray(np.arange(TQ)[:, None] <= np.arange(TQ)[None, :], BF16),
    }
    g_final = final_norm_g.reshape(1, d)
    ew = {"gate": exp_w_gate.astype(BF16), "up": exp_w_up.astype(BF16), "down": exp_w_down.astype(BF16),
          "s_gate": shr_w_gate.astype(BF16), "s_up": shr_w_up.astype(BF16), "s_down": shr_w_down.astype(BF16)}

    t_all = n_ctx + seq
    stream = (x, 0, ctx)
    for l in range(n_layers):
        last = l == n_layers - 1
        with_ctx = not last
        lw = _layer_weights(l, w_in, w_out, norm_ffn_g, glb_q_gain, glb_k_gain, mla_q_gain, mla_w_uq,
                            mla_kv_gain, mla_w_ukv)
        modtab = modtabs[l]
        qs, ks, vs, qg, kg, vg, qm, km, vm = _input_projection(
            stream, t_all, modtab, norm_mix_g[l].reshape(1, d), lw, tabs, n_ctx_blk)
        sink = jnp.pad(swa_sink[l] * LOG2E, (0, 8 - GQA_HEADS))
        y_mix = _attention({"swa": (qs, ks, vs), "glb": (qg, kg, vg), "mla": (qm, km, vm)}, sink, n_ctx, with_ctx)
        x_mid, rows_tok, meta, counts = _output_projection(y_mix, stream, t_all, modtab, lw, shared, n_ctx_blk,
                                                           with_ctx)
        n_tok = rows_tok.shape[0]
        dest, fill_blk, e_lo, e_hi, n_valid, n_sorted = _bucket_layout(meta, counts, n_tok)
        rows_sorted = _scatter_rows(dest, fill_blk, n_valid, rows_tok, n_sorted)
        y_sorted = _grouped_experts(e_lo, e_hi, n_valid, rows_sorted, ew, l)
        xs = _gather_residual(dest, y_sorted, x_mid, modtab, g_final, n_ctx_blk, with_ctx, final_norm=last)
        stream = (xs, n_ctx_blk, xs)
    return xs
```

```python
import functools

import jax
import jax.numpy as jnp
import numpy as np
from jax import lax
from jax.experimental import pallas as pl
from jax.experimental.pallas import tpu as pltpu

F32 = jnp.float32
BF16 = jnp.bfloat16

D_MODEL = 1024
GRID_W = 64
HEAD_DIM = 64
GQA_HEADS = 6
GQA_KV = 2
GQA_G = GQA_HEADS // GQA_KV
WINDOW = 128
MLA_HEADS = 4
MLA_NOPE = 64
MLA_ROPE = 32
MLA_V = 64
MLA_Q_RANK = 256
MLA_KV_RANK = 128
ROPE_THETA = 10000.0
N_EXPERTS = 16
N_GROUPS = 4
EXPERTS_PER_GROUP = 4
N_PAIRS = 6
N_BUCKETS = N_GROUPS * N_PAIRS
D_EXPERT = 512
EPS = 1e-6
NEG_INF = -1e30

LANES = 128
TQ = 256
TK = 256
TM = 256
SWA_SLAB = 128
V_ROWS = 80
LOG2E = 1.4426950408889634
ROW_W = D_MODEL + LANES
VMEM_LIMIT = 56 * 1024 * 1024

_SQ, _SK, _SV, _GQ, _GK, _GV, _MQ, _MKV, _KR = 0, 384, 512, 640, 1024, 1152, 1280, 1536, 1664
_A_SQ, _A_GQ, _A_MQ, _A_SV, _A_GV, _A_MKV, _WA_ROWS = 0, 384, 768, 1024, 1152, 1280, 1408

_PAIR_LO = (0, 0, 0, 1, 1, 2)
_PAIR_HI = (1, 2, 3, 2, 3, 3)


def _cparams(n_axes):
    return pltpu.CompilerParams(dimension_semantics=("arbitrary",) * n_axes,
                                vmem_limit_bytes=VMEM_LIMIT)


def _mod_kernel(c_ref, w_ref, b_ref, o_ref):
    c = c_ref[...]
    a = (c * jax.nn.sigmoid(c)).astype(BF16)
    o_ref[0] = jnp.dot(a, w_ref[0].astype(BF16), preferred_element_type=F32) + b_ref[0]


def _modulation(c_rows, w_mod, b_mod):
    n_layers, d, width = w_mod.shape
    rows = c_rows.shape[0]
    nb = 1536
    return pl.pallas_call(
        _mod_kernel,
        out_shape=jax.ShapeDtypeStruct((n_layers, rows, width), F32),
        grid=(n_layers, width // nb),
        in_specs=[pl.BlockSpec((rows, d), lambda l, j: (0, 0)),
                  pl.BlockSpec((1, d, nb), lambda l, j: (l, 0, j)),
                  pl.BlockSpec((1, 1, nb), lambda l, j: (l, 0, j))],
        out_specs=pl.BlockSpec((1, rows, nb), lambda l, j: (l, 0, j)),
        compiler_params=_cparams(2),
        name="adaln_mod",
    )(c_rows, w_mod, b_mod.reshape(n_layers, 1, width))


def _rope(x, cos, sin_signed, nf):
    lane = lax.broadcasted_iota(jnp.int32, (1, LANES), 1)
    first = (lane % (2 * nf)) < nf
    tiles = []
    for t in range(x.shape[1] // LANES):
        xt = x[:, t * LANES:(t + 1) * LANES]
        partner = jnp.where(first, pltpu.roll(xt, LANES - nf, 1), pltpu.roll(xt, nf, 1))
        tiles.append(xt * cos + partner * sin_signed)
    return tiles[0] if len(tiles) == 1 else jnp.concatenate(tiles, axis=1)


def _rope_t(x, cos_t, sin_t, nf):
    partner = jnp.concatenate([x[nf:2 * nf], x[:nf], x[3 * nf:], x[2 * nf:3 * nf]], axis=0)
    return x * cos_t + partner * sin_t


def _emit_v_t(ref, v_t, slab):
    ones_blk = jnp.where(lax.broadcasted_iota(jnp.int32, (V_ROWS - HEAD_DIM, TQ), 0) == 0, 1.0, 0.0)
    parts = []
    for h in range(v_t.shape[0] // HEAD_DIM):
        parts += [v_t[h * HEAD_DIM:(h + 1) * HEAD_DIM], ones_blk]
    ext = jnp.concatenate(parts, axis=0).astype(ref.dtype)
    for s in range(TQ // slab):
        ref[0, s] = ext[:, s * slab:(s + 1) * slab]


def _rms(x, eps=EPS):
    return x * lax.rsqrt(jnp.mean(x * x, axis=-1, keepdims=True) + eps)


def _stream_specs(stream, n_ctx_blk, blk_off):
    lat, lat_off, ctx = stream
    d = lat.shape[2]
    return [pl.BlockSpec((1, TQ, d), lambda b, i: (b, jnp.maximum(i + blk_off - n_ctx_blk, 0) + lat_off, 0)),
            pl.BlockSpec((1, TQ, d), lambda b, i: (b, jnp.minimum(i + blk_off, n_ctx_blk - 1), 0))]


def _rms_t(x, eps=EPS):
    return x * lax.rsqrt(jnp.mean(x * x, axis=0, keepdims=True) + eps)


def _in_kernel(x_ref, c_ref, mod_ref, g_ref, wa_ref, wb_ref, cos64_ref, sin64_ref, cosm_ref, sinm_ref,
               cos64t_ref, sin64t_ref, cosmt_ref, sinmt_ref, gqg_ref, gkg_ref, mqg_ref, mkvg_ref, mkvgt_ref,
               wuqt_ref, wukvk_ref, wukvvt_ref, bd_ref,
               qs_ref, ks_ref, vs_ref, qg_ref, kg_ref, vg_ref, qm_ref, km_ref, vm_ref, *, mla_scale, n_ctx_blk):
    x = jnp.where(pl.program_id(1) < n_ctx_blk, c_ref[0], x_ref[0])
    mod = mod_ref[0, 0]
    h = _rms(x) * g_ref[...]
    h = h * (1.0 + mod[1:2]) + mod[0:1]
    za = jnp.dot(wa_ref[...], h.T.astype(BF16), preferred_element_type=F32)
    zb = jnp.dot(h.astype(BF16), wb_ref[...], preferred_element_type=F32)
    cos_t, sin_t = cos64t_ref[...], sin64t_ref[...]
    zeros = jnp.zeros((HEAD_DIM, TQ), F32)

    def emit_gqa_q(ref, q_t, prep):
        for hd in range(GQA_HEADS):
            q = prep(q_t[hd * HEAD_DIM:(hd + 1) * HEAD_DIM])
            q = _rope_t(q, cos_t, sin_t, 16)
            tile = [q, zeros] if hd // GQA_G == 0 else [zeros, q]
            ref[0, hd] = jnp.concatenate(tile, axis=0).astype(ref.dtype)

    emit_gqa_q(qs_ref, za[_A_SQ:_A_GQ], lambda q: q * (HEAD_DIM ** -0.5 * LOG2E))
    ks_ref[0] = _rope(zb[:, 0:128], cos64_ref[...], sin64_ref[...], 16).astype(BF16)
    _emit_v_t(vs_ref, za[_A_SV:_A_GV], SWA_SLAB)

    emit_gqa_q(qg_ref, za[_A_GQ:_A_MQ], lambda q: _rms_t(q) * gqg_ref[...])
    gk = zb[:, 128:256]
    sq = gk * gk
    hi = sq.astype(BF16)
    lo = (sq - hi.astype(F32)).astype(BF16)
    ssum = (jnp.dot(hi, bd_ref[...], preferred_element_type=F32)
            + jnp.dot(lo, bd_ref[...], preferred_element_type=F32))
    gk = gk * lax.rsqrt(ssum * (1.0 / HEAD_DIM) + EPS) * gkg_ref[...]
    kg_ref[0] = _rope(gk, cos64_ref[...], sin64_ref[...], 16).astype(BF16)
    _emit_v_t(vg_ref, za[_A_GV:_A_MKV], TK)

    qn = _rms_t(za[_A_MQ:_A_SV]) * mqg_ref[...]
    mq = jnp.dot(wuqt_ref[...], qn.astype(BF16), preferred_element_type=F32)
    for hd in range(MLA_HEADS):
        t = mq[hd * LANES:(hd + 1) * LANES]
        r = _rope_t(t[MLA_NOPE:MLA_NOPE + MLA_ROPE], cosmt_ref[...], sinmt_ref[...], 8)
        t = jnp.concatenate([t[:MLA_NOPE], r, t[MLA_NOPE + MLA_ROPE:]], axis=0) * (mla_scale * LOG2E)
        qm_ref[0, hd] = t.astype(qm_ref.dtype)
    kvn = _rms(zb[:, 256:384]) * mkvg_ref[...]
    k_nope = jnp.dot(kvn.astype(BF16), wukvk_ref[...], preferred_element_type=F32)
    kr = _rope(zb[:, 384:512], cosm_ref[...], sinm_ref[...], 8)
    km_ref[0] = (k_nope + jnp.concatenate([kr] * MLA_HEADS, axis=1)).astype(BF16)
    kvn_t = _rms_t(za[_A_MKV:_WA_ROWS]) * mkvgt_ref[...]
    _emit_v_t(vm_ref, jnp.dot(wukvvt_ref[...], kvn_t.astype(BF16), preferred_element_type=F32), TK)


def _input_projection(stream, t_all, modtab, g_mix, lw, tabs, n_ctx_blk):
    bsz, _, d = stream[0].shape
    nblk = t_all // TQ
    tok = lambda w: pl.BlockSpec((1, TQ, w), lambda b, i: (b, i, 0))
    q_t = lambda nh: pl.BlockSpec((1, nh, LANES, TQ), lambda b, i: (b, 0, 0, i))
    v_t = lambda nh, slab: pl.BlockSpec((1, TQ // slab, nh * V_ROWS, slab), lambda b, i: (b, i, 0, 0))
    full = lambda a: pl.BlockSpec(a.shape, lambda b, i: (0,) * a.ndim)
    tab = pl.BlockSpec((TQ, LANES), lambda b, i: (i, 0))
    q_shape = lambda nh: jax.ShapeDtypeStruct((bsz, nh, LANES, t_all), BF16)
    k_shape = lambda w: jax.ShapeDtypeStruct((bsz, t_all, w), BF16)
    v_shape = lambda nh, slab: jax.ShapeDtypeStruct((bsz, t_all // slab, nh * V_ROWS, slab), BF16)
    tab_t = lambda a: pl.BlockSpec((a.shape[0], TQ), lambda b, i: (0, i))
    consts = (lw["gqg_t"], lw["gkg"], lw["mqg_t"], lw["mkvg"], lw["mkvg_t"], lw["wuq_t"], lw["wukv_k"],
              lw["wukv_vt"], lw["bd"])
    return pl.pallas_call(
        functools.partial(_in_kernel, mla_scale=float((MLA_NOPE + MLA_ROPE) ** -0.5), n_ctx_blk=n_ctx_blk),
        out_shape=(q_shape(GQA_HEADS), k_shape(128), v_shape(GQA_KV, SWA_SLAB),
                   q_shape(GQA_HEADS), k_shape(128), v_shape(GQA_KV, TK),
                   q_shape(MLA_HEADS), k_shape(512), v_shape(MLA_HEADS, TK)),
        grid=(bsz, nblk),
        in_specs=_stream_specs(stream, n_ctx_blk, 0)
                 + [pl.BlockSpec((1, 1, 8, d), lambda b, i: (b, jnp.where(i < n_ctx_blk, 1, 0), 0, 0)),
                    full(g_mix), full(lw["wa"]), full(lw["wb"]), tab, tab, tab, tab]
                 + [tab_t(a) for a in tabs[4:]] + [full(a) for a in consts],
        out_specs=(q_t(GQA_HEADS), tok(128), v_t(GQA_KV, SWA_SLAB),
                   q_t(GQA_HEADS), tok(128), v_t(GQA_KV, TK),
                   q_t(MLA_HEADS), tok(512), v_t(MLA_HEADS, TK)),
        compiler_params=_cparams(2),
        name="in_proj",
    )(stream[0], stream[2], modtab, g_mix, lw["wa"], lw["wb"], *tabs, *consts)


_N_IN = {"swa": 5, "glb": 3, "mla": 3}
_N_SCRATCH = {"swa": 6, "glb": 12, "mla": 12}
_OUT_COL = {"swa": 0, "glb": GQA_HEADS * HEAD_DIM, "mla": 2 * GQA_HEADS * HEAD_DIM}


def _attn_kernel(*refs, n_ctx, n_ctx_blk, q_blk_off, seq):
    kinds = ("swa", "glb", "mla")
    n_in = sum(_N_IN[k] for k in kinds)
    o_ref = refs[n_in]
    i0, s0 = 0, n_in + 1
    for kind in kinds:
        _attn_group(refs[i0:i0 + _N_IN[kind]], o_ref, refs[s0:s0 + _N_SCRATCH[kind]], kind=kind, n_ctx=n_ctx,
                    n_ctx_blk=n_ctx_blk, q_blk_off=q_blk_off, seq=seq)
        i0 += _N_IN[kind]
        s0 += _N_SCRATCH[kind]


def _attn_group(ins, o_ref, scratch, *, kind, n_ctx, n_ctx_blk, q_blk_off, seq):
    if kind == "swa":
        sink_ref, bias_ref, q_ref, k_ref, v_ref = ins
        acc_ref, s_refs, p_refs = scratch[0], scratch[1:4], scratch[4:6]
    else:
        q_ref, k_ref, v_ref = ins
        acc_ref, m_ref = scratch[:2]
        s_refs, cm_refs, p_refs, al_refs = scratch[2:5], scratch[5:8], scratch[8:10], scratch[10:12]
    blk = pl.program_id(1) + q_blk_off
    is_lat = blk >= n_ctx_blk
    gqa = kind in ("swa", "glb")
    n_heads = GQA_HEADS if gqa else MLA_HEADS
    n_units = GQA_KV if gqa else MLA_HEADS
    unit_w = n_heads // n_units * TQ

    if kind == "swa":
        nk = TQ + 2 * WINDOW
        n_slab = n_ctx // SWA_SLAB
        row_l = lax.broadcasted_iota(jnp.int32, (V_ROWS, TQ), 0) == HEAD_DIM

        def scores(h, parts):
            cm = None
            for k_rows, _, bias, r0, nr in parts:
                s = jnp.dot(k_rows(), q_ref[0, h], preferred_element_type=F32)
                if bias is not None:
                    s = s + bias()
                s_refs[h % 3][r0:r0 + nr, :] = s
                c = jnp.max(s, axis=0, keepdims=True)
                cm = c if cm is None else jnp.maximum(cm, c)
            return cm

        def softmax(h, cm, nrows):
            m = jnp.maximum(cm, sink_ref[h])
            p_refs[h % 2][0:nrows, :] = jnp.exp2((s_refs[h % 3][0:nrows, :] - m).astype(BF16))
            return jnp.exp2(sink_ref[h] - m)

        def values(h, e_sink, parts):
            u, j = divmod(h, GQA_G)
            acc = None
            for _, v_t, _, r0, nr in parts:
                a = jnp.dot(v_t(u), p_refs[h % 2][r0:r0 + nr, :], preferred_element_type=F32)
                acc = a if acc is None else acc + a
            acc_ref[u, :, j * TQ:(j + 1) * TQ] = acc + jnp.where(row_l, e_sink, 0.0)

        def run(parts):
            nrows = sum(p[4] for p in parts)
            cms = {h: scores(h, parts) for h in range(min(2, n_heads))}
            e_sinks = {}
            for h in range(n_heads + 1):
                if h + 2 < n_heads:
                    cms[h + 2] = scores(h + 2, parts)
                if h < n_heads:
                    e_sinks[h] = softmax(h, cms[h], nrows)
                if h >= 1:
                    values(h - 1, e_sinks[h - 1], parts)

        ctx_part = (lambda: k_ref[0, 0:n_ctx, :],
                    lambda u: jnp.concatenate([v_ref[0, s, u * V_ROWS:(u + 1) * V_ROWS, :] for s in range(n_slab)],
                                              axis=1),
                    None, 0, n_ctx)

        def latent():
            q0 = (blk - n_ctx_blk) * TQ
            k0 = pl.multiple_of(jnp.clip(q0 - WINDOW, 0, seq - nk), WINDOW)
            geom = (q0 - k0) // WINDOW
            s0 = (n_ctx + k0) // SWA_SLAB
            win_part = (lambda: k_ref[0, pl.ds(pl.multiple_of(n_ctx + k0, WINDOW), nk), :],
                        lambda u: jnp.concatenate([v_ref[0, s0 + s, u * V_ROWS:(u + 1) * V_ROWS, :]
                                                   for s in range(nk // SWA_SLAB)], axis=1),
                        lambda: bias_ref[geom], n_ctx, nk)
            run([ctx_part, win_part])

        if q_blk_off < n_ctx_blk:
            pl.when(jnp.logical_not(is_lat))(lambda: run([ctx_part]))
            pl.when(is_lat)(latent)
        else:
            latent()
    else:
        m_ref[...] = jnp.full(m_ref.shape, NEG_INF, F32)
        acc_ref[...] = jnp.zeros(acc_ref.shape, F32)

        def qk(c, par, h):
            lanes = slice(h * TQ, (h + 1) * TQ)
            k_cols = slice(0, LANES) if gqa else slice(h * LANES, (h + 1) * LANES)
            k_rows = k_ref[0, pl.ds(pl.multiple_of(c * TK, TK), TK), k_cols]
            s = jnp.dot(k_rows, q_ref[0, h], preferred_element_type=F32)
            s_refs[par][:, lanes] = s
            cm_refs[par][:, lanes] = jnp.max(s, axis=0, keepdims=True)

        def softmax(sb, par, h):
            lanes = slice(h * TQ, (h + 1) * TQ)
            m_prev = m_ref[:, lanes]
            m_new = jnp.maximum(m_prev, cm_refs[sb][:, lanes])
            m_ref[:, lanes] = m_new
            al_refs[par][:, lanes] = jnp.exp2(m_prev - m_new)
            p_refs[par][:, lanes] = jnp.exp2((s_refs[sb][:, lanes] - m_new).astype(BF16))

        def pv(c, par, h):
            lanes = slice(h * TQ, (h + 1) * TQ)
            u, j = divmod(h, n_heads // n_units)
            ul = slice(j * TQ, (j + 1) * TQ)
            acc_ref[u, :, ul] = (acc_ref[u, :, ul] * al_refs[par][:, lanes]
                                 + jnp.dot(v_ref[0, c, u * V_ROWS:(u + 1) * V_ROWS, :], p_refs[par][:, lanes],
                                           preferred_element_type=F32))

        def step(i, r, n):
            for h in range(n_heads):
                if not isinstance(i, int) or i + 2 < n:
                    qk(i + 2, (r + 2) % 3, h)
                if not isinstance(i, int) or i < n:
                    softmax(r % 3, r % 2, h)
                if not isinstance(i, int) or i >= 1:
                    pv(i - 1, (r + 1) % 2, h)

        def run(n):
            for c in range(min(2, n)):
                for h in range(n_heads):
                    qk(c, c, h)
            step(0, 0, n)
            n_groups = max(0, (n - 3) // 6)

            def body(j, carry):
                for r in range(6):
                    step(1 + 6 * j + r, (1 + r) % 6, n)
                return carry
            if n_groups:
                lax.fori_loop(0, n_groups, body, 0)
            for i in range(1 + 6 * n_groups, n + 1):
                step(i, i % 6, n)

        if q_blk_off < n_ctx_blk:
            pl.when(jnp.logical_not(is_lat))(lambda: run(n_ctx // TK))
            pl.when(is_lat)(lambda: run((n_ctx + seq) // TK))
        else:
            run((n_ctx + seq) // TK)

    heads = []
    for u in range(n_units):
        a = acc_ref[u]
        o = a[:HEAD_DIM] / a[HEAD_DIM:HEAD_DIM + 1]
        heads += [o[:, j * TQ:(j + 1) * TQ] for j in range(unit_w // TQ)]
    tiles = [jnp.concatenate(heads[2 * t:2 * t + 2], axis=0).T for t in range(n_heads // 2)]
    c0 = _OUT_COL[kind]
    o_ref[0, :, c0:c0 + n_heads * HEAD_DIM] = jnp.concatenate(tiles, axis=1).astype(o_ref.dtype)


def _attention(groups, sink, n_ctx, with_ctx_queries):
    bsz, _, _, t_all = groups["swa"][0].shape
    seq = t_all - n_ctx
    n_ctx_blk = n_ctx // TQ
    q_blk_off = 0 if with_ctx_queries else n_ctx_blk
    nblk = t_all // TQ - q_blk_off
    in_specs, args, scratch = [], [], []
    for kind in ("swa", "glb", "mla"):
        q_t, k, v_t = groups[kind]
        n_heads = q_t.shape[1]
        n_units = GQA_KV if kind in ("swa", "glb") else MLA_HEADS
        nq = n_heads * TQ
        if kind == "swa":
            nk = TQ + 2 * WINDOW
            off = np.arange(TQ)[None, None, :] + WINDOW * np.arange(3)[:, None, None] - np.arange(nk)[None, :, None]
            bias = jnp.asarray(np.where(np.abs(off) <= WINDOW, 0.0, NEG_INF), F32)
            in_specs += [pl.BlockSpec(memory_space=pltpu.SMEM), pl.BlockSpec(bias.shape, lambda b, i: (0, 0, 0))]
            args += [sink, bias]
        in_specs += [pl.BlockSpec((1, n_heads, LANES, TQ), lambda b, i: (b, 0, 0, i + q_blk_off)),
                     pl.BlockSpec((1,) + k.shape[1:], lambda b, i: (b, 0, 0)),
                     pl.BlockSpec((1,) + v_t.shape[1:], lambda b, i: (b, 0, 0, 0))]
        args += [q_t, k, v_t]
        scratch += [pltpu.VMEM((n_units, V_ROWS, nq // n_units), F32)]
        if kind == "swa":
            rows = n_ctx + nk
            scratch += [pltpu.VMEM((rows, TQ), F32)] * 3 + [pltpu.VMEM((rows, TQ), BF16)] * 2
        else:
            scratch += ([pltpu.VMEM((1, nq), F32)] + [pltpu.VMEM((TK, nq), F32)] * 3 + [pltpu.VMEM((1, nq), F32)] * 3
                        + [pltpu.VMEM((TK, nq), BF16)] * 2 + [pltpu.VMEM((1, nq), F32)] * 2)
    return pl.pallas_call(
        functools.partial(_attn_kernel, n_ctx=n_ctx, n_ctx_blk=n_ctx_blk, q_blk_off=q_blk_off, seq=seq),
        out_shape=jax.ShapeDtypeStruct((bsz, nblk * TQ, D_MODEL), BF16),
        grid=(bsz, nblk),
        in_specs=in_specs,
        out_specs=pl.BlockSpec((1, TQ, D_MODEL), lambda b, i: (b, i, 0)),
        scratch_shapes=scratch,
        compiler_params=_cparams(2),
        name="attention",
    )(*args)


def _row_select(rows, idx):
    out = rows[0]
    for j in range(1, len(rows)):
        out = jnp.where(idx == j, rows[j], out)
    return out


def _route(scores, biased):
    def top2sum(a, b, c, d):
        hi1, lo1, hi2, lo2 = jnp.maximum(a, b), jnp.minimum(a, b), jnp.maximum(c, d), jnp.minimum(c, d)
        return jnp.maximum(hi1, hi2) + jnp.maximum(jnp.minimum(hi1, hi2), jnp.maximum(lo1, lo2))

    gs = [top2sum(*biased[4 * g:4 * g + 4]) for g in range(N_GROUPS)]
    best, gi = gs[0], jnp.zeros(gs[0].shape, jnp.int32)
    for g in range(1, N_GROUPS):
        better = gs[g] > best
        gi = jnp.where(better, g, gi)
        best = jnp.where(better, gs[g], best)
    a = [_row_select([biased[4 * g + j] for g in range(N_GROUPS)], gi) for j in range(EXPERTS_PER_GROUP)]
    s = [_row_select([scores[4 * g + j] for g in range(N_GROUPS)], gi) for j in range(EXPERTS_PER_GROUP)]
    v1, i1 = a[0], jnp.zeros(gi.shape, jnp.int32)
    for j in range(1, EXPERTS_PER_GROUP):
        better = a[j] > v1
        i1 = jnp.where(better, j, i1)
        v1 = jnp.where(better, a[j], v1)
    v2, i2 = jnp.full(v1.shape, -3.0e38, F32), jnp.zeros(gi.shape, jnp.int32)
    for j in range(EXPERTS_PER_GROUP):
        better = (i1 != j) & (a[j] > v2)
        i2 = jnp.where(better, j, i2)
        v2 = jnp.where(better, a[j], v2)
    lo, hi = jnp.minimum(i1, i2), jnp.maximum(i1, i2)
    pair = jnp.where(lo == 0, hi - 1, jnp.where(lo == 1, hi + 1, 5))
    s_lo, s_hi = _row_select(s, lo), _row_select(s, hi)
    den = s_lo + s_hi
    return gi * N_PAIRS + pair, s_lo / den, s_hi / den


def _out_kernel(y_ref, x_ref, c_ref, mod_ref, wout_ref, g_ref, rwh_ref, rwl_ref, rb_ref, tri_ref,
                xo_ref, row_ref, meta_ref, cnt_ref, carry_ref, *, n_ctx_blk, blk_off):
    @pl.when((pl.program_id(0) == 0) & (pl.program_id(1) == 0))
    def _():
        carry_ref[...] = jnp.zeros(carry_ref.shape, F32)

    y = y_ref[0]
    mod = mod_ref[0, 0]
    half = D_MODEL // 2
    a = jnp.concatenate([jnp.dot(y, wout_ref[:, :half], preferred_element_type=F32),
                         jnp.dot(y, wout_ref[:, half:], preferred_element_type=F32)], axis=1)
    x = jnp.where(pl.program_id(1) + blk_off < n_ctx_blk, c_ref[0], x_ref[0]) + mod[2:3] * a
    xo_ref[0] = x
    h = _rms(x) * g_ref[...]
    h = h * (1.0 + mod[4:5]) + mod[3:4]

    hh = h.astype(BF16)
    hl = (h - hh.astype(F32)).astype(BF16)
    logits = (jnp.dot(hh, rwh_ref[...], preferred_element_type=F32)
              + jnp.dot(hl, rwh_ref[...], preferred_element_type=F32)
              + jnp.dot(hh, rwl_ref[...], preferred_element_type=F32)).T[:32]
    sc = jax.nn.sigmoid(logits)
    bs = sc + rb_ref[...]
    scores = [sc[e:e + 1, :] for e in range(N_EXPERTS)]
    biased = [bs[e:e + 1, :] for e in range(N_EXPERTS)]
    bucket, g_lo, g_hi = _route(scores, biased)

    onehot = jnp.where(lax.broadcasted_iota(jnp.int32, (32, TQ), 0) == bucket, 1.0, 0.0)
    prefix = jnp.dot(onehot.astype(BF16), tri_ref[...], preferred_element_type=F32)
    carry = carry_ref[:, 0:1]
    rank = jnp.sum(onehot * (carry + prefix - 1.0), axis=0, keepdims=True)
    carry_new = jnp.broadcast_to(carry + prefix[:, TQ - 1:TQ], carry_ref.shape)
    carry_ref[...] = carry_new
    cnt_ref[...] = carry_new
    meta_ref[0] = jnp.concatenate([bucket, rank.astype(jnp.int32), jnp.zeros((6, TQ), jnp.int32)], axis=0)

    gates = jnp.concatenate([g_lo, g_hi, jnp.zeros((LANES - 2, TQ), F32)], axis=0)
    row_ref[...] = jnp.concatenate([h, gates.T], axis=1)


def _output_projection(y_mix, stream, t_all, modtab, lw, shared, n_ctx_blk, with_ctx):
    bsz, _, d = stream[0].shape
    off = 0 if with_ctx else n_ctx_blk
    nblk = t_all // TQ - off
    tok = lambda w: pl.BlockSpec((1, TQ, w), lambda b, i: (b, i, 0))
    full = lambda a: pl.BlockSpec(a.shape, lambda b, i: (0,) * a.ndim)
    consts = (lw["wout"], lw["g_ffn"], shared["rw_hi"], shared["rw_lo"], shared["rb"], shared["tri"])
    return pl.pallas_call(
        functools.partial(_out_kernel, n_ctx_blk=n_ctx_blk, blk_off=off),
        out_shape=(jax.ShapeDtypeStruct((bsz, nblk * TQ, d), F32),
                   jax.ShapeDtypeStruct((bsz * nblk * TQ, ROW_W), F32),
                   jax.ShapeDtypeStruct((bsz * nblk, 8, TQ), jnp.int32),
                   jax.ShapeDtypeStruct((32, LANES), F32)),
        grid=(bsz, nblk),
        in_specs=[tok(d)] + _stream_specs(stream, n_ctx_blk, off)
                 + [pl.BlockSpec((1, 1, 8, d), lambda b, i: (b, jnp.where(i + off < n_ctx_blk, 1, 0), 0, 0))]
                 + [full(a) for a in consts],
        out_specs=(tok(d),
                   pl.BlockSpec((TQ, ROW_W), lambda b, i: (b * nblk + i, 0)),
                   pl.BlockSpec((1, 8, TQ), lambda b, i: (b * nblk + i, 0, 0)),
                   pl.BlockSpec((32, LANES), lambda b, i: (0, 0))),
        scratch_shapes=[pltpu.VMEM((32, LANES), F32)],
        compiler_params=_cparams(2),
        name="out_proj_router",
    )(y_mix, stream[0], stream[2], modtab, *consts)


def _scatter_kernel(dest_ref, fill_ref, nv_ref, row_ref, out_ref, buf, sem, *, n_steps, n_blocks):
    i = pl.program_id(0)
    slot = i % 2

    def wait_slot(s):
        pltpu.make_async_copy(buf.at[s], out_ref.at[pl.ds(0, TQ)], sem.at[s]).wait()

    @pl.when(i == 0)
    def _():
        buf[1] = jnp.zeros((TQ, ROW_W), F32)

        def zero_block(blk):
            return pltpu.make_async_copy(buf.at[1], out_ref.at[pl.ds(pl.multiple_of(blk * TM, TM), TM)], sem.at[1])

        for wait in (False, True):
            for b in range(N_BUCKETS):
                @pl.when(fill_ref[b] >= 0)
                def _(b=b, wait=wait):
                    zero_block(fill_ref[b]).wait() if wait else zero_block(fill_ref[b]).start()

            def tail(blk, carry, wait=wait):
                zero_block(blk).wait() if wait else zero_block(blk).start()
                return carry
            lax.fori_loop(nv_ref[0], n_blocks, tail, 0)

    for s in range(2):
        @pl.when(slot == s)
        def _(s=s):
            @pl.when(i >= 2)
            def _():
                wait_slot(s)

            buf[s] = row_ref[...]
            for r in range(TQ):
                d = dest_ref[i * TQ + r]
                pltpu.make_async_copy(buf.at[s, pl.ds(r, 1)], out_ref.at[pl.ds(d, 1)], sem.at[s]).start()

    @pl.when(i == n_steps - 1)
    def _():
        wait_slot(slot)
        if n_steps >= 2:
            wait_slot(1 - slot)


def _scatter_rows(dest, fill_blk, n_valid, rows, n_sorted):
    n_tok = rows.shape[0]
    n_steps = n_tok // TQ
    assert TQ == TM
    return pl.pallas_call(
        functools.partial(_scatter_kernel, n_steps=n_steps, n_blocks=n_sorted // TM),
        out_shape=jax.ShapeDtypeStruct((n_sorted, ROW_W), F32),
        grid_spec=pltpu.PrefetchScalarGridSpec(
            num_scalar_prefetch=3,
            grid=(n_steps,),
            in_specs=[pl.BlockSpec((TQ, ROW_W), lambda i, d, f, nv: (i, 0))],
            out_specs=pl.BlockSpec(memory_space=pl.ANY),
            scratch_shapes=[pltpu.VMEM((2, TQ, ROW_W), F32), pltpu.SemaphoreType.DMA((2,))]),
        compiler_params=_cparams(1),
        name="moe_scatter",
    )(dest, fill_blk, n_valid, rows)


def _swiglu(h, w_gate, w_up, w_down):
    a = jnp.dot(h, w_gate, preferred_element_type=F32)
    a = a * jax.nn.sigmoid(a) * jnp.dot(h, w_up, preferred_element_type=F32)
    return jnp.dot(a.astype(BF16), w_down, preferred_element_type=F32)


def _moe_kernel(elo_ref, ehi_ref, nv_ref, row_ref, gl_ref, ul_ref, dl_ref, gh_ref, uh_ref, dh_ref,
                sg_ref, su_ref, sd_ref, y_ref):
    del elo_ref, ehi_ref
    j = pl.program_id(0)

    @pl.when(j < nv_ref[0])
    def _():
        rows = row_ref[...]
        h = rows[:, :D_MODEL].astype(BF16)
        g_lo = rows[:, D_MODEL:D_MODEL + 1]
        g_hi = rows[:, D_MODEL + 1:D_MODEL + 2]
        y_ref[...] = (g_lo * _swiglu(h, gl_ref[0, 0], ul_ref[0, 0], dl_ref[0, 0])
                      + g_hi * _swiglu(h, gh_ref[0, 0], uh_ref[0, 0], dh_ref[0, 0])
                      + _swiglu(h, sg_ref[0], su_ref[0], sd_ref[0]))

    @pl.when(j >= nv_ref[0])
    def _():
        y_ref[...] = jnp.zeros(y_ref.shape, F32)


def _grouped_experts(e_lo, e_hi, n_valid, rows_sorted, ew, layer):
    n_sorted = rows_sorted.shape[0]
    nb = n_sorted // TM
    routed = lambda w, tab: pl.BlockSpec((1, 1) + w.shape[2:], lambda j, lo, hi, nv: (layer, (lo, hi)[tab][j], 0, 0))
    shared = lambda w: pl.BlockSpec((1,) + w.shape[1:], lambda j, lo, hi, nv: (layer, 0, 0))
    return pl.pallas_call(
        _moe_kernel,
        out_shape=jax.ShapeDtypeStruct((n_sorted, D_MODEL), F32),
        grid_spec=pltpu.PrefetchScalarGridSpec(
            num_scalar_prefetch=3,
            grid=(nb,),
            in_specs=[pl.BlockSpec((TM, ROW_W), lambda j, lo, hi, nv: (j, 0))]
                     + [routed(ew[k], tab) for tab in (0, 1) for k in ("gate", "up", "down")]
                     + [shared(ew[k]) for k in ("s_gate", "s_up", "s_down")],
            out_specs=pl.BlockSpec((TM, D_MODEL), lambda j, lo, hi, nv: (j, 0))),
        compiler_params=_cparams(1),
        name="moe_experts",
    )(e_lo, e_hi, n_valid, rows_sorted, *(ew[k] for k in ("gate", "up", "down")) , *(ew[k] for k in ("gate", "up", "down")),
      *(ew[k] for k in ("s_gate", "s_up", "s_down")))


def _gather_kernel(dest_ref, y_ref, x_ref, mod_ref, gf_ref, o_ref, fbuf, sem, *, n_steps, final_norm):
    i = pl.program_id(0)
    slot = i % 2

    def issue(step, s):
        for r in range(TQ):
            d = dest_ref[step * TQ + r]
            pltpu.make_async_copy(y_ref.at[pl.ds(d, 1)], fbuf.at[s, pl.ds(r, 1)], sem.at[s]).start()

    @pl.when(i == 0)
    def _():
        issue(0, 0)

    for s in range(2):
        @pl.when((i + 1 < n_steps) & (slot == s))
        def _(s=s):
            issue(i + 1, 1 - s)

    pltpu.make_async_copy(y_ref.at[pl.ds(0, TQ)], fbuf.at[slot], sem.at[slot]).wait()
    x = x_ref[0] + mod_ref[0, 0][5:6] * fbuf[slot]
    if final_norm:
        x = _rms(x) * gf_ref[...]
    o_ref[0] = x


def _gather_residual(dest, y_sorted, x_mid, modtab, g_final, n_ctx_blk, with_ctx, final_norm):
    bsz, t_rows, d = x_mid.shape
    nblk = t_rows // TQ
    off = 0 if with_ctx else n_ctx_blk
    n_steps = bsz * nblk
    return pl.pallas_call(
        functools.partial(_gather_kernel, n_steps=n_steps, final_norm=final_norm),
        out_shape=jax.ShapeDtypeStruct((bsz, t_rows, d), F32),
        grid_spec=pltpu.PrefetchScalarGridSpec(
            num_scalar_prefetch=1,
            grid=(n_steps,),
            in_specs=[pl.BlockSpec(memory_space=pl.ANY),
                      pl.BlockSpec((1, TQ, d), lambda i, dst: (i // nblk, i % nblk, 0)),
                      pl.BlockSpec((1, 1, 8, d),
                                   lambda i, dst: (i // nblk, jnp.where(i % nblk + off < n_ctx_blk, 1, 0), 0, 0)),
                      pl.BlockSpec(g_final.shape, lambda i, dst: (0, 0))],
            out_specs=pl.BlockSpec((1, TQ, d), lambda i, dst: (i // nblk, i % nblk, 0)),
            scratch_shapes=[pltpu.VMEM((2, TQ, d), F32), pltpu.SemaphoreType.DMA((2,))]),
        compiler_params=_cparams(1),
        name="moe_gather",
    )(dest, y_sorted, x_mid, modtab, g_final)


def _bucket_layout(meta, counts, n_tok):
    bucket = meta[:, 0, :].reshape(-1)
    rank = meta[:, 1, :].reshape(-1)
    cnt = counts[:N_BUCKETS, 0].astype(jnp.int32)
    padded = (cnt + TM - 1) // TM * TM
    pad_end = jnp.cumsum(padded)
    pad_start = pad_end - padded
    dest = pad_start[bucket] + rank
    nb = n_tok // TM + N_BUCKETS
    n_valid = pad_end[-1] // TM
    blk = jnp.arange(nb, dtype=jnp.int32)
    blk_row = jnp.minimum(blk, n_valid - 1) * TM
    blk_bucket = jnp.minimum(jnp.sum(pad_end[None, :] <= blk_row[:, None], axis=1), N_BUCKETS - 1).astype(jnp.int32)
    grp, pair = blk_bucket // N_PAIRS, blk_bucket % N_PAIRS
    e_lo = grp * EXPERTS_PER_GROUP + jnp.asarray(_PAIR_LO, jnp.int32)[pair]
    e_hi = grp * EXPERTS_PER_GROUP + jnp.asarray(_PAIR_HI, jnp.int32)[pair]
    fill_blk = jnp.where(padded > 0, pad_end // TM - 1, -1).astype(jnp.int32)
    return dest.astype(jnp.int32), fill_blk, e_lo, e_hi, n_valid.reshape(1).astype(jnp.int32), nb * TM


def _rope_tables(seq, n_ctx):
    t = jnp.arange(seq)
    rows, cols = t // GRID_W, t % GRID_W

    def table(width, nf):
        lane = np.arange(width)
        half, j = lane // (2 * nf), lane % (2 * nf)
        inv = ROPE_THETA ** (-jnp.arange(nf, dtype=F32) / nf)
        pos = jnp.where(jnp.asarray(half == 0)[None, :], rows[:, None], cols[:, None]).astype(F32)
        ang = pos * inv[j % nf][None, :]
        sign = jnp.asarray(np.where(j < nf, -1.0, 1.0), F32)[None, :]
        return jnp.cos(ang), jnp.sin(ang) * sign

    cos64, sin64 = table(HEAD_DIM, 16)
    cos64, sin64 = jnp.tile(cos64, (1, 2)), jnp.tile(sin64, (1, 2))
    cosr, sinr = table(MLA_ROPE, 8)
    cosm = jnp.concatenate([jnp.ones((seq, MLA_NOPE), F32), cosr, jnp.ones((seq, 32), F32)], axis=1)
    sinm = jnp.concatenate([jnp.zeros((seq, MLA_NOPE), F32), sinr, jnp.zeros((seq, 32), F32)], axis=1)
    ctx1, ctx0 = jnp.ones((n_ctx, LANES), F32), jnp.zeros((n_ctx, LANES), F32)
    token_major = [jnp.concatenate([c, a], axis=0) for c, a in ((ctx1, cos64), (ctx0, sin64), (ctx1, cosm), (ctx0, sinm))]
    dim_major = [jnp.concatenate([jnp.full((a.shape[1], n_ctx), fill, F32), a.T], axis=1)
                 for a, fill in ((cos64[:, :HEAD_DIM], 1.0), (sin64[:, :HEAD_DIM], 0.0), (cosr, 1.0), (sinr, 0.0))]
    return tuple(token_major + dim_major)


def _layer_weights(l, w_in, w_out, norm_ffn_g, glb_q_gain, glb_k_gain, mla_q_gain, mla_w_uq, mla_kv_gain,
                   mla_w_ukv):
    d = w_in.shape[1]
    wi = w_in[l]
    wa = jnp.concatenate([wi[:, _SQ:_SK], wi[:, _GQ:_GK], wi[:, _MQ:_MKV], wi[:, _SV:_GQ], wi[:, _GV:_MQ],
                          wi[:, _MKV:_KR]], axis=1).T
    kr_tile = jnp.concatenate([jnp.zeros((d, MLA_NOPE), F32), wi[:, _KR:], jnp.zeros((d, 32), F32)], axis=1)
    wb = jnp.concatenate([wi[:, _SK:_SV], wi[:, _GK:_GV], wi[:, _MKV:_KR], kr_tile], axis=1)
    uq = mla_w_uq[l].reshape(MLA_Q_RANK, MLA_HEADS, MLA_NOPE + MLA_ROPE)
    wuq = jnp.concatenate([uq, jnp.zeros((MLA_Q_RANK, MLA_HEADS, 32), F32)], axis=2).reshape(MLA_Q_RANK, 512)
    ukv = mla_w_ukv[l].reshape(MLA_KV_RANK, MLA_HEADS, MLA_NOPE + MLA_V)
    wk = jnp.concatenate([ukv[:, :, :MLA_NOPE], jnp.zeros((MLA_KV_RANK, MLA_HEADS, 64), F32)], axis=2)
    bcast = lambda g: jnp.broadcast_to(g[:, None], (g.shape[0], TQ))
    head = np.arange(LANES) // HEAD_DIM
    return {
        "wa": wa.astype(BF16),
        "wb": wb.astype(BF16),
        "wuq_t": wuq.T.astype(BF16),
        "wukv_k": wk.reshape(MLA_KV_RANK, 512).astype(BF16),
        "wukv_vt": ukv[:, :, MLA_NOPE:].reshape(MLA_KV_RANK, 256).T.astype(BF16),
        "gqg_t": bcast(glb_q_gain[l] * (HEAD_DIM ** -0.5 * LOG2E)),
        "gkg": jnp.tile(glb_k_gain[l], GQA_KV).reshape(1, LANES),
        "mqg_t": bcast(mla_q_gain[l]),
        "mkvg": mla_kv_gain[l].reshape(1, -1),
        "mkvg_t": bcast(mla_kv_gain[l]),
        "bd": jnp.asarray(head[:, None] == head[None, :], BF16),
        "wout": w_out[l].astype(BF16),
        "g_ffn": norm_ffn_g[l].reshape(1, -1),
    }


def kernel(x, c, ctx, c_ctx, w_mod, b_mod, norm_mix_g, norm_ffn_g, w_in, w_out, swa_sink, glb_q_gain, glb_k_gain,
           mla_q_gain, mla_w_uq, mla_kv_gain, mla_w_ukv, router_w, router_bias, exp_w_gate, exp_w_up, exp_w_down,
           shr_w_gate, shr_w_up, shr_w_down, final_norm_g):
    bsz, seq, d = x.shape
    n_ctx = ctx.shape[1]
    n_layers = w_mod.shape[0]
    assert d == D_MODEL and n_ctx % TQ == 0 and seq % TQ == 0 and seq >= TQ + 2 * WINDOW and seq % GRID_W == 0
    n_ctx_blk = n_ctx // TQ

    rows = -(-(bsz + 1) // 8) * 8
    c_rows = jnp.concatenate([c, c_ctx[None, :], jnp.zeros((rows - bsz - 1, d), F32)], axis=0)
    mods = _modulation(c_rows, w_mod, b_mod)
    mod_x = mods[:, :bsz].reshape(n_layers, bsz, 6, d)
    mod_c = jnp.broadcast_to(mods[:, bsz].reshape(n_layers, 1, 6, d), (n_layers, bsz, 6, d))
    modtabs = jnp.pad(jnp.stack([mod_x, mod_c], axis=2), ((0, 0), (0, 0), (0, 0), (0, 2), (0, 0)))

    tabs = _rope_tables(seq, n_ctx)
    rw = jnp.pad(router_w, ((0, 0), (0, LANES - N_EXPERTS)))
    rw_hi = rw.astype(BF16)
    shared = {
        "rw_hi": rw_hi,
        "rw_lo": (rw - rw_hi.astype(F32)).astype(BF16),
        "rb": jnp.pad(router_bias, (0, 32 - N_EXPERTS)).reshape(32, 1),
        "tri": jnp.asarray(np.arange(TQ)[:, None] <= np.arange(TQ)[None, :], BF16),
    }
    g_final = final_norm_g.reshape(1, d)
    ew = {"gate": exp_w_gate.astype(BF16), "up": exp_w_up.astype(BF16), "down": exp_w_down.astype(BF16),
          "s_gate": shr_w_gate.astype(BF16), "s_up": shr_w_up.astype(BF16), "s_down": shr_w_down.astype(BF16)}

    t_all = n_ctx + seq
    stream = (x, 0, ctx)
    for l in range(n_layers):
        last = l == n_layers - 1
        with_ctx = not last
        lw = _layer_weights(l, w_in, w_out, norm_ffn_g, glb_q_gain, glb_k_gain, mla_q_gain, mla_w_uq,
                            mla_kv_gain, mla_w_ukv)
        modtab = modtabs[l]
        qs, ks, vs, qg, kg, vg, qm, km, vm = _input_projection(
            stream, t_all, modtab, norm_mix_g[l].reshape(1, d), lw, tabs, n_ctx_blk)
        sink = jnp.pad(swa_sink[l] * LOG2E, (0, 8 - GQA_HEADS))
        y_mix = _attention({"swa": (qs, ks, vs), "glb": (qg, kg, vg), "mla": (qm, km, vm)}, sink, n_ctx, with_ctx)
        x_mid, rows_tok, meta, counts = _output_projection(y_mix, stream, t_all, modtab, lw, shared, n_ctx_blk,
                                                           with_ctx)
        n_tok = rows_tok.shape[0]
        dest, fill_blk, e_lo, e_hi, n_valid, n_sorted = _bucket_layout(meta, counts, n_tok)
        rows_sorted = _scatter_rows(dest, fill_blk, n_valid, rows_tok, n_sorted)
        y_sorted = _grouped_experts(e_lo, e_hi, n_valid, rows_sorted, ew, l)
        xs = _gather_residual(dest, y_sorted, x_mid, modtab, g_final, n_ctx_blk, with_ctx, final_norm=last)
        stream = (xs, n_ctx_blk, xs)
    return xs
```

```python
import functools

import jax
import jax.numpy as jnp
import numpy as np
from jax import lax
from jax.experimental import pallas as pl
from jax.experimental.pallas import tpu as pltpu

F32 = jnp.float32
BF16 = jnp.bfloat16

D_MODEL = 1024
GRID_W = 64
HEAD_DIM = 64
GQA_HEADS = 6
GQA_KV = 2
GQA_G = GQA_HEADS // GQA_KV
WINDOW = 128
MLA_HEADS = 4
MLA_NOPE = 64
MLA_ROPE = 32
MLA_V = 64
MLA_Q_RANK = 256
MLA_KV_RANK = 128
ROPE_THETA = 10000.0
N_EXPERTS = 16
N_GROUPS = 4
EXPERTS_PER_GROUP = 4
N_PAIRS = 6
N_BUCKETS = N_GROUPS * N_PAIRS
D_EXPERT = 512
EPS = 1e-6
NEG_INF = -1e30

LANES = 128
TQ = 256
TK = 256
TM = 256
SWA_SLAB = 128
V_ROWS = 80
LOG2E = 1.4426950408889634
ROW_W = D_MODEL + LANES
VMEM_LIMIT = 56 * 1024 * 1024

_SQ, _SK, _SV, _GQ, _GK, _GV, _MQ, _MKV, _KR = 0, 384, 512, 640, 1024, 1152, 1280, 1536, 1664
_A_SQ, _A_GQ, _A_MQ, _A_SV, _A_GV, _A_MKV, _WA_ROWS = 0, 384, 768, 1024, 1152, 1280, 1408

_PAIR_LO = (0, 0, 0, 1, 1, 2)
_PAIR_HI = (1, 2, 3, 2, 3, 3)


def _cparams(n_axes):
    return pltpu.CompilerParams(dimension_semantics=("arbitrary",) * n_axes,
                                vmem_limit_bytes=VMEM_LIMIT)


def _mod_kernel(c_ref, w_ref, b_ref, o_ref):
    c = c_ref[...]
    a = (c * jax.nn.sigmoid(c)).astype(BF16)
    o_ref[0] = jnp.dot(a, w_ref[0].astype(BF16), preferred_element_type=F32) + b_ref[0]


def _modulation(c_rows, w_mod, b_mod):
    n_layers, d, width = w_mod.shape
    rows = c_rows.shape[0]
    nb = 1536
    return pl.pallas_call(
        _mod_kernel,
        out_shape=jax.ShapeDtypeStruct((n_layers, rows, width), F32),
        grid=(n_layers, width // nb),
        in_specs=[pl.BlockSpec((rows, d), lambda l, j: (0, 0)),
                  pl.BlockSpec((1, d, nb), lambda l, j: (l, 0, j)),
                  pl.BlockSpec((1, 1, nb), lambda l, j: (l, 0, j))],
        out_specs=pl.BlockSpec((1, rows, nb), lambda l, j: (l, 0, j)),
        compiler_params=_cparams(2),
        name="adaln_mod",
    )(c_rows, w_mod, b_mod.reshape(n_layers, 1, width))


def _rope(x, cos, sin_signed, nf):
    lane = lax.broadcasted_iota(jnp.int32, (1, LANES), 1)
    first = (lane % (2 * nf)) < nf
    tiles = []
    for t in range(x.shape[1] // LANES):
        xt = x[:, t * LANES:(t + 1) * LANES]
        partner = jnp.where(first, pltpu.roll(xt, LANES - nf, 1), pltpu.roll(xt, nf, 1))
        tiles.append(xt * cos + partner * sin_signed)
    return tiles[0] if len(tiles) == 1 else jnp.concatenate(tiles, axis=1)


def _rope_t(x, cos_t, sin_t, nf):
    partner = jnp.concatenate([x[nf:2 * nf], x[:nf], x[3 * nf:], x[2 * nf:3 * nf]], axis=0)
    return x * cos_t + partner * sin_t


def _emit_v_t(ref, v_t, slab):
    ones_blk = jnp.where(lax.broadcasted_iota(jnp.int32, (V_ROWS - HEAD_DIM, TQ), 0) == 0, 1.0, 0.0)
    parts = []
    for h in range(v_t.shape[0] // HEAD_DIM):
        parts += [v_t[h * HEAD_DIM:(h + 1) * HEAD_DIM], ones_blk]
    ext = jnp.concatenate(parts, axis=0).astype(ref.dtype)
    for s in range(TQ // slab):
        ref[0, s] = ext[:, s * slab:(s + 1) * slab]


def _rms(x, eps=EPS):
    return x * lax.rsqrt(jnp.mean(x * x, axis=-1, keepdims=True) + eps)


def _stream_specs(stream, n_ctx_blk, blk_off):
    lat, lat_off, ctx = stream
    d = lat.shape[2]
    return [pl.BlockSpec((1, TQ, d), lambda b, i: (b, jnp.maximum(i + blk_off - n_ctx_blk, 0) + lat_off, 0)),
            pl.BlockSpec((1, TQ, d), lambda b, i: (b, jnp.minimum(i + blk_off, n_ctx_blk - 1), 0))]


def _rms_t(x, eps=EPS):
    return x * lax.rsqrt(jnp.mean(x * x, axis=0, keepdims=True) + eps)


def _in_kernel(x_ref, c_ref, mod_ref, g_ref, wa_ref, wb_ref, cos64_ref, sin64_ref, cosm_ref, sinm_ref,
               cos64t_ref, sin64t_ref, cosmt_ref, sinmt_ref, gqg_ref, gkg_ref, mqg_ref, mkvg_ref, mkvgt_ref,
               wuqt_ref, wukvk_ref, wukvvt_ref, bd_ref,
               qs_ref, ks_ref, vs_ref, qg_ref, kg_ref, vg_ref, qm_ref, km_ref, vm_ref, *, mla_scale, n_ctx_blk):
    x = jnp.where(pl.program_id(1) < n_ctx_blk, c_ref[0], x_ref[0])
    mod = mod_ref[0, 0]
    h = _rms(x) * g_ref[...]
    h = h * (1.0 + mod[1:2]) + mod[0:1]
    hb = h.astype(BF16)
    za = lax.dot_general(wa_ref[...], hb, (((1,), (1,)), ((), ())), preferred_element_type=F32)
    zb = jnp.dot(hb, wb_ref[...], preferred_element_type=F32)
    cos_t, sin_t = cos64t_ref[...], sin64t_ref[...]
    zeros = jnp.zeros((HEAD_DIM, TQ), F32)

    def emit_gqa_q(ref, q_t, prep):
        for hd in range(GQA_HEADS):
            q = prep(q_t[hd * HEAD_DIM:(hd + 1) * HEAD_DIM])
            q = _rope_t(q, cos_t, sin_t, 16)
            tile = [q, zeros] if hd // GQA_G == 0 else [zeros, q]
            ref[0, hd] = jnp.concatenate(tile, axis=0).astype(ref.dtype)

    emit_gqa_q(qs_ref, za[_A_SQ:_A_GQ], lambda q: q * (HEAD_DIM ** -0.5 * LOG2E))
    ks_ref[0] = _rope(zb[:, 0:128], cos64_ref[...], sin64_ref[...], 16).astype(BF16)
    _emit_v_t(vs_ref, za[_A_SV:_A_GV], SWA_SLAB)

    emit_gqa_q(qg_ref, za[_A_GQ:_A_MQ], lambda q: _rms_t(q) * gqg_ref[...])
    gk = zb[:, 128:256]
    sq = gk * gk
    hi = sq.astype(BF16)
    lo = (sq - hi.astype(F32)).astype(BF16)
    ssum = (jnp.dot(hi, bd_ref[...], preferred_element_type=F32)
            + jnp.dot(lo, bd_ref[...], preferred_element_type=F32))
    gk = gk * lax.rsqrt(ssum * (1.0 / HEAD_DIM) + EPS) * gkg_ref[...]
    kg_ref[0] = _rope(gk, cos64_ref[...], sin64_ref[...], 16).astype(BF16)
    _emit_v_t(vg_ref, za[_A_GV:_A_MKV], TK)

    qn = _rms_t(za[_A_MQ:_A_SV]) * mqg_ref[...]
    mq = jnp.dot(wuqt_ref[...], qn.astype(BF16), preferred_element_type=F32)
    for hd in range(MLA_HEADS):
        t = mq[hd * LANES:(hd + 1) * LANES]
        r = _rope_t(t[MLA_NOPE:MLA_NOPE + MLA_ROPE], cosmt_ref[...], sinmt_ref[...], 8)
        t = jnp.concatenate([t[:MLA_NOPE], r, t[MLA_NOPE + MLA_ROPE:]], axis=0) * (mla_scale * LOG2E)
        qm_ref[0, hd] = t.astype(qm_ref.dtype)
    kvn = _rms(zb[:, 256:384]) * mkvg_ref[...]
    k_nope = jnp.dot(kvn.astype(BF16), wukvk_ref[...], preferred_element_type=F32)
    kr = _rope(zb[:, 384:512], cosm_ref[...], sinm_ref[...], 8)
    km_ref[0] = (k_nope + jnp.concatenate([kr] * MLA_HEADS, axis=1)).astype(BF16)
    kvn_t = _rms_t(za[_A_MKV:_WA_ROWS]) * mkvgt_ref[...]
    _emit_v_t(vm_ref, jnp.dot(wukvvt_ref[...], kvn_t.astype(BF16), preferred_element_type=F32), TK)


def _input_projection(stream, t_all, modtab, g_mix, lw, tabs, n_ctx_blk):
    bsz, _, d = stream[0].shape
    nblk = t_all // TQ
    tok = lambda w: pl.BlockSpec((1, TQ, w), lambda b, i: (b, i, 0))
    q_t = lambda nh: pl.BlockSpec((1, nh, LANES, TQ), lambda b, i: (b, 0, 0, i))
    v_t = lambda nh, slab: pl.BlockSpec((1, TQ // slab, nh * V_ROWS, slab), lambda b, i: (b, i, 0, 0))
    full = lambda a: pl.BlockSpec(a.shape, lambda b, i: (0,) * a.ndim)
    tab = pl.BlockSpec((TQ, LANES), lambda b, i: (i, 0))
    q_shape = lambda nh: jax.ShapeDtypeStruct((bsz, nh, LANES, t_all), BF16)
    k_shape = lambda w: jax.ShapeDtypeStruct((bsz, t_all, w), BF16)
    v_shape = lambda nh, slab: jax.ShapeDtypeStruct((bsz, t_all // slab, nh * V_ROWS, slab), BF16)
    tab_t = lambda a: pl.BlockSpec((a.shape[0], TQ), lambda b, i: (0, i))
    consts = (lw["gqg_t"], lw["gkg"], lw["mqg_t"], lw["mkvg"], lw["mkvg_t"], lw["wuq_t"], lw["wukv_k"],
              lw["wukv_vt"], lw["bd"])
    return pl.pallas_call(
        functools.partial(_in_kernel, mla_scale=float((MLA_NOPE + MLA_ROPE) ** -0.5), n_ctx_blk=n_ctx_blk),
        out_shape=(q_shape(GQA_HEADS), k_shape(128), v_shape(GQA_KV, SWA_SLAB),
                   q_shape(GQA_HEADS), k_shape(128), v_shape(GQA_KV, TK),
                   q_shape(MLA_HEADS), k_shape(512), v_shape(MLA_HEADS, TK)),
        grid=(bsz, nblk),
        in_specs=_stream_specs(stream, n_ctx_blk, 0)
                 + [pl.BlockSpec((1, 1, 8, d), lambda b, i: (b, jnp.where(i < n_ctx_blk, 1, 0), 0, 0)),
                    full(g_mix), full(lw["wa"]), full(lw["wb"]), tab, tab, tab, tab]
                 + [tab_t(a) for a in tabs[4:]] + [full(a) for a in consts],
        out_specs=(q_t(GQA_HEADS), tok(128), v_t(GQA_KV, SWA_SLAB),
                   q_t(GQA_HEADS), tok(128), v_t(GQA_KV, TK),
                   q_t(MLA_HEADS), tok(512), v_t(MLA_HEADS, TK)),
        compiler_params=_cparams(2),
        name="in_proj",
    )(stream[0], stream[2], modtab, g_mix, lw["wa"], lw["wb"], *tabs, *consts)


_N_IN = {"swa": 5, "glb": 3, "mla": 3}
_N_SCRATCH = {"swa": 6, "glb": 12, "mla": 12}
_OUT_COL = {"swa": 0, "glb": GQA_HEADS * HEAD_DIM, "mla": 2 * GQA_HEADS * HEAD_DIM}


def _attn_kernel(*refs, n_ctx, n_ctx_blk, q_blk_off, seq):
    kinds = ("swa", "glb", "mla")
    n_in = sum(_N_IN[k] for k in kinds)
    o_ref = refs[n_in]
    i0, s0 = 0, n_in + 1
    for kind in kinds:
        _attn_group(refs[i0:i0 + _N_IN[kind]], o_ref, refs[s0:s0 + _N_SCRATCH[kind]], kind=kind, n_ctx=n_ctx,
                    n_ctx_blk=n_ctx_blk, q_blk_off=q_blk_off, seq=seq)
        i0 += _N_IN[kind]
        s0 += _N_SCRATCH[kind]


def _attn_group(ins, o_ref, scratch, *, kind, n_ctx, n_ctx_blk, q_blk_off, seq):
    if kind == "swa":
        sink_ref, bias_ref, q_ref, k_ref, v_ref = ins
        acc_ref, s_refs, p_refs = scratch[0], scratch[1:4], scratch[4:6]
    else:
        q_ref, k_ref, v_ref = ins
        acc_ref, m_ref = scratch[:2]
        s_refs, cm_refs, p_refs, al_refs = scratch[2:5], scratch[5:8], scratch[8:10], scratch[10:12]
    blk = pl.program_id(1) + q_blk_off
    is_lat = blk >= n_ctx_blk
    gqa = kind in ("swa", "glb")
    n_heads = GQA_HEADS if gqa else MLA_HEADS
    n_units = GQA_KV if gqa else MLA_HEADS
    unit_w = n_heads // n_units * TQ

    if kind == "swa":
        nk = TQ + 2 * WINDOW
        n_slab = n_ctx // SWA_SLAB
        row_l = lax.broadcasted_iota(jnp.int32, (V_ROWS, TQ), 0) == HEAD_DIM

        def scores(h, parts):
            cm = None
            for k_rows, _, bias, r0, nr in parts:
                s = jnp.dot(k_rows(), q_ref[0, h], preferred_element_type=F32)
                if bias is not None:
                    s = s + bias()
                s_refs[h % 3][r0:r0 + nr, :] = s
                c = jnp.max(s, axis=0, keepdims=True)
                cm = c if cm is None else jnp.maximum(cm, c)
            return cm

        def softmax(h, cm, nrows):
            m = jnp.maximum(cm, sink_ref[h])
            p_refs[h % 2][0:nrows, :] = jnp.exp2((s_refs[h % 3][0:nrows, :] - m).astype(BF16))
            return jnp.exp2(sink_ref[h] - m)

        def values(h, e_sink, parts):
            u, j = divmod(h, GQA_G)
            acc = None
            for _, v_t, _, r0, nr in parts:
                a = jnp.dot(v_t(u), p_refs[h % 2][r0:r0 + nr, :], preferred_element_type=F32)
                acc = a if acc is None else acc + a
            acc_ref[u, :, j * TQ:(j + 1) * TQ] = acc + jnp.where(row_l, e_sink, 0.0)

        def run(parts):
            nrows = sum(p[4] for p in parts)
            cms = {h: scores(h, parts) for h in range(min(2, n_heads))}
            e_sinks = {}
            for h in range(n_heads + 1):
                if h + 2 < n_heads:
                    cms[h + 2] = scores(h + 2, parts)
                if h < n_heads:
                    e_sinks[h] = softmax(h, cms[h], nrows)
                if h >= 1:
                    values(h - 1, e_sinks[h - 1], parts)

        ctx_part = (lambda: k_ref[0, 0:n_ctx, :],
                    lambda u: jnp.concatenate([v_ref[0, s, u * V_ROWS:(u + 1) * V_ROWS, :] for s in range(n_slab)],
                                              axis=1),
                    None, 0, n_ctx)

        def latent():
            q0 = (blk - n_ctx_blk) * TQ
            k0 = pl.multiple_of(jnp.clip(q0 - WINDOW, 0, seq - nk), WINDOW)
            geom = (q0 - k0) // WINDOW
            s0 = (n_ctx + k0) // SWA_SLAB
            win_part = (lambda: k_ref[0, pl.ds(pl.multiple_of(n_ctx + k0, WINDOW), nk), :],
                        lambda u: jnp.concatenate([v_ref[0, s0 + s, u * V_ROWS:(u + 1) * V_ROWS, :]
                                                   for s in range(nk // SWA_SLAB)], axis=1),
                        lambda: bias_ref[geom], n_ctx, nk)
            run([ctx_part, win_part])

        if q_blk_off < n_ctx_blk:
            pl.when(jnp.logical_not(is_lat))(lambda: run([ctx_part]))
            pl.when(is_lat)(latent)
        else:
            latent()
    else:
        m_ref[...] = jnp.full(m_ref.shape, NEG_INF, F32)
        acc_ref[...] = jnp.zeros(acc_ref.shape, F32)

        def qk(c, par, h):
            lanes = slice(h * TQ, (h + 1) * TQ)
            k_cols = slice(0, LANES) if gqa else slice(h * LANES, (h + 1) * LANES)
            k_rows = k_ref[0, pl.ds(pl.multiple_of(c * TK, TK), TK), k_cols]
            s = jnp.dot(k_rows, q_ref[0, h], preferred_element_type=F32)
            s_refs[par][:, lanes] = s
            cm_refs[par][:, lanes] = jnp.max(s, axis=0, keepdims=True)

        def softmax(sb, par, h):
            lanes = slice(h * TQ, (h + 1) * TQ)
            m_prev = m_ref[:, lanes]
            m_new = jnp.maximum(m_prev, cm_refs[sb][:, lanes])
            m_ref[:, lanes] = m_new
            al_refs[par][:, lanes] = jnp.exp2(m_prev - m_new)
            p_refs[par][:, lanes] = jnp.exp2((s_refs[sb][:, lanes] - m_new).astype(BF16))

        def pv(c, par, h):
            lanes = slice(h * TQ, (h + 1) * TQ)
            u, j = divmod(h, n_heads // n_units)
            ul = slice(j * TQ, (j + 1) * TQ)
            acc_ref[u, :, ul] = (acc_ref[u, :, ul] * al_refs[par][:, lanes]
                                 + jnp.dot(v_ref[0, c, u * V_ROWS:(u + 1) * V_ROWS, :], p_refs[par][:, lanes],
                                           preferred_element_type=F32))

        def step(i, r, n):
            for h in range(n_heads):
                if not isinstance(i, int) or i + 2 < n:
                    qk(i + 2, (r + 2) % 3, h)
                if not isinstance(i, int) or i < n:
                    softmax(r % 3, r % 2, h)
                if not isinstance(i, int) or i >= 1:
                    pv(i - 1, (r + 1) % 2, h)

        def run(n):
            for c in range(min(2, n)):
                for h in range(n_heads):
                    qk(c, c, h)
            step(0, 0, n)
            n_groups = max(0, (n - 3) // 6)

            def body(j, carry):
                for r in range(6):
                    step(1 + 6 * j + r, (1 + r) % 6, n)
                return carry
            if n_groups:
                lax.fori_loop(0, n_groups, body, 0)
            for i in range(1 + 6 * n_groups, n + 1):
                step(i, i % 6, n)

        if q_blk_off < n_ctx_blk:
            pl.when(jnp.logical_not(is_lat))(lambda: run(n_ctx // TK))
            pl.when(is_lat)(lambda: run((n_ctx + seq) // TK))
        else:
            run((n_ctx + seq) // TK)

    heads = []
    for u in range(n_units):
        a = acc_ref[u]
        o = a[:HEAD_DIM] / a[HEAD_DIM:HEAD_DIM + 1]
        heads += [o[:, j * TQ:(j + 1) * TQ] for j in range(unit_w // TQ)]
    tiles = [jnp.concatenate(heads[2 * t:2 * t + 2], axis=0).T for t in range(n_heads // 2)]
    c0 = _OUT_COL[kind]
    o_ref[0, :, c0:c0 + n_heads * HEAD_DIM] = jnp.concatenate(tiles, axis=1).astype(o_ref.dtype)


def _attention(groups, sink, n_ctx, with_ctx_queries):
    bsz, _, _, t_all = groups["swa"][0].shape
    seq = t_all - n_ctx
    n_ctx_blk = n_ctx // TQ
    q_blk_off = 0 if with_ctx_queries else n_ctx_blk
    nblk = t_all // TQ - q_blk_off
    in_specs, args, scratch = [], [], []
    for kind in ("swa", "glb", "mla"):
        q_t, k, v_t = groups[kind]
        n_heads = q_t.shape[1]
        n_units = GQA_KV if kind in ("swa", "glb") else MLA_HEADS
        nq = n_heads * TQ
        if kind == "swa":
            nk = TQ + 2 * WINDOW
            off = np.arange(TQ)[None, None, :] + WINDOW * np.arange(3)[:, None, None] - np.arange(nk)[None, :, None]
            bias = jnp.asarray(np.where(np.abs(off) <= WINDOW, 0.0, NEG_INF), F32)
            in_specs += [pl.BlockSpec(memory_space=pltpu.SMEM), pl.BlockSpec(bias.shape, lambda b, i: (0, 0, 0))]
            args += [sink, bias]
        in_specs += [pl.BlockSpec((1, n_heads, LANES, TQ), lambda b, i: (b, 0, 0, i + q_blk_off)),
                     pl.BlockSpec((1,) + k.shape[1:], lambda b, i: (b, 0, 0)),
                     pl.BlockSpec((1,) + v_t.shape[1:], lambda b, i: (b, 0, 0, 0))]
        args += [q_t, k, v_t]
        scratch += [pltpu.VMEM((n_units, V_ROWS, nq // n_units), F32)]
        if kind == "swa":
            rows = n_ctx + nk
            scratch += [pltpu.VMEM((rows, TQ), F32)] * 3 + [pltpu.VMEM((rows, TQ), BF16)] * 2
        else:
            scratch += ([pltpu.VMEM((1, nq), F32)] + [pltpu.VMEM((TK, nq), F32)] * 3 + [pltpu.VMEM((1, nq), F32)] * 3
                        + [pltpu.VMEM((TK, nq), BF16)] * 2 + [pltpu.VMEM((1, nq), F32)] * 2)
    return pl.pallas_call(
        functools.partial(_attn_kernel, n_ctx=n_ctx, n_ctx_blk=n_ctx_blk, q_blk_off=q_blk_off, seq=seq),
        out_shape=jax.ShapeDtypeStruct((bsz, nblk * TQ, D_MODEL), BF16),
        grid=(bsz, nblk),
        in_specs=in_specs,
        out_specs=pl.BlockSpec((1, TQ, D_MODEL), lambda b, i: (b, i, 0)),
        scratch_shapes=scratch,
        compiler_params=_cparams(2),
        name="attention",
    )(*args)


def _row_select(rows, idx):
    out = rows[0]
    for j in range(1, len(rows)):
        out = jnp.where(idx == j, rows[j], out)
    return out


def _route(scores, biased):
    def top2sum(a, b, c, d):
        hi1, lo1, hi2, lo2 = jnp.maximum(a, b), jnp.minimum(a, b), jnp.maximum(c, d), jnp.minimum(c, d)
        return jnp.maximum(hi1, hi2) + jnp.maximum(jnp.minimum(hi1, hi2), jnp.maximum(lo1, lo2))

    gs = [top2sum(*biased[4 * g:4 * g + 4]) for g in range(N_GROUPS)]
    best, gi = gs[0], jnp.zeros(gs[0].shape, jnp.int32)
    for g in range(1, N_GROUPS):
        better = gs[g] > best
        gi = jnp.where(better, g, gi)
        best = jnp.where(better, gs[g], best)
    a = [_row_select([biased[4 * g + j] for g in range(N_GROUPS)], gi) for j in range(EXPERTS_PER_GROUP)]
    s = [_row_select([scores[4 * g + j] for g in range(N_GROUPS)], gi) for j in range(EXPERTS_PER_GROUP)]
    v1, i1 = a[0], jnp.zeros(gi.shape, jnp.int32)
    for j in range(1, EXPERTS_PER_GROUP):
        better = a[j] > v1
        i1 = jnp.where(better, j, i1)
        v1 = jnp.where(better, a[j], v1)
    v2, i2 = jnp.full(v1.shape, -3.0e38, F32), jnp.zeros(gi.shape, jnp.int32)
    for j in range(EXPERTS_PER_GROUP):
        better = (i1 != j) & (a[j] > v2)
        i2 = jnp.where(better, j, i2)
        v2 = jnp.where(better, a[j], v2)
    lo, hi = jnp.minimum(i1, i2), jnp.maximum(i1, i2)
    pair = jnp.where(lo == 0, hi - 1, jnp.where(lo == 1, hi + 1, 5))
    s_lo, s_hi = _row_select(s, lo), _row_select(s, hi)
    den = s_lo + s_hi
    return gi * N_PAIRS + pair, s_lo / den, s_hi / den


def _out_kernel(y_ref, x_ref, c_ref, mod_ref, wout_ref, g_ref, rwh_ref, rwl_ref, rb_ref, tri_ref,
                xo_ref, row_ref, meta_ref, cnt_ref, carry_ref, *, n_ctx_blk, blk_off):
    @pl.when((pl.program_id(0) == 0) & (pl.program_id(1) == 0))
    def _():
        carry_ref[...] = jnp.zeros(carry_ref.shape, F32)

    y = y_ref[0]
    mod = mod_ref[0, 0]
    half = D_MODEL // 2
    a = jnp.concatenate([jnp.dot(y, wout_ref[:, :half], preferred_element_type=F32),
                         jnp.dot(y, wout_ref[:, half:], preferred_element_type=F32)], axis=1)
    x = jnp.where(pl.program_id(1) + blk_off < n_ctx_blk, c_ref[0], x_ref[0]) + mod[2:3] * a
    xo_ref[0] = x
    h = _rms(x) * g_ref[...]
    h = h * (1.0 + mod[4:5]) + mod[3:4]

    hh = h.astype(BF16)
    hl = (h - hh.astype(F32)).astype(BF16)
    logits = (jnp.dot(hh, rwh_ref[...], preferred_element_type=F32)
              + jnp.dot(hl, rwh_ref[...], preferred_element_type=F32)
              + jnp.dot(hh, rwl_ref[...], preferred_element_type=F32)).T[:32]
    sc = jax.nn.sigmoid(logits)
    bs = sc + rb_ref[...]
    scores = [sc[e:e + 1, :] for e in range(N_EXPERTS)]
    biased = [bs[e:e + 1, :] for e in range(N_EXPERTS)]
    bucket, g_lo, g_hi = _route(scores, biased)

    onehot = jnp.where(lax.broadcasted_iota(jnp.int32, (32, TQ), 0) == bucket, 1.0, 0.0)
    prefix = jnp.dot(onehot.astype(BF16), tri_ref[...], preferred_element_type=F32)
    carry = carry_ref[:, 0:1]
    rank = jnp.sum(onehot * (carry + prefix - 1.0), axis=0, keepdims=True)
    carry_new = jnp.broadcast_to(carry + prefix[:, TQ - 1:TQ], carry_ref.shape)
    carry_ref[...] = carry_new
    cnt_ref[...] = carry_new
    meta_ref[0] = jnp.concatenate([bucket, rank.astype(jnp.int32), jnp.zeros((6, TQ), jnp.int32)], axis=0)

    gates = jnp.concatenate([g_lo, g_hi, jnp.zeros((LANES - 2, TQ), F32)], axis=0)
    row_ref[...] = jnp.concatenate([h, gates.T], axis=1)


def _output_projection(y_mix, stream, t_all, modtab, lw, shared, n_ctx_blk, with_ctx):
    bsz, _, d = stream[0].shape
    off = 0 if with_ctx else n_ctx_blk
    nblk = t_all // TQ - off
    tok = lambda w: pl.BlockSpec((1, TQ, w), lambda b, i: (b, i, 0))
    full = lambda a: pl.BlockSpec(a.shape, lambda b, i: (0,) * a.ndim)
    consts = (lw["wout"], lw["g_ffn"], shared["rw_hi"], shared["rw_lo"], shared["rb"], shared["tri"])
    return pl.pallas_call(
        functools.partial(_out_kernel, n_ctx_blk=n_ctx_blk, blk_off=off),
        out_shape=(jax.ShapeDtypeStruct((bsz, nblk * TQ, d), F32),
                   jax.ShapeDtypeStruct((bsz * nblk * TQ, ROW_W), F32),
                   jax.ShapeDtypeStruct((bsz * nblk, 8, TQ), jnp.int32),
                   jax.ShapeDtypeStruct((32, LANES), F32)),
        grid=(bsz, nblk),
        in_specs=[tok(d)] + _stream_specs(stream, n_ctx_blk, off)
                 + [pl.BlockSpec((1, 1, 8, d), lambda b, i: (b, jnp.where(i + off < n_ctx_blk, 1, 0), 0, 0))]
                 + [full(a) for a in consts],
        out_specs=(tok(d),
                   pl.BlockSpec((TQ, ROW_W), lambda b, i: (b * nblk + i, 0)),
                   pl.BlockSpec((1, 8, TQ), lambda b, i: (b * nblk + i, 0, 0)),
                   pl.BlockSpec((32, LANES), lambda b, i: (0, 0))),
        scratch_shapes=[pltpu.VMEM((32, LANES), F32)],
        compiler_params=_cparams(2),
        name="out_proj_router",
    )(y_mix, stream[0], stream[2], modtab, *consts)


def _scatter_kernel(dest_ref, fill_ref, nv_ref, row_ref, out_ref, buf, sem, *, n_steps, n_blocks):
    i = pl.program_id(0)
    slot = i % 2

    def wait_slot(s):
        pltpu.make_async_copy(buf.at[s], out_ref.at[pl.ds(0, TQ)], sem.at[s]).wait()

    @pl.when(i == 0)
    def _():
        buf[1] = jnp.zeros((TQ, ROW_W), F32)

        def zero_block(blk):
            return pltpu.make_async_copy(buf.at[1], out_ref.at[pl.ds(pl.multiple_of(blk * TM, TM), TM)], sem.at[1])

        for wait in (False, True):
            for b in range(N_BUCKETS):
                @pl.when(fill_ref[b] >= 0)
                def _(b=b, wait=wait):
                    zero_block(fill_ref[b]).wait() if wait else zero_block(fill_ref[b]).start()

            def tail(blk, carry, wait=wait):
                zero_block(blk).wait() if wait else zero_block(blk).start()
                return carry
            lax.fori_loop(nv_ref[0], n_blocks, tail, 0)

    for s in range(2):
        @pl.when(slot == s)
        def _(s=s):
            @pl.when(i >= 2)
            def _():
                wait_slot(s)

            buf[s] = row_ref[...]
            for r in range(TQ):
                d = dest_ref[i * TQ + r]
                pltpu.make_async_copy(buf.at[s, pl.ds(r, 1)], out_ref.at[pl.ds(d, 1)], sem.at[s]).start(priority=r % 2)

    @pl.when(i == n_steps - 1)
    def _():
        wait_slot(slot)
        if n_steps >= 2:
            wait_slot(1 - slot)


def _scatter_rows(dest, fill_blk, n_valid, rows, n_sorted):
    n_tok = rows.shape[0]
    n_steps = n_tok // TQ
    assert TQ == TM
    return pl.pallas_call(
        functools.partial(_scatter_kernel, n_steps=n_steps, n_blocks=n_sorted // TM),
        out_shape=jax.ShapeDtypeStruct((n_sorted, ROW_W), F32),
        grid_spec=pltpu.PrefetchScalarGridSpec(
            num_scalar_prefetch=3,
            grid=(n_steps,),
            in_specs=[pl.BlockSpec((TQ, ROW_W), lambda i, d, f, nv: (i, 0))],
            out_specs=pl.BlockSpec(memory_space=pl.ANY),
            scratch_shapes=[pltpu.VMEM((2, TQ, ROW_W), F32), pltpu.SemaphoreType.DMA((2,))]),
        compiler_params=_cparams(1),
        name="moe_scatter",
    )(dest, fill_blk, n_valid, rows)


def _swiglu(h, w_gate, w_up, w_down):
    a = jnp.dot(h, w_gate, preferred_element_type=F32)
    a = a * jax.nn.sigmoid(a) * jnp.dot(h, w_up, preferred_element_type=F32)
    return jnp.dot(a.astype(BF16), w_down, preferred_element_type=F32)


def _moe_kernel(elo_ref, ehi_ref, nv_ref, row_ref, gl_ref, ul_ref, dl_ref, gh_ref, uh_ref, dh_ref,
                sg_ref, su_ref, sd_ref, y_ref):
    del elo_ref, ehi_ref
    j = pl.program_id(0)

    @pl.when(j < nv_ref[0])
    def _():
        rows = row_ref[...]
        h = rows[:, :D_MODEL].astype(BF16)
        g_lo = rows[:, D_MODEL:D_MODEL + 1]
        g_hi = rows[:, D_MODEL + 1:D_MODEL + 2]
        y_ref[...] = (g_lo * _swiglu(h, gl_ref[0, 0], ul_ref[0, 0], dl_ref[0, 0])
                      + g_hi * _swiglu(h, gh_ref[0, 0], uh_ref[0, 0], dh_ref[0, 0])
                      + _swiglu(h, sg_ref[0], su_ref[0], sd_ref[0]))

    @pl.when(j >= nv_ref[0])
    def _():
        y_ref[...] = jnp.zeros(y_ref.shape, F32)


def _grouped_experts(e_lo, e_hi, n_valid, rows_sorted, ew, layer):
    n_sorted = rows_sorted.shape[0]
    nb = n_sorted // TM
    routed = lambda w, tab: pl.BlockSpec((1, 1) + w.shape[2:], lambda j, lo, hi, nv: (layer, (lo, hi)[tab][j], 0, 0))
    shared = lambda w: pl.BlockSpec((1,) + w.shape[1:], lambda j, lo, hi, nv: (layer, 0, 0))
    return pl.pallas_call(
        _moe_kernel,
        out_shape=jax.ShapeDtypeStruct((n_sorted, D_MODEL), F32),
        grid_spec=pltpu.PrefetchScalarGridSpec(
            num_scalar_prefetch=3,
            grid=(nb,),
            in_specs=[pl.BlockSpec((TM, ROW_W), lambda j, lo, hi, nv: (j, 0))]
                     + [routed(ew[k], tab) for tab in (0, 1) for k in ("gate", "up", "down")]
                     + [shared(ew[k]) for k in ("s_gate", "s_up", "s_down")],
            out_specs=pl.BlockSpec((TM, D_MODEL), lambda j, lo, hi, nv: (j, 0))),
        compiler_params=_cparams(1),
        name="moe_experts",
    )(e_lo, e_hi, n_valid, rows_sorted, *(ew[k] for k in ("gate", "up", "down")) , *(ew[k] for k in ("gate", "up", "down")),
      *(ew[k] for k in ("s_gate", "s_up", "s_down")))


def _gather_kernel(dest_ref, y_ref, x_ref, mod_ref, gf_ref, o_ref, fbuf, sem, *, n_steps, final_norm):
    i = pl.program_id(0)
    slot = i % 2

    def issue(step, s):
        for r in range(TQ):
            d = dest_ref[step * TQ + r]
            pltpu.make_async_copy(y_ref.at[pl.ds(d, 1)], fbuf.at[s, pl.ds(r, 1)], sem.at[s]).start(priority=r % 2)

    @pl.when(i == 0)
    def _():
        issue(0, 0)

    for s in range(2):
        @pl.when((i + 1 < n_steps) & (slot == s))
        def _(s=s):
            issue(i + 1, 1 - s)

    pltpu.make_async_copy(y_ref.at[pl.ds(0, TQ)], fbuf.at[slot], sem.at[slot]).wait()
    x = x_ref[0] + mod_ref[0, 0][5:6] * fbuf[slot]
    if final_norm:
        x = _rms(x) * gf_ref[...]
    o_ref[0] = x


def _gather_residual(dest, y_sorted, x_mid, modtab, g_final, n_ctx_blk, with_ctx, final_norm):
    bsz, t_rows, d = x_mid.shape
    nblk = t_rows // TQ
    off = 0 if with_ctx else n_ctx_blk
    n_steps = bsz * nblk
    return pl.pallas_call(
        functools.partial(_gather_kernel, n_steps=n_steps, final_norm=final_norm),
        out_shape=jax.ShapeDtypeStruct((bsz, t_rows, d), F32),
        grid_spec=pltpu.PrefetchScalarGridSpec(
            num_scalar_prefetch=1,
            grid=(n_steps,),
            in_specs=[pl.BlockSpec(memory_space=pl.ANY),
                      pl.BlockSpec((1, TQ, d), lambda i, dst: (i // nblk, i % nblk, 0)),
                      pl.BlockSpec((1, 1, 8, d),
                                   lambda i, dst: (i // nblk, jnp.where(i % nblk + off < n_ctx_blk, 1, 0), 0, 0)),
                      pl.BlockSpec(g_final.shape, lambda i, dst: (0, 0))],
            out_specs=pl.BlockSpec((1, TQ, d), lambda i, dst: (i // nblk, i % nblk, 0)),
            scratch_shapes=[pltpu.VMEM((2, TQ, d), F32), pltpu.SemaphoreType.DMA((2,))]),
        compiler_params=_cparams(1),
        name="moe_gather",
    )(dest, y_sorted, x_mid, modtab, g_final)


def _bucket_layout(meta, counts, n_tok):
    bucket = meta[:, 0, :].reshape(-1)
    rank = meta[:, 1, :].reshape(-1)
    cnt = counts[:N_BUCKETS, 0].astype(jnp.int32)
    padded = (cnt + TM - 1) // TM * TM
    pad_end = jnp.cumsum(padded)
    pad_start = pad_end - padded
    dest = pad_start[bucket] + rank
    nb = n_tok // TM + N_BUCKETS
    n_valid = pad_end[-1] // TM
    blk = jnp.arange(nb, dtype=jnp.int32)
    blk_row = jnp.minimum(blk, n_valid - 1) * TM
    blk_bucket = jnp.minimum(jnp.sum(pad_end[None, :] <= blk_row[:, None], axis=1), N_BUCKETS - 1).astype(jnp.int32)
    grp, pair = blk_bucket // N_PAIRS, blk_bucket % N_PAIRS
    e_lo = grp * EXPERTS_PER_GROUP + jnp.asarray(_PAIR_LO, jnp.int32)[pair]
    e_hi = grp * EXPERTS_PER_GROUP + jnp.asarray(_PAIR_HI, jnp.int32)[pair]
    fill_blk = jnp.where(padded > 0, pad_end // TM - 1, -1).astype(jnp.int32)
    return dest.astype(jnp.int32), fill_blk, e_lo, e_hi, n_valid.reshape(1).astype(jnp.int32), nb * TM


def _rope_tables(seq, n_ctx):
    t = jnp.arange(seq)
    rows, cols = t // GRID_W, t % GRID_W

    def table(width, nf):
        lane = np.arange(width)
        half, j = lane // (2 * nf), lane % (2 * nf)
        inv = ROPE_THETA ** (-jnp.arange(nf, dtype=F32) / nf)
        pos = jnp.where(jnp.asarray(half == 0)[None, :], rows[:, None], cols[:, None]).astype(F32)
        ang = pos * inv[j % nf][None, :]
        sign = jnp.asarray(np.where(j < nf, -1.0, 1.0), F32)[None, :]
        return jnp.cos(ang), jnp.sin(ang) * sign

    cos64, sin64 = table(HEAD_DIM, 16)
    cos64, sin64 = jnp.tile(cos64, (1, 2)), jnp.tile(sin64, (1, 2))
    cosr, sinr = table(MLA_ROPE, 8)
    cosm = jnp.concatenate([jnp.ones((seq, MLA_NOPE), F32), cosr, jnp.ones((seq, 32), F32)], axis=1)
    sinm = jnp.concatenate([jnp.zeros((seq, MLA_NOPE), F32), sinr, jnp.zeros((seq, 32), F32)], axis=1)
    ctx1, ctx0 = jnp.ones((n_ctx, LANES), F32), jnp.zeros((n_ctx, LANES), F32)
    token_major = [jnp.concatenate([c, a], axis=0) for c, a in ((ctx1, cos64), (ctx0, sin64), (ctx1, cosm), (ctx0, sinm))]
    dim_major = [jnp.concatenate([jnp.full((a.shape[1], n_ctx), fill, F32), a.T], axis=1)
                 for a, fill in ((cos64[:, :HEAD_DIM], 1.0), (sin64[:, :HEAD_DIM], 0.0), (cosr, 1.0), (sinr, 0.0))]
    return tuple(token_major + dim_major)


def _layer_weights(l, w_in, w_out, norm_ffn_g, glb_q_gain, glb_k_gain, mla_q_gain, mla_w_uq, mla_kv_gain,
                   mla_w_ukv):
    d = w_in.shape[1]
    wi = w_in[l]
    wa = jnp.concatenate([wi[:, _SQ:_SK], wi[:, _GQ:_GK], wi[:, _MQ:_MKV], wi[:, _SV:_GQ], wi[:, _GV:_MQ],
                          wi[:, _MKV:_KR]], axis=1).T
    kr_tile = jnp.concatenate([jnp.zeros((d, MLA_NOPE), F32), wi[:, _KR:], jnp.zeros((d, 32), F32)], axis=1)
    wb = jnp.concatenate([wi[:, _SK:_SV], wi[:, _GK:_GV], wi[:, _MKV:_KR], kr_tile], axis=1)
    uq = mla_w_uq[l].reshape(MLA_Q_RANK, MLA_HEADS, MLA_NOPE + MLA_ROPE)
    wuq = jnp.concatenate([uq, jnp.zeros((MLA_Q_RANK, MLA_HEADS, 32), F32)], axis=2).reshape(MLA_Q_RANK, 512)
    ukv = mla_w_ukv[l].reshape(MLA_KV_RANK, MLA_HEADS, MLA_NOPE + MLA_V)
    wk = jnp.concatenate([ukv[:, :, :MLA_NOPE], jnp.zeros((MLA_KV_RANK, MLA_HEADS, 64), F32)], axis=2)
    bcast = lambda g: jnp.broadcast_to(g[:, None], (g.shape[0], TQ))
    head = np.arange(LANES) // HEAD_DIM
    return {
        "wa": wa.astype(BF16),
        "wb": wb.astype(BF16),
        "wuq_t": wuq.T.astype(BF16),
        "wukv_k": wk.reshape(MLA_KV_RANK, 512).astype(BF16),
        "wukv_vt": ukv[:, :, MLA_NOPE:].reshape(MLA_KV_RANK, 256).T.astype(BF16),
        "gqg_t": bcast(glb_q_gain[l] * (HEAD_DIM ** -0.5 * LOG2E)),
        "gkg": jnp.tile(glb_k_gain[l], GQA_KV).reshape(1, LANES),
        "mqg_t": bcast(mla_q_gain[l]),
        "mkvg": mla_kv_gain[l].reshape(1, -1),
        "mkvg_t": bcast(mla_kv_gain[l]),
        "bd": jnp.asarray(head[:, None] == head[None, :], BF16),
        "wout": w_out[l].astype(BF16),
        "g_ffn": norm_ffn_g[l].reshape(1, -1),
    }


def kernel(x, c, ctx, c_ctx, w_mod, b_mod, norm_mix_g, norm_ffn_g, w_in, w_out, swa_sink, glb_q_gain, glb_k_gain,
           mla_q_gain, mla_w_uq, mla_kv_gain, mla_w_ukv, router_w, router_bias, exp_w_gate, exp_w_up, exp_w_down,
           shr_w_gate, shr_w_up, shr_w_down, final_norm_g):
    bsz, seq, d = x.shape
    n_ctx = ctx.shape[1]
    n_layers = w_mod.shape[0]
    assert d == D_MODEL and n_ctx % TQ == 0 and seq % TQ == 0 and seq >= TQ + 2 * WINDOW and seq % GRID_W == 0
    n_ctx_blk = n_ctx // TQ

    rows = -(-(bsz + 1) // 8) * 8
    c_rows = jnp.concatenate([c, c_ctx[None, :], jnp.zeros((rows - bsz - 1, d), F32)], axis=0)
    mods = _modulation(c_rows, w_mod, b_mod)
    mod_x = mods[:, :bsz].reshape(n_layers, bsz, 6, d)
    mod_c = jnp.broadcast_to(mods[:, bsz].reshape(n_layers, 1, 6, d), (n_layers, bsz, 6, d))
    modtabs = jnp.pad(jnp.stack([mod_x, mod_c], axis=2), ((0, 0), (0, 0), (0, 0), (0, 2), (0, 0)))

    tabs = _rope_tables(seq, n_ctx)
    rw = jnp.pad(router_w, ((0, 0), (0, LANES - N_EXPERTS)))
    rw_hi = rw.astype(BF16)
    shared = {
        "rw_hi": rw_hi,
        "rw_lo": (rw - rw_hi.astype(F32)).astype(BF16),
        "rb": jnp.pad(router_bias, (0, 32 - N_EXPERTS)).reshape(32, 1),
        "tri": jnp.asarray(np.arange(TQ)[:, None] <= np.arange(TQ)[None, :], BF16),
    }
    g_final = final_norm_g.reshape(1, d)
    ew = {"gate": exp_w_gate.astype(BF16), "up": exp_w_up.astype(BF16), "down": exp_w_down.astype(BF16),
          "s_gate": shr_w_gate.astype(BF16), "s_up": shr_w_up.astype(BF16), "s_down": shr_w_down.astype(BF16)}

    t_all = n_ctx + seq
    stream = (x, 0, ctx)
    for l in range(n_layers):
        last = l == n_layers - 1
        with_ctx = not last
        lw = _layer_weights(l, w_in, w_out, norm_ffn_g, glb_q_gain, glb_k_gain, mla_q_gain, mla_w_uq,
                            mla_kv_gain, mla_w_ukv)
        modtab = modtabs[l]
        qs, ks, vs, qg, kg, vg, qm, km, vm = _input_projection(
            stream, t_all, modtab, norm_mix_g[l].reshape(1, d), lw, tabs, n_ctx_blk)
        sink = jnp.pad(swa_sink[l] * LOG2E, (0, 8 - GQA_HEADS))
        y_mix = _attention({"swa": (qs, ks, vs), "glb": (qg, kg, vg), "mla": (qm, km, vm)}, sink, n_ctx, with_ctx)
        x_mid, rows_tok, meta, counts = _output_projection(y_mix, stream, t_all, modtab, lw, shared, n_ctx_blk,
                                                           with_ctx)
        n_tok = rows_tok.shape[0]
        dest, fill_blk, e_lo, e_hi, n_valid, n_sorted = _bucket_layout(meta, counts, n_tok)
        rows_sorted = _scatter_rows(dest, fill_blk, n_valid, rows_tok, n_sorted)
        y_sorted = _grouped_experts(e_lo, e_hi, n_valid, rows_sorted, ew, l)
        xs = _gather_residual(dest, y_sorted, x_mid, modtab, g_final, n_ctx_blk, with_ctx, final_norm=last)
        stream = (xs, n_ctx_blk, xs)
    return xs
```

```python
import functools

import jax
import jax.numpy as jnp
import numpy as np
from jax import lax
from jax.experimental import pallas as pl
from jax.experimental.pallas import tpu as pltpu

F32 = jnp.float32
BF16 = jnp.bfloat16

D_MODEL = 1024
GRID_W = 64
HEAD_DIM = 64
GQA_HEADS = 6
GQA_KV = 2
GQA_G = GQA_HEADS // GQA_KV
WINDOW = 128
MLA_HEADS = 4
MLA_NOPE = 64
MLA_ROPE = 32
MLA_V = 64
MLA_Q_RANK = 256
MLA_KV_RANK = 128
ROPE_THETA = 10000.0
N_EXPERTS = 16
N_GROUPS = 4
EXPERTS_PER_GROUP = 4
N_PAIRS = 6
N_BUCKETS = N_GROUPS * N_PAIRS
D_EXPERT = 512
EPS = 1e-6
NEG_INF = -1e30

LANES = 128
TQ = 256
TK = 256
TM = 256
SWA_SLAB = 128
V_ROWS = 80
LOG2E = 1.4426950408889634
ROW_W = D_MODEL + LANES
VMEM_LIMIT = 56 * 1024 * 1024

_SQ, _SK, _SV, _GQ, _GK, _GV, _MQ, _MKV, _KR = 0, 384, 512, 640, 1024, 1152, 1280, 1536, 1664
_A_SQ, _A_GQ, _A_MQ, _A_SV, _A_GV, _A_MKV, _WA_ROWS = 0, 384, 768, 1024, 1152, 1280, 1408

_PAIR_LO = (0, 0, 0, 1, 1, 2)
_PAIR_HI = (1, 2, 3, 2, 3, 3)


def _cparams(n_axes):
    return pltpu.CompilerParams(dimension_semantics=("arbitrary",) * n_axes,
                                vmem_limit_bytes=VMEM_LIMIT)


def _mod_kernel(c_ref, w_ref, b_ref, o_ref):
    c = c_ref[...]
    a = (c * jax.nn.sigmoid(c)).astype(BF16)
    o_ref[0] = jnp.dot(a, w_ref[0].astype(BF16), preferred_element_type=F32) + b_ref[0]


def _modulation(c_rows, w_mod, b_mod):
    n_layers, d, width = w_mod.shape
    rows = c_rows.shape[0]
    nb = 1536
    return pl.pallas_call(
        _mod_kernel,
        out_shape=jax.ShapeDtypeStruct((n_layers, rows, width), F32),
        grid=(n_layers, width // nb),
        in_specs=[pl.BlockSpec((rows, d), lambda l, j: (0, 0)),
                  pl.BlockSpec((1, d, nb), lambda l, j: (l, 0, j)),
                  pl.BlockSpec((1, 1, nb), lambda l, j: (l, 0, j))],
        out_specs=pl.BlockSpec((1, rows, nb), lambda l, j: (l, 0, j)),
        compiler_params=_cparams(2),
        name="adaln_mod",
    )(c_rows, w_mod, b_mod.reshape(n_layers, 1, width))


def _rope(x, cos, sin_signed, nf):
    lane = lax.broadcasted_iota(jnp.int32, (1, LANES), 1)
    first = (lane % (2 * nf)) < nf
    tiles = []
    for t in range(x.shape[1] // LANES):
        xt = x[:, t * LANES:(t + 1) * LANES]
        partner = jnp.where(first, pltpu.roll(xt, LANES - nf, 1), pltpu.roll(xt, nf, 1))
        tiles.append(xt * cos + partner * sin_signed)
    return tiles[0] if len(tiles) == 1 else jnp.concatenate(tiles, axis=1)


def _rope_t(x, cos_t, sin_t, nf):
    partner = jnp.concatenate([x[nf:2 * nf], x[:nf], x[3 * nf:], x[2 * nf:3 * nf]], axis=0)
    return x * cos_t + partner * sin_t


def _emit_v_t(ref, v_t, slab):
    ones_blk = jnp.where(lax.broadcasted_iota(jnp.int32, (V_ROWS - HEAD_DIM, TQ), 0) == 0, 1.0, 0.0)
    parts = []
    for h in range(v_t.shape[0] // HEAD_DIM):
        parts += [v_t[h * HEAD_DIM:(h + 1) * HEAD_DIM], ones_blk]
    ext = jnp.concatenate(parts, axis=0).astype(ref.dtype)
    for s in range(TQ // slab):
        ref[0, s] = ext[:, s * slab:(s + 1) * slab]


def _rms(x, eps=EPS):
    return x * lax.rsqrt(jnp.mean(x * x, axis=-1, keepdims=True) + eps)


def _stream_specs(stream, n_ctx_blk, blk_off):
    lat, lat_off, ctx = stream
    d = lat.shape[2]
    return [pl.BlockSpec((1, TQ, d), lambda b, i: (b, jnp.maximum(i + blk_off - n_ctx_blk, 0) + lat_off, 0)),
            pl.BlockSpec((1, TQ, d), lambda b, i: (b, jnp.minimum(i + blk_off, n_ctx_blk - 1), 0))]


def _rms_t(x, eps=EPS):
    return x * lax.rsqrt(jnp.mean(x * x, axis=0, keepdims=True) + eps)


def _in_kernel(x_ref, c_ref, mod_ref, g_ref, wa_ref, wb_ref, cos64_ref, sin64_ref, cosm_ref, sinm_ref,
               cos64t_ref, sin64t_ref, cosmt_ref, sinmt_ref, gqg_ref, gkg_ref, mqg_ref, mkvg_ref, mkvgt_ref,
               wuqt_ref, wukvk_ref, wukvvt_ref, bd_ref,
               qs_ref, ks_ref, vs_ref, qg_ref, kg_ref, vg_ref, qm_ref, km_ref, vm_ref, *, mla_scale, n_ctx_blk):
    x = jnp.where(pl.program_id(1) < n_ctx_blk, c_ref[0], x_ref[0])
    mod = mod_ref[0, 0]
    h = _rms(x) * g_ref[...]
    h = h * (1.0 + mod[1:2]) + mod[0:1]
    hb = h.astype(BF16)
    za = lax.dot_general(wa_ref[...], hb, (((1,), (1,)), ((), ())), preferred_element_type=F32)
    zb = jnp.dot(hb, wb_ref[...], preferred_element_type=F32)
    cos_t, sin_t = cos64t_ref[...], sin64t_ref[...]
    zeros = jnp.zeros((HEAD_DIM, TQ), F32)

    def emit_gqa_q(ref, q_t, prep):
        for hd in range(GQA_HEADS):
            q = prep(q_t[hd * HEAD_DIM:(hd + 1) * HEAD_DIM])
            q = _rope_t(q, cos_t, sin_t, 16)
            tile = [q, zeros] if hd // GQA_G == 0 else [zeros, q]
            ref[0, hd] = jnp.concatenate(tile, axis=0).astype(ref.dtype)

    emit_gqa_q(qs_ref, za[_A_SQ:_A_GQ], lambda q: q * (HEAD_DIM ** -0.5 * LOG2E))
    ks_ref[0] = _rope(zb[:, 0:128], cos64_ref[...], sin64_ref[...], 16).astype(BF16)
    _emit_v_t(vs_ref, za[_A_SV:_A_GV], SWA_SLAB)

    emit_gqa_q(qg_ref, za[_A_GQ:_A_MQ], lambda q: _rms_t(q) * gqg_ref[...])
    gk = zb[:, 128:256]
    sq = gk * gk
    hi = sq.astype(BF16)
    lo = (sq - hi.astype(F32)).astype(BF16)
    ssum = (jnp.dot(hi, bd_ref[...], preferred_element_type=F32)
            + jnp.dot(lo, bd_ref[...], preferred_element_type=F32))
    gk = gk * lax.rsqrt(ssum * (1.0 / HEAD_DIM) + EPS) * gkg_ref[...]
    kg_ref[0] = _rope(gk, cos64_ref[...], sin64_ref[...], 16).astype(BF16)
    _emit_v_t(vg_ref, za[_A_GV:_A_MKV], TK)

    qn = _rms_t(za[_A_MQ:_A_SV]) * mqg_ref[...]
    mq = jnp.dot(wuqt_ref[...], qn.astype(BF16), preferred_element_type=F32)
    for hd in range(MLA_HEADS):
        t = mq[hd * LANES:(hd + 1) * LANES]
        r = _rope_t(t[MLA_NOPE:MLA_NOPE + MLA_ROPE], cosmt_ref[...], sinmt_ref[...], 8)
        t = jnp.concatenate([t[:MLA_NOPE], r, t[MLA_NOPE + MLA_ROPE:]], axis=0) * (mla_scale * LOG2E)
        qm_ref[0, hd] = t.astype(qm_ref.dtype)
    kvn = _rms(zb[:, 256:384]) * mkvg_ref[...]
    k_nope = jnp.dot(kvn.astype(BF16), wukvk_ref[...], preferred_element_type=F32)
    kr = _rope(zb[:, 384:512], cosm_ref[...], sinm_ref[...], 8)
    km_ref[0] = (k_nope + jnp.concatenate([kr] * MLA_HEADS, axis=1)).astype(BF16)
    kvn_t = _rms_t(za[_A_MKV:_WA_ROWS]) * mkvgt_ref[...]
    _emit_v_t(vm_ref, jnp.dot(wukvvt_ref[...], kvn_t.astype(BF16), preferred_element_type=F32), TK)


def _input_projection(stream, t_all, modtab, g_mix, lw, tabs, n_ctx_blk):
    bsz, _, d = stream[0].shape
    nblk = t_all // TQ
    tok = lambda w: pl.BlockSpec((1, TQ, w), lambda b, i: (b, i, 0))
    q_t = lambda nh: pl.BlockSpec((1, nh, LANES, TQ), lambda b, i: (b, 0, 0, i))
    v_t = lambda nh, slab: pl.BlockSpec((1, TQ // slab, nh * V_ROWS, slab), lambda b, i: (b, i, 0, 0))
    full = lambda a: pl.BlockSpec(a.shape, lambda b, i: (0,) * a.ndim)
    tab = pl.BlockSpec((TQ, LANES), lambda b, i: (i, 0))
    q_shape = lambda nh: jax.ShapeDtypeStruct((bsz, nh, LANES, t_all), BF16)
    k_shape = lambda w: jax.ShapeDtypeStruct((bsz, t_all, w), BF16)
    v_shape = lambda nh, slab: jax.ShapeDtypeStruct((bsz, t_all // slab, nh * V_ROWS, slab), BF16)
    tab_t = lambda a: pl.BlockSpec((a.shape[0], TQ), lambda b, i: (0, i))
    consts = (lw["gqg_t"], lw["gkg"], lw["mqg_t"], lw["mkvg"], lw["mkvg_t"], lw["wuq_t"], lw["wukv_k"],
              lw["wukv_vt"], lw["bd"])
    return pl.pallas_call(
        functools.partial(_in_kernel, mla_scale=float((MLA_NOPE + MLA_ROPE) ** -0.5), n_ctx_blk=n_ctx_blk),
        out_shape=(q_shape(GQA_HEADS), k_shape(128), v_shape(GQA_KV, SWA_SLAB),
                   q_shape(GQA_HEADS), k_shape(128), v_shape(GQA_KV, TK),
                   q_shape(MLA_HEADS), k_shape(512), v_shape(MLA_HEADS, TK)),
        grid=(bsz, nblk),
        in_specs=_stream_specs(stream, n_ctx_blk, 0)
                 + [pl.BlockSpec((1, 1, 8, d), lambda b, i: (b, jnp.where(i < n_ctx_blk, 1, 0), 0, 0)),
                    full(g_mix), full(lw["wa"]), full(lw["wb"]), tab, tab, tab, tab]
                 + [tab_t(a) for a in tabs[4:]] + [full(a) for a in consts],
        out_specs=(q_t(GQA_HEADS), tok(128), v_t(GQA_KV, SWA_SLAB),
                   q_t(GQA_HEADS), tok(128), v_t(GQA_KV, TK),
                   q_t(MLA_HEADS), tok(512), v_t(MLA_HEADS, TK)),
        compiler_params=_cparams(2),
        name="in_proj",
    )(stream[0], stream[2], modtab, g_mix, lw["wa"], lw["wb"], *tabs, *consts)


_N_IN = {"swa": 5, "glb": 3, "mla": 3}
_N_SCRATCH = {"swa": 6, "glb": 12, "mla": 12}
_OUT_COL = {"swa": 0, "glb": GQA_HEADS * HEAD_DIM, "mla": 2 * GQA_HEADS * HEAD_DIM}


def _attn_kernel(*refs, n_ctx, n_ctx_blk, q_blk_off, seq):
    kinds = ("swa", "glb", "mla")
    n_in = sum(_N_IN[k] for k in kinds)
    o_ref = refs[n_in]
    i0, s0 = 0, n_in + 1
    for kind in kinds:
        _attn_group(refs[i0:i0 + _N_IN[kind]], o_ref, refs[s0:s0 + _N_SCRATCH[kind]], kind=kind, n_ctx=n_ctx,
                    n_ctx_blk=n_ctx_blk, q_blk_off=q_blk_off, seq=seq)
        i0 += _N_IN[kind]
        s0 += _N_SCRATCH[kind]


def _attn_group(ins, o_ref, scratch, *, kind, n_ctx, n_ctx_blk, q_blk_off, seq):
    if kind == "swa":
        sink_ref, bias_ref, q_ref, k_ref, v_ref = ins
        acc_ref, s_refs, p_refs = scratch[0], scratch[1:4], scratch[4:6]
    else:
        q_ref, k_ref, v_ref = ins
        acc_ref, m_ref = scratch[:2]
        s_refs, cm_refs, p_refs, al_refs = scratch[2:5], scratch[5:8], scratch[8:10], scratch[10:12]
    blk = pl.program_id(1) + q_blk_off
    is_lat = blk >= n_ctx_blk
    gqa = kind in ("swa", "glb")
    n_heads = GQA_HEADS if gqa else MLA_HEADS
    n_units = GQA_KV if gqa else MLA_HEADS
    unit_w = n_heads // n_units * TQ

    if kind == "swa":
        nk = TQ + 2 * WINDOW
        n_slab = n_ctx // SWA_SLAB
        row_l = lax.broadcasted_iota(jnp.int32, (V_ROWS, TQ), 0) == HEAD_DIM

        def scores(h, parts):
            cm = None
            for k_rows, _, bias, r0, nr in parts:
                s = jnp.dot(k_rows(), q_ref[0, h], preferred_element_type=F32)
                if bias is not None:
                    s = s + bias()
                s_refs[h % 3][r0:r0 + nr, :] = s
                c = jnp.max(s, axis=0, keepdims=True)
                cm = c if cm is None else jnp.maximum(cm, c)
            return cm

        def softmax(h, cm, nrows):
            m = jnp.maximum(cm, sink_ref[h])
            p_refs[h % 2][0:nrows, :] = jnp.exp2((s_refs[h % 3][0:nrows, :] - m).astype(BF16))
            return jnp.exp2(sink_ref[h] - m)

        def values(h, e_sink, parts):
            u, j = divmod(h, GQA_G)
            acc = None
            for _, v_t, _, r0, nr in parts:
                a = jnp.dot(v_t(u), p_refs[h % 2][r0:r0 + nr, :], preferred_element_type=F32)
                acc = a if acc is None else acc + a
            acc_ref[u, :, j * TQ:(j + 1) * TQ] = acc + jnp.where(row_l, e_sink, 0.0)

        def run(parts):
            nrows = sum(p[4] for p in parts)
            cms = {h: scores(h, parts) for h in range(min(2, n_heads))}
            e_sinks = {}
            for h in range(n_heads + 1):
                if h + 2 < n_heads:
                    cms[h + 2] = scores(h + 2, parts)
                if h < n_heads:
                    e_sinks[h] = softmax(h, cms[h], nrows)
                if h >= 1:
                    values(h - 1, e_sinks[h - 1], parts)

        ctx_part = (lambda: k_ref[0, 0:n_ctx, :],
                    lambda u: jnp.concatenate([v_ref[0, s, u * V_ROWS:(u + 1) * V_ROWS, :] for s in range(n_slab)],
                                              axis=1),
                    None, 0, n_ctx)

        def latent():
            q0 = (blk - n_ctx_blk) * TQ
            k0 = pl.multiple_of(jnp.clip(q0 - WINDOW, 0, seq - nk), WINDOW)
            geom = (q0 - k0) // WINDOW
            s0 = (n_ctx + k0) // SWA_SLAB
            win_part = (lambda: k_ref[0, pl.ds(pl.multiple_of(n_ctx + k0, WINDOW), nk), :],
                        lambda u: jnp.concatenate([v_ref[0, s0 + s, u * V_ROWS:(u + 1) * V_ROWS, :]
                                                   for s in range(nk // SWA_SLAB)], axis=1),
                        lambda: bias_ref[geom], n_ctx, nk)
            run([ctx_part, win_part])

        if q_blk_off < n_ctx_blk:
            pl.when(jnp.logical_not(is_lat))(lambda: run([ctx_part]))
            pl.when(is_lat)(latent)
        else:
            latent()
    else:
        m_ref[...] = jnp.full(m_ref.shape, NEG_INF, F32)
        acc_ref[...] = jnp.zeros(acc_ref.shape, F32)

        def qk(c, par, h):
            lanes = slice(h * TQ, (h + 1) * TQ)
            k_cols = slice(0, LANES) if gqa else slice(h * LANES, (h + 1) * LANES)
            k_rows = k_ref[0, pl.ds(pl.multiple_of(c * TK, TK), TK), k_cols]
            s = jnp.dot(k_rows, q_ref[0, h], preferred_element_type=F32)
            s_refs[par][:, lanes] = s
            cm_refs[par][:, lanes] = jnp.max(s, axis=0, keepdims=True)

        def softmax(sb, par, h):
            lanes = slice(h * TQ, (h + 1) * TQ)
            m_prev = m_ref[:, lanes]
            m_new = jnp.maximum(m_prev, cm_refs[sb][:, lanes])
            m_ref[:, lanes] = m_new
            al_refs[par][:, lanes] = jnp.exp2(m_prev - m_new)
            p_refs[par][:, lanes] = jnp.exp2((s_refs[sb][:, lanes] - m_new).astype(BF16))

        def pv(c, par, h):
            lanes = slice(h * TQ, (h + 1) * TQ)
            u, j = divmod(h, n_heads // n_units)
            ul = slice(j * TQ, (j + 1) * TQ)
            acc_ref[u, :, ul] = (acc_ref[u, :, ul] * al_refs[par][:, lanes]
                                 + jnp.dot(v_ref[0, c, u * V_ROWS:(u + 1) * V_ROWS, :], p_refs[par][:, lanes],
                                           preferred_element_type=F32))

        def step(i, r, n):
            for h in range(n_heads):
                if not isinstance(i, int) or i + 2 < n:
                    qk(i + 2, (r + 2) % 3, h)
                if not isinstance(i, int) or i < n:
                    softmax(r % 3, r % 2, h)
                if not isinstance(i, int) or i >= 1:
                    pv(i - 1, (r + 1) % 2, h)

        def run(n):
            for c in range(min(2, n)):
                for h in range(n_heads):
                    qk(c, c, h)
            step(0, 0, n)
            n_groups = max(0, (n - 3) // 6)

            def body(j, carry):
                for r in range(6):
                    step(1 + 6 * j + r, (1 + r) % 6, n)
                return carry
            if n_groups:
                lax.fori_loop(0, n_groups, body, 0)
            for i in range(1 + 6 * n_groups, n + 1):
                step(i, i % 6, n)

        if q_blk_off < n_ctx_blk:
            pl.when(jnp.logical_not(is_lat))(lambda: run(n_ctx // TK))
            pl.when(is_lat)(lambda: run((n_ctx + seq) // TK))
        else:
            run((n_ctx + seq) // TK)

    heads = []
    for u in range(n_units):
        a = acc_ref[u]
        o = a[:HEAD_DIM] / a[HEAD_DIM:HEAD_DIM + 1]
        heads += [o[:, j * TQ:(j + 1) * TQ] for j in range(unit_w // TQ)]
    tiles = [jnp.concatenate(heads[2 * t:2 * t + 2], axis=0).T for t in range(n_heads // 2)]
    c0 = _OUT_COL[kind]
    o_ref[0, :, c0:c0 + n_heads * HEAD_DIM] = jnp.concatenate(tiles, axis=1).astype(o_ref.dtype)


def _attention(groups, sink, n_ctx, with_ctx_queries):
    bsz, _, _, t_all = groups["swa"][0].shape
    seq = t_all - n_ctx
    n_ctx_blk = n_ctx // TQ
    q_blk_off = 0 if with_ctx_queries else n_ctx_blk
    nblk = t_all // TQ - q_blk_off
    in_specs, args, scratch = [], [], []
    for kind in ("swa", "glb", "mla"):
        q_t, k, v_t = groups[kind]
        n_heads = q_t.shape[1]
        n_units = GQA_KV if kind in ("swa", "glb") else MLA_HEADS
        nq = n_heads * TQ
        if kind == "swa":
            nk = TQ + 2 * WINDOW
            off = np.arange(TQ)[None, None, :] + WINDOW * np.arange(3)[:, None, None] - np.arange(nk)[None, :, None]
            bias = jnp.asarray(np.where(np.abs(off) <= WINDOW, 0.0, NEG_INF), F32)
            in_specs += [pl.BlockSpec(memory_space=pltpu.SMEM), pl.BlockSpec(bias.shape, lambda b, i: (0, 0, 0))]
            args += [sink, bias]
        in_specs += [pl.BlockSpec((1, n_heads, LANES, TQ), lambda b, i: (b, 0, 0, i + q_blk_off)),
                     pl.BlockSpec((1,) + k.shape[1:], lambda b, i: (b, 0, 0)),
                     pl.BlockSpec((1,) + v_t.shape[1:], lambda b, i: (b, 0, 0, 0))]
        args += [q_t, k, v_t]
        scratch += [pltpu.VMEM((n_units, V_ROWS, nq // n_units), F32)]
        if kind == "swa":
            rows = n_ctx + nk
            scratch += [pltpu.VMEM((rows, TQ), F32)] * 3 + [pltpu.VMEM((rows, TQ), BF16)] * 2
        else:
            scratch += ([pltpu.VMEM((1, nq), F32)] + [pltpu.VMEM((TK, nq), F32)] * 3 + [pltpu.VMEM((1, nq), F32)] * 3
                        + [pltpu.VMEM((TK, nq), BF16)] * 2 + [pltpu.VMEM((1, nq), F32)] * 2)
    return pl.pallas_call(
        functools.partial(_attn_kernel, n_ctx=n_ctx, n_ctx_blk=n_ctx_blk, q_blk_off=q_blk_off, seq=seq),
        out_shape=jax.ShapeDtypeStruct((bsz, nblk * TQ, D_MODEL), BF16),
        grid=(bsz, nblk),
        in_specs=in_specs,
        out_specs=pl.BlockSpec((1, TQ, D_MODEL), lambda b, i: (b, i, 0)),
        scratch_shapes=scratch,
        compiler_params=_cparams(2),
        name="attention",
    )(*args)


def _row_select(rows, idx):
    out = rows[0]
    for j in range(1, len(rows)):
        out = jnp.where(idx == j, rows[j], out)
    return out


def _route(scores, biased):
    def top2sum(a, b, c, d):
        hi1, lo1, hi2, lo2 = jnp.maximum(a, b), jnp.minimum(a, b), jnp.maximum(c, d), jnp.minimum(c, d)
        return jnp.maximum(hi1, hi2) + jnp.maximum(jnp.minimum(hi1, hi2), jnp.maximum(lo1, lo2))

    gs = [top2sum(*biased[4 * g:4 * g + 4]) for g in range(N_GROUPS)]
    best, gi = gs[0], jnp.zeros(gs[0].shape, jnp.int32)
    for g in range(1, N_GROUPS):
        better = gs[g] > best
        gi = jnp.where(better, g, gi)
        best = jnp.where(better, gs[g], best)
    a = [_row_select([biased[4 * g + j] for g in range(N_GROUPS)], gi) for j in range(EXPERTS_PER_GROUP)]
    s = [_row_select([scores[4 * g + j] for g in range(N_GROUPS)], gi) for j in range(EXPERTS_PER_GROUP)]
    v1, i1 = a[0], jnp.zeros(gi.shape, jnp.int32)
    for j in range(1, EXPERTS_PER_GROUP):
        better = a[j] > v1
        i1 = jnp.where(better, j, i1)
        v1 = jnp.where(better, a[j], v1)
    v2, i2 = jnp.full(v1.shape, -3.0e38, F32), jnp.zeros(gi.shape, jnp.int32)
    for j in range(EXPERTS_PER_GROUP):
        better = (i1 != j) & (a[j] > v2)
        i2 = jnp.where(better, j, i2)
        v2 = jnp.where(better, a[j], v2)
    lo, hi = jnp.minimum(i1, i2), jnp.maximum(i1, i2)
    pair = jnp.where(lo == 0, hi - 1, jnp.where(lo == 1, hi + 1, 5))
    s_lo, s_hi = _row_select(s, lo), _row_select(s, hi)
    den = s_lo + s_hi
    return gi * N_PAIRS + pair, s_lo / den, s_hi / den


def _out_kernel(y_ref, x_ref, c_ref, mod_ref, wout_ref, g_ref, rwh_ref, rwl_ref, rb_ref, tri_ref,
                xo_ref, row_ref, meta_ref, cnt_ref, carry_ref, *, n_ctx_blk, blk_off):
    @pl.when((pl.program_id(0) == 0) & (pl.program_id(1) == 0))
    def _():
        carry_ref[...] = jnp.zeros(carry_ref.shape, F32)

    y = y_ref[0]
    mod = mod_ref[0, 0]
    half = D_MODEL // 2
    a = jnp.concatenate([jnp.dot(y, wout_ref[:, :half], preferred_element_type=F32),
                         jnp.dot(y, wout_ref[:, half:], preferred_element_type=F32)], axis=1)
    x = jnp.where(pl.program_id(1) + blk_off < n_ctx_blk, c_ref[0], x_ref[0]) + mod[2:3] * a
    xo_ref[0] = x
    h = _rms(x) * g_ref[...]
    h = h * (1.0 + mod[4:5]) + mod[3:4]

    hh = h.astype(BF16)
    hl = (h - hh.astype(F32)).astype(BF16)
    logits = (jnp.dot(hh, rwh_ref[...], preferred_element_type=F32)
              + jnp.dot(hl, rwh_ref[...], preferred_element_type=F32)
              + jnp.dot(hh, rwl_ref[...], preferred_element_type=F32)).T[:32]
    sc = jax.nn.sigmoid(logits)
    bs = sc + rb_ref[...]
    scores = [sc[e:e + 1, :] for e in range(N_EXPERTS)]
    biased = [bs[e:e + 1, :] for e in range(N_EXPERTS)]
    bucket, g_lo, g_hi = _route(scores, biased)

    onehot = jnp.where(lax.broadcasted_iota(jnp.int32, (32, TQ), 0) == bucket, 1.0, 0.0)
    prefix = jnp.dot(onehot.astype(BF16), tri_ref[...], preferred_element_type=F32)
    carry = carry_ref[:, 0:1]
    rank = jnp.sum(onehot * (carry + prefix - 1.0), axis=0, keepdims=True)
    carry_new = jnp.broadcast_to(carry + prefix[:, TQ - 1:TQ], carry_ref.shape)
    carry_ref[...] = carry_new
    cnt_ref[...] = carry_new
    meta_ref[0] = jnp.concatenate([bucket, rank.astype(jnp.int32), jnp.zeros((6, TQ), jnp.int32)], axis=0)

    gates = jnp.concatenate([g_lo, g_hi, jnp.zeros((LANES - 2, TQ), F32)], axis=0)
    row_ref[...] = jnp.concatenate([h, gates.T], axis=1)


def _output_projection(y_mix, stream, t_all, modtab, lw, shared, n_ctx_blk, with_ctx):
    bsz, _, d = stream[0].shape
    off = 0 if with_ctx else n_ctx_blk
    nblk = t_all // TQ - off
    tok = lambda w: pl.BlockSpec((1, TQ, w), lambda b, i: (b, i, 0))
    full = lambda a: pl.BlockSpec(a.shape, lambda b, i: (0,) * a.ndim)
    consts = (lw["wout"], lw["g_ffn"], shared["rw_hi"], shared["rw_lo"], shared["rb"], shared["tri"])
    return pl.pallas_call(
        functools.partial(_out_kernel, n_ctx_blk=n_ctx_blk, blk_off=off),
        out_shape=(jax.ShapeDtypeStruct((bsz, nblk * TQ, d), F32),
                   jax.ShapeDtypeStruct((bsz * nblk * TQ, ROW_W), F32),
                   jax.ShapeDtypeStruct((bsz * nblk, 8, TQ), jnp.int32),
                   jax.ShapeDtypeStruct((32, LANES), F32)),
        grid=(bsz, nblk),
        in_specs=[tok(d)] + _stream_specs(stream, n_ctx_blk, off)
                 + [pl.BlockSpec((1, 1, 8, d), lambda b, i: (b, jnp.where(i + off < n_ctx_blk, 1, 0), 0, 0))]
                 + [full(a) for a in consts],
        out_specs=(tok(d),
                   pl.BlockSpec((TQ, ROW_W), lambda b, i: (b * nblk + i, 0)),
                   pl.BlockSpec((1, 8, TQ), lambda b, i: (b * nblk + i, 0, 0)),
                   pl.BlockSpec((32, LANES), lambda b, i: (0, 0))),
        scratch_shapes=[pltpu.VMEM((32, LANES), F32)],
        compiler_params=_cparams(2),
        name="out_proj_router",
    )(y_mix, stream[0], stream[2], modtab, *consts)


def _scatter_kernel(dest_ref, fill_ref, nv_ref, row_ref, out_ref, buf, sem, *, n_steps, n_blocks):
    i = pl.program_id(0)
    slot = i % 2

    def wait_slot(s):
        pltpu.make_async_copy(buf.at[s], out_ref.at[pl.ds(0, TQ)], sem.at[s]).wait()

    @pl.when(i == 0)
    def _():
        buf[1] = jnp.zeros((TQ, ROW_W), F32)

        def zero_block(blk):
            return pltpu.make_async_copy(buf.at[1], out_ref.at[pl.ds(pl.multiple_of(blk * TM, TM), TM)], sem.at[1])

        for wait in (False, True):
            for b in range(N_BUCKETS):
                @pl.when(fill_ref[b] >= 0)
                def _(b=b, wait=wait):
                    zero_block(fill_ref[b]).wait() if wait else zero_block(fill_ref[b]).start()

            def tail(blk, carry, wait=wait):
                zero_block(blk).wait() if wait else zero_block(blk).start()
                return carry
            lax.fori_loop(nv_ref[0], n_blocks, tail, 0)

    for s in range(2):
        @pl.when(slot == s)
        def _(s=s):
            @pl.when(i >= 2)
            def _():
                wait_slot(s)

            buf[s] = row_ref[...]
            for r in range(TQ):
                d = dest_ref[i * TQ + r]
                pltpu.make_async_copy(buf.at[s, pl.ds(r, 1)], out_ref.at[pl.ds(d, 1)], sem.at[s]).start()

    @pl.when(i == n_steps - 1)
    def _():
        wait_slot(slot)
        if n_steps >= 2:
            wait_slot(1 - slot)


def _scatter_rows(dest, fill_blk, n_valid, rows, n_sorted):
    n_tok = rows.shape[0]
    n_steps = n_tok // TQ
    assert TQ == TM
    return pl.pallas_call(
        functools.partial(_scatter_kernel, n_steps=n_steps, n_blocks=n_sorted // TM),
        out_shape=jax.ShapeDtypeStruct((n_sorted, ROW_W), F32),
        grid_spec=pltpu.PrefetchScalarGridSpec(
            num_scalar_prefetch=3,
            grid=(n_steps,),
            in_specs=[pl.BlockSpec((TQ, ROW_W), lambda i, d, f, nv: (i, 0))],
            out_specs=pl.BlockSpec(memory_space=pl.ANY),
            scratch_shapes=[pltpu.VMEM((2, TQ, ROW_W), F32), pltpu.SemaphoreType.DMA((2,))]),
        compiler_params=_cparams(1),
        name="moe_scatter",
    )(dest, fill_blk, n_valid, rows)


def _swiglu(h, w_gate, w_up, w_down):
    a = jnp.dot(h, w_gate, preferred_element_type=F32)
    a = a * jax.nn.sigmoid(a) * jnp.dot(h, w_up, preferred_element_type=F32)
    return jnp.dot(a.astype(BF16), w_down, preferred_element_type=F32)


def _moe_kernel(elo_ref, ehi_ref, nv_ref, row_ref, gl_ref, ul_ref, dl_ref, gh_ref, uh_ref, dh_ref,
                sg_ref, su_ref, sd_ref, y_ref):
    del elo_ref, ehi_ref
    j = pl.program_id(0)

    @pl.when(j < nv_ref[0])
    def _():
        rows = row_ref[...]
        h = rows[:, :D_MODEL].astype(BF16)
        g_lo = rows[:, D_MODEL:D_MODEL + 1]
        g_hi = rows[:, D_MODEL + 1:D_MODEL + 2]
        y_ref[...] = (g_lo * _swiglu(h, gl_ref[0, 0], ul_ref[0, 0], dl_ref[0, 0])
                      + g_hi * _swiglu(h, gh_ref[0, 0], uh_ref[0, 0], dh_ref[0, 0])
                      + _swiglu(h, sg_ref[0], su_ref[0], sd_ref[0]))

    @pl.when(j >= nv_ref[0])
    def _():
        y_ref[...] = jnp.zeros(y_ref.shape, F32)


def _grouped_experts(e_lo, e_hi, n_valid, rows_sorted, ew, layer):
    n_sorted = rows_sorted.shape[0]
    nb = n_sorted // TM
    routed = lambda w, tab: pl.BlockSpec((1, 1) + w.shape[2:], lambda j, lo, hi, nv: (layer, (lo, hi)[tab][j], 0, 0))
    shared = lambda w: pl.BlockSpec((1,) + w.shape[1:], lambda j, lo, hi, nv: (layer, 0, 0))
    return pl.pallas_call(
        _moe_kernel,
        out_shape=jax.ShapeDtypeStruct((n_sorted, D_MODEL), F32),
        grid_spec=pltpu.PrefetchScalarGridSpec(
            num_scalar_prefetch=3,
            grid=(nb,),
            in_specs=[pl.BlockSpec((TM, ROW_W), lambda j, lo, hi, nv: (j, 0))]
                     + [routed(ew[k], tab) for tab in (0, 1) for k in ("gate", "up", "down")]
                     + [shared(ew[k]) for k in ("s_gate", "s_up", "s_down")],
            out_specs=pl.BlockSpec((TM, D_MODEL), lambda j, lo, hi, nv: (j, 0))),
        compiler_params=_cparams(1),
        name="moe_experts",
    )(e_lo, e_hi, n_valid, rows_sorted, *(ew[k] for k in ("gate", "up", "down")) , *(ew[k] for k in ("gate", "up", "down")),
      *(ew[k] for k in ("s_gate", "s_up", "s_down")))


def _gather_kernel(dest_ref, y_ref, x_ref, mod_ref, gf_ref, o_ref, fbuf, sem, *, n_steps, final_norm):
    i = pl.program_id(0)
    slot = i % 2

    def issue(step, s):
        for r in range(TQ):
            d = dest_ref[step * TQ + r]
            pltpu.make_async_copy(y_ref.at[pl.ds(d, 1)], fbuf.at[s, pl.ds(r, 1)], sem.at[s]).start()

    @pl.when(i == 0)
    def _():
        issue(0, 0)

    for s in range(2):
        @pl.when((i + 1 < n_steps) & (slot == s))
        def _(s=s):
            issue(i + 1, 1 - s)

    pltpu.make_async_copy(y_ref.at[pl.ds(0, TQ)], fbuf.at[slot], sem.at[slot]).wait()
    x = x_ref[0] + mod_ref[0, 0][5:6] * fbuf[slot]
    if final_norm:
        x = _rms(x) * gf_ref[...]
    o_ref[0] = x


def _gather_residual(dest, y_sorted, x_mid, modtab, g_final, n_ctx_blk, with_ctx, final_norm):
    bsz, t_rows, d = x_mid.shape
    nblk = t_rows // TQ
    off = 0 if with_ctx else n_ctx_blk
    n_steps = bsz * nblk
    return pl.pallas_call(
        functools.partial(_gather_kernel, n_steps=n_steps, final_norm=final_norm),
        out_shape=jax.ShapeDtypeStruct((bsz, t_rows, d), F32),
        grid_spec=pltpu.PrefetchScalarGridSpec(
            num_scalar_prefetch=1,
            grid=(n_steps,),
            in_specs=[pl.BlockSpec(memory_space=pl.ANY),
                      pl.BlockSpec((1, TQ, d), lambda i, dst: (i // nblk, i % nblk, 0)),
                      pl.BlockSpec((1, 1, 8, d),
                                   lambda i, dst: (i // nblk, jnp.where(i % nblk + off < n_ctx_blk, 1, 0), 0, 0)),
                      pl.BlockSpec(g_final.shape, lambda i, dst: (0, 0))],
            out_specs=pl.BlockSpec((1, TQ, d), lambda i, dst: (i // nblk, i % nblk, 0)),
            scratch_shapes=[pltpu.VMEM((2, TQ, d), F32), pltpu.SemaphoreType.DMA((2,))]),
        compiler_params=_cparams(1),
        name="moe_gather",
    )(dest, y_sorted, x_mid, modtab, g_final)


def _bucket_layout(meta, counts, n_tok):
    bucket = meta[:, 0, :].reshape(-1)
    rank = meta[:, 1, :].reshape(-1)
    cnt = counts[:N_BUCKETS, 0].astype(jnp.int32)
    padded = (cnt + TM - 1) // TM * TM
    pad_end = jnp.cumsum(padded)
    pad_start = pad_end - padded
    dest = pad_start[bucket] + rank
    nb = n_tok // TM + N_BUCKETS
    n_valid = pad_end[-1] // TM
    blk = jnp.arange(nb, dtype=jnp.int32)
    blk_row = jnp.minimum(blk, n_valid - 1) * TM
    blk_bucket = jnp.minimum(jnp.sum(pad_end[None, :] <= blk_row[:, None], axis=1), N_BUCKETS - 1).astype(jnp.int32)
    grp, pair = blk_bucket // N_PAIRS, blk_bucket % N_PAIRS
    e_lo = grp * EXPERTS_PER_GROUP + jnp.asarray(_PAIR_LO, jnp.int32)[pair]
    e_hi = grp * EXPERTS_PER_GROUP + jnp.asarray(_PAIR_HI, jnp.int32)[pair]
    fill_blk = jnp.where(padded > 0, pad_end // TM - 1, -1).astype(jnp.int32)
    return dest.astype(jnp.int32), fill_blk, e_lo, e_hi, n_valid.reshape(1).astype(jnp.int32), nb * TM


def _rope_tables(seq, n_ctx):
    t = jnp.arange(seq)
    rows, cols = t // GRID_W, t % GRID_W

    def table(width, nf):
        lane = np.arange(width)
        half, j = lane // (2 * nf), lane % (2 * nf)
        inv = ROPE_THETA ** (-jnp.arange(nf, dtype=F32) / nf)
        pos = jnp.where(jnp.asarray(half == 0)[None, :], rows[:, None], cols[:, None]).astype(F32)
        ang = pos * inv[j % nf][None, :]
        sign = jnp.asarray(np.where(j < nf, -1.0, 1.0), F32)[None, :]
        return jnp.cos(ang), jnp.sin(ang) * sign

    cos64, sin64 = table(HEAD_DIM, 16)
    cos64, sin64 = jnp.tile(cos64, (1, 2)), jnp.tile(sin64, (1, 2))
    cosr, sinr = table(MLA_ROPE, 8)
    cosm = jnp.concatenate([jnp.ones((seq, MLA_NOPE), F32), cosr, jnp.ones((seq, 32), F32)], axis=1)
    sinm = jnp.concatenate([jnp.zeros((seq, MLA_NOPE), F32), sinr, jnp.zeros((seq, 32), F32)], axis=1)
    ctx1, ctx0 = jnp.ones((n_ctx, LANES), F32), jnp.zeros((n_ctx, LANES), F32)
    token_major = [jnp.concatenate([c, a], axis=0) for c, a in ((ctx1, cos64), (ctx0, sin64), (ctx1, cosm), (ctx0, sinm))]
    dim_major = [jnp.concatenate([jnp.full((a.shape[1], n_ctx), fill, F32), a.T], axis=1)
                 for a, fill in ((cos64[:, :HEAD_DIM], 1.0), (sin64[:, :HEAD_DIM], 0.0), (cosr, 1.0), (sinr, 0.0))]
    return tuple(token_major + dim_major)


def _layer_weights(l, w_in, w_out, norm_ffn_g, glb_q_gain, glb_k_gain, mla_q_gain, mla_w_uq, mla_kv_gain,
                   mla_w_ukv):
    d = w_in.shape[1]
    wi = w_in[l]
    wa = jnp.concatenate([wi[:, _SQ:_SK], wi[:, _GQ:_GK], wi[:, _MQ:_MKV], wi[:, _SV:_GQ], wi[:, _GV:_MQ],
                          wi[:, _MKV:_KR]], axis=1).T
    kr_tile = jnp.concatenate([jnp.zeros((d, MLA_NOPE), F32), wi[:, _KR:], jnp.zeros((d, 32), F32)], axis=1)
    wb = jnp.concatenate([wi[:, _SK:_SV], wi[:, _GK:_GV], wi[:, _MKV:_KR], kr_tile], axis=1)
    uq = mla_w_uq[l].reshape(MLA_Q_RANK, MLA_HEADS, MLA_NOPE + MLA_ROPE)
    wuq = jnp.concatenate([uq, jnp.zeros((MLA_Q_RANK, MLA_HEADS, 32), F32)], axis=2).reshape(MLA_Q_RANK, 512)
    ukv = mla_w_ukv[l].reshape(MLA_KV_RANK, MLA_HEADS, MLA_NOPE + MLA_V)
    wk = jnp.concatenate([ukv[:, :, :MLA_NOPE], jnp.zeros((MLA_KV_RANK, MLA_HEADS, 64), F32)], axis=2)
    bcast = lambda g: jnp.broadcast_to(g[:, None], (g.shape[0], TQ))
    head = np.arange(LANES) // HEAD_DIM
    return {
        "wa": wa.astype(BF16),
        "wb": wb.astype(BF16),
        "wuq_t": wuq.T.astype(BF16),
        "wukv_k": wk.reshape(MLA_KV_RANK, 512).astype(BF16),
        "wukv_vt": ukv[:, :, MLA_NOPE:].reshape(MLA_KV_RANK, 256).T.astype(BF16),
        "gqg_t": bcast(glb_q_gain[l] * (HEAD_DIM ** -0.5 * LOG2E)),
        "gkg": jnp.tile(glb_k_gain[l], GQA_KV).reshape(1, LANES),
        "mqg_t": bcast(mla_q_gain[l]),
        "mkvg": mla_kv_gain[l].reshape(1, -1),
        "mkvg_t": bcast(mla_kv_gain[l]),
        "bd": jnp.asarray(head[:, None] == head[None, :], BF16),
        "wout": w_out[l].astype(BF16),
        "g_ffn": norm_ffn_g[l].reshape(1, -1),
    }


def kernel(x, c, ctx, c_ctx, w_mod, b_mod, norm_mix_g, norm_ffn_g, w_in, w_out, swa_sink, glb_q_gain, glb_k_gain,
           mla_q_gain, mla_w_uq, mla_kv_gain, mla_w_ukv, router_w, router_bias, exp_w_gate, exp_w_up, exp_w_down,
           shr_w_gate, shr_w_up, shr_w_down, final_norm_g):
    bsz, seq, d = x.shape
    n_ctx = ctx.shape[1]
    n_layers = w_mod.shape[0]
    assert d == D_MODEL and n_ctx % TQ == 0 and seq % TQ == 0 and seq >= TQ + 2 * WINDOW and seq % GRID_W == 0
    n_ctx_blk = n_ctx // TQ

    rows = -(-(bsz + 1) // 8) * 8
    c_rows = jnp.concatenate([c, c_ctx[None, :], jnp.zeros((rows - bsz - 1, d), F32)], axis=0)
    mods = _modulation(c_rows, w_mod, b_mod)
    mod_x = mods[:, :bsz].reshape(n_layers, bsz, 6, d)
    mod_c = jnp.broadcast_to(mods[:, bsz].reshape(n_layers, 1, 6, d), (n_layers, bsz, 6, d))
    modtabs = jnp.pad(jnp.stack([mod_x, mod_c], axis=2), ((0, 0), (0, 0), (0, 0), (0, 2), (0, 0)))

    tabs = _rope_tables(seq, n_ctx)
    rw = jnp.pad(router_w, ((0, 0), (0, LANES - N_EXPERTS)))
    rw_hi = rw.astype(BF16)
    shared = {
        "rw_hi": rw_hi,
        "rw_lo": (rw - rw_hi.astype(F32)).astype(BF16),
        "rb": jnp.pad(router_bias, (0, 32 - N_EXPERTS)).reshape(32, 1),
        "tri": jnp.asarray(np.arange(TQ)[:, None] <= np.arange(TQ)[None, :], BF16),
    }
    g_final = final_norm_g.reshape(1, d)
    ew = {"gate": exp_w_gate.astype(BF16), "up": exp_w_up.astype(BF16), "down": exp_w_down.astype(BF16),
          "s_gate": shr_w_gate.astype(BF16), "s_up": shr_w_up.astype(BF16), "s_down": shr_w_down.astype(BF16)}

    t_all = n_ctx + seq
    stream = (x, 0, ctx)
    for l in range(n_layers):
        last = l == n_layers - 1
        with_ctx = not last
        lw = _layer_weights(l, w_in, w_out, norm_ffn_g, glb_q_gain, glb_k_gain, mla_q_gain, mla_w_uq,
                            mla_kv_gain, mla_w_ukv)
        modtab = modtabs[l]
        qs, ks, vs, qg, kg, vg, qm, km, vm = _input_projection(
            stream, t_all, modtab, norm_mix_g[l].reshape(1, d), lw, tabs, n_ctx_blk)
        sink = jnp.pad(swa_sink[l] * LOG2E, (0, 8 - GQA_HEADS))
        y_mix = _attention({"swa": (qs, ks, vs), "glb": (qg, kg, vg), "mla": (qm, km, vm)}, sink, n_ctx, with_ctx)
        x_mid, rows_tok, meta, counts = _output_projection(y_mix, stream, t_all, modtab, lw, shared, n_ctx_blk,
                                                           with_ctx)
        n_tok = rows_tok.shape[0]
        dest, fill_blk, e_lo, e_hi, n_valid, n_sorted = _bucket_layout(meta, counts, n_tok)
        rows_sorted = _scatter_rows(dest, fill_blk, n_valid, rows_tok, n_sorted)
        y_sorted = _grouped_experts(e_lo, e_hi, n_valid, rows_sorted, ew, l)
        xs = _gather_residual(dest, y_sorted, x_mid, modtab, g_final, n_ctx_blk, with_ctx, final_norm=last)
        stream = (xs, n_ctx_blk, xs)
    return xs
```

```python
import functools

import jax
import jax.numpy as jnp
import numpy as np
from jax import lax
from jax.experimental import pallas as pl
from jax.experimental.pallas import tpu as pltpu

F32 = jnp.float32
BF16 = jnp.bfloat16

D_MODEL = 1024
GRID_W = 64
HEAD_DIM = 64
GQA_HEADS = 6
GQA_KV = 2
GQA_G = GQA_HEADS // GQA_KV
WINDOW = 128
MLA_HEADS = 4
MLA_NOPE = 64
MLA_ROPE = 32
MLA_V = 64
MLA_Q_RANK = 256
MLA_KV_RANK = 128
ROPE_THETA = 10000.0
N_EXPERTS = 16
N_GROUPS = 4
EXPERTS_PER_GROUP = 4
N_PAIRS = 6
N_BUCKETS = N_GROUPS * N_PAIRS
D_EXPERT = 512
EPS = 1e-6
NEG_INF = -1e30

LANES = 128
TQ = 256
TK = 256
TM = 256
TS = 512
SWA_SLAB = 128
V_ROWS = 80
LOG2E = 1.4426950408889634
ROW_W = D_MODEL + LANES
VMEM_LIMIT = 56 * 1024 * 1024

_SQ, _SK, _SV, _GQ, _GK, _GV, _MQ, _MKV, _KR = 0, 384, 512, 640, 1024, 1152, 1280, 1536, 1664
_A_SQ, _A_GQ, _A_MQ, _A_SV, _A_GV, _A_MKV, _WA_ROWS = 0, 384, 768, 1024, 1152, 1280, 1408

_PAIR_LO = (0, 0, 0, 1, 1, 2)
_PAIR_HI = (1, 2, 3, 2, 3, 3)


def _cparams(n_axes):
    return pltpu.CompilerParams(dimension_semantics=("arbitrary",) * n_axes,
                                vmem_limit_bytes=VMEM_LIMIT)


def _mod_kernel(c_ref, w_ref, b_ref, o_ref):
    c = c_ref[...]
    a = (c * jax.nn.sigmoid(c)).astype(BF16)
    o_ref[0] = jnp.dot(a, w_ref[0].astype(BF16), preferred_element_type=F32) + b_ref[0]


def _modulation(c_rows, w_mod, b_mod):
    n_layers, d, width = w_mod.shape
    rows = c_rows.shape[0]
    nb = 1536
    return pl.pallas_call(
        _mod_kernel,
        out_shape=jax.ShapeDtypeStruct((n_layers, rows, width), F32),
        grid=(n_layers, width // nb),
        in_specs=[pl.BlockSpec((rows, d), lambda l, j: (0, 0)),
                  pl.BlockSpec((1, d, nb), lambda l, j: (l, 0, j)),
                  pl.BlockSpec((1, 1, nb), lambda l, j: (l, 0, j))],
        out_specs=pl.BlockSpec((1, rows, nb), lambda l, j: (l, 0, j)),
        compiler_params=_cparams(2),
        name="adaln_mod",
    )(c_rows, w_mod, b_mod.reshape(n_layers, 1, width))


def _rope(x, cos, sin_signed, nf):
    lane = lax.broadcasted_iota(jnp.int32, (1, LANES), 1)
    first = (lane % (2 * nf)) < nf
    tiles = []
    for t in range(x.shape[1] // LANES):
        xt = x[:, t * LANES:(t + 1) * LANES]
        partner = jnp.where(first, pltpu.roll(xt, LANES - nf, 1), pltpu.roll(xt, nf, 1))
        tiles.append(xt * cos + partner * sin_signed)
    return tiles[0] if len(tiles) == 1 else jnp.concatenate(tiles, axis=1)


def _rope_t(x, cos_t, sin_t, nf):
    partner = jnp.concatenate([x[nf:2 * nf], x[:nf], x[3 * nf:], x[2 * nf:3 * nf]], axis=0)
    return x * cos_t + partner * sin_t


def _emit_v_t(ref, v_t, slab):
    ones_blk = jnp.where(lax.broadcasted_iota(jnp.int32, (V_ROWS - HEAD_DIM, TQ), 0) == 0, 1.0, 0.0)
    parts = []
    for h in range(v_t.shape[0] // HEAD_DIM):
        parts += [v_t[h * HEAD_DIM:(h + 1) * HEAD_DIM], ones_blk]
    ext = jnp.concatenate(parts, axis=0).astype(ref.dtype)
    for s in range(TQ // slab):
        ref[0, s] = ext[:, s * slab:(s + 1) * slab]


def _rms(x, eps=EPS):
    return x * lax.rsqrt(jnp.mean(x * x, axis=-1, keepdims=True) + eps)


def _stream_specs(stream, n_ctx_blk, blk_off):
    lat, lat_off, ctx = stream
    d = lat.shape[2]
    return [pl.BlockSpec((1, TQ, d), lambda b, i: (b, jnp.maximum(i + blk_off - n_ctx_blk, 0) + lat_off, 0)),
            pl.BlockSpec((1, TQ, d), lambda b, i: (b, jnp.minimum(i + blk_off, n_ctx_blk - 1), 0))]


def _rms_t(x, eps=EPS):
    return x * lax.rsqrt(jnp.mean(x * x, axis=0, keepdims=True) + eps)


def _in_kernel(x_ref, c_ref, mod_ref, g_ref, wa_ref, wb_ref, cos64_ref, sin64_ref, cosm_ref, sinm_ref,
               cos64t_ref, sin64t_ref, cosmt_ref, sinmt_ref, gqg_ref, gkg_ref, mqg_ref, mkvg_ref, mkvgt_ref,
               wuqt_ref, wukvk_ref, wukvvt_ref, bd_ref,
               qs_ref, ks_ref, vs_ref, qg_ref, kg_ref, vg_ref, qm_ref, km_ref, vm_ref, *, mla_scale, n_ctx_blk):
    x = jnp.where(pl.program_id(1) < n_ctx_blk, c_ref[0], x_ref[0])
    mod = mod_ref[0, 0]
    h = _rms(x) * g_ref[...]
    h = h * (1.0 + mod[1:2]) + mod[0:1]
    hb = h.astype(BF16)
    za = lax.dot_general(wa_ref[...], hb, (((1,), (1,)), ((), ())), preferred_element_type=F32)
    zb = jnp.dot(hb, wb_ref[...], preferred_element_type=F32)
    cos_t, sin_t = cos64t_ref[...], sin64t_ref[...]
    zeros = jnp.zeros((HEAD_DIM, TQ), F32)

    def emit_gqa_q(ref, q_t, prep):
        for hd in range(GQA_HEADS):
            q = prep(q_t[hd * HEAD_DIM:(hd + 1) * HEAD_DIM])
            q = _rope_t(q, cos_t, sin_t, 16)
            tile = [q, zeros] if hd // GQA_G == 0 else [zeros, q]
            ref[0, hd] = jnp.concatenate(tile, axis=0).astype(ref.dtype)

    emit_gqa_q(qs_ref, za[_A_SQ:_A_GQ], lambda q: q * (HEAD_DIM ** -0.5 * LOG2E))
    ks_ref[0] = _rope(zb[:, 0:128], cos64_ref[...], sin64_ref[...], 16).astype(BF16)
    _emit_v_t(vs_ref, za[_A_SV:_A_GV], SWA_SLAB)

    emit_gqa_q(qg_ref, za[_A_GQ:_A_MQ], lambda q: _rms_t(q) * gqg_ref[...])
    gk = zb[:, 128:256]
    sq = gk * gk
    hi = sq.astype(BF16)
    lo = (sq - hi.astype(F32)).astype(BF16)
    ssum = (jnp.dot(hi, bd_ref[...], preferred_element_type=F32)
            + jnp.dot(lo, bd_ref[...], preferred_element_type=F32))
    gk = gk * lax.rsqrt(ssum * (1.0 / HEAD_DIM) + EPS) * gkg_ref[...]
    kg_ref[0] = _rope(gk, cos64_ref[...], sin64_ref[...], 16).astype(BF16)
    _emit_v_t(vg_ref, za[_A_GV:_A_MKV], TK)

    qn = _rms_t(za[_A_MQ:_A_SV]) * mqg_ref[...]
    mq = jnp.dot(wuqt_ref[...], qn.astype(BF16), preferred_element_type=F32)
    for hd in range(MLA_HEADS):
        t = mq[hd * LANES:(hd + 1) * LANES]
        r = _rope_t(t[MLA_NOPE:MLA_NOPE + MLA_ROPE], cosmt_ref[...], sinmt_ref[...], 8)
        t = jnp.concatenate([t[:MLA_NOPE], r, t[MLA_NOPE + MLA_ROPE:]], axis=0) * (mla_scale * LOG2E)
        qm_ref[0, hd] = t.astype(qm_ref.dtype)
    kvn = _rms(zb[:, 256:384]) * mkvg_ref[...]
    k_nope = jnp.dot(kvn.astype(BF16), wukvk_ref[...], preferred_element_type=F32)
    kr = _rope(zb[:, 384:512], cosm_ref[...], sinm_ref[...], 8)
    km_ref[0] = (k_nope + jnp.concatenate([kr] * MLA_HEADS, axis=1)).astype(BF16)
    kvn_t = _rms_t(za[_A_MKV:_WA_ROWS]) * mkvgt_ref[...]
    _emit_v_t(vm_ref, jnp.dot(wukvvt_ref[...], kvn_t.astype(BF16), preferred_element_type=F32), TK)


def _input_projection(stream, t_all, modtab, g_mix, lw, tabs, n_ctx_blk):
    bsz, _, d = stream[0].shape
    nblk = t_all // TQ
    tok = lambda w: pl.BlockSpec((1, TQ, w), lambda b, i: (b, i, 0))
    q_t = lambda nh: pl.BlockSpec((1, nh, LANES, TQ), lambda b, i: (b, 0, 0, i))
    v_t = lambda nh, slab: pl.BlockSpec((1, TQ // slab, nh * V_ROWS, slab), lambda b, i: (b, i, 0, 0))
    full = lambda a: pl.BlockSpec(a.shape, lambda b, i: (0,) * a.ndim)
    tab = pl.BlockSpec((TQ, LANES), lambda b, i: (i, 0))
    q_shape = lambda nh: jax.ShapeDtypeStruct((bsz, nh, LANES, t_all), BF16)
    k_shape = lambda w: jax.ShapeDtypeStruct((bsz, t_all, w), BF16)
    v_shape = lambda nh, slab: jax.ShapeDtypeStruct((bsz, t_all // slab, nh * V_ROWS, slab), BF16)
    tab_t = lambda a: pl.BlockSpec((a.shape[0], TQ), lambda b, i: (0, i))
    consts = (lw["gqg_t"], lw["gkg"], lw["mqg_t"], lw["mkvg"], lw["mkvg_t"], lw["wuq_t"], lw["wukv_k"],
              lw["wukv_vt"], lw["bd"])
    return pl.pallas_call(
        functools.partial(_in_kernel, mla_scale=float((MLA_NOPE + MLA_ROPE) ** -0.5), n_ctx_blk=n_ctx_blk),
        out_shape=(q_shape(GQA_HEADS), k_shape(128), v_shape(GQA_KV, SWA_SLAB),
                   q_shape(GQA_HEADS), k_shape(128), v_shape(GQA_KV, TK),
                   q_shape(MLA_HEADS), k_shape(512), v_shape(MLA_HEADS, TK)),
        grid=(bsz, nblk),
        in_specs=_stream_specs(stream, n_ctx_blk, 0)
                 + [pl.BlockSpec((1, 1, 8, d), lambda b, i: (b, jnp.where(i < n_ctx_blk, 1, 0), 0, 0)),
                    full(g_mix), full(lw["wa"]), full(lw["wb"]), tab, tab, tab, tab]
                 + [tab_t(a) for a in tabs[4:]] + [full(a) for a in consts],
        out_specs=(q_t(GQA_HEADS), tok(128), v_t(GQA_KV, SWA_SLAB),
                   q_t(GQA_HEADS), tok(128), v_t(GQA_KV, TK),
                   q_t(MLA_HEADS), tok(512), v_t(MLA_HEADS, TK)),
        compiler_params=_cparams(2),
        name="in_proj",
    )(stream[0], stream[2], modtab, g_mix, lw["wa"], lw["wb"], *tabs, *consts)


_N_IN = {"swa": 5, "glb": 3, "mla": 3}
_N_SCRATCH = {"swa": 6, "glb": 12, "mla": 12}
_OUT_COL = {"swa": 0, "glb": GQA_HEADS * HEAD_DIM, "mla": 2 * GQA_HEADS * HEAD_DIM}


def _attn_kernel(*refs, n_ctx, n_ctx_blk, q_blk_off, seq):
    kinds = ("swa", "glb", "mla")
    n_in = sum(_N_IN[k] for k in kinds)
    o_ref = refs[n_in]
    i0, s0 = 0, n_in + 1
    for kind in kinds:
        _attn_group(refs[i0:i0 + _N_IN[kind]], o_ref, refs[s0:s0 + _N_SCRATCH[kind]], kind=kind, n_ctx=n_ctx,
                    n_ctx_blk=n_ctx_blk, q_blk_off=q_blk_off, seq=seq)
        i0 += _N_IN[kind]
        s0 += _N_SCRATCH[kind]


def _attn_group(ins, o_ref, scratch, *, kind, n_ctx, n_ctx_blk, q_blk_off, seq):
    if kind == "swa":
        sink_ref, bias_ref, q_ref, k_ref, v_ref = ins
        acc_ref, s_refs, p_refs = scratch[0], scratch[1:4], scratch[4:6]
    else:
        q_ref, k_ref, v_ref = ins
        acc_ref, m_ref = scratch[:2]
        s_refs, cm_refs, p_refs, al_refs = scratch[2:5], scratch[5:8], scratch[8:10], scratch[10:12]
    blk = pl.program_id(1) + q_blk_off
    is_lat = blk >= n_ctx_blk
    gqa = kind in ("swa", "glb")
    n_heads = GQA_HEADS if gqa else MLA_HEADS
    n_units = GQA_KV if gqa else MLA_HEADS
    unit_w = n_heads // n_units * TQ

    if kind == "swa":
        nk = TQ + 2 * WINDOW
        n_slab = n_ctx // SWA_SLAB
        row_l = lax.broadcasted_iota(jnp.int32, (V_ROWS, TQ), 0) == HEAD_DIM

        def scores(h, parts):
            cm = None
            for k_rows, _, bias, r0, nr in parts:
                s = jnp.dot(k_rows(), q_ref[0, h], preferred_element_type=F32)
                if bias is not None:
                    s = s + bias()
                s_refs[h % 3][r0:r0 + nr, :] = s
                c = jnp.max(s, axis=0, keepdims=True)
                cm = c if cm is None else jnp.maximum(cm, c)
            return cm

        def softmax(h, cm, nrows):
            m = jnp.maximum(cm, sink_ref[h])
            p_refs[h % 2][0:nrows, :] = jnp.exp2((s_refs[h % 3][0:nrows, :] - m).astype(BF16))
            return jnp.exp2(sink_ref[h] - m)

        def values(h, e_sink, parts):
            u, j = divmod(h, GQA_G)
            acc = None
            for _, v_t, _, r0, nr in parts:
                a = jnp.dot(v_t(u), p_refs[h % 2][r0:r0 + nr, :], preferred_element_type=F32)
                acc = a if acc is None else acc + a
            acc_ref[u, :, j * TQ:(j + 1) * TQ] = acc + jnp.where(row_l, e_sink, 0.0)

        def run(parts):
            nrows = sum(p[4] for p in parts)
            cms = {h: scores(h, parts) for h in range(min(2, n_heads))}
            e_sinks = {}
            for h in range(n_heads + 1):
                if h + 2 < n_heads:
                    cms[h + 2] = scores(h + 2, parts)
                if h < n_heads:
                    e_sinks[h] = softmax(h, cms[h], nrows)
                if h >= 1:
                    values(h - 1, e_sinks[h - 1], parts)

        ctx_part = (lambda: k_ref[0, 0:n_ctx, :],
                    lambda u: jnp.concatenate([v_ref[0, s, u * V_ROWS:(u + 1) * V_ROWS, :] for s in range(n_slab)],
                                              axis=1),
                    None, 0, n_ctx)

        def latent():
            q0 = (blk - n_ctx_blk) * TQ
            k0 = pl.multiple_of(jnp.clip(q0 - WINDOW, 0, seq - nk), WINDOW)
            geom = (q0 - k0) // WINDOW
            s0 = (n_ctx + k0) // SWA_SLAB
            win_part = (lambda: k_ref[0, pl.ds(pl.multiple_of(n_ctx + k0, WINDOW), nk), :],
                        lambda u: jnp.concatenate([v_ref[0, s0 + s, u * V_ROWS:(u + 1) * V_ROWS, :]
                                                   for s in range(nk // SWA_SLAB)], axis=1),
                        lambda: bias_ref[geom], n_ctx, nk)
            run([ctx_part, win_part])

        if q_blk_off < n_ctx_blk:
            pl.when(jnp.logical_not(is_lat))(lambda: run([ctx_part]))
            pl.when(is_lat)(latent)
        else:
            latent()
    else:
        m_ref[...] = jnp.full(m_ref.shape, NEG_INF, F32)
        acc_ref[...] = jnp.zeros(acc_ref.shape, F32)

        def qk(c, par, h):
            lanes = slice(h * TQ, (h + 1) * TQ)
            k_cols = slice(0, LANES) if gqa else slice(h * LANES, (h + 1) * LANES)
            k_rows = k_ref[0, pl.ds(pl.multiple_of(c * TK, TK), TK), k_cols]
            s = jnp.dot(k_rows, q_ref[0, h], preferred_element_type=F32)
            s_refs[par][:, lanes] = s
            cm_refs[par][:, lanes] = jnp.max(s, axis=0, keepdims=True)

        def softmax(sb, par, h):
            lanes = slice(h * TQ, (h + 1) * TQ)
            m_prev = m_ref[:, lanes]
            m_new = jnp.maximum(m_prev, cm_refs[sb][:, lanes])
            m_ref[:, lanes] = m_new
            al_refs[par][:, lanes] = jnp.exp2(m_prev - m_new)
            p_refs[par][:, lanes] = jnp.exp2((s_refs[sb][:, lanes] - m_new).astype(BF16))

        def pv(c, par, h):
            lanes = slice(h * TQ, (h + 1) * TQ)
            u, j = divmod(h, n_heads // n_units)
            ul = slice(j * TQ, (j + 1) * TQ)
            acc_ref[u, :, ul] = (acc_ref[u, :, ul] * al_refs[par][:, lanes]
                                 + jnp.dot(v_ref[0, c, u * V_ROWS:(u + 1) * V_ROWS, :], p_refs[par][:, lanes],
                                           preferred_element_type=F32))

        def step(i, r, n):
            for h in range(n_heads):
                if not isinstance(i, int) or i + 2 < n:
                    qk(i + 2, (r + 2) % 3, h)
                if not isinstance(i, int) or i < n:
                    softmax(r % 3, r % 2, h)
                if not isinstance(i, int) or i >= 1:
                    pv(i - 1, (r + 1) % 2, h)

        def run(n):
            for c in range(min(2, n)):
                for h in range(n_heads):
                    qk(c, c, h)
            step(0, 0, n)
            n_groups = max(0, (n - 3) // 6)

            def body(j, carry):
                for r in range(6):
                    step(1 + 6 * j + r, (1 + r) % 6, n)
                return carry
            if n_groups:
                lax.fori_loop(0, n_groups, body, 0)
            for i in range(1 + 6 * n_groups, n + 1):
                step(i, i % 6, n)

        if q_blk_off < n_ctx_blk:
            pl.when(jnp.logical_not(is_lat))(lambda: run(n_ctx // TK))
            pl.when(is_lat)(lambda: run((n_ctx + seq) // TK))
        else:
            run((n_ctx + seq) // TK)

    heads = []
    for u in range(n_units):
        a = acc_ref[u]
        o = a[:HEAD_DIM] / a[HEAD_DIM:HEAD_DIM + 1]
        heads += [o[:, j * TQ:(j + 1) * TQ] for j in range(unit_w // TQ)]
    tiles = [jnp.concatenate(heads[2 * t:2 * t + 2], axis=0).T for t in range(n_heads // 2)]
    c0 = _OUT_COL[kind]
    o_ref[0, :, c0:c0 + n_heads * HEAD_DIM] = jnp.concatenate(tiles, axis=1).astype(o_ref.dtype)


def _attention(groups, sink, n_ctx, with_ctx_queries):
    bsz, _, _, t_all = groups["swa"][0].shape
    seq = t_all - n_ctx
    n_ctx_blk = n_ctx // TQ
    q_blk_off = 0 if with_ctx_queries else n_ctx_blk
    nblk = t_all // TQ - q_blk_off
    in_specs, args, scratch = [], [], []
    for kind in ("swa", "glb", "mla"):
        q_t, k, v_t = groups[kind]
        n_heads = q_t.shape[1]
        n_units = GQA_KV if kind in ("swa", "glb") else MLA_HEADS
        nq = n_heads * TQ
        if kind == "swa":
            nk = TQ + 2 * WINDOW
            off = np.arange(TQ)[None, None, :] + WINDOW * np.arange(3)[:, None, None] - np.arange(nk)[None, :, None]
            bias = jnp.asarray(np.where(np.abs(off) <= WINDOW, 0.0, NEG_INF), F32)
            in_specs += [pl.BlockSpec(memory_space=pltpu.SMEM), pl.BlockSpec(bias.shape, lambda b, i: (0, 0, 0))]
            args += [sink, bias]
        in_specs += [pl.BlockSpec((1, n_heads, LANES, TQ), lambda b, i: (b, 0, 0, i + q_blk_off)),
                     pl.BlockSpec((1,) + k.shape[1:], lambda b, i: (b, 0, 0)),
                     pl.BlockSpec((1,) + v_t.shape[1:], lambda b, i: (b, 0, 0, 0))]
        args += [q_t, k, v_t]
        scratch += [pltpu.VMEM((n_units, V_ROWS, nq // n_units), F32)]
        if kind == "swa":
            rows = n_ctx + nk
            scratch += [pltpu.VMEM((rows, TQ), F32)] * 3 + [pltpu.VMEM((rows, TQ), BF16)] * 2
        else:
            scratch += ([pltpu.VMEM((1, nq), F32)] + [pltpu.VMEM((TK, nq), F32)] * 3 + [pltpu.VMEM((1, nq), F32)] * 3
                        + [pltpu.VMEM((TK, nq), BF16)] * 2 + [pltpu.VMEM((1, nq), F32)] * 2)
    return pl.pallas_call(
        functools.partial(_attn_kernel, n_ctx=n_ctx, n_ctx_blk=n_ctx_blk, q_blk_off=q_blk_off, seq=seq),
        out_shape=jax.ShapeDtypeStruct((bsz, nblk * TQ, D_MODEL), BF16),
        grid=(bsz, nblk),
        in_specs=in_specs,
        out_specs=pl.BlockSpec((1, TQ, D_MODEL), lambda b, i: (b, i, 0)),
        scratch_shapes=scratch,
        compiler_params=_cparams(2),
        name="attention",
    )(*args)


def _row_select(rows, idx):
    out = rows[0]
    for j in range(1, len(rows)):
        out = jnp.where(idx == j, rows[j], out)
    return out


def _route(scores, biased):
    def top2sum(a, b, c, d):
        hi1, lo1, hi2, lo2 = jnp.maximum(a, b), jnp.minimum(a, b), jnp.maximum(c, d), jnp.minimum(c, d)
        return jnp.maximum(hi1, hi2) + jnp.maximum(jnp.minimum(hi1, hi2), jnp.maximum(lo1, lo2))

    gs = [top2sum(*biased[4 * g:4 * g + 4]) for g in range(N_GROUPS)]
    best, gi = gs[0], jnp.zeros(gs[0].shape, jnp.int32)
    for g in range(1, N_GROUPS):
        better = gs[g] > best
        gi = jnp.where(better, g, gi)
        best = jnp.where(better, gs[g], best)
    a = [_row_select([biased[4 * g + j] for g in range(N_GROUPS)], gi) for j in range(EXPERTS_PER_GROUP)]
    s = [_row_select([scores[4 * g + j] for g in range(N_GROUPS)], gi) for j in range(EXPERTS_PER_GROUP)]
    v1, i1 = a[0], jnp.zeros(gi.shape, jnp.int32)
    for j in range(1, EXPERTS_PER_GROUP):
        better = a[j] > v1
        i1 = jnp.where(better, j, i1)
        v1 = jnp.where(better, a[j], v1)
    v2, i2 = jnp.full(v1.shape, -3.0e38, F32), jnp.zeros(gi.shape, jnp.int32)
    for j in range(EXPERTS_PER_GROUP):
        better = (i1 != j) & (a[j] > v2)
        i2 = jnp.where(better, j, i2)
        v2 = jnp.where(better, a[j], v2)
    lo, hi = jnp.minimum(i1, i2), jnp.maximum(i1, i2)
    pair = jnp.where(lo == 0, hi - 1, jnp.where(lo == 1, hi + 1, 5))
    s_lo, s_hi = _row_select(s, lo), _row_select(s, hi)
    den = s_lo + s_hi
    return gi * N_PAIRS + pair, s_lo / den, s_hi / den


def _out_kernel(y_ref, x_ref, c_ref, mod_ref, wout_ref, g_ref, rwh_ref, rwl_ref, rb_ref, tri_ref,
                xo_ref, row_ref, meta_ref, cnt_ref, carry_ref, *, n_ctx_blk, blk_off):
    @pl.when((pl.program_id(0) == 0) & (pl.program_id(1) == 0))
    def _():
        carry_ref[...] = jnp.zeros(carry_ref.shape, F32)

    y = y_ref[0]
    mod = mod_ref[0, 0]
    half = D_MODEL // 2
    a = jnp.concatenate([jnp.dot(y, wout_ref[:, :half], preferred_element_type=F32),
                         jnp.dot(y, wout_ref[:, half:], preferred_element_type=F32)], axis=1)
    x = jnp.where(pl.program_id(1) + blk_off < n_ctx_blk, c_ref[0], x_ref[0]) + mod[2:3] * a
    xo_ref[0] = x
    h = _rms(x) * g_ref[...]
    h = h * (1.0 + mod[4:5]) + mod[3:4]

    hh = h.astype(BF16)
    hl = (h - hh.astype(F32)).astype(BF16)
    logits = (jnp.dot(hh, rwh_ref[...], preferred_element_type=F32)
              + jnp.dot(hl, rwh_ref[...], preferred_element_type=F32)
              + jnp.dot(hh, rwl_ref[...], preferred_element_type=F32)).T[:32]
    sc = jax.nn.sigmoid(logits)
    bs = sc + rb_ref[...]
    scores = [sc[e:e + 1, :] for e in range(N_EXPERTS)]
    biased = [bs[e:e + 1, :] for e in range(N_EXPERTS)]
    bucket, g_lo, g_hi = _route(scores, biased)

    onehot = jnp.where(lax.broadcasted_iota(jnp.int32, (32, TQ), 0) == bucket, 1.0, 0.0)
    prefix = jnp.dot(onehot.astype(BF16), tri_ref[...], preferred_element_type=F32)
    carry = carry_ref[:, 0:1]
    rank = jnp.sum(onehot * (carry + prefix - 1.0), axis=0, keepdims=True)
    carry_new = jnp.broadcast_to(carry + prefix[:, TQ - 1:TQ], carry_ref.shape)
    carry_ref[...] = carry_new
    cnt_ref[...] = carry_new
    meta_ref[0] = jnp.concatenate([bucket, rank.astype(jnp.int32), jnp.zeros((6, TQ), jnp.int32)], axis=0)

    gates = jnp.concatenate([g_lo, g_hi, jnp.zeros((LANES - 2, TQ), F32)], axis=0)
    row_ref[...] = jnp.concatenate([h, gates.T], axis=1)


def _output_projection(y_mix, stream, t_all, modtab, lw, shared, n_ctx_blk, with_ctx):
    bsz, _, d = stream[0].shape
    off = 0 if with_ctx else n_ctx_blk
    nblk = t_all // TQ - off
    tok = lambda w: pl.BlockSpec((1, TQ, w), lambda b, i: (b, i, 0))
    full = lambda a: pl.BlockSpec(a.shape, lambda b, i: (0,) * a.ndim)
    consts = (lw["wout"], lw["g_ffn"], shared["rw_hi"], shared["rw_lo"], shared["rb"], shared["tri"])
    return pl.pallas_call(
        functools.partial(_out_kernel, n_ctx_blk=n_ctx_blk, blk_off=off),
        out_shape=(jax.ShapeDtypeStruct((bsz, nblk * TQ, d), F32),
                   jax.ShapeDtypeStruct((bsz * nblk * TQ, ROW_W), F32),
                   jax.ShapeDtypeStruct((bsz * nblk, 8, TQ), jnp.int32),
                   jax.ShapeDtypeStruct((32, LANES), F32)),
        grid=(bsz, nblk),
        in_specs=[tok(d)] + _stream_specs(stream, n_ctx_blk, off)
                 + [pl.BlockSpec((1, 1, 8, d), lambda b, i: (b, jnp.where(i + off < n_ctx_blk, 1, 0), 0, 0))]
                 + [full(a) for a in consts],
        out_specs=(tok(d),
                   pl.BlockSpec((TQ, ROW_W), lambda b, i: (b * nblk + i, 0)),
                   pl.BlockSpec((1, 8, TQ), lambda b, i: (b * nblk + i, 0, 0)),
                   pl.BlockSpec((32, LANES), lambda b, i: (0, 0))),
        scratch_shapes=[pltpu.VMEM((32, LANES), F32)],
        compiler_params=_cparams(2),
        name="out_proj_router",
    )(y_mix, stream[0], stream[2], modtab, *consts)


def _scatter_kernel(dest_ref, fill_ref, nv_ref, row_ref, out_ref, buf, sem, *, n_steps, n_blocks):
    i = pl.program_id(0)
    slot = i % 2

    def wait_slot(s):
        pltpu.make_async_copy(buf.at[s], out_ref.at[pl.ds(0, TS)], sem.at[s]).wait()

    @pl.when(i == 0)
    def _():
        buf[1, 0:TM] = jnp.zeros((TM, ROW_W), F32)

        def zero_block(blk):
            return pltpu.make_async_copy(buf.at[1, pl.ds(0, TM)], out_ref.at[pl.ds(pl.multiple_of(blk * TM, TM), TM)],
                                         sem.at[1])

        for wait in (False, True):
            for b in range(N_BUCKETS):
                @pl.when(fill_ref[b] >= 0)
                def _(b=b, wait=wait):
                    zero_block(fill_ref[b]).wait() if wait else zero_block(fill_ref[b]).start()

            def tail(blk, carry, wait=wait):
                zero_block(blk).wait() if wait else zero_block(blk).start()
                return carry
            lax.fori_loop(nv_ref[0], n_blocks, tail, 0)

    for s in range(2):
        @pl.when(slot == s)
        def _(s=s):
            @pl.when(i >= 2)
            def _():
                wait_slot(s)

            buf[s] = row_ref[...]
            for r in range(TS):
                d = dest_ref[i * TS + r]
                pltpu.make_async_copy(buf.at[s, pl.ds(r, 1)], out_ref.at[pl.ds(d, 1)], sem.at[s]).start()

    @pl.when(i == n_steps - 1)
    def _():
        wait_slot(slot)
        if n_steps >= 2:
            wait_slot(1 - slot)


def _scatter_rows(dest, fill_blk, n_valid, rows, n_sorted):
    n_tok = rows.shape[0]
    assert n_tok % TS == 0
    n_steps = n_tok // TS
    return pl.pallas_call(
        functools.partial(_scatter_kernel, n_steps=n_steps, n_blocks=n_sorted // TM),
        out_shape=jax.ShapeDtypeStruct((n_sorted, ROW_W), F32),
        grid_spec=pltpu.PrefetchScalarGridSpec(
            num_scalar_prefetch=3,
            grid=(n_steps,),
            in_specs=[pl.BlockSpec((TS, ROW_W), lambda i, d, f, nv: (i, 0))],
            out_specs=pl.BlockSpec(memory_space=pl.ANY),
            scratch_shapes=[pltpu.VMEM((2, TS, ROW_W), F32), pltpu.SemaphoreType.DMA((2,))]),
        compiler_params=_cparams(1),
        name="moe_scatter",
    )(dest, fill_blk, n_valid, rows)


def _swiglu(h, w_gate, w_up, w_down):
    a = jnp.dot(h, w_gate, preferred_element_type=F32)
    a = a * jax.nn.sigmoid(a) * jnp.dot(h, w_up, preferred_element_type=F32)
    return jnp.dot(a.astype(BF16), w_down, preferred_element_type=F32)


def _moe_kernel(elo_ref, ehi_ref, nv_ref, row_ref, gl_ref, ul_ref, dl_ref, gh_ref, uh_ref, dh_ref,
                sg_ref, su_ref, sd_ref, y_ref):
    del elo_ref, ehi_ref
    j = pl.program_id(0)

    @pl.when(j < nv_ref[0])
    def _():
        rows = row_ref[...]
        h = rows[:, :D_MODEL].astype(BF16)
        g_lo = rows[:, D_MODEL:D_MODEL + 1]
        g_hi = rows[:, D_MODEL + 1:D_MODEL + 2]
        y_ref[...] = (g_lo * _swiglu(h, gl_ref[0, 0], ul_ref[0, 0], dl_ref[0, 0])
                      + g_hi * _swiglu(h, gh_ref[0, 0], uh_ref[0, 0], dh_ref[0, 0])
                      + _swiglu(h, sg_ref[0], su_ref[0], sd_ref[0]))

    @pl.when(j >= nv_ref[0])
    def _():
        y_ref[...] = jnp.zeros(y_ref.shape, F32)


def _grouped_experts(e_lo, e_hi, n_valid, rows_sorted, ew, layer):
    n_sorted = rows_sorted.shape[0]
    nb = n_sorted // TM
    routed = lambda w, tab: pl.BlockSpec((1, 1) + w.shape[2:], lambda j, lo, hi, nv: (layer, (lo, hi)[tab][j], 0, 0))
    shared = lambda w: pl.BlockSpec((1,) + w.shape[1:], lambda j, lo, hi, nv: (layer, 0, 0))
    return pl.pallas_call(
        _moe_kernel,
        out_shape=jax.ShapeDtypeStruct((n_sorted, D_MODEL), F32),
        grid_spec=pltpu.PrefetchScalarGridSpec(
            num_scalar_prefetch=3,
            grid=(nb,),
            in_specs=[pl.BlockSpec((TM, ROW_W), lambda j, lo, hi, nv: (j, 0))]
                     + [routed(ew[k], tab) for tab in (0, 1) for k in ("gate", "up", "down")]
                     + [shared(ew[k]) for k in ("s_gate", "s_up", "s_down")],
            out_specs=pl.BlockSpec((TM, D_MODEL), lambda j, lo, hi, nv: (j, 0))),
        compiler_params=_cparams(1),
        name="moe_experts",
    )(e_lo, e_hi, n_valid, rows_sorted, *(ew[k] for k in ("gate", "up", "down")) , *(ew[k] for k in ("gate", "up", "down")),
      *(ew[k] for k in ("s_gate", "s_up", "s_down")))


def _gather_kernel(dest_ref, y_ref, x_ref, mod_ref, gf_ref, o_ref, fbuf, sem, *, n_steps, final_norm, tr):
    i = pl.program_id(0)
    slot = i % 2

    def issue(step, s):
        for r in range(tr):
            d = dest_ref[step * tr + r]
            pltpu.make_async_copy(y_ref.at[pl.ds(d, 1)], fbuf.at[s, pl.ds(r, 1)], sem.at[s]).start()

    @pl.when(i == 0)
    def _():
        issue(0, 0)

    for s in range(2):
        @pl.when((i + 1 < n_steps) & (slot == s))
        def _(s=s):
            issue(i + 1, 1 - s)

    pltpu.make_async_copy(y_ref.at[pl.ds(0, tr)], fbuf.at[slot], sem.at[slot]).wait()
    x = x_ref[0] + mod_ref[0, 0][5:6] * fbuf[slot]
    if final_norm:
        x = _rms(x) * gf_ref[...]
    o_ref[0] = x


def _gather_residual(dest, y_sorted, x_mid, modtab, g_final, n_ctx_blk, with_ctx, final_norm):
    bsz, t_rows, d = x_mid.shape
    tr = TS if (not with_ctx and t_rows % TS == 0) else TQ
    nblk = t_rows // tr
    off = 0 if with_ctx else n_ctx_blk
    n_steps = bsz * nblk
    return pl.pallas_call(
        functools.partial(_gather_kernel, n_steps=n_steps, final_norm=final_norm, tr=tr),
        out_shape=jax.ShapeDtypeStruct((bsz, t_rows, d), F32),
        grid_spec=pltpu.PrefetchScalarGridSpec(
            num_scalar_prefetch=1,
            grid=(n_steps,),
            in_specs=[pl.BlockSpec(memory_space=pl.ANY),
                      pl.BlockSpec((1, tr, d), lambda i, dst: (i // nblk, i % nblk, 0)),
                      pl.BlockSpec((1, 1, 8, d),
                                   lambda i, dst: (i // nblk, jnp.where(i % nblk * (tr // TQ) + off < n_ctx_blk, 1, 0), 0, 0)),
                      pl.BlockSpec(g_final.shape, lambda i, dst: (0, 0))],
            out_specs=pl.BlockSpec((1, tr, d), lambda i, dst: (i // nblk, i % nblk, 0)),
            scratch_shapes=[pltpu.VMEM((2, tr, d), F32), pltpu.SemaphoreType.DMA((2,))]),
        compiler_params=_cparams(1),
        name="moe_gather",
    )(dest, y_sorted, x_mid, modtab, g_final)


def _bucket_layout(meta, counts, n_tok):
    bucket = meta[:, 0, :].reshape(-1)
    rank = meta[:, 1, :].reshape(-1)
    cnt = counts[:N_BUCKETS, 0].astype(jnp.int32)
    padded = (cnt + TM - 1) // TM * TM
    pad_end = jnp.cumsum(padded)
    pad_start = pad_end - padded
    dest = pad_start[bucket] + rank
    nb = n_tok // TM + N_BUCKETS
    n_valid = pad_end[-1] // TM
    blk = jnp.arange(nb, dtype=jnp.int32)
    blk_row = jnp.minimum(blk, n_valid - 1) * TM
    blk_bucket = jnp.minimum(jnp.sum(pad_end[None, :] <= blk_row[:, None], axis=1), N_BUCKETS - 1).astype(jnp.int32)
    grp, pair = blk_bucket // N_PAIRS, blk_bucket % N_PAIRS
    e_lo = grp * EXPERTS_PER_GROUP + jnp.asarray(_PAIR_LO, jnp.int32)[pair]
    e_hi = grp * EXPERTS_PER_GROUP + jnp.asarray(_PAIR_HI, jnp.int32)[pair]
    fill_blk = jnp.where(padded > 0, pad_end // TM - 1, -1).astype(jnp.int32)
    return dest.astype(jnp.int32), fill_blk, e_lo, e_hi, n_valid.reshape(1).astype(jnp.int32), nb * TM


def _rope_tables(seq, n_ctx):
    t = jnp.arange(seq)
    rows, cols = t // GRID_W, t % GRID_W

    def table(width, nf):
        lane = np.arange(width)
        half, j = lane // (2 * nf), lane % (2 * nf)
        inv = ROPE_THETA ** (-jnp.arange(nf, dtype=F32) / nf)
        pos = jnp.where(jnp.asarray(half == 0)[None, :], rows[:, None], cols[:, None]).astype(F32)
        ang = pos * inv[j % nf][None, :]
        sign = jnp.asarray(np.where(j < nf, -1.0, 1.0), F32)[None, :]
        return jnp.cos(ang), jnp.sin(ang) * sign

    cos64, sin64 = table(HEAD_DIM, 16)
    cos64, sin64 = jnp.tile(cos64, (1, 2)), jnp.tile(sin64, (1, 2))
    cosr, sinr = table(MLA_ROPE, 8)
    cosm = jnp.concatenate([jnp.ones((seq, MLA_NOPE), F32), cosr, jnp.ones((seq, 32), F32)], axis=1)
    sinm = jnp.concatenate([jnp.zeros((seq, MLA_NOPE), F32), sinr, jnp.zeros((seq, 32), F32)], axis=1)
    ctx1, ctx0 = jnp.ones((n_ctx, LANES), F32), jnp.zeros((n_ctx, LANES), F32)
    token_major = [jnp.concatenate([c, a], axis=0) for c, a in ((ctx1, cos64), (ctx0, sin64), (ctx1, cosm), (ctx0, sinm))]
    dim_major = [jnp.concatenate([jnp.full((a.shape[1], n_ctx), fill, F32), a.T], axis=1)
                 for a, fill in ((cos64[:, :HEAD_DIM], 1.0), (sin64[:, :HEAD_DIM], 0.0), (cosr, 1.0), (sinr, 0.0))]
    return tuple(token_major + dim_major)


def _layer_weights(l, w_in, w_out, norm_ffn_g, glb_q_gain, glb_k_gain, mla_q_gain, mla_w_uq, mla_kv_gain,
                   mla_w_ukv):
    d = w_in.shape[1]
    wi = w_in[l]
    wa = jnp.concatenate([wi[:, _SQ:_SK], wi[:, _GQ:_GK], wi[:, _MQ:_MKV], wi[:, _SV:_GQ], wi[:, _GV:_MQ],
                          wi[:, _MKV:_KR]], axis=1).T
    kr_tile = jnp.concatenate([jnp.zeros((d, MLA_NOPE), F32), wi[:, _KR:], jnp.zeros((d, 32), F32)], axis=1)
    wb = jnp.concatenate([wi[:, _SK:_SV], wi[:, _GK:_GV], wi[:, _MKV:_KR], kr_tile], axis=1)
    uq = mla_w_uq[l].reshape(MLA_Q_RANK, MLA_HEADS, MLA_NOPE + MLA_ROPE)
    wuq = jnp.concatenate([uq, jnp.zeros((MLA_Q_RANK, MLA_HEADS, 32), F32)], axis=2).reshape(MLA_Q_RANK, 512)
    ukv = mla_w_ukv[l].reshape(MLA_KV_RANK, MLA_HEADS, MLA_NOPE + MLA_V)
    wk = jnp.concatenate([ukv[:, :, :MLA_NOPE], jnp.zeros((MLA_KV_RANK, MLA_HEADS, 64), F32)], axis=2)
    bcast = lambda g: jnp.broadcast_to(g[:, None], (g.shape[0], TQ))
    head = np.arange(LANES) // HEAD_DIM
    return {
        "wa": wa.astype(BF16),
        "wb": wb.astype(BF16),
        "wuq_t": wuq.T.astype(BF16),
        "wukv_k": wk.reshape(MLA_KV_RANK, 512).astype(BF16),
        "wukv_vt": ukv[:, :, MLA_NOPE:].reshape(MLA_KV_RANK, 256).T.astype(BF16),
        "gqg_t": bcast(glb_q_gain[l] * (HEAD_DIM ** -0.5 * LOG2E)),
        "gkg": jnp.tile(glb_k_gain[l], GQA_KV).reshape(1, LANES),
        "mqg_t": bcast(mla_q_gain[l]),
        "mkvg": mla_kv_gain[l].reshape(1, -1),
        "mkvg_t": bcast(mla_kv_gain[l]),
        "bd": jnp.asarray(head[:, None] == head[None, :], BF16),
        "wout": w_out[l].astype(BF16),
        "g_ffn": norm_ffn_g[l].reshape(1, -1),
    }


def kernel(x, c, ctx, c_ctx, w_mod, b_mod, norm_mix_g, norm_ffn_g, w_in, w_out, swa_sink, glb_q_gain, glb_k_gain,
           mla_q_gain, mla_w_uq, mla_kv_gain, mla_w_ukv, router_w, router_bias, exp_w_gate, exp_w_up, exp_w_down,
           shr_w_gate, shr_w_up, shr_w_down, final_norm_g):
    bsz, seq, d = x.shape
    n_ctx = ctx.shape[1]
    n_layers = w_mod.shape[0]
    assert d == D_MODEL and n_ctx % TQ == 0 and seq % TQ == 0 and seq >= TQ + 2 * WINDOW and seq % GRID_W == 0
    n_ctx_blk = n_ctx // TQ

    rows = -(-(bsz + 1) // 8) * 8
    c_rows = jnp.concatenate([c, c_ctx[None, :], jnp.zeros((rows - bsz - 1, d), F32)], axis=0)
    mods = _modulation(c_rows, w_mod, b_mod)
    mod_x = mods[:, :bsz].reshape(n_layers, bsz, 6, d)
    mod_c = jnp.broadcast_to(mods[:, bsz].reshape(n_layers, 1, 6, d), (n_layers, bsz, 6, d))
    modtabs = jnp.pad(jnp.stack([mod_x, mod_c], axis=2), ((0, 0), (0, 0), (0, 0), (0, 2), (0, 0)))

    tabs = _rope_tables(seq, n_ctx)
    rw = jnp.pad(router_w, ((0, 0), (0, LANES - N_EXPERTS)))
    rw_hi = rw.astype(BF16)
    shared = {
        "rw_hi": rw_hi,
        "rw_lo": (rw - rw_hi.astype(F32)).astype(BF16),
        "rb": jnp.pad(router_bias, (0, 32 - N_EXPERTS)).reshape(32, 1),
        "tri": jnp.asarray(np.arange(TQ)[:, None] <= np.arange(TQ)[None, :], BF16),
    }
    g_final = final_norm_g.reshape(1, d)
    ew = {"gate": exp_w_gate.astype(BF16), "up": exp_w_up.astype(BF16), "down": exp_w_down.astype(BF16),
          "s_gate": shr_w_gate.astype(BF16), "s_up": shr_w_up.astype(BF16), "s_down": shr_w_down.astype(BF16)}

    t_all = n_ctx + seq
    stream = (x, 0, ctx)
    for l in range(n_layers):
        last = l == n_layers - 1
        with_ctx = not last
        lw = _layer_weights(l, w_in, w_out, norm_ffn_g, glb_q_gain, glb_k_gain, mla_q_gain, mla_w_uq,
                            mla_kv_gain, mla_w_ukv)
        modtab = modtabs[l]
        qs, ks, vs, qg, kg, vg, qm, km, vm = _input_projection(
            stream, t_all, modtab, norm_mix_g[l].reshape(1, d), lw, tabs, n_ctx_blk)
        sink = jnp.pad(swa_sink[l] * LOG2E, (0, 8 - GQA_HEADS))
        y_mix = _attention({"swa": (qs, ks, vs), "glb": (qg, kg, vg), "mla": (qm, km, vm)}, sink, n_ctx, with_ctx)
        x_mid, rows_tok, meta, counts = _output_projection(y_mix, stream, t_all, modtab, lw, shared, n_ctx_blk,
                                                           with_ctx)
        n_tok = rows_tok.shape[0]
        dest, fill_blk, e_lo, e_hi, n_valid, n_sorted = _bucket_layout(meta, counts, n_tok)
        rows_sorted = _scatter_rows(dest, fill_blk, n_valid, rows_tok, n_sorted)
        y_sorted = _grouped_experts(e_lo, e_hi, n_valid, rows_sorted, ew, l)
        xs = _gather_residual(dest, y_sorted, x_mid, modtab, g_final, n_ctx_blk, with_ctx, final_norm=last)
        stream = (xs, n_ctx_blk, xs)
    return xs
```

```python
import functools

import jax
import jax.numpy as jnp
import numpy as np
from jax import lax
from jax.experimental import pallas as pl
from jax.experimental.pallas import tpu as pltpu

F32 = jnp.float32
BF16 = jnp.bfloat16

D_MODEL = 1024
GRID_W = 64
HEAD_DIM = 64
GQA_HEADS = 6
GQA_KV = 2
GQA_G = GQA_HEADS // GQA_KV
WINDOW = 128
MLA_HEADS = 4
MLA_NOPE = 64
MLA_ROPE = 32
MLA_V = 64
MLA_Q_RANK = 256
MLA_KV_RANK = 128
ROPE_THETA = 10000.0
N_EXPERTS = 16
N_GROUPS = 4
EXPERTS_PER_GROUP = 4
N_PAIRS = 6
N_BUCKETS = N_GROUPS * N_PAIRS
D_EXPERT = 512
EPS = 1e-6
NEG_INF = -1e30

LANES = 128
TQ = 256
TK = 256
TM = 256
TS = 1024
SWA_SLAB = 128
V_ROWS = 80
LOG2E = 1.4426950408889634
ROW_W = D_MODEL + LANES
VMEM_LIMIT = 56 * 1024 * 1024

_SQ, _SK, _SV, _GQ, _GK, _GV, _MQ, _MKV, _KR = 0, 384, 512, 640, 1024, 1152, 1280, 1536, 1664
_A_SQ, _A_GQ, _A_MQ, _A_SV, _A_GV, _A_MKV, _WA_ROWS = 0, 384, 768, 1024, 1152, 1280, 1408

_PAIR_LO = (0, 0, 0, 1, 1, 2)
_PAIR_HI = (1, 2, 3, 2, 3, 3)


def _cparams(n_axes):
    return pltpu.CompilerParams(dimension_semantics=("arbitrary",) * n_axes,
                                vmem_limit_bytes=VMEM_LIMIT)


def _mod_kernel(c_ref, w_ref, b_ref, o_ref):
    c = c_ref[...]
    a = (c * jax.nn.sigmoid(c)).astype(BF16)
    o_ref[0] = jnp.dot(a, w_ref[0].astype(BF16), preferred_element_type=F32) + b_ref[0]


def _modulation(c_rows, w_mod, b_mod):
    n_layers, d, width = w_mod.shape
    rows = c_rows.shape[0]
    nb = 1536
    return pl.pallas_call(
        _mod_kernel,
        out_shape=jax.ShapeDtypeStruct((n_layers, rows, width), F32),
        grid=(n_layers, width // nb),
        in_specs=[pl.BlockSpec((rows, d), lambda l, j: (0, 0)),
                  pl.BlockSpec((1, d, nb), lambda l, j: (l, 0, j)),
                  pl.BlockSpec((1, 1, nb), lambda l, j: (l, 0, j))],
        out_specs=pl.BlockSpec((1, rows, nb), lambda l, j: (l, 0, j)),
        compiler_params=_cparams(2),
        name="adaln_mod",
    )(c_rows, w_mod, b_mod.reshape(n_layers, 1, width))


def _rope(x, cos, sin_signed, nf):
    lane = lax.broadcasted_iota(jnp.int32, (1, LANES), 1)
    first = (lane % (2 * nf)) < nf
    tiles = []
    for t in range(x.shape[1] // LANES):
        xt = x[:, t * LANES:(t + 1) * LANES]
        partner = jnp.where(first, pltpu.roll(xt, LANES - nf, 1), pltpu.roll(xt, nf, 1))
        tiles.append(xt * cos + partner * sin_signed)
    return tiles[0] if len(tiles) == 1 else jnp.concatenate(tiles, axis=1)


def _rope_t(x, cos_t, sin_t, nf):
    partner = jnp.concatenate([x[nf:2 * nf], x[:nf], x[3 * nf:], x[2 * nf:3 * nf]], axis=0)
    return x * cos_t + partner * sin_t


def _emit_v_t(ref, v_t, slab):
    ones_blk = jnp.where(lax.broadcasted_iota(jnp.int32, (V_ROWS - HEAD_DIM, TQ), 0) == 0, 1.0, 0.0)
    parts = []
    for h in range(v_t.shape[0] // HEAD_DIM):
        parts += [v_t[h * HEAD_DIM:(h + 1) * HEAD_DIM], ones_blk]
    ext = jnp.concatenate(parts, axis=0).astype(ref.dtype)
    for s in range(TQ // slab):
        ref[0, s] = ext[:, s * slab:(s + 1) * slab]


def _rms(x, eps=EPS):
    return x * lax.rsqrt(jnp.mean(x * x, axis=-1, keepdims=True) + eps)


def _stream_specs(stream, n_ctx_blk, blk_off):
    lat, lat_off, ctx = stream
    d = lat.shape[2]
    return [pl.BlockSpec((1, TQ, d), lambda b, i: (b, jnp.maximum(i + blk_off - n_ctx_blk, 0) + lat_off, 0)),
            pl.BlockSpec((1, TQ, d), lambda b, i: (b, jnp.minimum(i + blk_off, n_ctx_blk - 1), 0))]


def _rms_t(x, eps=EPS):
    return x * lax.rsqrt(jnp.mean(x * x, axis=0, keepdims=True) + eps)


def _in_kernel(x_ref, c_ref, mod_ref, g_ref, wa_ref, wb_ref, cos64_ref, sin64_ref, cosm_ref, sinm_ref,
               cos64t_ref, sin64t_ref, cosmt_ref, sinmt_ref, gqg_ref, gkg_ref, mqg_ref, mkvg_ref, mkvgt_ref,
               wuqt_ref, wukvk_ref, wukvvt_ref, bd_ref,
               qs_ref, ks_ref, vs_ref, qg_ref, kg_ref, vg_ref, qm_ref, km_ref, vm_ref, *, mla_scale, n_ctx_blk):
    x = jnp.where(pl.program_id(1) < n_ctx_blk, c_ref[0], x_ref[0])
    mod = mod_ref[0, 0]
    h = _rms(x) * g_ref[...]
    h = h * (1.0 + mod[1:2]) + mod[0:1]
    hb = h.astype(BF16)
    za = lax.dot_general(wa_ref[...], hb, (((1,), (1,)), ((), ())), preferred_element_type=F32)
    zb = jnp.dot(hb, wb_ref[...], preferred_element_type=F32)
    cos_t, sin_t = cos64t_ref[...], sin64t_ref[...]
    zeros = jnp.zeros((HEAD_DIM, TQ), F32)

    def emit_gqa_q(ref, q_t, prep):
        for hd in range(GQA_HEADS):
            q = prep(q_t[hd * HEAD_DIM:(hd + 1) * HEAD_DIM])
            q = _rope_t(q, cos_t, sin_t, 16)
            tile = [q, zeros] if hd // GQA_G == 0 else [zeros, q]
            ref[0, hd] = jnp.concatenate(tile, axis=0).astype(ref.dtype)

    emit_gqa_q(qs_ref, za[_A_SQ:_A_GQ], lambda q: q * (HEAD_DIM ** -0.5 * LOG2E))
    ks_ref[0] = _rope(zb[:, 0:128], cos64_ref[...], sin64_ref[...], 16).astype(BF16)
    _emit_v_t(vs_ref, za[_A_SV:_A_GV], SWA_SLAB)

    emit_gqa_q(qg_ref, za[_A_GQ:_A_MQ], lambda q: _rms_t(q) * gqg_ref[...])
    gk = zb[:, 128:256]
    sq = gk * gk
    hi = sq.astype(BF16)
    lo = (sq - hi.astype(F32)).astype(BF16)
    ssum = (jnp.dot(hi, bd_ref[...], preferred_element_type=F32)
            + jnp.dot(lo, bd_ref[...], preferred_element_type=F32))
    gk = gk * lax.rsqrt(ssum * (1.0 / HEAD_DIM) + EPS) * gkg_ref[...]
    kg_ref[0] = _rope(gk, cos64_ref[...], sin64_ref[...], 16).astype(BF16)
    _emit_v_t(vg_ref, za[_A_GV:_A_MKV], TK)

    qn = _rms_t(za[_A_MQ:_A_SV]) * mqg_ref[...]
    mq = jnp.dot(wuqt_ref[...], qn.astype(BF16), preferred_element_type=F32)
    for hd in range(MLA_HEADS):
        t = mq[hd * LANES:(hd + 1) * LANES]
        r = _rope_t(t[MLA_NOPE:MLA_NOPE + MLA_ROPE], cosmt_ref[...], sinmt_ref[...], 8)
        t = jnp.concatenate([t[:MLA_NOPE], r, t[MLA_NOPE + MLA_ROPE:]], axis=0) * (mla_scale * LOG2E)
        qm_ref[0, hd] = t.astype(qm_ref.dtype)
    kvn = _rms(zb[:, 256:384]) * mkvg_ref[...]
    k_nope = jnp.dot(kvn.astype(BF16), wukvk_ref[...], preferred_element_type=F32)
    kr = _rope(zb[:, 384:512], cosm_ref[...], sinm_ref[...], 8)
    km_ref[0] = (k_nope + jnp.concatenate([kr] * MLA_HEADS, axis=1)).astype(BF16)
    kvn_t = _rms_t(za[_A_MKV:_WA_ROWS]) * mkvgt_ref[...]
    _emit_v_t(vm_ref, jnp.dot(wukvvt_ref[...], kvn_t.astype(BF16), preferred_element_type=F32), TK)


def _input_projection(stream, t_all, modtab, g_mix, lw, tabs, n_ctx_blk):
    bsz, _, d = stream[0].shape
    nblk = t_all // TQ
    tok = lambda w: pl.BlockSpec((1, TQ, w), lambda b, i: (b, i, 0))
    q_t = lambda nh: pl.BlockSpec((1, nh, LANES, TQ), lambda b, i: (b, 0, 0, i))
    v_t = lambda nh, slab: pl.BlockSpec((1, TQ // slab, nh * V_ROWS, slab), lambda b, i: (b, i, 0, 0))
    full = lambda a: pl.BlockSpec(a.shape, lambda b, i: (0,) * a.ndim)
    tab = pl.BlockSpec((TQ, LANES), lambda b, i: (i, 0))
    q_shape = lambda nh: jax.ShapeDtypeStruct((bsz, nh, LANES, t_all), BF16)
    k_shape = lambda w: jax.ShapeDtypeStruct((bsz, t_all, w), BF16)
    v_shape = lambda nh, slab: jax.ShapeDtypeStruct((bsz, t_all // slab, nh * V_ROWS, slab), BF16)
    tab_t = lambda a: pl.BlockSpec((a.shape[0], TQ), lambda b, i: (0, i))
    consts = (lw["gqg_t"], lw["gkg"], lw["mqg_t"], lw["mkvg"], lw["mkvg_t"], lw["wuq_t"], lw["wukv_k"],
              lw["wukv_vt"], lw["bd"])
    return pl.pallas_call(
        functools.partial(_in_kernel, mla_scale=float((MLA_NOPE + MLA_ROPE) ** -0.5), n_ctx_blk=n_ctx_blk),
        out_shape=(q_shape(GQA_HEADS), k_shape(128), v_shape(GQA_KV, SWA_SLAB),
                   q_shape(GQA_HEADS), k_shape(128), v_shape(GQA_KV, TK),
                   q_shape(MLA_HEADS), k_shape(512), v_shape(MLA_HEADS, TK)),
        grid=(bsz, nblk),
        in_specs=_stream_specs(stream, n_ctx_blk, 0)
                 + [pl.BlockSpec((1, 1, 8, d), lambda b, i: (b, jnp.where(i < n_ctx_blk, 1, 0), 0, 0)),
                    full(g_mix), full(lw["wa"]), full(lw["wb"]), tab, tab, tab, tab]
                 + [tab_t(a) for a in tabs[4:]] + [full(a) for a in consts],
        out_specs=(q_t(GQA_HEADS), tok(128), v_t(GQA_KV, SWA_SLAB),
                   q_t(GQA_HEADS), tok(128), v_t(GQA_KV, TK),
                   q_t(MLA_HEADS), tok(512), v_t(MLA_HEADS, TK)),
        compiler_params=_cparams(2),
        name="in_proj",
    )(stream[0], stream[2], modtab, g_mix, lw["wa"], lw["wb"], *tabs, *consts)


_N_IN = {"swa": 5, "glb": 3, "mla": 3}
_N_SCRATCH = {"swa": 6, "glb": 12, "mla": 12}
_OUT_COL = {"swa": 0, "glb": GQA_HEADS * HEAD_DIM, "mla": 2 * GQA_HEADS * HEAD_DIM}


def _attn_kernel(*refs, n_ctx, n_ctx_blk, q_blk_off, seq):
    kinds = ("swa", "glb", "mla")
    n_in = sum(_N_IN[k] for k in kinds)
    o_ref = refs[n_in]
    i0, s0 = 0, n_in + 1
    for kind in kinds:
        _attn_group(refs[i0:i0 + _N_IN[kind]], o_ref, refs[s0:s0 + _N_SCRATCH[kind]], kind=kind, n_ctx=n_ctx,
                    n_ctx_blk=n_ctx_blk, q_blk_off=q_blk_off, seq=seq)
        i0 += _N_IN[kind]
        s0 += _N_SCRATCH[kind]


def _attn_group(ins, o_ref, scratch, *, kind, n_ctx, n_ctx_blk, q_blk_off, seq):
    if kind == "swa":
        sink_ref, bias_ref, q_ref, k_ref, v_ref = ins
        acc_ref, s_refs, p_refs = scratch[0], scratch[1:4], scratch[4:6]
    else:
        q_ref, k_ref, v_ref = ins
        acc_ref, m_ref = scratch[:2]
        s_refs, cm_refs, p_refs, al_refs = scratch[2:5], scratch[5:8], scratch[8:10], scratch[10:12]
    blk = pl.program_id(1) + q_blk_off
    is_lat = blk >= n_ctx_blk
    gqa = kind in ("swa", "glb")
    n_heads = GQA_HEADS if gqa else MLA_HEADS
    n_units = GQA_KV if gqa else MLA_HEADS
    unit_w = n_heads // n_units * TQ

    if kind == "swa":
        nk = TQ + 2 * WINDOW
        n_slab = n_ctx // SWA_SLAB
        row_l = lax.broadcasted_iota(jnp.int32, (V_ROWS, TQ), 0) == HEAD_DIM

        def scores(h, parts):
            cm = None
            for k_rows, _, bias, r0, nr in parts:
                s = jnp.dot(k_rows(), q_ref[0, h], preferred_element_type=F32)
                if bias is not None:
                    s = s + bias()
                s_refs[h % 3][r0:r0 + nr, :] = s
                c = jnp.max(s, axis=0, keepdims=True)
                cm = c if cm is None else jnp.maximum(cm, c)
            return cm

        def softmax(h, cm, nrows):
            m = jnp.maximum(cm, sink_ref[h])
            p_refs[h % 2][0:nrows, :] = jnp.exp2((s_refs[h % 3][0:nrows, :] - m).astype(BF16))
            return jnp.exp2(sink_ref[h] - m)

        def values(h, e_sink, parts):
            u, j = divmod(h, GQA_G)
            acc = None
            for _, v_t, _, r0, nr in parts:
                a = jnp.dot(v_t(u), p_refs[h % 2][r0:r0 + nr, :], preferred_element_type=F32)
                acc = a if acc is None else acc + a
            acc_ref[u, :, j * TQ:(j + 1) * TQ] = acc + jnp.where(row_l, e_sink, 0.0)

        def run(parts):
            nrows = sum(p[4] for p in parts)
            cms = {h: scores(h, parts) for h in range(min(2, n_heads))}
            e_sinks = {}
            for h in range(n_heads + 1):
                if h + 2 < n_heads:
                    cms[h + 2] = scores(h + 2, parts)
                if h < n_heads:
                    e_sinks[h] = softmax(h, cms[h], nrows)
                if h >= 1:
                    values(h - 1, e_sinks[h - 1], parts)

        ctx_part = (lambda: k_ref[0, 0:n_ctx, :],
                    lambda u: jnp.concatenate([v_ref[0, s, u * V_ROWS:(u + 1) * V_ROWS, :] for s in range(n_slab)],
                                              axis=1),
                    None, 0, n_ctx)

        def latent():
            q0 = (blk - n_ctx_blk) * TQ
            k0 = pl.multiple_of(jnp.clip(q0 - WINDOW, 0, seq - nk), WINDOW)
            geom = (q0 - k0) // WINDOW
            s0 = (n_ctx + k0) // SWA_SLAB
            win_part = (lambda: k_ref[0, pl.ds(pl.multiple_of(n_ctx + k0, WINDOW), nk), :],
                        lambda u: jnp.concatenate([v_ref[0, s0 + s, u * V_ROWS:(u + 1) * V_ROWS, :]
                                                   for s in range(nk // SWA_SLAB)], axis=1),
                        lambda: bias_ref[geom], n_ctx, nk)
            run([ctx_part, win_part])

        if q_blk_off < n_ctx_blk:
            pl.when(jnp.logical_not(is_lat))(lambda: run([ctx_part]))
            pl.when(is_lat)(latent)
        else:
            latent()
    else:
        m_ref[...] = jnp.full(m_ref.shape, NEG_INF, F32)
        acc_ref[...] = jnp.zeros(acc_ref.shape, F32)

        def qk(c, par, h):
            lanes = slice(h * TQ, (h + 1) * TQ)
            k_cols = slice(0, LANES) if gqa else slice(h * LANES, (h + 1) * LANES)
            k_rows = k_ref[0, pl.ds(pl.multiple_of(c * TK, TK), TK), k_cols]
            s = jnp.dot(k_rows, q_ref[0, h], preferred_element_type=F32)
            s_refs[par][:, lanes] = s
            cm_refs[par][:, lanes] = jnp.max(s, axis=0, keepdims=True)

        def softmax(sb, par, h):
            lanes = slice(h * TQ, (h + 1) * TQ)
            m_prev = m_ref[:, lanes]
            m_new = jnp.maximum(m_prev, cm_refs[sb][:, lanes])
            m_ref[:, lanes] = m_new
            al_refs[par][:, lanes] = jnp.exp2(m_prev - m_new)
            p_refs[par][:, lanes] = jnp.exp2((s_refs[sb][:, lanes] - m_new).astype(BF16))

        def pv(c, par, h):
            lanes = slice(h * TQ, (h + 1) * TQ)
            u, j = divmod(h, n_heads // n_units)
            ul = slice(j * TQ, (j + 1) * TQ)
            acc_ref[u, :, ul] = (acc_ref[u, :, ul] * al_refs[par][:, lanes]
                                 + jnp.dot(v_ref[0, c, u * V_ROWS:(u + 1) * V_ROWS, :], p_refs[par][:, lanes],
                                           preferred_element_type=F32))

        def step(i, r, n):
            for h in range(n_heads):
                if not isinstance(i, int) or i + 2 < n:
                    qk(i + 2, (r + 2) % 3, h)
                if not isinstance(i, int) or i < n:
                    softmax(r % 3, r % 2, h)
                if not isinstance(i, int) or i >= 1:
                    pv(i - 1, (r + 1) % 2, h)

        def run(n):
            for c in range(min(2, n)):
                for h in range(n_heads):
                    qk(c, c, h)
            step(0, 0, n)
            n_groups = max(0, (n - 3) // 6)

            def body(j, carry):
                for r in range(6):
                    step(1 + 6 * j + r, (1 + r) % 6, n)
                return carry
            if n_groups:
                lax.fori_loop(0, n_groups, body, 0)
            for i in range(1 + 6 * n_groups, n + 1):
                step(i, i % 6, n)

        if q_blk_off < n_ctx_blk:
            pl.when(jnp.logical_not(is_lat))(lambda: run(n_ctx // TK))
            pl.when(is_lat)(lambda: run((n_ctx + seq) // TK))
        else:
            run((n_ctx + seq) // TK)

    heads = []
    for u in range(n_units):
        a = acc_ref[u]
        o = a[:HEAD_DIM] / a[HEAD_DIM:HEAD_DIM + 1]
        heads += [o[:, j * TQ:(j + 1) * TQ] for j in range(unit_w // TQ)]
    tiles = [jnp.concatenate(heads[2 * t:2 * t + 2], axis=0).T for t in range(n_heads // 2)]
    c0 = _OUT_COL[kind]
    o_ref[0, :, c0:c0 + n_heads * HEAD_DIM] = jnp.concatenate(tiles, axis=1).astype(o_ref.dtype)


def _attention(groups, sink, n_ctx, with_ctx_queries):
    bsz, _, _, t_all = groups["swa"][0].shape
    seq = t_all - n_ctx
    n_ctx_blk = n_ctx // TQ
    q_blk_off = 0 if with_ctx_queries else n_ctx_blk
    nblk = t_all // TQ - q_blk_off
    in_specs, args, scratch = [], [], []
    for kind in ("swa", "glb", "mla"):
        q_t, k, v_t = groups[kind]
        n_heads = q_t.shape[1]
        n_units = GQA_KV if kind in ("swa", "glb") else MLA_HEADS
        nq = n_heads * TQ
        if kind == "swa":
            nk = TQ + 2 * WINDOW
            off = np.arange(TQ)[None, None, :] + WINDOW * np.arange(3)[:, None, None] - np.arange(nk)[None, :, None]
            bias = jnp.asarray(np.where(np.abs(off) <= WINDOW, 0.0, NEG_INF), F32)
            in_specs += [pl.BlockSpec(memory_space=pltpu.SMEM), pl.BlockSpec(bias.shape, lambda b, i: (0, 0, 0))]
            args += [sink, bias]
        in_specs += [pl.BlockSpec((1, n_heads, LANES, TQ), lambda b, i: (b, 0, 0, i + q_blk_off)),
                     pl.BlockSpec((1,) + k.shape[1:], lambda b, i: (b, 0, 0)),
                     pl.BlockSpec((1,) + v_t.shape[1:], lambda b, i: (b, 0, 0, 0))]
        args += [q_t, k, v_t]
        scratch += [pltpu.VMEM((n_units, V_ROWS, nq // n_units), F32)]
        if kind == "swa":
            rows = n_ctx + nk
            scratch += [pltpu.VMEM((rows, TQ), F32)] * 3 + [pltpu.VMEM((rows, TQ), BF16)] * 2
        else:
            scratch += ([pltpu.VMEM((1, nq), F32)] + [pltpu.VMEM((TK, nq), F32)] * 3 + [pltpu.VMEM((1, nq), F32)] * 3
                        + [pltpu.VMEM((TK, nq), BF16)] * 2 + [pltpu.VMEM((1, nq), F32)] * 2)
    return pl.pallas_call(
        functools.partial(_attn_kernel, n_ctx=n_ctx, n_ctx_blk=n_ctx_blk, q_blk_off=q_blk_off, seq=seq),
        out_shape=jax.ShapeDtypeStruct((bsz, nblk * TQ, D_MODEL), BF16),
        grid=(bsz, nblk),
        in_specs=in_specs,
        out_specs=pl.BlockSpec((1, TQ, D_MODEL), lambda b, i: (b, i, 0)),
        scratch_shapes=scratch,
        compiler_params=_cparams(2),
        name="attention",
    )(*args)


def _row_select(rows, idx):
    out = rows[0]
    for j in range(1, len(rows)):
        out = jnp.where(idx == j, rows[j], out)
    return out


def _route(scores, biased):
    def top2sum(a, b, c, d):
        hi1, lo1, hi2, lo2 = jnp.maximum(a, b), jnp.minimum(a, b), jnp.maximum(c, d), jnp.minimum(c, d)
        return jnp.maximum(hi1, hi2) + jnp.maximum(jnp.minimum(hi1, hi2), jnp.maximum(lo1, lo2))

    gs = [top2sum(*biased[4 * g:4 * g + 4]) for g in range(N_GROUPS)]
    best, gi = gs[0], jnp.zeros(gs[0].shape, jnp.int32)
    for g in range(1, N_GROUPS):
        better = gs[g] > best
        gi = jnp.where(better, g, gi)
        best = jnp.where(better, gs[g], best)
    a = [_row_select([biased[4 * g + j] for g in range(N_GROUPS)], gi) for j in range(EXPERTS_PER_GROUP)]
    s = [_row_select([scores[4 * g + j] for g in range(N_GROUPS)], gi) for j in range(EXPERTS_PER_GROUP)]
    v1, i1 = a[0], jnp.zeros(gi.shape, jnp.int32)
    for j in range(1, EXPERTS_PER_GROUP):
        better = a[j] > v1
        i1 = jnp.where(better, j, i1)
        v1 = jnp.where(better, a[j], v1)
    v2, i2 = jnp.full(v1.shape, -3.0e38, F32), jnp.zeros(gi.shape, jnp.int32)
    for j in range(EXPERTS_PER_GROUP):
        better = (i1 != j) & (a[j] > v2)
        i2 = jnp.where(better, j, i2)
        v2 = jnp.where(better, a[j], v2)
    lo, hi = jnp.minimum(i1, i2), jnp.maximum(i1, i2)
    pair = jnp.where(lo == 0, hi - 1, jnp.where(lo == 1, hi + 1, 5))
    s_lo, s_hi = _row_select(s, lo), _row_select(s, hi)
    den = s_lo + s_hi
    return gi * N_PAIRS + pair, s_lo / den, s_hi / den


def _out_kernel(y_ref, x_ref, c_ref, mod_ref, wout_ref, g_ref, rwh_ref, rwl_ref, rb_ref, tri_ref,
                xo_ref, row_ref, meta_ref, cnt_ref, carry_ref, *, n_ctx_blk, blk_off):
    @pl.when((pl.program_id(0) == 0) & (pl.program_id(1) == 0))
    def _():
        carry_ref[...] = jnp.zeros(carry_ref.shape, F32)

    y = y_ref[0]
    mod = mod_ref[0, 0]
    half = D_MODEL // 2
    a = jnp.concatenate([jnp.dot(y, wout_ref[:, :half], preferred_element_type=F32),
                         jnp.dot(y, wout_ref[:, half:], preferred_element_type=F32)], axis=1)
    x = jnp.where(pl.program_id(1) + blk_off < n_ctx_blk, c_ref[0], x_ref[0]) + mod[2:3] * a
    xo_ref[0] = x
    h = _rms(x) * g_ref[...]
    h = h * (1.0 + mod[4:5]) + mod[3:4]

    hh = h.astype(BF16)
    hl = (h - hh.astype(F32)).astype(BF16)
    logits = (jnp.dot(hh, rwh_ref[...], preferred_element_type=F32)
              + jnp.dot(hl, rwh_ref[...], preferred_element_type=F32)
              + jnp.dot(hh, rwl_ref[...], preferred_element_type=F32)).T[:32]
    sc = jax.nn.sigmoid(logits)
    bs = sc + rb_ref[...]
    scores = [sc[e:e + 1, :] for e in range(N_EXPERTS)]
    biased = [bs[e:e + 1, :] for e in range(N_EXPERTS)]
    bucket, g_lo, g_hi = _route(scores, biased)

    onehot = jnp.where(lax.broadcasted_iota(jnp.int32, (32, TQ), 0) == bucket, 1.0, 0.0)
    prefix = jnp.dot(onehot.astype(BF16), tri_ref[...], preferred_element_type=F32)
    carry = carry_ref[:, 0:1]
    rank = jnp.sum(onehot * (carry + prefix - 1.0), axis=0, keepdims=True)
    carry_new = jnp.broadcast_to(carry + prefix[:, TQ - 1:TQ], carry_ref.shape)
    carry_ref[...] = carry_new
    cnt_ref[...] = carry_new
    meta_ref[0] = jnp.concatenate([bucket, rank.astype(jnp.int32), jnp.zeros((6, TQ), jnp.int32)], axis=0)

    gates = jnp.concatenate([g_lo, g_hi, jnp.zeros((LANES - 2, TQ), F32)], axis=0)
    row_ref[...] = jnp.concatenate([h, gates.T], axis=1)


def _output_projection(y_mix, stream, t_all, modtab, lw, shared, n_ctx_blk, with_ctx):
    bsz, _, d = stream[0].shape
    off = 0 if with_ctx else n_ctx_blk
    nblk = t_all // TQ - off
    tok = lambda w: pl.BlockSpec((1, TQ, w), lambda b, i: (b, i, 0))
    full = lambda a: pl.BlockSpec(a.shape, lambda b, i: (0,) * a.ndim)
    consts = (lw["wout"], lw["g_ffn"], shared["rw_hi"], shared["rw_lo"], shared["rb"], shared["tri"])
    return pl.pallas_call(
        functools.partial(_out_kernel, n_ctx_blk=n_ctx_blk, blk_off=off),
        out_shape=(jax.ShapeDtypeStruct((bsz, nblk * TQ, d), F32),
                   jax.ShapeDtypeStruct((bsz * nblk * TQ, ROW_W), F32),
                   jax.ShapeDtypeStruct((bsz * nblk, 8, TQ), jnp.int32),
                   jax.ShapeDtypeStruct((32, LANES), F32)),
        grid=(bsz, nblk),
        in_specs=[tok(d)] + _stream_specs(stream, n_ctx_blk, off)
                 + [pl.BlockSpec((1, 1, 8, d), lambda b, i: (b, jnp.where(i + off < n_ctx_blk, 1, 0), 0, 0))]
                 + [full(a) for a in consts],
        out_specs=(tok(d),
                   pl.BlockSpec((TQ, ROW_W), lambda b, i: (b * nblk + i, 0)),
                   pl.BlockSpec((1, 8, TQ), lambda b, i: (b * nblk + i, 0, 0)),
                   pl.BlockSpec((32, LANES), lambda b, i: (0, 0))),
        scratch_shapes=[pltpu.VMEM((32, LANES), F32)],
        compiler_params=_cparams(2),
        name="out_proj_router",
    )(y_mix, stream[0], stream[2], modtab, *consts)


def _scatter_kernel(dest_ref, fill_ref, nv_ref, row_ref, out_ref, buf, sem, *, n_steps, n_blocks):
    i = pl.program_id(0)
    slot = i % 2

    def wait_slot(s):
        pltpu.make_async_copy(buf.at[s], out_ref.at[pl.ds(0, TS)], sem.at[s]).wait()

    @pl.when(i == 0)
    def _():
        buf[1, 0:TM] = jnp.zeros((TM, ROW_W), F32)

        def zero_block(blk):
            return pltpu.make_async_copy(buf.at[1, pl.ds(0, TM)], out_ref.at[pl.ds(pl.multiple_of(blk * TM, TM), TM)],
                                         sem.at[1])

        for wait in (False, True):
            for b in range(N_BUCKETS):
                @pl.when(fill_ref[b] >= 0)
                def _(b=b, wait=wait):
                    zero_block(fill_ref[b]).wait() if wait else zero_block(fill_ref[b]).start()

            def tail(blk, carry, wait=wait):
                zero_block(blk).wait() if wait else zero_block(blk).start()
                return carry
            lax.fori_loop(nv_ref[0], n_blocks, tail, 0)

    for s in range(2):
        @pl.when(slot == s)
        def _(s=s):
            @pl.when(i >= 2)
            def _():
                wait_slot(s)

            buf[s] = row_ref[...]
            for r in range(TS):
                d = dest_ref[i * TS + r]
                pltpu.make_async_copy(buf.at[s, pl.ds(r, 1)], out_ref.at[pl.ds(d, 1)], sem.at[s]).start()

    @pl.when(i == n_steps - 1)
    def _():
        wait_slot(slot)
        if n_steps >= 2:
            wait_slot(1 - slot)


def _scatter_rows(dest, fill_blk, n_valid, rows, n_sorted):
    n_tok = rows.shape[0]
    assert n_tok % TS == 0
    n_steps = n_tok // TS
    return pl.pallas_call(
        functools.partial(_scatter_kernel, n_steps=n_steps, n_blocks=n_sorted // TM),
        out_shape=jax.ShapeDtypeStruct((n_sorted, ROW_W), F32),
        grid_spec=pltpu.PrefetchScalarGridSpec(
            num_scalar_prefetch=3,
            grid=(n_steps,),
            in_specs=[pl.BlockSpec((TS, ROW_W), lambda i, d, f, nv: (i, 0))],
            out_specs=pl.BlockSpec(memory_space=pl.ANY),
            scratch_shapes=[pltpu.VMEM((2, TS, ROW_W), F32), pltpu.SemaphoreType.DMA((2,))]),
        compiler_params=_cparams(1),
        name="moe_scatter",
    )(dest, fill_blk, n_valid, rows)


def _swiglu(h, w_gate, w_up, w_down):
    a = jnp.dot(h, w_gate, preferred_element_type=F32)
    a = a * jax.nn.sigmoid(a) * jnp.dot(h, w_up, preferred_element_type=F32)
    return jnp.dot(a.astype(BF16), w_down, preferred_element_type=F32)


def _moe_kernel(elo_ref, ehi_ref, nv_ref, row_ref, gl_ref, ul_ref, dl_ref, gh_ref, uh_ref, dh_ref,
                sg_ref, su_ref, sd_ref, y_ref):
    del elo_ref, ehi_ref
    j = pl.program_id(0)

    @pl.when(j < nv_ref[0])
    def _():
        rows = row_ref[...]
        h = rows[:, :D_MODEL].astype(BF16)
        g_lo = rows[:, D_MODEL:D_MODEL + 1]
        g_hi = rows[:, D_MODEL + 1:D_MODEL + 2]
        y_ref[...] = (g_lo * _swiglu(h, gl_ref[0, 0], ul_ref[0, 0], dl_ref[0, 0])
                      + g_hi * _swiglu(h, gh_ref[0, 0], uh_ref[0, 0], dh_ref[0, 0])
                      + _swiglu(h, sg_ref[0], su_ref[0], sd_ref[0]))

    @pl.when(j >= nv_ref[0])
    def _():
        y_ref[...] = jnp.zeros(y_ref.shape, F32)


def _grouped_experts(e_lo, e_hi, n_valid, rows_sorted, ew, layer):
    n_sorted = rows_sorted.shape[0]
    nb = n_sorted // TM
    routed = lambda w, tab: pl.BlockSpec((1, 1) + w.shape[2:], lambda j, lo, hi, nv: (layer, (lo, hi)[tab][j], 0, 0))
    shared = lambda w: pl.BlockSpec((1,) + w.shape[1:], lambda j, lo, hi, nv: (layer, 0, 0))
    return pl.pallas_call(
        _moe_kernel,
        out_shape=jax.ShapeDtypeStruct((n_sorted, D_MODEL), F32),
        grid_spec=pltpu.PrefetchScalarGridSpec(
            num_scalar_prefetch=3,
            grid=(nb,),
            in_specs=[pl.BlockSpec((TM, ROW_W), lambda j, lo, hi, nv: (j, 0))]
                     + [routed(ew[k], tab) for tab in (0, 1) for k in ("gate", "up", "down")]
                     + [shared(ew[k]) for k in ("s_gate", "s_up", "s_down")],
            out_specs=pl.BlockSpec((TM, D_MODEL), lambda j, lo, hi, nv: (j, 0))),
        compiler_params=_cparams(1),
        name="moe_experts",
    )(e_lo, e_hi, n_valid, rows_sorted, *(ew[k] for k in ("gate", "up", "down")) , *(ew[k] for k in ("gate", "up", "down")),
      *(ew[k] for k in ("s_gate", "s_up", "s_down")))


def _gather_kernel(dest_ref, y_ref, x_ref, mod_ref, gf_ref, o_ref, fbuf, sem, *, n_steps, final_norm, tr):
    i = pl.program_id(0)
    slot = i % 2

    def issue(step, s):
        for r in range(tr):
            d = dest_ref[step * tr + r]
            pltpu.make_async_copy(y_ref.at[pl.ds(d, 1)], fbuf.at[s, pl.ds(r, 1)], sem.at[s]).start()

    @pl.when(i == 0)
    def _():
        issue(0, 0)

    for s in range(2):
        @pl.when((i + 1 < n_steps) & (slot == s))
        def _(s=s):
            issue(i + 1, 1 - s)

    pltpu.make_async_copy(y_ref.at[pl.ds(0, tr)], fbuf.at[slot], sem.at[slot]).wait()
    x = x_ref[0] + mod_ref[0, 0][5:6] * fbuf[slot]
    if final_norm:
        x = _rms(x) * gf_ref[...]
    o_ref[0] = x


def _gather_residual(dest, y_sorted, x_mid, modtab, g_final, n_ctx_blk, with_ctx, final_norm):
    bsz, t_rows, d = x_mid.shape
    tr = TS if (not with_ctx and t_rows % TS == 0) else TQ
    nblk = t_rows // tr
    off = 0 if with_ctx else n_ctx_blk
    n_steps = bsz * nblk
    return pl.pallas_call(
        functools.partial(_gather_kernel, n_steps=n_steps, final_norm=final_norm, tr=tr),
        out_shape=jax.ShapeDtypeStruct((bsz, t_rows, d), F32),
        grid_spec=pltpu.PrefetchScalarGridSpec(
            num_scalar_prefetch=1,
            grid=(n_steps,),
            in_specs=[pl.BlockSpec(memory_space=pl.ANY),
                      pl.BlockSpec((1, tr, d), lambda i, dst: (i // nblk, i % nblk, 0)),
                      pl.BlockSpec((1, 1, 8, d),
                                   lambda i, dst: (i // nblk, jnp.where(i % nblk * (tr // TQ) + off < n_ctx_blk, 1, 0), 0, 0)),
                      pl.BlockSpec(g_final.shape, lambda i, dst: (0, 0))],
            out_specs=pl.BlockSpec((1, tr, d), lambda i, dst: (i // nblk, i % nblk, 0)),
            scratch_shapes=[pltpu.VMEM((2, tr, d), F32), pltpu.SemaphoreType.DMA((2,))]),
        compiler_params=_cparams(1),
        name="moe_gather",
    )(dest, y_sorted, x_mid, modtab, g_final)


def _bucket_layout(meta, counts, n_tok):
    bucket = meta[:, 0, :].reshape(-1)
    rank = meta[:, 1, :].reshape(-1)
    cnt = counts[:N_BUCKETS, 0].astype(jnp.int32)
    padded = (cnt + TM - 1) // TM * TM
    pad_end = jnp.cumsum(padded)
    pad_start = pad_end - padded
    dest = pad_start[bucket] + rank
    nb = n_tok // TM + N_BUCKETS
    n_valid = pad_end[-1] // TM
    blk = jnp.arange(nb, dtype=jnp.int32)
    blk_row = jnp.minimum(blk, n_valid - 1) * TM
    blk_bucket = jnp.minimum(jnp.sum(pad_end[None, :] <= blk_row[:, None], axis=1), N_BUCKETS - 1).astype(jnp.int32)
    grp, pair = blk_bucket // N_PAIRS, blk_bucket % N_PAIRS
    e_lo = grp * EXPERTS_PER_GROUP + jnp.asarray(_PAIR_LO, jnp.int32)[pair]
    e_hi = grp * EXPERTS_PER_GROUP + jnp.asarray(_PAIR_HI, jnp.int32)[pair]
    fill_blk = jnp.where(padded > 0, pad_end // TM - 1, -1).astype(jnp.int32)
    return dest.astype(jnp.int32), fill_blk, e_lo, e_hi, n_valid.reshape(1).astype(jnp.int32), nb * TM


def _rope_tables(seq, n_ctx):
    t = jnp.arange(seq)
    rows, cols = t // GRID_W, t % GRID_W

    def table(width, nf):
        lane = np.arange(width)
        half, j = lane // (2 * nf), lane % (2 * nf)
        inv = ROPE_THETA ** (-jnp.arange(nf, dtype=F32) / nf)
        pos = jnp.where(jnp.asarray(half == 0)[None, :], rows[:, None], cols[:, None]).astype(F32)
        ang = pos * inv[j % nf][None, :]
        sign = jnp.asarray(np.where(j < nf, -1.0, 1.0), F32)[None, :]
        return jnp.cos(ang), jnp.sin(ang) * sign

    cos64, sin64 = table(HEAD_DIM, 16)
    cos64, sin64 = jnp.tile(cos64, (1, 2)), jnp.tile(sin64, (1, 2))
    cosr, sinr = table(MLA_ROPE, 8)
    cosm = jnp.concatenate([jnp.ones((seq, MLA_NOPE), F32), cosr, jnp.ones((seq, 32), F32)], axis=1)
    sinm = jnp.concatenate([jnp.zeros((seq, MLA_NOPE), F32), sinr, jnp.zeros((seq, 32), F32)], axis=1)
    ctx1, ctx0 = jnp.ones((n_ctx, LANES), F32), jnp.zeros((n_ctx, LANES), F32)
    token_major = [jnp.concatenate([c, a], axis=0) for c, a in ((ctx1, cos64), (ctx0, sin64), (ctx1, cosm), (ctx0, sinm))]
    dim_major = [jnp.concatenate([jnp.full((a.shape[1], n_ctx), fill, F32), a.T], axis=1)
                 for a, fill in ((cos64[:, :HEAD_DIM], 1.0), (sin64[:, :HEAD_DIM], 0.0), (cosr, 1.0), (sinr, 0.0))]
    return tuple(token_major + dim_major)


def _layer_weights(l, w_in, w_out, norm_ffn_g, glb_q_gain, glb_k_gain, mla_q_gain, mla_w_uq, mla_kv_gain,
                   mla_w_ukv):
    d = w_in.shape[1]
    wi = w_in[l]
    wa = jnp.concatenate([wi[:, _SQ:_SK], wi[:, _GQ:_GK], wi[:, _MQ:_MKV], wi[:, _SV:_GQ], wi[:, _GV:_MQ],
                          wi[:, _MKV:_KR]], axis=1).T
    kr_tile = jnp.concatenate([jnp.zeros((d, MLA_NOPE), F32), wi[:, _KR:], jnp.zeros((d, 32), F32)], axis=1)
    wb = jnp.concatenate([wi[:, _SK:_SV], wi[:, _GK:_GV], wi[:, _MKV:_KR], kr_tile], axis=1)
    uq = mla_w_uq[l].reshape(MLA_Q_RANK, MLA_HEADS, MLA_NOPE + MLA_ROPE)
    wuq = jnp.concatenate([uq, jnp.zeros((MLA_Q_RANK, MLA_HEADS, 32), F32)], axis=2).reshape(MLA_Q_RANK, 512)
    ukv = mla_w_ukv[l].reshape(MLA_KV_RANK, MLA_HEADS, MLA_NOPE + MLA_V)
    wk = jnp.concatenate([ukv[:, :, :MLA_NOPE], jnp.zeros((MLA_KV_RANK, MLA_HEADS, 64), F32)], axis=2)
    bcast = lambda g: jnp.broadcast_to(g[:, None], (g.shape[0], TQ))
    head = np.arange(LANES) // HEAD_DIM
    return {
        "wa": wa.astype(BF16),
        "wb": wb.astype(BF16),
        "wuq_t": wuq.T.astype(BF16),
        "wukv_k": wk.reshape(MLA_KV_RANK, 512).astype(BF16),
        "wukv_vt": ukv[:, :, MLA_NOPE:].reshape(MLA_KV_RANK, 256).T.astype(BF16),
        "gqg_t": bcast(glb_q_gain[l] * (HEAD_DIM ** -0.5 * LOG2E)),
        "gkg": jnp.tile(glb_k_gain[l], GQA_KV).reshape(1, LANES),
        "mqg_t": bcast(mla_q_gain[l]),
        "mkvg": mla_kv_gain[l].reshape(1, -1),
        "mkvg_t": bcast(mla_kv_gain[l]),
        "bd": jnp.asarray(head[:, None] == head[None, :], BF16),
        "wout": w_out[l].astype(BF16),
        "g_ffn": norm_ffn_g[l].reshape(1, -1),
    }


def kernel(x, c, ctx, c_ctx, w_mod, b_mod, norm_mix_g, norm_ffn_g, w_in, w_out, swa_sink, glb_q_gain, glb_k_gain,
           mla_q_gain, mla_w_uq, mla_kv_gain, mla_w_ukv, router_w, router_bias, exp_w_gate, exp_w_up, exp_w_down,
           shr_w_gate, shr_w_up, shr_w_down, final_norm_g):
    bsz, seq, d = x.shape
    n_ctx = ctx.shape[1]
    n_layers = w_mod.shape[0]
    assert d == D_MODEL and n_ctx % TQ == 0 and seq % TQ == 0 and seq >= TQ + 2 * WINDOW and seq % GRID_W == 0
    n_ctx_blk = n_ctx // TQ

    rows = -(-(bsz + 1) // 8) * 8
    c_rows = jnp.concatenate([c, c_ctx[None, :], jnp.zeros((rows - bsz - 1, d), F32)], axis=0)
    mods = _modulation(c_rows, w_mod, b_mod)
    mod_x = mods[:, :bsz].reshape(n_layers, bsz, 6, d)
    mod_c = jnp.broadcast_to(mods[:, bsz].reshape(n_layers, 1, 6, d), (n_layers, bsz, 6, d))
    modtabs = jnp.pad(jnp.stack([mod_x, mod_c], axis=2), ((0, 0), (0, 0), (0, 0), (0, 2), (0, 0)))

    tabs = _rope_tables(seq, n_ctx)
    rw = jnp.pad(router_w, ((0, 0), (0, LANES - N_EXPERTS)))
    rw_hi = rw.astype(BF16)
    shared = {
        "rw_hi": rw_hi,
        "rw_lo": (rw - rw_hi.astype(F32)).astype(BF16),
        "rb": jnp.pad(router_bias, (0, 32 - N_EXPERTS)).reshape(32, 1),
        "tri": jnp.asarray(np.arange(TQ)[:, None] <= np.arange(TQ)[None, :], BF16),
    }
    g_final = final_norm_g.reshape(1, d)
    ew = {"gate": exp_w_gate.astype(BF16), "up": exp_w_up.astype(BF16), "down": exp_w_down.astype(BF16),
          "s_gate": shr_w_gate.astype(BF16), "s_up": shr_w_up.astype(BF16), "s_down": shr_w_down.astype(BF16)}

    t_all = n_ctx + seq
    stream = (x, 0, ctx)
    for l in range(n_layers):
        last = l == n_layers - 1
        with_ctx = not last
        lw = _layer_weights(l, w_in, w_out, norm_ffn_g, glb_q_gain, glb_k_gain, mla_q_gain, mla_w_uq,
                            mla_kv_gain, mla_w_ukv)
        modtab = modtabs[l]
        qs, ks, vs, qg, kg, vg, qm, km, vm = _input_projection(
            stream, t_all, modtab, norm_mix_g[l].reshape(1, d), lw, tabs, n_ctx_blk)
        sink = jnp.pad(swa_sink[l] * LOG2E, (0, 8 - GQA_HEADS))
        y_mix = _attention({"swa": (qs, ks, vs), "glb": (qg, kg, vg), "mla": (qm, km, vm)}, sink, n_ctx, with_ctx)
        x_mid, rows_tok, meta, counts = _output_projection(y_mix, stream, t_all, modtab, lw, shared, n_ctx_blk,
                                                           with_ctx)
        n_tok = rows_tok.shape[0]
        dest, fill_blk, e_lo, e_hi, n_valid, n_sorted = _bucket_layout(meta, counts, n_tok)
        rows_sorted = _scatter_rows(dest, fill_blk, n_valid, rows_tok, n_sorted)
        y_sorted = _grouped_experts(e_lo, e_hi, n_valid, rows_sorted, ew, l)
        xs = _gather_residual(dest, y_sorted, x_mid, modtab, g_final, n_ctx_blk, with_ctx, final_norm=last)
        stream = (xs, n_ctx_blk, xs)
    return xs
```

```python
import functools

import jax
import jax.numpy as jnp
import numpy as np
from jax import lax
from jax.experimental import pallas as pl
from jax.experimental.pallas import tpu as pltpu

F32 = jnp.float32
BF16 = jnp.bfloat16

D_MODEL = 1024
GRID_W = 64
HEAD_DIM = 64
GQA_HEADS = 6
GQA_KV = 2
GQA_G = GQA_HEADS // GQA_KV
WINDOW = 128
MLA_HEADS = 4
MLA_NOPE = 64
MLA_ROPE = 32
MLA_V = 64
MLA_Q_RANK = 256
MLA_KV_RANK = 128
ROPE_THETA = 10000.0
N_EXPERTS = 16
N_GROUPS = 4
EXPERTS_PER_GROUP = 4
N_PAIRS = 6
N_BUCKETS = N_GROUPS * N_PAIRS
D_EXPERT = 512
EPS = 1e-6
NEG_INF = -1e30

LANES = 128
TQ = 256
TK = 256
TM = 256
TS = 512
SWA_SLAB = 128
V_ROWS = 80
LOG2E = 1.4426950408889634
ROW_W = D_MODEL + LANES
VMEM_LIMIT = 56 * 1024 * 1024

_SQ, _SK, _SV, _GQ, _GK, _GV, _MQ, _MKV, _KR = 0, 384, 512, 640, 1024, 1152, 1280, 1536, 1664
_A_SQ, _A_GQ, _A_MQ, _A_SV, _A_GV, _A_MKV, _WA_ROWS = 0, 384, 768, 1024, 1152, 1280, 1408

_PAIR_LO = (0, 0, 0, 1, 1, 2)
_PAIR_HI = (1, 2, 3, 2, 3, 3)


def _cparams(n_axes):
    return pltpu.CompilerParams(dimension_semantics=("arbitrary",) * n_axes,
                                vmem_limit_bytes=VMEM_LIMIT)


def _mod_kernel(c_ref, w_ref, b_ref, o_ref):
    c = c_ref[...]
    a = (c * jax.nn.sigmoid(c)).astype(BF16)
    o_ref[0] = jnp.dot(a, w_ref[0].astype(BF16), preferred_element_type=F32) + b_ref[0]


def _modulation(c_rows, w_mod, b_mod):
    n_layers, d, width = w_mod.shape
    rows = c_rows.shape[0]
    nb = 1536
    return pl.pallas_call(
        _mod_kernel,
        out_shape=jax.ShapeDtypeStruct((n_layers, rows, width), F32),
        grid=(n_layers, width // nb),
        in_specs=[pl.BlockSpec((rows, d), lambda l, j: (0, 0)),
                  pl.BlockSpec((1, d, nb), lambda l, j: (l, 0, j)),
                  pl.BlockSpec((1, 1, nb), lambda l, j: (l, 0, j))],
        out_specs=pl.BlockSpec((1, rows, nb), lambda l, j: (l, 0, j)),
        compiler_params=_cparams(2),
        name="adaln_mod",
    )(c_rows, w_mod, b_mod.reshape(n_layers, 1, width))


def _rope(x, cos, sin_signed, nf):
    lane = lax.broadcasted_iota(jnp.int32, (1, LANES), 1)
    first = (lane % (2 * nf)) < nf
    tiles = []
    for t in range(x.shape[1] // LANES):
        xt = x[:, t * LANES:(t + 1) * LANES]
        partner = jnp.where(first, pltpu.roll(xt, LANES - nf, 1), pltpu.roll(xt, nf, 1))
        tiles.append(xt * cos + partner * sin_signed)
    return tiles[0] if len(tiles) == 1 else jnp.concatenate(tiles, axis=1)


def _rope_t(x, cos_t, sin_t, nf):
    partner = jnp.concatenate([x[nf:2 * nf], x[:nf], x[3 * nf:], x[2 * nf:3 * nf]], axis=0)
    return x * cos_t + partner * sin_t


def _emit_v_t(ref, v_t, slab):
    ones_blk = jnp.where(lax.broadcasted_iota(jnp.int32, (V_ROWS - HEAD_DIM, TQ), 0) == 0, 1.0, 0.0)
    parts = []
    for h in range(v_t.shape[0] // HEAD_DIM):
        parts += [v_t[h * HEAD_DIM:(h + 1) * HEAD_DIM], ones_blk]
    ext = jnp.concatenate(parts, axis=0).astype(ref.dtype)
    for s in range(TQ // slab):
        ref[0, s] = ext[:, s * slab:(s + 1) * slab]


def _rms(x, eps=EPS):
    return x * lax.rsqrt(jnp.mean(x * x, axis=-1, keepdims=True) + eps)


def _stream_specs(stream, n_ctx_blk, blk_off):
    lat, lat_off, ctx = stream
    d = lat.shape[2]
    return [pl.BlockSpec((1, TQ, d), lambda b, i: (b, jnp.maximum(i + blk_off - n_ctx_blk, 0) + lat_off, 0)),
            pl.BlockSpec((1, TQ, d), lambda b, i: (b, jnp.minimum(i + blk_off, n_ctx_blk - 1), 0))]


def _rms_t(x, eps=EPS):
    return x * lax.rsqrt(jnp.mean(x * x, axis=0, keepdims=True) + eps)


def _in_kernel(x_ref, c_ref, mod_ref, g_ref, wa_ref, wb_ref, cos64_ref, sin64_ref, cosm_ref, sinm_ref,
               cos64t_ref, sin64t_ref, cosmt_ref, sinmt_ref, gqg_ref, gkg_ref, mqg_ref, mkvg_ref, mkvgt_ref,
               wuqt_ref, wukvk_ref, wukvvt_ref, bd_ref,
               qs_ref, ks_ref, vs_ref, qg_ref, kg_ref, vg_ref, qm_ref, km_ref, vm_ref, *, mla_scale, n_ctx_blk):
    x = jnp.where(pl.program_id(1) < n_ctx_blk, c_ref[0], x_ref[0])
    mod = mod_ref[0, 0]
    h = _rms(x) * g_ref[...]
    h = h * (1.0 + mod[1:2]) + mod[0:1]
    hb = h.astype(BF16)
    za = lax.dot_general(wa_ref[...], hb, (((1,), (1,)), ((), ())), preferred_element_type=F32)
    zb = jnp.dot(hb, wb_ref[...], preferred_element_type=F32)
    cos_t, sin_t = cos64t_ref[...], sin64t_ref[...]
    zeros = jnp.zeros((HEAD_DIM, TQ), F32)

    def emit_gqa_q(ref, q_t, prep):
        for hd in range(GQA_HEADS):
            q = prep(q_t[hd * HEAD_DIM:(hd + 1) * HEAD_DIM])
            q = _rope_t(q, cos_t, sin_t, 16)
            tile = [q, zeros] if hd // GQA_G == 0 else [zeros, q]
            ref[0, hd] = jnp.concatenate(tile, axis=0).astype(ref.dtype)

    emit_gqa_q(qs_ref, za[_A_SQ:_A_GQ], lambda q: q * (HEAD_DIM ** -0.5 * LOG2E))
    ks_ref[0] = _rope(zb[:, 0:128], cos64_ref[...], sin64_ref[...], 16).astype(BF16)
    _emit_v_t(vs_ref, za[_A_SV:_A_GV], SWA_SLAB)

    emit_gqa_q(qg_ref, za[_A_GQ:_A_MQ], lambda q: _rms_t(q) * gqg_ref[...])
    gk = zb[:, 128:256]
    sq = gk * gk
    hi = sq.astype(BF16)
    lo = (sq - hi.astype(F32)).astype(BF16)
    ssum = (jnp.dot(hi, bd_ref[...], preferred_element_type=F32)
            + jnp.dot(lo, bd_ref[...], preferred_element_type=F32))
    gk = gk * lax.rsqrt(ssum * (1.0 / HEAD_DIM) + EPS) * gkg_ref[...]
    kg_ref[0] = _rope(gk, cos64_ref[...], sin64_ref[...], 16).astype(BF16)
    _emit_v_t(vg_ref, za[_A_GV:_A_MKV], TK)

    qn = _rms_t(za[_A_MQ:_A_SV]) * mqg_ref[...]
    mq = jnp.dot(wuqt_ref[...], qn.astype(BF16), preferred_element_type=F32)
    for hd in range(MLA_HEADS):
        t = mq[hd * LANES:(hd + 1) * LANES]
        r = _rope_t(t[MLA_NOPE:MLA_NOPE + MLA_ROPE], cosmt_ref[...], sinmt_ref[...], 8)
        t = jnp.concatenate([t[:MLA_NOPE], r, t[MLA_NOPE + MLA_ROPE:]], axis=0) * (mla_scale * LOG2E)
        qm_ref[0, hd] = t.astype(qm_ref.dtype)
    kvn = _rms(zb[:, 256:384]) * mkvg_ref[...]
    k_nope = jnp.dot(kvn.astype(BF16), wukvk_ref[...], preferred_element_type=F32)
    kr = _rope(zb[:, 384:512], cosm_ref[...], sinm_ref[...], 8)
    km_ref[0] = (k_nope + jnp.concatenate([kr] * MLA_HEADS, axis=1)).astype(BF16)
    kvn_t = _rms_t(za[_A_MKV:_WA_ROWS]) * mkvgt_ref[...]
    _emit_v_t(vm_ref, jnp.dot(wukvvt_ref[...], kvn_t.astype(BF16), preferred_element_type=F32), TK)


def _input_projection(stream, t_all, modtab, g_mix, lw, tabs, n_ctx_blk):
    bsz, _, d = stream[0].shape
    nblk = t_all // TQ
    tok = lambda w: pl.BlockSpec((1, TQ, w), lambda b, i: (b, i, 0))
    q_t = lambda nh: pl.BlockSpec((1, nh, LANES, TQ), lambda b, i: (b, 0, 0, i))
    v_t = lambda nh, slab: pl.BlockSpec((1, TQ // slab, nh * V_ROWS, slab), lambda b, i: (b, i, 0, 0))
    full = lambda a: pl.BlockSpec(a.shape, lambda b, i: (0,) * a.ndim)
    tab = pl.BlockSpec((TQ, LANES), lambda b, i: (i, 0))
    q_shape = lambda nh: jax.ShapeDtypeStruct((bsz, nh, LANES, t_all), BF16)
    k_shape = lambda w: jax.ShapeDtypeStruct((bsz, t_all, w), BF16)
    v_shape = lambda nh, slab: jax.ShapeDtypeStruct((bsz, t_all // slab, nh * V_ROWS, slab), BF16)
    tab_t = lambda a: pl.BlockSpec((a.shape[0], TQ), lambda b, i: (0, i))
    consts = (lw["gqg_t"], lw["gkg"], lw["mqg_t"], lw["mkvg"], lw["mkvg_t"], lw["wuq_t"], lw["wukv_k"],
              lw["wukv_vt"], lw["bd"])
    return pl.pallas_call(
        functools.partial(_in_kernel, mla_scale=float((MLA_NOPE + MLA_ROPE) ** -0.5), n_ctx_blk=n_ctx_blk),
        out_shape=(q_shape(GQA_HEADS), k_shape(128), v_shape(GQA_KV, SWA_SLAB),
                   q_shape(GQA_HEADS), k_shape(128), v_shape(GQA_KV, TK),
                   q_shape(MLA_HEADS), k_shape(512), v_shape(MLA_HEADS, TK)),
        grid=(bsz, nblk),
        in_specs=_stream_specs(stream, n_ctx_blk, 0)
                 + [pl.BlockSpec((1, 1, 8, d), lambda b, i: (b, jnp.where(i < n_ctx_blk, 1, 0), 0, 0)),
                    full(g_mix), full(lw["wa"]), full(lw["wb"]), tab, tab, tab, tab]
                 + [tab_t(a) for a in tabs[4:]] + [full(a) for a in consts],
        out_specs=(q_t(GQA_HEADS), tok(128), v_t(GQA_KV, SWA_SLAB),
                   q_t(GQA_HEADS), tok(128), v_t(GQA_KV, TK),
                   q_t(MLA_HEADS), tok(512), v_t(MLA_HEADS, TK)),
        compiler_params=_cparams(2),
        name="in_proj",
    )(stream[0], stream[2], modtab, g_mix, lw["wa"], lw["wb"], *tabs, *consts)


_N_IN = {"swa": 5, "glb": 3, "mla": 3}
_N_SCRATCH = {"swa": 6, "glb": 12, "mla": 12}
_OUT_COL = {"swa": 0, "glb": GQA_HEADS * HEAD_DIM, "mla": 2 * GQA_HEADS * HEAD_DIM}


def _attn_kernel(*refs, n_ctx, n_ctx_blk, q_blk_off, seq):
    kinds = ("swa", "glb", "mla")
    n_in = sum(_N_IN[k] for k in kinds)
    o_ref = refs[n_in]
    i0, s0 = 0, n_in + 1
    for kind in kinds:
        _attn_group(refs[i0:i0 + _N_IN[kind]], o_ref, refs[s0:s0 + _N_SCRATCH[kind]], kind=kind, n_ctx=n_ctx,
                    n_ctx_blk=n_ctx_blk, q_blk_off=q_blk_off, seq=seq)
        i0 += _N_IN[kind]
        s0 += _N_SCRATCH[kind]


def _attn_group(ins, o_ref, scratch, *, kind, n_ctx, n_ctx_blk, q_blk_off, seq):
    if kind == "swa":
        sink_ref, bias_ref, q_ref, k_ref, v_ref = ins
        acc_ref, s_refs, p_refs = scratch[0], scratch[1:4], scratch[4:6]
    else:
        q_ref, k_ref, v_ref = ins
        acc_ref, m_ref = scratch[:2]
        s_refs, cm_refs, p_refs, al_refs = scratch[2:5], scratch[5:8], scratch[8:10], scratch[10:12]
    blk = pl.program_id(1) + q_blk_off
    is_lat = blk >= n_ctx_blk
    gqa = kind in ("swa", "glb")
    n_heads = GQA_HEADS if gqa else MLA_HEADS
    n_units = GQA_KV if gqa else MLA_HEADS
    unit_w = n_heads // n_units * TQ

    if kind == "swa":
        nk = TQ + 2 * WINDOW
        n_slab = n_ctx // SWA_SLAB
        row_l = lax.broadcasted_iota(jnp.int32, (V_ROWS, TQ), 0) == HEAD_DIM

        def scores(h, parts):
            cm = None
            for k_rows, _, bias, r0, nr in parts:
                s = jnp.dot(k_rows(), q_ref[0, h], preferred_element_type=F32)
                if bias is not None:
                    s = s + bias()
                s_refs[h % 3][r0:r0 + nr, :] = s
                c = jnp.max(s, axis=0, keepdims=True)
                cm = c if cm is None else jnp.maximum(cm, c)
            return cm

        def softmax(h, cm, nrows):
            m = jnp.maximum(cm, sink_ref[h])
            p_refs[h % 2][0:nrows, :] = jnp.exp2((s_refs[h % 3][0:nrows, :] - m).astype(BF16))
            return jnp.exp2(sink_ref[h] - m)

        def values(h, e_sink, parts):
            u, j = divmod(h, GQA_G)
            acc = None
            for _, v_t, _, r0, nr in parts:
                a = jnp.dot(v_t(u), p_refs[h % 2][r0:r0 + nr, :], preferred_element_type=F32)
                acc = a if acc is None else acc + a
            acc_ref[u, :, j * TQ:(j + 1) * TQ] = acc + jnp.where(row_l, e_sink, 0.0)

        def run(parts):
            nrows = sum(p[4] for p in parts)
            cms = {h: scores(h, parts) for h in range(min(2, n_heads))}
            e_sinks = {}
            for h in range(n_heads + 1):
                if h + 2 < n_heads:
                    cms[h + 2] = scores(h + 2, parts)
                if h < n_heads:
                    e_sinks[h] = softmax(h, cms[h], nrows)
                if h >= 1:
                    values(h - 1, e_sinks[h - 1], parts)

        ctx_part = (lambda: k_ref[0, 0:n_ctx, :],
                    lambda u: jnp.concatenate([v_ref[0, s, u * V_ROWS:(u + 1) * V_ROWS, :] for s in range(n_slab)],
                                              axis=1),
                    None, 0, n_ctx)

        def latent():
            q0 = (blk - n_ctx_blk) * TQ
            k0 = pl.multiple_of(jnp.clip(q0 - WINDOW, 0, seq - nk), WINDOW)
            geom = (q0 - k0) // WINDOW
            s0 = (n_ctx + k0) // SWA_SLAB
            win_part = (lambda: k_ref[0, pl.ds(pl.multiple_of(n_ctx + k0, WINDOW), nk), :],
                        lambda u: jnp.concatenate([v_ref[0, s0 + s, u * V_ROWS:(u + 1) * V_ROWS, :]
                                                   for s in range(nk // SWA_SLAB)], axis=1),
                        lambda: bias_ref[geom], n_ctx, nk)
            run([ctx_part, win_part])

        if q_blk_off < n_ctx_blk:
            pl.when(jnp.logical_not(is_lat))(lambda: run([ctx_part]))
            pl.when(is_lat)(latent)
        else:
            latent()
    else:
        m_ref[...] = jnp.full(m_ref.shape, NEG_INF, F32)
        acc_ref[...] = jnp.zeros(acc_ref.shape, F32)

        def qk(c, par, h):
            lanes = slice(h * TQ, (h + 1) * TQ)
            k_cols = slice(0, LANES) if gqa else slice(h * LANES, (h + 1) * LANES)
            k_rows = k_ref[0, pl.ds(pl.multiple_of(c * TK, TK), TK), k_cols]
            s = jnp.dot(k_rows, q_ref[0, h], preferred_element_type=F32)
            s_refs[par][:, lanes] = s
            cm_refs[par][:, lanes] = jnp.max(s, axis=0, keepdims=True)

        def softmax(sb, par, h):
            lanes = slice(h * TQ, (h + 1) * TQ)
            m_prev = m_ref[:, lanes]
            m_new = jnp.maximum(m_prev, cm_refs[sb][:, lanes])
            m_ref[:, lanes] = m_new
            al_refs[par][:, lanes] = jnp.exp2(m_prev - m_new)
            p_refs[par][:, lanes] = jnp.exp2((s_refs[sb][:, lanes] - m_new).astype(BF16))

        def pv(c, par, h):
            lanes = slice(h * TQ, (h + 1) * TQ)
            u, j = divmod(h, n_heads // n_units)
            ul = slice(j * TQ, (j + 1) * TQ)
            acc_ref[u, :, ul] = (acc_ref[u, :, ul] * al_refs[par][:, lanes]
                                 + jnp.dot(v_ref[0, c, u * V_ROWS:(u + 1) * V_ROWS, :], p_refs[par][:, lanes],
                                           preferred_element_type=F32))

        def step(i, r, n):
            for h in range(n_heads):
                if not isinstance(i, int) or i + 2 < n:
                    qk(i + 2, (r + 2) % 3, h)
                if not isinstance(i, int) or i < n:
                    softmax(r % 3, r % 2, h)
                if not isinstance(i, int) or i >= 1:
                    pv(i - 1, (r + 1) % 2, h)

        def run(n):
            for c in range(min(2, n)):
                for h in range(n_heads):
                    qk(c, c, h)
            step(0, 0, n)
            n_groups = max(0, (n - 3) // 6)

            def body(j, carry):
                for r in range(6):
                    step(1 + 6 * j + r, (1 + r) % 6, n)
                return carry
            if n_groups:
                lax.fori_loop(0, n_groups, body, 0)
            for i in range(1 + 6 * n_groups, n + 1):
                step(i, i % 6, n)

        if q_blk_off < n_ctx_blk:
            pl.when(jnp.logical_not(is_lat))(lambda: run(n_ctx // TK))
            pl.when(is_lat)(lambda: run((n_ctx + seq) // TK))
        else:
            run((n_ctx + seq) // TK)

    heads = []
    for u in range(n_units):
        a = acc_ref[u]
        o = a[:HEAD_DIM] / a[HEAD_DIM:HEAD_DIM + 1]
        heads += [o[:, j * TQ:(j + 1) * TQ] for j in range(unit_w // TQ)]
    tiles = [jnp.concatenate(heads[2 * t:2 * t + 2], axis=0).T for t in range(n_heads // 2)]
    c0 = _OUT_COL[kind]
    o_ref[0, :, c0:c0 + n_heads * HEAD_DIM] = jnp.concatenate(tiles, axis=1).astype(o_ref.dtype)


def _attention(groups, sink, n_ctx, with_ctx_queries):
    bsz, _, _, t_all = groups["swa"][0].shape
    seq = t_all - n_ctx
    n_ctx_blk = n_ctx // TQ
    q_blk_off = 0 if with_ctx_queries else n_ctx_blk
    nblk = t_all // TQ - q_blk_off
    in_specs, args, scratch = [], [], []
    for kind in ("swa", "glb", "mla"):
        q_t, k, v_t = groups[kind]
        n_heads = q_t.shape[1]
        n_units = GQA_KV if kind in ("swa", "glb") else MLA_HEADS
        nq = n_heads * TQ
        if kind == "swa":
            nk = TQ + 2 * WINDOW
            off = np.arange(TQ)[None, None, :] + WINDOW * np.arange(3)[:, None, None] - np.arange(nk)[None, :, None]
            bias = jnp.asarray(np.where(np.abs(off) <= WINDOW, 0.0, NEG_INF), F32)
            in_specs += [pl.BlockSpec(memory_space=pltpu.SMEM), pl.BlockSpec(bias.shape, lambda b, i: (0, 0, 0))]
            args += [sink, bias]
        in_specs += [pl.BlockSpec((1, n_heads, LANES, TQ), lambda b, i: (b, 0, 0, i + q_blk_off)),
                     pl.BlockSpec((1,) + k.shape[1:], lambda b, i: (b, 0, 0)),
                     pl.BlockSpec((1,) + v_t.shape[1:], lambda b, i: (b, 0, 0, 0))]
        args += [q_t, k, v_t]
        scratch += [pltpu.VMEM((n_units, V_ROWS, nq // n_units), F32)]
        if kind == "swa":
            rows = n_ctx + nk
            scratch += [pltpu.VMEM((rows, TQ), F32)] * 3 + [pltpu.VMEM((rows, TQ), BF16)] * 2
        else:
            scratch += ([pltpu.VMEM((1, nq), F32)] + [pltpu.VMEM((TK, nq), F32)] * 3 + [pltpu.VMEM((1, nq), F32)] * 3
                        + [pltpu.VMEM((TK, nq), BF16)] * 2 + [pltpu.VMEM((1, nq), F32)] * 2)
    return pl.pallas_call(
        functools.partial(_attn_kernel, n_ctx=n_ctx, n_ctx_blk=n_ctx_blk, q_blk_off=q_blk_off, seq=seq),
        out_shape=jax.ShapeDtypeStruct((bsz, nblk * TQ, D_MODEL), BF16),
        grid=(bsz, nblk),
        in_specs=in_specs,
        out_specs=pl.BlockSpec((1, TQ, D_MODEL), lambda b, i: (b, i, 0)),
        scratch_shapes=scratch,
        compiler_params=_cparams(2),
        name="attention",
    )(*args)


def _row_select(rows, idx):
    out = rows[0]
    for j in range(1, len(rows)):
        out = jnp.where(idx == j, rows[j], out)
    return out


def _route(scores, biased):
    def top2sum(a, b, c, d):
        hi1, lo1, hi2, lo2 = jnp.maximum(a, b), jnp.minimum(a, b), jnp.maximum(c, d), jnp.minimum(c, d)
        return jnp.maximum(hi1, hi2) + jnp.maximum(jnp.minimum(hi1, hi2), jnp.maximum(lo1, lo2))

    gs = [top2sum(*biased[4 * g:4 * g + 4]) for g in range(N_GROUPS)]
    best, gi = gs[0], jnp.zeros(gs[0].shape, jnp.int32)
    for g in range(1, N_GROUPS):
        better = gs[g] > best
        gi = jnp.where(better, g, gi)
        best = jnp.where(better, gs[g], best)
    a = [_row_select([biased[4 * g + j] for g in range(N_GROUPS)], gi) for j in range(EXPERTS_PER_GROUP)]
    s = [_row_select([scores[4 * g + j] for g in range(N_GROUPS)], gi) for j in range(EXPERTS_PER_GROUP)]
    v1, i1 = a[0], jnp.zeros(gi.shape, jnp.int32)
    for j in range(1, EXPERTS_PER_GROUP):
        better = a[j] > v1
        i1 = jnp.where(better, j, i1)
        v1 = jnp.where(better, a[j], v1)
    v2, i2 = jnp.full(v1.shape, -3.0e38, F32), jnp.zeros(gi.shape, jnp.int32)
    for j in range(EXPERTS_PER_GROUP):
        better = (i1 != j) & (a[j] > v2)
        i2 = jnp.where(better, j, i2)
        v2 = jnp.where(better, a[j], v2)
    lo, hi = jnp.minimum(i1, i2), jnp.maximum(i1, i2)
    pair = jnp.where(lo == 0, hi - 1, jnp.where(lo == 1, hi + 1, 5))
    s_lo, s_hi = _row_select(s, lo), _row_select(s, hi)
    den = s_lo + s_hi
    return gi * N_PAIRS + pair, s_lo / den, s_hi / den


def _out_kernel(y_ref, x_ref, c_ref, mod_ref, wout_ref, g_ref, rwh_ref, rwl_ref, rb_ref, tri_ref,
                xo_ref, row_ref, meta_ref, cnt_ref, carry_ref, *, n_ctx_blk, blk_off):
    @pl.when((pl.program_id(0) == 0) & (pl.program_id(1) == 0))
    def _():
        carry_ref[...] = jnp.zeros(carry_ref.shape, F32)

    y = y_ref[0]
    mod = mod_ref[0, 0]
    half = D_MODEL // 2
    a = jnp.concatenate([jnp.dot(y, wout_ref[:, :half], preferred_element_type=F32),
                         jnp.dot(y, wout_ref[:, half:], preferred_element_type=F32)], axis=1)
    x = jnp.where(pl.program_id(1) + blk_off < n_ctx_blk, c_ref[0], x_ref[0]) + mod[2:3] * a
    xo_ref[0] = x
    h = _rms(x) * g_ref[...]
    h = h * (1.0 + mod[4:5]) + mod[3:4]

    hh = h.astype(BF16)
    hl = (h - hh.astype(F32)).astype(BF16)
    logits = (jnp.dot(hh, rwh_ref[...], preferred_element_type=F32)
              + jnp.dot(hl, rwh_ref[...], preferred_element_type=F32)
              + jnp.dot(hh, rwl_ref[...], preferred_element_type=F32)).T[:32]
    sc = jax.nn.sigmoid(logits)
    bs = sc + rb_ref[...]
    scores = [sc[e:e + 1, :] for e in range(N_EXPERTS)]
    biased = [bs[e:e + 1, :] for e in range(N_EXPERTS)]
    bucket, g_lo, g_hi = _route(scores, biased)

    onehot = jnp.where(lax.broadcasted_iota(jnp.int32, (32, TQ), 0) == bucket, 1.0, 0.0)
    prefix = jnp.dot(onehot.astype(BF16), tri_ref[...], preferred_element_type=F32)
    carry = carry_ref[:, 0:1]
    rank = jnp.sum(onehot * (carry + prefix - 1.0), axis=0, keepdims=True)
    carry_new = jnp.broadcast_to(carry + prefix[:, TQ - 1:TQ], carry_ref.shape)
    carry_ref[...] = carry_new
    cnt_ref[...] = carry_new
    meta_ref[0] = jnp.concatenate([bucket, rank.astype(jnp.int32), jnp.zeros((6, TQ), jnp.int32)], axis=0)

    gates = jnp.concatenate([g_lo, g_hi, jnp.zeros((LANES - 2, TQ), F32)], axis=0)
    row_ref[...] = jnp.concatenate([h, gates.T], axis=1)


def _output_projection(y_mix, stream, t_all, modtab, lw, shared, n_ctx_blk, with_ctx):
    bsz, _, d = stream[0].shape
    off = 0 if with_ctx else n_ctx_blk
    nblk = t_all // TQ - off
    tok = lambda w: pl.BlockSpec((1, TQ, w), lambda b, i: (b, i, 0))
    full = lambda a: pl.BlockSpec(a.shape, lambda b, i: (0,) * a.ndim)
    consts = (lw["wout"], lw["g_ffn"], shared["rw_hi"], shared["rw_lo"], shared["rb"], shared["tri"])
    return pl.pallas_call(
        functools.partial(_out_kernel, n_ctx_blk=n_ctx_blk, blk_off=off),
        out_shape=(jax.ShapeDtypeStruct((bsz, nblk * TQ, d), F32),
                   jax.ShapeDtypeStruct((bsz * nblk * TQ, ROW_W), F32),
                   jax.ShapeDtypeStruct((bsz * nblk, 8, TQ), jnp.int32),
                   jax.ShapeDtypeStruct((32, LANES), F32)),
        grid=(bsz, nblk),
        in_specs=[tok(d)] + _stream_specs(stream, n_ctx_blk, off)
                 + [pl.BlockSpec((1, 1, 8, d), lambda b, i: (b, jnp.where(i + off < n_ctx_blk, 1, 0), 0, 0))]
                 + [full(a) for a in consts],
        out_specs=(tok(d),
                   pl.BlockSpec((TQ, ROW_W), lambda b, i: (b * nblk + i, 0)),
                   pl.BlockSpec((1, 8, TQ), lambda b, i: (b * nblk + i, 0, 0)),
                   pl.BlockSpec((32, LANES), lambda b, i: (0, 0))),
        scratch_shapes=[pltpu.VMEM((32, LANES), F32)],
        compiler_params=_cparams(2),
        name="out_proj_router",
    )(y_mix, stream[0], stream[2], modtab, *consts)


def _scatter_kernel(dest_ref, fill_ref, nv_ref, row_ref, out_ref, buf, sem, *, n_steps, n_blocks):
    i = pl.program_id(0)
    slot = i % 2

    def wait_slot(s):
        pltpu.make_async_copy(buf.at[s], out_ref.at[pl.ds(0, TS)], sem.at[s]).wait()

    @pl.when(i == 0)
    def _():
        buf[1, 0:TM] = jnp.zeros((TM, ROW_W), F32)

        def zero_block(blk):
            return pltpu.make_async_copy(buf.at[1, pl.ds(0, TM)], out_ref.at[pl.ds(pl.multiple_of(blk * TM, TM), TM)],
                                         sem.at[1])

        for wait in (False, True):
            for b in range(N_BUCKETS):
                @pl.when(fill_ref[b] >= 0)
                def _(b=b, wait=wait):
                    zero_block(fill_ref[b]).wait() if wait else zero_block(fill_ref[b]).start()

            def tail(blk, carry, wait=wait):
                zero_block(blk).wait() if wait else zero_block(blk).start()
                return carry
            lax.fori_loop(nv_ref[0], n_blocks, tail, 0)

    for s in range(2):
        @pl.when(slot == s)
        def _(s=s):
            @pl.when(i >= 2)
            def _():
                wait_slot(s)

            buf[s] = row_ref[...]
            for r in range(TS):
                d = dest_ref[i * TS + r]
                pltpu.make_async_copy(buf.at[s, pl.ds(r, 1)], out_ref.at[pl.ds(d, 1)], sem.at[s]).start()

    @pl.when(i == n_steps - 1)
    def _():
        wait_slot(slot)
        if n_steps >= 2:
            wait_slot(1 - slot)


def _scatter_rows(dest, fill_blk, n_valid, rows, n_sorted):
    n_tok = rows.shape[0]
    assert n_tok % TS == 0
    n_steps = n_tok // TS
    return pl.pallas_call(
        functools.partial(_scatter_kernel, n_steps=n_steps, n_blocks=n_sorted // TM),
        out_shape=jax.ShapeDtypeStruct((n_sorted, ROW_W), F32),
        grid_spec=pltpu.PrefetchScalarGridSpec(
            num_scalar_prefetch=3,
            grid=(n_steps,),
            in_specs=[pl.BlockSpec((TS, ROW_W), lambda i, d, f, nv: (i, 0))],
            out_specs=pl.BlockSpec(memory_space=pl.ANY),
            scratch_shapes=[pltpu.VMEM((2, TS, ROW_W), F32), pltpu.SemaphoreType.DMA((2,))]),
        compiler_params=_cparams(1),
        name="moe_scatter",
    )(dest, fill_blk, n_valid, rows)


def _swiglu(h, w_gate, w_up, w_down):
    a = jnp.dot(h, w_gate, preferred_element_type=F32)
    a = a * jax.nn.sigmoid(a) * jnp.dot(h, w_up, preferred_element_type=F32)
    return jnp.dot(a.astype(BF16), w_down, preferred_element_type=F32)


def _moe_kernel(elo_ref, ehi_ref, nv_ref, row_ref, gl_ref, ul_ref, dl_ref, gh_ref, uh_ref, dh_ref,
                sg_ref, su_ref, sd_ref, y_ref):
    del elo_ref, ehi_ref
    j = pl.program_id(0)

    @pl.when(j < nv_ref[0])
    def _():
        rows = row_ref[...]
        h = rows[:, :D_MODEL].astype(BF16)
        g_lo = rows[:, D_MODEL:D_MODEL + 1]
        g_hi = rows[:, D_MODEL + 1:D_MODEL + 2]
        y_ref[...] = (g_lo * _swiglu(h, gl_ref[0, 0], ul_ref[0, 0], dl_ref[0, 0])
                      + g_hi * _swiglu(h, gh_ref[0, 0], uh_ref[0, 0], dh_ref[0, 0])
                      + _swiglu(h, sg_ref[0], su_ref[0], sd_ref[0]))

    @pl.when(j >= nv_ref[0])
    def _():
        y_ref[...] = jnp.zeros(y_ref.shape, F32)


def _grouped_experts(e_lo, e_hi, n_valid, rows_sorted, ew, layer):
    n_sorted = rows_sorted.shape[0]
    nb = n_sorted // TM
    routed = lambda w, tab: pl.BlockSpec((1, 1) + w.shape[2:], lambda j, lo, hi, nv: (layer, (lo, hi)[tab][j], 0, 0))
    shared = lambda w: pl.BlockSpec((1,) + w.shape[1:], lambda j, lo, hi, nv: (layer, 0, 0))
    return pl.pallas_call(
        _moe_kernel,
        out_shape=jax.ShapeDtypeStruct((n_sorted, D_MODEL), F32),
        grid_spec=pltpu.PrefetchScalarGridSpec(
            num_scalar_prefetch=3,
            grid=(nb,),
            in_specs=[pl.BlockSpec((TM, ROW_W), lambda j, lo, hi, nv: (j, 0))]
                     + [routed(ew[k], tab) for tab in (0, 1) for k in ("gate", "up", "down")]
                     + [shared(ew[k]) for k in ("s_gate", "s_up", "s_down")],
            out_specs=pl.BlockSpec((TM, D_MODEL), lambda j, lo, hi, nv: (j, 0))),
        compiler_params=_cparams(1),
        name="moe_experts",
    )(e_lo, e_hi, n_valid, rows_sorted, *(ew[k] for k in ("gate", "up", "down")) , *(ew[k] for k in ("gate", "up", "down")),
      *(ew[k] for k in ("s_gate", "s_up", "s_down")))


def _gather_kernel(dest_ref, y_ref, x_ref, *rest, n_steps, final_norm):
    halves = TS // TQ
    mod_refs, (gf_ref, o_ref, fbuf, sem) = rest[:halves], rest[halves:]
    i = pl.program_id(0)
    slot = i % 2

    def issue(step, s):
        for r in range(TS):
            d = dest_ref[step * TS + r]
            pltpu.make_async_copy(y_ref.at[pl.ds(d, 1)], fbuf.at[s, pl.ds(r, 1)], sem.at[s]).start()

    @pl.when(i == 0)
    def _():
        issue(0, 0)

    for s in range(2):
        @pl.when((i + 1 < n_steps) & (slot == s))
        def _(s=s):
            issue(i + 1, 1 - s)

    pltpu.make_async_copy(y_ref.at[pl.ds(0, TS)], fbuf.at[slot], sem.at[slot]).wait()
    for k in range(halves):
        rows = slice(k * TQ, (k + 1) * TQ)
        x = x_ref[rows, :] + mod_refs[k][0, 0][5:6] * fbuf[slot, rows, :]
        if final_norm:
            x = _rms(x) * gf_ref[...]
        o_ref[rows, :] = x


def _gather_residual(dest, y_sorted, x_mid, modtab, g_final, n_ctx_blk, with_ctx, final_norm):
    bsz, t_rows, d = x_mid.shape
    nblk = t_rows // TQ
    off = 0 if with_ctx else n_ctx_blk
    halves = TS // TQ
    assert (bsz * nblk) % halves == 0
    n_steps = bsz * nblk // halves

    def mod_spec(k):
        def index(i, dst):
            blk = i * halves + k
            return (blk // nblk, jnp.where(blk % nblk + off < n_ctx_blk, 1, 0), 0, 0)
        return pl.BlockSpec((1, 1, 8, d), index)

    out = pl.pallas_call(
        functools.partial(_gather_kernel, n_steps=n_steps, final_norm=final_norm),
        out_shape=jax.ShapeDtypeStruct((bsz * t_rows, d), F32),
        grid_spec=pltpu.PrefetchScalarGridSpec(
            num_scalar_prefetch=1,
            grid=(n_steps,),
            in_specs=[pl.BlockSpec(memory_space=pl.ANY),
                      pl.BlockSpec((TS, d), lambda i, dst: (i, 0))]
                     + [mod_spec(k) for k in range(halves)]
                     + [pl.BlockSpec(g_final.shape, lambda i, dst: (0, 0))],
            out_specs=pl.BlockSpec((TS, d), lambda i, dst: (i, 0)),
            scratch_shapes=[pltpu.VMEM((2, TS, d), F32), pltpu.SemaphoreType.DMA((2,))]),
        compiler_params=_cparams(1),
        name="moe_gather",
    )(dest, y_sorted, x_mid.reshape(bsz * t_rows, d), *([modtab] * halves), g_final)
    return out.reshape(bsz, t_rows, d)


def _bucket_layout(meta, counts, n_tok):
    bucket = meta[:, 0, :].reshape(-1)
    rank = meta[:, 1, :].reshape(-1)
    cnt = counts[:N_BUCKETS, 0].astype(jnp.int32)
    padded = (cnt + TM - 1) // TM * TM
    pad_end = jnp.cumsum(padded)
    pad_start = pad_end - padded
    dest = pad_start[bucket] + rank
    nb = n_tok // TM + N_BUCKETS
    n_valid = pad_end[-1] // TM
    blk = jnp.arange(nb, dtype=jnp.int32)
    blk_row = jnp.minimum(blk, n_valid - 1) * TM
    blk_bucket = jnp.minimum(jnp.sum(pad_end[None, :] <= blk_row[:, None], axis=1), N_BUCKETS - 1).astype(jnp.int32)
    grp, pair = blk_bucket // N_PAIRS, blk_bucket % N_PAIRS
    e_lo = grp * EXPERTS_PER_GROUP + jnp.asarray(_PAIR_LO, jnp.int32)[pair]
    e_hi = grp * EXPERTS_PER_GROUP + jnp.asarray(_PAIR_HI, jnp.int32)[pair]
    fill_blk = jnp.where(padded > 0, pad_end // TM - 1, -1).astype(jnp.int32)
    return dest.astype(jnp.int32), fill_blk, e_lo, e_hi, n_valid.reshape(1).astype(jnp.int32), nb * TM


def _rope_tables(seq, n_ctx):
    t = jnp.arange(seq)
    rows, cols = t // GRID_W, t % GRID_W

    def table(width, nf):
        lane = np.arange(width)
        half, j = lane // (2 * nf), lane % (2 * nf)
        inv = ROPE_THETA ** (-jnp.arange(nf, dtype=F32) / nf)
        pos = jnp.where(jnp.asarray(half == 0)[None, :], rows[:, None], cols[:, None]).astype(F32)
        ang = pos * inv[j % nf][None, :]
        sign = jnp.asarray(np.where(j < nf, -1.0, 1.0), F32)[None, :]
        return jnp.cos(ang), jnp.sin(ang) * sign

    cos64, sin64 = table(HEAD_DIM, 16)
    cos64, sin64 = jnp.tile(cos64, (1, 2)), jnp.tile(sin64, (1, 2))
    cosr, sinr = table(MLA_ROPE, 8)
    cosm = jnp.concatenate([jnp.ones((seq, MLA_NOPE), F32), cosr, jnp.ones((seq, 32), F32)], axis=1)
    sinm = jnp.concatenate([jnp.zeros((seq, MLA_NOPE), F32), sinr, jnp.zeros((seq, 32), F32)], axis=1)
    ctx1, ctx0 = jnp.ones((n_ctx, LANES), F32), jnp.zeros((n_ctx, LANES), F32)
    token_major = [jnp.concatenate([c, a], axis=0) for c, a in ((ctx1, cos64), (ctx0, sin64), (ctx1, cosm), (ctx0, sinm))]
    dim_major = [jnp.concatenate([jnp.full((a.shape[1], n_ctx), fill, F32), a.T], axis=1)
                 for a, fill in ((cos64[:, :HEAD_DIM], 1.0), (sin64[:, :HEAD_DIM], 0.0), (cosr, 1.0), (sinr, 0.0))]
    return tuple(token_major + dim_major)


def _layer_weights(l, w_in, w_out, norm_ffn_g, glb_q_gain, glb_k_gain, mla_q_gain, mla_w_uq, mla_kv_gain,
                   mla_w_ukv):
    d = w_in.shape[1]
    wi = w_in[l]
    wa = jnp.concatenate([wi[:, _SQ:_SK], wi[:, _GQ:_GK], wi[:, _MQ:_MKV], wi[:, _SV:_GQ], wi[:, _GV:_MQ],
                          wi[:, _MKV:_KR]], axis=1).T
    kr_tile = jnp.concatenate([jnp.zeros((d, MLA_NOPE), F32), wi[:, _KR:], jnp.zeros((d, 32), F32)], axis=1)
    wb = jnp.concatenate([wi[:, _SK:_SV], wi[:, _GK:_GV], wi[:, _MKV:_KR], kr_tile], axis=1)
    uq = mla_w_uq[l].reshape(MLA_Q_RANK, MLA_HEADS, MLA_NOPE + MLA_ROPE)
    wuq = jnp.concatenate([uq, jnp.zeros((MLA_Q_RANK, MLA_HEADS, 32), F32)], axis=2).reshape(MLA_Q_RANK, 512)
    ukv = mla_w_ukv[l].reshape(MLA_KV_RANK, MLA_HEADS, MLA_NOPE + MLA_V)
    wk = jnp.concatenate([ukv[:, :, :MLA_NOPE], jnp.zeros((MLA_KV_RANK, MLA_HEADS, 64), F32)], axis=2)
    bcast = lambda g: jnp.broadcast_to(g[:, None], (g.shape[0], TQ))
    head = np.arange(LANES) // HEAD_DIM
    return {
        "wa": wa.astype(BF16),
        "wb": wb.astype(BF16),
        "wuq_t": wuq.T.astype(BF16),
        "wukv_k": wk.reshape(MLA_KV_RANK, 512).astype(BF16),
        "wukv_vt": ukv[:, :, MLA_NOPE:].reshape(MLA_KV_RANK, 256).T.astype(BF16),
        "gqg_t": bcast(glb_q_gain[l] * (HEAD_DIM ** -0.5 * LOG2E)),
        "gkg": jnp.tile(glb_k_gain[l], GQA_KV).reshape(1, LANES),
        "mqg_t": bcast(mla_q_gain[l]),
        "mkvg": mla_kv_gain[l].reshape(1, -1),
        "mkvg_t": bcast(mla_kv_gain[l]),
        "bd": jnp.asarray(head[:, None] == head[None, :], BF16),
        "wout": w_out[l].astype(BF16),
        "g_ffn": norm_ffn_g[l].reshape(1, -1),
    }


def kernel(x, c, ctx, c_ctx, w_mod, b_mod, norm_mix_g, norm_ffn_g, w_in, w_out, swa_sink, glb_q_gain, glb_k_gain,
           mla_q_gain, mla_w_uq, mla_kv_gain, mla_w_ukv, router_w, router_bias, exp_w_gate, exp_w_up, exp_w_down,
           shr_w_gate, shr_w_up, shr_w_down, final_norm_g):
    bsz, seq, d = x.shape
    n_ctx = ctx.shape[1]
    n_layers = w_mod.shape[0]
    assert d == D_MODEL and n_ctx % TQ == 0 and seq % TQ == 0 and seq >= TQ + 2 * WINDOW and seq % GRID_W == 0
    n_ctx_blk = n_ctx // TQ

    rows = -(-(bsz + 1) // 8) * 8
    c_rows = jnp.concatenate([c, c_ctx[None, :], jnp.zeros((rows - bsz - 1, d), F32)], axis=0)
    mods = _modulation(c_rows, w_mod, b_mod)
    mod_x = mods[:, :bsz].reshape(n_layers, bsz, 6, d)
    mod_c = jnp.broadcast_to(mods[:, bsz].reshape(n_layers, 1, 6, d), (n_layers, bsz, 6, d))
    modtabs = jnp.pad(jnp.stack([mod_x, mod_c], axis=2), ((0, 0), (0, 0), (0, 0), (0, 2), (0, 0)))

    tabs = _rope_tables(seq, n_ctx)
    rw = jnp.pad(router_w, ((0, 0), (0, LANES - N_EXPERTS)))
    rw_hi = rw.astype(BF16)
    shared = {
        "rw_hi": rw_hi,
        "rw_lo": (rw - rw_hi.astype(F32)).astype(BF16),
        "rb": jnp.pad(router_bias, (0, 32 - N_EXPERTS)).reshape(32, 1),
        "tri": jnp.asarray(np.arange(TQ)[:, None] <= np.arange(TQ)[None, :], BF16),
    }
    g_final = final_norm_g.reshape(1, d)
    ew = {"gate": exp_w_gate.astype(BF16), "up": exp_w_up.astype(BF16), "down": exp_w_down.astype(BF16),
          "s_gate": shr_w_gate.astype(BF16), "s_up": shr_w_up.astype(BF16), "s_down": shr_w_down.astype(BF16)}

    t_all = n_ctx + seq
    stream = (x, 0, ctx)
    for l in range(n_layers):
        last = l == n_layers - 1
        with_ctx = not last
        lw = _layer_weights(l, w_in, w_out, norm_ffn_g, glb_q_gain, glb_k_gain, mla_q_gain, mla_w_uq,
                            mla_kv_gain, mla_w_ukv)
        modtab = modtabs[l]
        qs, ks, vs, qg, kg, vg, qm, km, vm = _input_projection(
            stream, t_all, modtab, norm_mix_g[l].reshape(1, d), lw, tabs, n_ctx_blk)
        sink = jnp.pad(swa_sink[l] * LOG2E, (0, 8 - GQA_HEADS))
        y_mix = _attention({"swa": (qs, ks, vs), "glb": (qg, kg, vg), "mla": (qm, km, vm)}, sink, n_ctx, with_ctx)
        x_mid, rows_tok, meta, counts = _output_projection(y_mix, stream, t_all, modtab, lw, shared, n_ctx_blk,
                                                           with_ctx)
        n_tok = rows_tok.shape[0]
        dest, fill_blk, e_lo, e_hi, n_valid, n_sorted = _bucket_layout(meta, counts, n_tok)
        rows_sorted = _scatter_rows(dest, fill_blk, n_valid, rows_tok, n_sorted)
        y_sorted = _grouped_experts(e_lo, e_hi, n_valid, rows_sorted, ew, l)
        xs = _gather_residual(dest, y_sorted, x_mid, modtab, g_final, n_ctx_blk, with_ctx, final_norm=last)
        stream = (xs, n_ctx_blk, xs)
    return xs
```

```python
import functools

import jax
import jax.numpy as jnp
import numpy as np
from jax import lax
from jax.experimental import pallas as pl
from jax.experimental.pallas import tpu as pltpu

F32 = jnp.float32
BF16 = jnp.bfloat16

D_MODEL = 1024
GRID_W = 64
HEAD_DIM = 64
GQA_HEADS = 6
GQA_KV = 2
GQA_G = GQA_HEADS // GQA_KV
WINDOW = 128
MLA_HEADS = 4
MLA_NOPE = 64
MLA_ROPE = 32
MLA_V = 64
MLA_Q_RANK = 256
MLA_KV_RANK = 128
ROPE_THETA = 10000.0
N_EXPERTS = 16
N_GROUPS = 4
EXPERTS_PER_GROUP = 4
N_PAIRS = 6
N_BUCKETS = N_GROUPS * N_PAIRS
D_EXPERT = 512
EPS = 1e-6
NEG_INF = -1e30

LANES = 128
TQ = 256
TK = 256
TM = 256
TS = 512
SWA_SLAB = 128
V_ROWS = 80
LOG2E = 1.4426950408889634
ROW_W = D_MODEL + LANES
VMEM_LIMIT = 56 * 1024 * 1024

_SQ, _SK, _SV, _GQ, _GK, _GV, _MQ, _MKV, _KR = 0, 384, 512, 640, 1024, 1152, 1280, 1536, 1664
_A_SQ, _A_GQ, _A_MQ, _A_SV, _A_GV, _A_MKV, _WA_ROWS = 0, 384, 768, 1024, 1152, 1280, 1408

_PAIR_LO = (0, 0, 0, 1, 1, 2)
_PAIR_HI = (1, 2, 3, 2, 3, 3)


def _cparams(n_axes):
    return pltpu.CompilerParams(dimension_semantics=("arbitrary",) * n_axes,
                                vmem_limit_bytes=VMEM_LIMIT)


def _mod_kernel(c_ref, w_ref, b_ref, o_ref):
    c = c_ref[...]
    a = (c * jax.nn.sigmoid(c)).astype(BF16)
    o_ref[0] = jnp.dot(a, w_ref[0].astype(BF16), preferred_element_type=F32) + b_ref[0]


def _modulation(c_rows, w_mod, b_mod):
    n_layers, d, width = w_mod.shape
    rows = c_rows.shape[0]
    nb = 1536
    return pl.pallas_call(
        _mod_kernel,
        out_shape=jax.ShapeDtypeStruct((n_layers, rows, width), F32),
        grid=(n_layers, width // nb),
        in_specs=[pl.BlockSpec((rows, d), lambda l, j: (0, 0)),
                  pl.BlockSpec((1, d, nb), lambda l, j: (l, 0, j)),
                  pl.BlockSpec((1, 1, nb), lambda l, j: (l, 0, j))],
        out_specs=pl.BlockSpec((1, rows, nb), lambda l, j: (l, 0, j)),
        compiler_params=_cparams(2),
        name="adaln_mod",
    )(c_rows, w_mod, b_mod.reshape(n_layers, 1, width))


def _rope(x, cos, sin_signed, nf):
    lane = lax.broadcasted_iota(jnp.int32, (1, LANES), 1)
    first = (lane % (2 * nf)) < nf
    tiles = []
    for t in range(x.shape[1] // LANES):
        xt = x[:, t * LANES:(t + 1) * LANES]
        partner = jnp.where(first, pltpu.roll(xt, LANES - nf, 1), pltpu.roll(xt, nf, 1))
        tiles.append(xt * cos + partner * sin_signed)
    return tiles[0] if len(tiles) == 1 else jnp.concatenate(tiles, axis=1)


def _rope_t(x, cos_t, sin_t, nf):
    partner = jnp.concatenate([x[nf:2 * nf], x[:nf], x[3 * nf:], x[2 * nf:3 * nf]], axis=0)
    return x * cos_t + partner * sin_t


def _emit_v_t(ref, v_t, slab):
    ones_blk = jnp.where(lax.broadcasted_iota(jnp.int32, (V_ROWS - HEAD_DIM, TQ), 0) == 0, 1.0, 0.0)
    parts = []
    for h in range(v_t.shape[0] // HEAD_DIM):
        parts += [v_t[h * HEAD_DIM:(h + 1) * HEAD_DIM], ones_blk]
    ext = jnp.concatenate(parts, axis=0).astype(ref.dtype)
    for s in range(TQ // slab):
        ref[0, s] = ext[:, s * slab:(s + 1) * slab]


def _rms(x, eps=EPS):
    return x * lax.rsqrt(jnp.mean(x * x, axis=-1, keepdims=True) + eps)


def _stream_specs(stream, n_ctx_blk, blk_off):
    lat, lat_off, ctx = stream
    d = lat.shape[2]
    return [pl.BlockSpec((1, TQ, d), lambda b, i: (b, jnp.maximum(i + blk_off - n_ctx_blk, 0) + lat_off, 0)),
            pl.BlockSpec((1, TQ, d), lambda b, i: (b, jnp.minimum(i + blk_off, n_ctx_blk - 1), 0))]


def _rms_t(x, eps=EPS):
    return x * lax.rsqrt(jnp.mean(x * x, axis=0, keepdims=True) + eps)


def _in_kernel(x_ref, c_ref, mod_ref, g_ref, wa_ref, wb_ref, cos64_ref, sin64_ref, cosm_ref, sinm_ref,
               cos64t_ref, sin64t_ref, cosmt_ref, sinmt_ref, gqg_ref, gkg_ref, mqg_ref, mkvg_ref, mkvgt_ref,
               wuqt_ref, wukvk_ref, wukvvt_ref, bd_ref,
               qs_ref, ks_ref, vs_ref, qg_ref, kg_ref, vg_ref, qm_ref, km_ref, vm_ref, *, mla_scale, n_ctx_blk):
    x = jnp.where(pl.program_id(1) < n_ctx_blk, c_ref[0], x_ref[0])
    mod = mod_ref[0, 0]
    h = _rms(x) * g_ref[...]
    h = h * (1.0 + mod[1:2]) + mod[0:1]
    hb = h.astype(BF16)
    za = lax.dot_general(wa_ref[...], hb, (((1,), (1,)), ((), ())), preferred_element_type=F32)
    zb = jnp.dot(hb, wb_ref[...], preferred_element_type=F32)
    cos_t, sin_t = cos64t_ref[...], sin64t_ref[...]
    zeros = jnp.zeros((HEAD_DIM, TQ), F32)

    def emit_gqa_q(ref, q_t, prep):
        for hd in range(GQA_HEADS):
            q = prep(q_t[hd * HEAD_DIM:(hd + 1) * HEAD_DIM])
            q = _rope_t(q, cos_t, sin_t, 16)
            tile = [q, zeros] if hd // GQA_G == 0 else [zeros, q]
            ref[0, hd] = jnp.concatenate(tile, axis=0).astype(ref.dtype)

    emit_gqa_q(qs_ref, za[_A_SQ:_A_GQ], lambda q: q * (HEAD_DIM ** -0.5 * LOG2E))
    ks_ref[0] = _rope(zb[:, 0:128], cos64_ref[...], sin64_ref[...], 16).astype(BF16)
    _emit_v_t(vs_ref, za[_A_SV:_A_GV], SWA_SLAB)

    emit_gqa_q(qg_ref, za[_A_GQ:_A_MQ], lambda q: _rms_t(q) * gqg_ref[...])
    gk = zb[:, 128:256]
    sq = gk * gk
    hi = sq.astype(BF16)
    lo = (sq - hi.astype(F32)).astype(BF16)
    ssum = (jnp.dot(hi, bd_ref[...], preferred_element_type=F32)
            + jnp.dot(lo, bd_ref[...], preferred_element_type=F32))
    gk = gk * lax.rsqrt(ssum * (1.0 / HEAD_DIM) + EPS) * gkg_ref[...]
    kg_ref[0] = _rope(gk, cos64_ref[...], sin64_ref[...], 16).astype(BF16)
    _emit_v_t(vg_ref, za[_A_GV:_A_MKV], TK)

    qn = _rms_t(za[_A_MQ:_A_SV]) * mqg_ref[...]
    mq = jnp.dot(wuqt_ref[...], qn.astype(BF16), preferred_element_type=F32)
    for hd in range(MLA_HEADS):
        t = mq[hd * LANES:(hd + 1) * LANES]
        r = _rope_t(t[MLA_NOPE:MLA_NOPE + MLA_ROPE], cosmt_ref[...], sinmt_ref[...], 8)
        t = jnp.concatenate([t[:MLA_NOPE], r, t[MLA_NOPE + MLA_ROPE:]], axis=0) * (mla_scale * LOG2E)
        qm_ref[0, hd] = t.astype(qm_ref.dtype)
    kvn = _rms(zb[:, 256:384]) * mkvg_ref[...]
    k_nope = jnp.dot(kvn.astype(BF16), wukvk_ref[...], preferred_element_type=F32)
    kr = _rope(zb[:, 384:512], cosm_ref[...], sinm_ref[...], 8)
    km_ref[0] = (k_nope + jnp.concatenate([kr] * MLA_HEADS, axis=1)).astype(BF16)
    kvn_t = _rms_t(za[_A_MKV:_WA_ROWS]) * mkvgt_ref[...]
    _emit_v_t(vm_ref, jnp.dot(wukvvt_ref[...], kvn_t.astype(BF16), preferred_element_type=F32), TK)


def _input_projection(stream, t_all, modtab, g_mix, lw, tabs, n_ctx_blk):
    bsz, _, d = stream[0].shape
    nblk = t_all // TQ
    tok = lambda w: pl.BlockSpec((1, TQ, w), lambda b, i: (b, i, 0))
    q_t = lambda nh: pl.BlockSpec((1, nh, LANES, TQ), lambda b, i: (b, 0, 0, i))
    v_t = lambda nh, slab: pl.BlockSpec((1, TQ // slab, nh * V_ROWS, slab), lambda b, i: (b, i, 0, 0))
    full = lambda a: pl.BlockSpec(a.shape, lambda b, i: (0,) * a.ndim)
    tab = pl.BlockSpec((TQ, LANES), lambda b, i: (i, 0))
    q_shape = lambda nh: jax.ShapeDtypeStruct((bsz, nh, LANES, t_all), BF16)
    k_shape = lambda w: jax.ShapeDtypeStruct((bsz, t_all, w), BF16)
    v_shape = lambda nh, slab: jax.ShapeDtypeStruct((bsz, t_all // slab, nh * V_ROWS, slab), BF16)
    tab_t = lambda a: pl.BlockSpec((a.shape[0], TQ), lambda b, i: (0, i))
    consts = (lw["gqg_t"], lw["gkg"], lw["mqg_t"], lw["mkvg"], lw["mkvg_t"], lw["wuq_t"], lw["wukv_k"],
              lw["wukv_vt"], lw["bd"])
    return pl.pallas_call(
        functools.partial(_in_kernel, mla_scale=float((MLA_NOPE + MLA_ROPE) ** -0.5), n_ctx_blk=n_ctx_blk),
        out_shape=(q_shape(GQA_HEADS), k_shape(128), v_shape(GQA_KV, SWA_SLAB),
                   q_shape(GQA_HEADS), k_shape(128), v_shape(GQA_KV, TK),
                   q_shape(MLA_HEADS), k_shape(512), v_shape(MLA_HEADS, TK)),
        grid=(bsz, nblk),
        in_specs=_stream_specs(stream, n_ctx_blk, 0)
                 + [pl.BlockSpec((1, 1, 8, d), lambda b, i: (b, jnp.where(i < n_ctx_blk, 1, 0), 0, 0)),
                    full(g_mix), full(lw["wa"]), full(lw["wb"]), tab, tab, tab, tab]
                 + [tab_t(a) for a in tabs[4:]] + [full(a) for a in consts],
        out_specs=(q_t(GQA_HEADS), tok(128), v_t(GQA_KV, SWA_SLAB),
                   q_t(GQA_HEADS), tok(128), v_t(GQA_KV, TK),
                   q_t(MLA_HEADS), tok(512), v_t(MLA_HEADS, TK)),
        compiler_params=_cparams(2),
        name="in_proj",
    )(stream[0], stream[2], modtab, g_mix, lw["wa"], lw["wb"], *tabs, *consts)


_N_IN = {"swa": 5, "glb": 3, "mla": 3}
_N_SCRATCH = {"swa": 6, "glb": 12, "mla": 12}
_OUT_COL = {"swa": 0, "glb": GQA_HEADS * HEAD_DIM, "mla": 2 * GQA_HEADS * HEAD_DIM}


def _attn_kernel(*refs, n_ctx, n_ctx_blk, q_blk_off, seq):
    kinds = ("swa", "glb", "mla")
    n_in = sum(_N_IN[k] for k in kinds)
    o_ref = refs[n_in]
    i0, s0 = 0, n_in + 1
    for kind in kinds:
        _attn_group(refs[i0:i0 + _N_IN[kind]], o_ref, refs[s0:s0 + _N_SCRATCH[kind]], kind=kind, n_ctx=n_ctx,
                    n_ctx_blk=n_ctx_blk, q_blk_off=q_blk_off, seq=seq)
        i0 += _N_IN[kind]
        s0 += _N_SCRATCH[kind]


def _attn_group(ins, o_ref, scratch, *, kind, n_ctx, n_ctx_blk, q_blk_off, seq):
    if kind == "swa":
        sink_ref, bias_ref, q_ref, k_ref, v_ref = ins
        acc_ref, s_refs, p_refs = scratch[0], scratch[1:4], scratch[4:6]
    else:
        q_ref, k_ref, v_ref = ins
        acc_ref, m_ref = scratch[:2]
        s_refs, cm_refs, p_refs, al_refs = scratch[2:5], scratch[5:8], scratch[8:10], scratch[10:12]
    blk = pl.program_id(1) + q_blk_off
    is_lat = blk >= n_ctx_blk
    gqa = kind in ("swa", "glb")
    n_heads = GQA_HEADS if gqa else MLA_HEADS
    n_units = GQA_KV if gqa else MLA_HEADS
    unit_w = n_heads // n_units * TQ

    if kind == "swa":
        nk = TQ + 2 * WINDOW
        n_slab = n_ctx // SWA_SLAB
        row_l = lax.broadcasted_iota(jnp.int32, (V_ROWS, TQ), 0) == HEAD_DIM

        def scores(h, parts):
            cm = None
            for k_rows, _, bias, r0, nr in parts:
                s = jnp.dot(k_rows(), q_ref[0, h], preferred_element_type=F32)
                if bias is not None:
                    s = s + bias()
                s_refs[h % 3][r0:r0 + nr, :] = s
                c = jnp.max(s, axis=0, keepdims=True)
                cm = c if cm is None else jnp.maximum(cm, c)
            return cm

        def softmax(h, cm, nrows):
            m = jnp.maximum(cm, sink_ref[h])
            p_refs[h % 2][0:nrows, :] = jnp.exp2((s_refs[h % 3][0:nrows, :] - m).astype(BF16))
            return jnp.exp2(sink_ref[h] - m)

        def values(h, e_sink, parts):
            u, j = divmod(h, GQA_G)
            acc = None
            for _, v_t, _, r0, nr in parts:
                a = jnp.dot(v_t(u), p_refs[h % 2][r0:r0 + nr, :], preferred_element_type=F32)
                acc = a if acc is None else acc + a
            acc_ref[u, :, j * TQ:(j + 1) * TQ] = acc + jnp.where(row_l, e_sink, 0.0)

        def run(parts):
            nrows = sum(p[4] for p in parts)
            cms = {h: scores(h, parts) for h in range(min(2, n_heads))}
            e_sinks = {}
            for h in range(n_heads + 1):
                if h + 2 < n_heads:
                    cms[h + 2] = scores(h + 2, parts)
                if h < n_heads:
                    e_sinks[h] = softmax(h, cms[h], nrows)
                if h >= 1:
                    values(h - 1, e_sinks[h - 1], parts)

        ctx_part = (lambda: k_ref[0, 0:n_ctx, :],
                    lambda u: jnp.concatenate([v_ref[0, s, u * V_ROWS:(u + 1) * V_ROWS, :] for s in range(n_slab)],
                                              axis=1),
                    None, 0, n_ctx)

        def latent():
            q0 = (blk - n_ctx_blk) * TQ
            k0 = pl.multiple_of(jnp.clip(q0 - WINDOW, 0, seq - nk), WINDOW)
            geom = (q0 - k0) // WINDOW
            s0 = (n_ctx + k0) // SWA_SLAB
            win_part = (lambda: k_ref[0, pl.ds(pl.multiple_of(n_ctx + k0, WINDOW), nk), :],
                        lambda u: jnp.concatenate([v_ref[0, s0 + s, u * V_ROWS:(u + 1) * V_ROWS, :]
                                                   for s in range(nk // SWA_SLAB)], axis=1),
                        lambda: bias_ref[geom], n_ctx, nk)
            run([ctx_part, win_part])

        if q_blk_off < n_ctx_blk:
            pl.when(jnp.logical_not(is_lat))(lambda: run([ctx_part]))
            pl.when(is_lat)(latent)
        else:
            latent()
    else:
        m_ref[...] = jnp.full(m_ref.shape, NEG_INF, F32)
        acc_ref[...] = jnp.zeros(acc_ref.shape, F32)

        def qk(c, par, h):
            lanes = slice(h * TQ, (h + 1) * TQ)
            k_cols = slice(0, LANES) if gqa else slice(h * LANES, (h + 1) * LANES)
            k_rows = k_ref[0, pl.ds(pl.multiple_of(c * TK, TK), TK), k_cols]
            s = jnp.dot(k_rows, q_ref[0, h], preferred_element_type=F32)
            s_refs[par][:, lanes] = s
            cm_refs[par][:, lanes] = jnp.max(s, axis=0, keepdims=True)

        def softmax(sb, par, h):
            lanes = slice(h * TQ, (h + 1) * TQ)
            m_prev = m_ref[:, lanes]
            m_new = jnp.maximum(m_prev, cm_refs[sb][:, lanes])
            m_ref[:, lanes] = m_new
            al_refs[par][:, lanes] = jnp.exp2(m_prev - m_new)
            p_refs[par][:, lanes] = jnp.exp2((s_refs[sb][:, lanes] - m_new).astype(BF16))

        def pv(c, par, h):
            lanes = slice(h * TQ, (h + 1) * TQ)
            u, j = divmod(h, n_heads // n_units)
            ul = slice(j * TQ, (j + 1) * TQ)
            acc_ref[u, :, ul] = (acc_ref[u, :, ul] * al_refs[par][:, lanes]
                                 + jnp.dot(v_ref[0, c, u * V_ROWS:(u + 1) * V_ROWS, :], p_refs[par][:, lanes],
                                           preferred_element_type=F32))

        def step(i, r, n):
            for h in range(n_heads):
                if not isinstance(i, int) or i + 2 < n:
                    qk(i + 2, (r + 2) % 3, h)
                if not isinstance(i, int) or i < n:
                    softmax(r % 3, r % 2, h)
                if not isinstance(i, int) or i >= 1:
                    pv(i - 1, (r + 1) % 2, h)

        def run(n):
            for c in range(min(2, n)):
                for h in range(n_heads):
                    qk(c, c, h)
            step(0, 0, n)
            n_groups = max(0, (n - 3) // 6)

            def body(j, carry):
                for r in range(6):
                    step(1 + 6 * j + r, (1 + r) % 6, n)
                return carry
            if n_groups:
                lax.fori_loop(0, n_groups, body, 0)
            for i in range(1 + 6 * n_groups, n + 1):
                step(i, i % 6, n)

        if q_blk_off < n_ctx_blk:
            pl.when(jnp.logical_not(is_lat))(lambda: run(n_ctx // TK))
            pl.when(is_lat)(lambda: run((n_ctx + seq) // TK))
        else:
            run((n_ctx + seq) // TK)

    heads = []
    for u in range(n_units):
        a = acc_ref[u]
        o = a[:HEAD_DIM] / a[HEAD_DIM:HEAD_DIM + 1]
        heads += [o[:, j * TQ:(j + 1) * TQ] for j in range(unit_w // TQ)]
    tiles = [jnp.concatenate(heads[2 * t:2 * t + 2], axis=0).T for t in range(n_heads // 2)]
    c0 = _OUT_COL[kind]
    o_ref[0, :, c0:c0 + n_heads * HEAD_DIM] = jnp.concatenate(tiles, axis=1).astype(o_ref.dtype)


def _attention(groups, sink, n_ctx, with_ctx_queries):
    bsz, _, _, t_all = groups["swa"][0].shape
    seq = t_all - n_ctx
    n_ctx_blk = n_ctx // TQ
    q_blk_off = 0 if with_ctx_queries else n_ctx_blk
    nblk = t_all // TQ - q_blk_off
    in_specs, args, scratch = [], [], []
    for kind in ("swa", "glb", "mla"):
        q_t, k, v_t = groups[kind]
        n_heads = q_t.shape[1]
        n_units = GQA_KV if kind in ("swa", "glb") else MLA_HEADS
        nq = n_heads * TQ
        if kind == "swa":
            nk = TQ + 2 * WINDOW
            off = np.arange(TQ)[None, None, :] + WINDOW * np.arange(3)[:, None, None] - np.arange(nk)[None, :, None]
            bias = jnp.asarray(np.where(np.abs(off) <= WINDOW, 0.0, NEG_INF), F32)
            in_specs += [pl.BlockSpec(memory_space=pltpu.SMEM), pl.BlockSpec(bias.shape, lambda b, i: (0, 0, 0))]
            args += [sink, bias]
        in_specs += [pl.BlockSpec((1, n_heads, LANES, TQ), lambda b, i: (b, 0, 0, i + q_blk_off)),
                     pl.BlockSpec((1,) + k.shape[1:], lambda b, i: (b, 0, 0)),
                     pl.BlockSpec((1,) + v_t.shape[1:], lambda b, i: (b, 0, 0, 0))]
        args += [q_t, k, v_t]
        scratch += [pltpu.VMEM((n_units, V_ROWS, nq // n_units), F32)]
        if kind == "swa":
            rows = n_ctx + nk
            scratch += [pltpu.VMEM((rows, TQ), F32)] * 3 + [pltpu.VMEM((rows, TQ), BF16)] * 2
        else:
            scratch += ([pltpu.VMEM((1, nq), F32)] + [pltpu.VMEM((TK, nq), F32)] * 3 + [pltpu.VMEM((1, nq), F32)] * 3
                        + [pltpu.VMEM((TK, nq), BF16)] * 2 + [pltpu.VMEM((1, nq), F32)] * 2)
    return pl.pallas_call(
        functools.partial(_attn_kernel, n_ctx=n_ctx, n_ctx_blk=n_ctx_blk, q_blk_off=q_blk_off, seq=seq),
        out_shape=jax.ShapeDtypeStruct((bsz, nblk * TQ, D_MODEL), BF16),
        grid=(bsz, nblk),
        in_specs=in_specs,
        out_specs=pl.BlockSpec((1, TQ, D_MODEL), lambda b, i: (b, i, 0)),
        scratch_shapes=scratch,
        compiler_params=_cparams(2),
        name="attention",
    )(*args)


def _row_select(rows, idx):
    out = rows[0]
    for j in range(1, len(rows)):
        out = jnp.where(idx == j, rows[j], out)
    return out


def _route(scores, biased):
    def top2sum(a, b, c, d):
        hi1, lo1, hi2, lo2 = jnp.maximum(a, b), jnp.minimum(a, b), jnp.maximum(c, d), jnp.minimum(c, d)
        return jnp.maximum(hi1, hi2) + jnp.maximum(jnp.minimum(hi1, hi2), jnp.maximum(lo1, lo2))

    gs = [top2sum(*biased[4 * g:4 * g + 4]) for g in range(N_GROUPS)]
    best, gi = gs[0], jnp.zeros(gs[0].shape, jnp.int32)
    for g in range(1, N_GROUPS):
        better = gs[g] > best
        gi = jnp.where(better, g, gi)
        best = jnp.where(better, gs[g], best)
    a = [_row_select([biased[4 * g + j] for g in range(N_GROUPS)], gi) for j in range(EXPERTS_PER_GROUP)]
    s = [_row_select([scores[4 * g + j] for g in range(N_GROUPS)], gi) for j in range(EXPERTS_PER_GROUP)]
    v1, i1 = a[0], jnp.zeros(gi.shape, jnp.int32)
    for j in range(1, EXPERTS_PER_GROUP):
        better = a[j] > v1
        i1 = jnp.where(better, j, i1)
        v1 = jnp.where(better, a[j], v1)
    v2, i2 = jnp.full(v1.shape, -3.0e38, F32), jnp.zeros(gi.shape, jnp.int32)
    for j in range(EXPERTS_PER_GROUP):
        better = (i1 != j) & (a[j] > v2)
        i2 = jnp.where(better, j, i2)
        v2 = jnp.where(better, a[j], v2)
    lo, hi = jnp.minimum(i1, i2), jnp.maximum(i1, i2)
    pair = jnp.where(lo == 0, hi - 1, jnp.where(lo == 1, hi + 1, 5))
    s_lo, s_hi = _row_select(s, lo), _row_select(s, hi)
    den = s_lo + s_hi
    return gi * N_PAIRS + pair, s_lo / den, s_hi / den


def _out_kernel(y_ref, x_ref, c_ref, mod_ref, wout_ref, g_ref, rw_ref, rb_ref, tri_ref,
                xo_ref, row_ref, meta_ref, cnt_ref, carry_ref, *, n_ctx_blk, blk_off):
    @pl.when((pl.program_id(0) == 0) & (pl.program_id(1) == 0))
    def _():
        carry_ref[...] = jnp.zeros(carry_ref.shape, F32)

    y = y_ref[0]
    mod = mod_ref[0, 0]
    half = D_MODEL // 2
    a = jnp.concatenate([jnp.dot(y, wout_ref[:, :half], preferred_element_type=F32),
                         jnp.dot(y, wout_ref[:, half:], preferred_element_type=F32)], axis=1)
    x = jnp.where(pl.program_id(1) + blk_off < n_ctx_blk, c_ref[0], x_ref[0]) + mod[2:3] * a
    xo_ref[0] = x
    h = _rms(x) * g_ref[...]
    h = h * (1.0 + mod[4:5]) + mod[3:4]

    hh = h.astype(BF16)
    hl = (h - hh.astype(F32)).astype(BF16)
    t = (jnp.dot(hh, rw_ref[...], preferred_element_type=F32) + jnp.dot(hl, rw_ref[...], preferred_element_type=F32)).T
    logits = t[0:32] + t[32:64]
    sc = jax.nn.sigmoid(logits)
    bs = sc + rb_ref[...]
    scores = [sc[e:e + 1, :] for e in range(N_EXPERTS)]
    biased = [bs[e:e + 1, :] for e in range(N_EXPERTS)]
    bucket, g_lo, g_hi = _route(scores, biased)

    onehot = jnp.where(lax.broadcasted_iota(jnp.int32, (32, TQ), 0) == bucket, 1.0, 0.0)
    prefix = jnp.dot(onehot.astype(BF16), tri_ref[...], preferred_element_type=F32)
    carry = carry_ref[:, 0:1]
    rank = jnp.sum(onehot * (carry + prefix - 1.0), axis=0, keepdims=True)
    carry_new = jnp.broadcast_to(carry + prefix[:, TQ - 1:TQ], carry_ref.shape)
    carry_ref[...] = carry_new
    cnt_ref[...] = carry_new
    meta_ref[0] = jnp.concatenate([bucket, rank.astype(jnp.int32), jnp.zeros((6, TQ), jnp.int32)], axis=0)

    gates = jnp.concatenate([g_lo, g_hi, jnp.zeros((LANES - 2, TQ), F32)], axis=0)
    row_ref[...] = jnp.concatenate([h, gates.T], axis=1)


def _output_projection(y_mix, stream, t_all, modtab, lw, shared, n_ctx_blk, with_ctx):
    bsz, _, d = stream[0].shape
    off = 0 if with_ctx else n_ctx_blk
    nblk = t_all // TQ - off
    tok = lambda w: pl.BlockSpec((1, TQ, w), lambda b, i: (b, i, 0))
    full = lambda a: pl.BlockSpec(a.shape, lambda b, i: (0,) * a.ndim)
    consts = (lw["wout"], lw["g_ffn"], shared["rw"], shared["rb"], shared["tri"])
    return pl.pallas_call(
        functools.partial(_out_kernel, n_ctx_blk=n_ctx_blk, blk_off=off),
        out_shape=(jax.ShapeDtypeStruct((bsz, nblk * TQ, d), F32),
                   jax.ShapeDtypeStruct((bsz * nblk * TQ, ROW_W), F32),
                   jax.ShapeDtypeStruct((bsz * nblk, 8, TQ), jnp.int32),
                   jax.ShapeDtypeStruct((32, LANES), F32)),
        grid=(bsz, nblk),
        in_specs=[tok(d)] + _stream_specs(stream, n_ctx_blk, off)
                 + [pl.BlockSpec((1, 1, 8, d), lambda b, i: (b, jnp.where(i + off < n_ctx_blk, 1, 0), 0, 0))]
                 + [full(a) for a in consts],
        out_specs=(tok(d),
                   pl.BlockSpec((TQ, ROW_W), lambda b, i: (b * nblk + i, 0)),
                   pl.BlockSpec((1, 8, TQ), lambda b, i: (b * nblk + i, 0, 0)),
                   pl.BlockSpec((32, LANES), lambda b, i: (0, 0))),
        scratch_shapes=[pltpu.VMEM((32, LANES), F32)],
        compiler_params=_cparams(2),
        name="out_proj_router",
    )(y_mix, stream[0], stream[2], modtab, *consts)


def _scatter_kernel(dest_ref, fill_ref, nv_ref, row_ref, out_ref, buf, sem, *, n_steps, n_blocks):
    i = pl.program_id(0)
    slot = i % 2

    def wait_slot(s):
        pltpu.make_async_copy(buf.at[s], out_ref.at[pl.ds(0, TS)], sem.at[s]).wait()

    @pl.when(i == 0)
    def _():
        buf[1, 0:TM] = jnp.zeros((TM, ROW_W), F32)

        def zero_block(blk):
            return pltpu.make_async_copy(buf.at[1, pl.ds(0, TM)], out_ref.at[pl.ds(pl.multiple_of(blk * TM, TM), TM)],
                                         sem.at[1])

        for wait in (False, True):
            for b in range(N_BUCKETS):
                @pl.when(fill_ref[b] >= 0)
                def _(b=b, wait=wait):
                    zero_block(fill_ref[b]).wait() if wait else zero_block(fill_ref[b]).start()

            def tail(blk, carry, wait=wait):
                zero_block(blk).wait() if wait else zero_block(blk).start()
                return carry
            lax.fori_loop(nv_ref[0], n_blocks, tail, 0)

    for s in range(2):
        @pl.when(slot == s)
        def _(s=s):
            @pl.when(i >= 2)
            def _():
                wait_slot(s)

            buf[s] = row_ref[...]
            for r in range(TS):
                d = dest_ref[i * TS + r]
                pltpu.make_async_copy(buf.at[s, pl.ds(r, 1)], out_ref.at[pl.ds(d, 1)], sem.at[s]).start()

    @pl.when(i == n_steps - 1)
    def _():
        wait_slot(slot)
        if n_steps >= 2:
            wait_slot(1 - slot)


def _scatter_rows(dest, fill_blk, n_valid, rows, n_sorted):
    n_tok = rows.shape[0]
    assert n_tok % TS == 0
    n_steps = n_tok // TS
    return pl.pallas_call(
        functools.partial(_scatter_kernel, n_steps=n_steps, n_blocks=n_sorted // TM),
        out_shape=jax.ShapeDtypeStruct((n_sorted, ROW_W), F32),
        grid_spec=pltpu.PrefetchScalarGridSpec(
            num_scalar_prefetch=3,
            grid=(n_steps,),
            in_specs=[pl.BlockSpec((TS, ROW_W), lambda i, d, f, nv: (i, 0))],
            out_specs=pl.BlockSpec(memory_space=pl.ANY),
            scratch_shapes=[pltpu.VMEM((2, TS, ROW_W), F32), pltpu.SemaphoreType.DMA((2,))]),
        compiler_params=_cparams(1),
        name="moe_scatter",
    )(dest, fill_blk, n_valid, rows)


def _swiglu(h, w_gate, w_up, w_down):
    a = jnp.dot(h, w_gate, preferred_element_type=F32)
    a = a * jax.nn.sigmoid(a) * jnp.dot(h, w_up, preferred_element_type=F32)
    return jnp.dot(a.astype(BF16), w_down, preferred_element_type=F32)


def _moe_kernel(elo_ref, ehi_ref, nv_ref, row_ref, gl_ref, ul_ref, dl_ref, gh_ref, uh_ref, dh_ref,
                sg_ref, su_ref, sd_ref, y_ref):
    del elo_ref, ehi_ref
    j = pl.program_id(0)

    @pl.when(j < nv_ref[0])
    def _():
        rows = row_ref[...]
        h = rows[:, :D_MODEL].astype(BF16)
        g_lo = rows[:, D_MODEL:D_MODEL + 1]
        g_hi = rows[:, D_MODEL + 1:D_MODEL + 2]
        y_ref[...] = (g_lo * _swiglu(h, gl_ref[0, 0], ul_ref[0, 0], dl_ref[0, 0])
                      + g_hi * _swiglu(h, gh_ref[0, 0], uh_ref[0, 0], dh_ref[0, 0])
                      + _swiglu(h, sg_ref[0], su_ref[0], sd_ref[0]))

    @pl.when(j >= nv_ref[0])
    def _():
        y_ref[...] = jnp.zeros(y_ref.shape, F32)


def _grouped_experts(e_lo, e_hi, n_valid, rows_sorted, ew, layer):
    n_sorted = rows_sorted.shape[0]
    nb = n_sorted // TM
    routed = lambda w, tab: pl.BlockSpec((1, 1) + w.shape[2:], lambda j, lo, hi, nv: (layer, (lo, hi)[tab][j], 0, 0))
    shared = lambda w: pl.BlockSpec((1,) + w.shape[1:], lambda j, lo, hi, nv: (layer, 0, 0))
    return pl.pallas_call(
        _moe_kernel,
        out_shape=jax.ShapeDtypeStruct((n_sorted, D_MODEL), F32),
        grid_spec=pltpu.PrefetchScalarGridSpec(
            num_scalar_prefetch=3,
            grid=(nb,),
            in_specs=[pl.BlockSpec((TM, ROW_W), lambda j, lo, hi, nv: (j, 0))]
                     + [routed(ew[k], tab) for tab in (0, 1) for k in ("gate", "up", "down")]
                     + [shared(ew[k]) for k in ("s_gate", "s_up", "s_down")],
            out_specs=pl.BlockSpec((TM, D_MODEL), lambda j, lo, hi, nv: (j, 0))),
        compiler_params=_cparams(1),
        name="moe_experts",
    )(e_lo, e_hi, n_valid, rows_sorted, *(ew[k] for k in ("gate", "up", "down")) , *(ew[k] for k in ("gate", "up", "down")),
      *(ew[k] for k in ("s_gate", "s_up", "s_down")))


def _gather_kernel(dest_ref, y_ref, x_ref, *rest, n_steps, final_norm):
    halves = TS // TQ
    mod_refs, (gf_ref, o_ref, fbuf, sem) = rest[:halves], rest[halves:]
    i = pl.program_id(0)
    slot = i % 2

    def issue(step, s):
        for r in range(TS):
            d = dest_ref[step * TS + r]
            pltpu.make_async_copy(y_ref.at[pl.ds(d, 1)], fbuf.at[s, pl.ds(r, 1)], sem.at[s]).start()

    @pl.when(i == 0)
    def _():
        issue(0, 0)

    for s in range(2):
        @pl.when((i + 1 < n_steps) & (slot == s))
        def _(s=s):
            issue(i + 1, 1 - s)

    pltpu.make_async_copy(y_ref.at[pl.ds(0, TS)], fbuf.at[slot], sem.at[slot]).wait()
    for k in range(halves):
        rows = slice(k * TQ, (k + 1) * TQ)
        x = x_ref[rows, :] + mod_refs[k][0, 0][5:6] * fbuf[slot, rows, :]
        if final_norm:
            x = _rms(x) * gf_ref[...]
        o_ref[rows, :] = x


def _gather_residual(dest, y_sorted, x_mid, modtab, g_final, n_ctx_blk, with_ctx, final_norm):
    bsz, t_rows, d = x_mid.shape
    nblk = t_rows // TQ
    off = 0 if with_ctx else n_ctx_blk
    halves = TS // TQ
    assert (bsz * nblk) % halves == 0
    n_steps = bsz * nblk // halves

    def mod_spec(k):
        def index(i, dst):
            blk = i * halves + k
            return (blk // nblk, jnp.where(blk % nblk + off < n_ctx_blk, 1, 0), 0, 0)
        return pl.BlockSpec((1, 1, 8, d), index)

    out = pl.pallas_call(
        functools.partial(_gather_kernel, n_steps=n_steps, final_norm=final_norm),
        out_shape=jax.ShapeDtypeStruct((bsz * t_rows, d), F32),
        grid_spec=pltpu.PrefetchScalarGridSpec(
            num_scalar_prefetch=1,
            grid=(n_steps,),
            in_specs=[pl.BlockSpec(memory_space=pl.ANY),
                      pl.BlockSpec((TS, d), lambda i, dst: (i, 0))]
                     + [mod_spec(k) for k in range(halves)]
                     + [pl.BlockSpec(g_final.shape, lambda i, dst: (0, 0))],
            out_specs=pl.BlockSpec((TS, d), lambda i, dst: (i, 0)),
            scratch_shapes=[pltpu.VMEM((2, TS, d), F32), pltpu.SemaphoreType.DMA((2,))]),
        compiler_params=_cparams(1),
        name="moe_gather",
    )(dest, y_sorted, x_mid.reshape(bsz * t_rows, d), *([modtab] * halves), g_final)
    return out.reshape(bsz, t_rows, d)


def _bucket_layout(meta, counts, n_tok):
    bucket = meta[:, 0, :].reshape(-1)
    rank = meta[:, 1, :].reshape(-1)
    cnt = counts[:N_BUCKETS, 0].astype(jnp.int32)
    padded = (cnt + TM - 1) // TM * TM
    pad_end = jnp.cumsum(padded)
    pad_start = pad_end - padded
    dest = pad_start[bucket] + rank
    nb = n_tok // TM + N_BUCKETS
    n_valid = pad_end[-1] // TM
    blk = jnp.arange(nb, dtype=jnp.int32)
    blk_row = jnp.minimum(blk, n_valid - 1) * TM
    blk_bucket = jnp.minimum(jnp.sum(pad_end[None, :] <= blk_row[:, None], axis=1), N_BUCKETS - 1).astype(jnp.int32)
    grp, pair = blk_bucket // N_PAIRS, blk_bucket % N_PAIRS
    e_lo = grp * EXPERTS_PER_GROUP + jnp.asarray(_PAIR_LO, jnp.int32)[pair]
    e_hi = grp * EXPERTS_PER_GROUP + jnp.asarray(_PAIR_HI, jnp.int32)[pair]
    fill_blk = jnp.where(padded > 0, pad_end // TM - 1, -1).astype(jnp.int32)
    return dest.astype(jnp.int32), fill_blk, e_lo, e_hi, n_valid.reshape(1).astype(jnp.int32), nb * TM


def _rope_tables(seq, n_ctx):
    t = jnp.arange(seq)
    rows, cols = t // GRID_W, t % GRID_W

    def table(width, nf):
        lane = np.arange(width)
        half, j = lane // (2 * nf), lane % (2 * nf)
        inv = ROPE_THETA ** (-jnp.arange(nf, dtype=F32) / nf)
        pos = jnp.where(jnp.asarray(half == 0)[None, :], rows[:, None], cols[:, None]).astype(F32)
        ang = pos * inv[j % nf][None, :]
        sign = jnp.asarray(np.where(j < nf, -1.0, 1.0), F32)[None, :]
        return jnp.cos(ang), jnp.sin(ang) * sign

    cos64, sin64 = table(HEAD_DIM, 16)
    cos64, sin64 = jnp.tile(cos64, (1, 2)), jnp.tile(sin64, (1, 2))
    cosr, sinr = table(MLA_ROPE, 8)
    cosm = jnp.concatenate([jnp.ones((seq, MLA_NOPE), F32), cosr, jnp.ones((seq, 32), F32)], axis=1)
    sinm = jnp.concatenate([jnp.zeros((seq, MLA_NOPE), F32), sinr, jnp.zeros((seq, 32), F32)], axis=1)
    ctx1, ctx0 = jnp.ones((n_ctx, LANES), F32), jnp.zeros((n_ctx, LANES), F32)
    token_major = [jnp.concatenate([c, a], axis=0) for c, a in ((ctx1, cos64), (ctx0, sin64), (ctx1, cosm), (ctx0, sinm))]
    dim_major = [jnp.concatenate([jnp.full((a.shape[1], n_ctx), fill, F32), a.T], axis=1)
                 for a, fill in ((cos64[:, :HEAD_DIM], 1.0), (sin64[:, :HEAD_DIM], 0.0), (cosr, 1.0), (sinr, 0.0))]
    return tuple(token_major + dim_major)


def _layer_weights(l, w_in, w_out, norm_ffn_g, glb_q_gain, glb_k_gain, mla_q_gain, mla_w_uq, mla_kv_gain,
                   mla_w_ukv):
    d = w_in.shape[1]
    wi = w_in[l]
    wa = jnp.concatenate([wi[:, _SQ:_SK], wi[:, _GQ:_GK], wi[:, _MQ:_MKV], wi[:, _SV:_GQ], wi[:, _GV:_MQ],
                          wi[:, _MKV:_KR]], axis=1).T
    kr_tile = jnp.concatenate([jnp.zeros((d, MLA_NOPE), F32), wi[:, _KR:], jnp.zeros((d, 32), F32)], axis=1)
    wb = jnp.concatenate([wi[:, _SK:_SV], wi[:, _GK:_GV], wi[:, _MKV:_KR], kr_tile], axis=1)
    uq = mla_w_uq[l].reshape(MLA_Q_RANK, MLA_HEADS, MLA_NOPE + MLA_ROPE)
    wuq = jnp.concatenate([uq, jnp.zeros((MLA_Q_RANK, MLA_HEADS, 32), F32)], axis=2).reshape(MLA_Q_RANK, 512)
    ukv = mla_w_ukv[l].reshape(MLA_KV_RANK, MLA_HEADS, MLA_NOPE + MLA_V)
    wk = jnp.concatenate([ukv[:, :, :MLA_NOPE], jnp.zeros((MLA_KV_RANK, MLA_HEADS, 64), F32)], axis=2)
    bcast = lambda g: jnp.broadcast_to(g[:, None], (g.shape[0], TQ))
    head = np.arange(LANES) // HEAD_DIM
    return {
        "wa": wa.astype(BF16),
        "wb": wb.astype(BF16),
        "wuq_t": wuq.T.astype(BF16),
        "wukv_k": wk.reshape(MLA_KV_RANK, 512).astype(BF16),
        "wukv_vt": ukv[:, :, MLA_NOPE:].reshape(MLA_KV_RANK, 256).T.astype(BF16),
        "gqg_t": bcast(glb_q_gain[l] * (HEAD_DIM ** -0.5 * LOG2E)),
        "gkg": jnp.tile(glb_k_gain[l], GQA_KV).reshape(1, LANES),
        "mqg_t": bcast(mla_q_gain[l]),
        "mkvg": mla_kv_gain[l].reshape(1, -1),
        "mkvg_t": bcast(mla_kv_gain[l]),
        "bd": jnp.asarray(head[:, None] == head[None, :], BF16),
        "wout": w_out[l].astype(BF16),
        "g_ffn": norm_ffn_g[l].reshape(1, -1),
    }


def kernel(x, c, ctx, c_ctx, w_mod, b_mod, norm_mix_g, norm_ffn_g, w_in, w_out, swa_sink, glb_q_gain, glb_k_gain,
           mla_q_gain, mla_w_uq, mla_kv_gain, mla_w_ukv, router_w, router_bias, exp_w_gate, exp_w_up, exp_w_down,
           shr_w_gate, shr_w_up, shr_w_down, final_norm_g):
    bsz, seq, d = x.shape
    n_ctx = ctx.shape[1]
    n_layers = w_mod.shape[0]
    assert d == D_MODEL and n_ctx % TQ == 0 and seq % TQ == 0 and seq >= TQ + 2 * WINDOW and seq % GRID_W == 0
    n_ctx_blk = n_ctx // TQ

    rows = -(-(bsz + 1) // 8) * 8
    c_rows = jnp.concatenate([c, c_ctx[None, :], jnp.zeros((rows - bsz - 1, d), F32)], axis=0)
    mods = _modulation(c_rows, w_mod, b_mod)
    mod_x = mods[:, :bsz].reshape(n_layers, bsz, 6, d)
    mod_c = jnp.broadcast_to(mods[:, bsz].reshape(n_layers, 1, 6, d), (n_layers, bsz, 6, d))
    modtabs = jnp.pad(jnp.stack([mod_x, mod_c], axis=2), ((0, 0), (0, 0), (0, 0), (0, 2), (0, 0)))

    tabs = _rope_tables(seq, n_ctx)
    rw_hi = router_w.astype(BF16)
    rw_lo = (router_w - rw_hi.astype(F32)).astype(BF16)
    zero_cols = lambda n: jnp.zeros((d, n), BF16)
    shared = {
        "rw": jnp.concatenate([rw_hi, zero_cols(32 - N_EXPERTS), rw_lo, zero_cols(LANES - 32 - N_EXPERTS)], axis=1),
        "rb": jnp.pad(router_bias, (0, 32 - N_EXPERTS)).reshape(32, 1),
        "tri": jnp.asarray(np.arange(TQ)[:, None] <= np.arange(TQ)[None, :], BF16),
    }
    g_final = final_norm_g.reshape(1, d)
    ew = {"gate": exp_w_gate.astype(BF16), "up": exp_w_up.astype(BF16), "down": exp_w_down.astype(BF16),
          "s_gate": shr_w_gate.astype(BF16), "s_up": shr_w_up.astype(BF16), "s_down": shr_w_down.astype(BF16)}

    t_all = n_ctx + seq
    stream = (x, 0, ctx)
    for l in range(n_layers):
        last = l == n_layers - 1
        with_ctx = not last
        lw = _layer_weights(l, w_in, w_out, norm_ffn_g, glb_q_gain, glb_k_gain, mla_q_gain, mla_w_uq,
                            mla_kv_gain, mla_w_ukv)
        modtab = modtabs[l]
        qs, ks, vs, qg, kg, vg, qm, km, vm = _input_projection(
            stream, t_all, modtab, norm_mix_g[l].reshape(1, d), lw, tabs, n_ctx_blk)
        sink = jnp.pad(swa_sink[l] * LOG2E, (0, 8 - GQA_HEADS))
        y_mix = _attention({"swa": (qs, ks, vs), "glb": (qg, kg, vg), "mla": (qm, km, vm)}, sink, n_ctx, with_ctx)
        x_mid, rows_tok, meta, counts = _output_projection(y_mix, stream, t_all, modtab, lw, shared, n_ctx_blk,
                                                           with_ctx)
        n_tok = rows_tok.shape[0]
        dest, fill_blk, e_lo, e_hi, n_valid, n_sorted = _bucket_layout(meta, counts, n_tok)
        rows_sorted = _scatter_rows(dest, fill_blk, n_valid, rows_tok, n_sorted)
        y_sorted = _grouped_experts(e_lo, e_hi, n_valid, rows_sorted, ew, l)
        xs = _gather_residual(dest, y_sorted, x_mid, modtab, g_final, n_ctx_blk, with_ctx, final_norm=last)
        stream = (xs, n_ctx_blk, xs)
    return xs
```
